```python
import math
import jax
import jax.numpy as jnp
from jax import lax
import numpy as np

D_MODEL = 2048
BATCH = 32
SEQ = 256
DEPTH = 2
DEC_BATCH = 2
DEC_SEQ = 4096
PAST_LEN = 512

GRID_W = 64
N_EVEN = (DEPTH + 1) // 2
N_ODD = DEPTH // 2
RMS_EPS = 1e-6
HYENA_W = D_MODEL
HYENA_ORDER = 2
HYENA_CONV_W = 3
HYENA_BANDS = 16
HYENA_EMB = 1 + 2 * HYENA_BANDS
HYENA_FFN_W = 64
HYENA_FAST_DECAY = 0.3
HYENA_SLOW_DECAY = 1.5
HYENA_TARGET = 1e-2
SSD_INNER = D_MODEL
SSD_HEAD_DIM = 64
SSD_HEADS = SSD_INNER // SSD_HEAD_DIM
SSD_STATE = 128
SSD_GROUPS = 8
SSD_CHUNK = 128
SSD_CONV_W = 3
SSD_CONV_CH = SSD_INNER + 2 * SSD_GROUPS * SSD_STATE
EVEN_IN_COLS = 3 * HYENA_W + SSD_INNER + SSD_CONV_CH + 2 * SSD_HEADS
MIX_W = HYENA_W + SSD_INNER
HEAD_DIM = 64
N_HEADS = D_MODEL // HEAD_DIM
N_KV_HEADS = 8
Q_PER_KV = N_HEADS // N_KV_HEADS
WINDOW = 128
ATTN_BLOCK = 128
ROPE_BASE = 10000.0
ATTN_SCALE = HEAD_DIM ** -0.5
MOE_GROUPS = 4
EXPERTS_PER_GROUP = 4
N_EXPERTS = MOE_GROUPS * EXPERTS_PER_GROUP
TOP_K = 2
D_EXPERT = 512

kernel_name = 'hyena_ssd_swa_hmoe_prefix_denoise_step'


def _rmsnorm(x, g):
    x32 = x.astype(jnp.float32)
    y = x32 * lax.rsqrt(jnp.mean(x32 * x32, axis=-1, keepdims=True) + RMS_EPS)
    return y.astype(x.dtype) * g


def _modulate(x, shift, scale):
    return x * (1.0 + scale) + shift


def _dwconv(x, w, b):
    width = w.shape[0]
    pad = width // 2
    L = x.shape[1]
    xp = jnp.pad(x, ((0, 0), (pad, pad), (0, 0)))
    return sum(xp[:, k:k + L] * w[k] for k in range(width)) + b


def _hyena_filters(L, f_w1, f_b1, f_freq1, f_w2, f_b2, f_freq2, f_w3):
    f32 = jnp.float32
    t = jnp.arange(L, dtype=f32)
    t01 = t / (L - 1)
    bands = jnp.linspace(1e-4, HYENA_BANDS - 1, HYENA_BANDS, dtype=f32)
    ang = (2.0 * math.pi / L) * t[:, None] * bands[None, :]
    feat = jnp.concatenate([t01[:, None], jnp.cos(ang), -jnp.sin(ang)], axis=-1)
    hid = jnp.sin(f_freq1.astype(f32) * (feat @ f_w1.astype(f32) + f_b1.astype(f32)))
    hid = jnp.sin(f_freq2.astype(f32) * (hid @ f_w2.astype(f32) + f_b2.astype(f32)))
    filt = (hid @ f_w3.astype(f32)).reshape(L, 2, HYENA_ORDER, HYENA_W)
    deltas = jnp.abs(jnp.linspace(math.log(HYENA_TARGET) / HYENA_SLOW_DECAY,
                                  math.log(HYENA_TARGET) / HYENA_FAST_DECAY, HYENA_W, dtype=f32))
    filt = filt * jnp.exp(-t01[:, None, None, None] * deltas)
    fwd, bwd = filt[:, 0], filt[:, 1]
    return jnp.concatenate([fwd, jnp.zeros_like(fwd[:1]), bwd[:0:-1]], axis=0)


def _hyena(u, f_w1, f_b1, f_freq1, f_w2, f_b2, f_freq2, f_w3, hy_bias):
    bsz, L, _ = u.shape
    x1, x2, v = jnp.split(u, 3, axis=-1)
    kf = jnp.fft.rfft(_hyena_filters(L, f_w1, f_b1, f_freq1, f_w2, f_b2, f_freq2, f_w3), axis=0)
    z = v
    for o, gate in enumerate((x1, x2)):
        zf = jnp.fft.rfft(z.astype(jnp.float32), n=2 * L, axis=1)
        conv = jnp.fft.irfft(zf * kf[None, :, o], n=2 * L, axis=1)[:, :L]
        z = gate * (conv.astype(u.dtype) + hy_bias[o] * z)
    return z


def _segsum(x):
    T = x.shape[-1]
    cs = jnp.cumsum(x, axis=-1)
    diff = cs[..., :, None] - cs[..., None, :]
    mask = jnp.tril(jnp.ones((T, T), dtype=bool))
    return jnp.where(mask, diff, -jnp.inf)


def _ssd_scan(x, dt, a_log, b, c, init):
    f32 = jnp.float32
    bsz, L = x.shape[:2]
    nc = L // SSD_CHUNK
    G, J, P, N, Q = SSD_GROUPS, SSD_HEADS // SSD_GROUPS, SSD_HEAD_DIM, SSD_STATE, SSD_CHUNK
    da = dt * (-jnp.exp(a_log.astype(f32)))
    xd = (x.astype(f32) * dt[..., None]).reshape(bsz, nc, Q, G, J, P)
    bq = b.astype(f32).reshape(bsz, nc, Q, G, N)
    cq = c.astype(f32).reshape(bsz, nc, Q, G, N)
    da = da.reshape(bsz, nc, Q, G, J).transpose(0, 3, 4, 1, 2)
    a_cum = jnp.cumsum(da, axis=-1)
    decay_in = jnp.exp(_segsum(da))
    cb = jnp.einsum('bclgn,bcsgn->bgcls', cq, bq)
    y_diag = jnp.einsum('bgjcls,bcsgjp->bclgjp', cb[:, :, None] * decay_in, xd)
    decay_st = jnp.exp(a_cum[..., -1:] - a_cum)
    st = jnp.einsum('bclgn,bgjcl,bclgjp->bcgjpn', bq, decay_st, xd)
    st = jnp.concatenate([init.astype(f32).reshape(bsz, 1, G, J, P, N), st], axis=1)
    chunk_decay = jnp.exp(_segsum(jnp.pad(a_cum[..., -1], ((0, 0), (0, 0), (0, 0), (1, 0)))))
    st = jnp.einsum('bgjzc,bcgjpn->bzgjpn', chunk_decay, st)
    y_off = jnp.einsum('bclgn,bcgjpn,bgjcl->bclgjp', cq, st[:, :-1], jnp.exp(a_cum))
    y = (y_diag + y_off).reshape(bsz, L, G * J, P)
    return y, st[:, -1].reshape(bsz, G * J, P, N).astype(x.dtype)


def _ssd_mixer(z, xbc, dt, conv_w, conv_b, dt_bias, a_log, d_skip, norm_w, init_fwd, init_bwd):
    bsz, L, _ = z.shape
    xbc = jax.nn.silu(_dwconv(xbc, conv_w, conv_b))
    xs, bs, cs = jnp.split(xbc, [SSD_INNER, SSD_INNER + SSD_GROUPS * SSD_STATE], axis=-1)
    xs = xs.reshape(bsz, L, SSD_HEADS, SSD_HEAD_DIM)
    bs = bs.reshape(bsz, L, SSD_GROUPS, SSD_STATE)
    cs = cs.reshape(bsz, L, SSD_GROUPS, SSD_STATE)
    dt = jax.nn.softplus((dt.reshape(bsz, L, 2, SSD_HEADS) + dt_bias).astype(jnp.float32))
    flip = lambda a: a[:, ::-1]
    y_f, s_f = _ssd_scan(xs, dt[:, :, 0], a_log[0], bs, cs, init_fwd)
    y_b, s_b = _ssd_scan(flip(xs), flip(dt[:, :, 1]), a_log[1], flip(bs), flip(cs), init_bwd)
    y = y_f + flip(y_b) + xs.astype(jnp.float32) * d_skip[:, None].astype(jnp.float32)
    y = y.reshape(bsz, L, SSD_INNER).astype(z.dtype)
    return _rmsnorm(y * jax.nn.silu(z), norm_w), s_f, s_b


def _even_mixer(h, w_in, w_out, hy_conv_w, hy_conv_b, f_w1, f_b1, f_freq1, f_w2, f_b2, f_freq2, f_w3, hy_bias,
                ssd_conv_w, ssd_conv_b, ssd_dt_bias, ssd_a_log, ssd_d, ssd_norm, init_fwd, init_bwd):
    proj = h @ w_in
    hy_in, z, xbc, dt = jnp.split(
        proj, [3 * HYENA_W, 3 * HYENA_W + SSD_INNER, 3 * HYENA_W + SSD_INNER + SSD_CONV_CH], axis=-1)
    y_a = _hyena(_dwconv(hy_in, hy_conv_w, hy_conv_b), f_w1, f_b1, f_freq1, f_w2, f_b2, f_freq2, f_w3, hy_bias)
    y_b, s_f, s_b = _ssd_mixer(z, xbc, dt, ssd_conv_w, ssd_conv_b, ssd_dt_bias, ssd_a_log, ssd_d, ssd_norm,
                               init_fwd, init_bwd)
    return jnp.concatenate([y_a, y_b], axis=-1) @ w_out, s_f, s_b


def _axial_rope(x):
    bsz, L, nh, hd = x.shape
    rows = L // GRID_W
    row = jnp.repeat(jnp.arange(rows, dtype=jnp.float32), GRID_W)
    col = jnp.tile(jnp.arange(GRID_W, dtype=jnp.float32), rows)
    half = hd // 2
    quarter = half // 2
    inv = ROPE_BASE ** (-jnp.arange(quarter, dtype=jnp.float32) / quarter)
    x32 = x.astype(jnp.float32)

    def rot(xa, pos):
        ang = pos[:, None] * inv[None, :]
        cos = jnp.cos(ang)[None, :, None, :]
        sin = jnp.sin(ang)[None, :, None, :]
        a, b = xa[..., :quarter], xa[..., quarter:]
        return jnp.concatenate([a * cos - b * sin, b * cos + a * sin], axis=-1)

    return jnp.concatenate([rot(x32[..., :half], row), rot(x32[..., half:], col)], axis=-1).astype(x.dtype)


def _attn_qkv(h, wq, wk, wv):
    bsz, L, _ = h.shape
    q = (h @ wq).reshape(bsz, L, N_HEADS, HEAD_DIM)
    k = (h @ wk).reshape(bsz, L, N_KV_HEADS, HEAD_DIM)
    v = (h @ wv).reshape(bsz, L, N_KV_HEADS, HEAD_DIM)
    return q, k, v


def _context_attention(q, k, v, sink):
    bsz, L = q.shape[:2]
    nb = L // ATTN_BLOCK
    qb = jnp.moveaxis(q.reshape(bsz, nb, ATTN_BLOCK, N_KV_HEADS, Q_PER_KV, HEAD_DIM), 1, 0)
    sink_col = sink.astype(jnp.float32).reshape(1, N_KV_HEADS, Q_PER_KV, 1, 1)

    def block(qi):
        s = jnp.einsum('bqkgd,bskd->bkgqs', qi, k).astype(jnp.float32) * ATTN_SCALE
        s = jnp.concatenate([s, jnp.broadcast_to(sink_col, s.shape[:-1] + (1,))], axis=-1)
        p = jax.nn.softmax(s, axis=-1)[..., :-1].astype(v.dtype)
        return jnp.einsum('bkgqs,bskd->bqkgd', p, v)

    out = lax.map(block, qb)
    return jnp.moveaxis(out, 0, 1).reshape(bsz, L, N_HEADS * HEAD_DIM)


def _latent_attention(q, k, v, k_ctx, v_ctx, sink):
    bsz, L = q.shape[:2]
    nb = L // ATTN_BLOCK
    n_loc = 3 * ATTN_BLOCK
    n_ctx = k_ctx.shape[1]
    qb = q.reshape(bsz, nb, ATTN_BLOCK, N_KV_HEADS, Q_PER_KV, HEAD_DIM)
    pad = ((0, 0), (ATTN_BLOCK, ATTN_BLOCK), (0, 0), (0, 0))
    kp, vp = jnp.pad(k, pad), jnp.pad(v, pad)
    sink_col = sink.astype(jnp.float32).reshape(1, N_KV_HEADS, Q_PER_KV, 1, 1)

    def block(i):
        qi = lax.dynamic_index_in_dim(qb, i, axis=1, keepdims=False)
        ki = lax.dynamic_slice_in_dim(kp, i * ATTN_BLOCK, n_loc, axis=1)
        vi = lax.dynamic_slice_in_dim(vp, i * ATTN_BLOCK, n_loc, axis=1)
        qpos = i * ATTN_BLOCK + jnp.arange(ATTN_BLOCK)
        kpos = (i - 1) * ATTN_BLOCK + jnp.arange(n_loc)
        ok = (jnp.abs(qpos[:, None] - kpos[None, :]) <= WINDOW) & (kpos[None, :] >= 0) & (kpos[None, :] < L)
        s_loc = jnp.where(ok, jnp.einsum('bqkgd,bskd->bkgqs', qi, ki).astype(jnp.float32) * ATTN_SCALE, -jnp.inf)
        s_ctx = jnp.einsum('bqkgd,bskd->bkgqs', qi, k_ctx).astype(jnp.float32) * ATTN_SCALE
        s = jnp.concatenate([s_loc, s_ctx, jnp.broadcast_to(sink_col, s_loc.shape[:-1] + (1,))], axis=-1)
        p = jax.nn.softmax(s, axis=-1).astype(v.dtype)
        return (jnp.einsum('bkgqs,bskd->bqkgd', p[..., :n_loc], vi)
                + jnp.einsum('bkgqs,bskd->bqkgd', p[..., n_loc:n_loc + n_ctx], v_ctx))

    out = lax.map(block, jnp.arange(nb))
    return jnp.moveaxis(out, 0, 1).reshape(bsz, L, N_HEADS * HEAD_DIM)


def _moe(h, w_group, b_group, w_expert, b_expert, w1, w3, w2):
    f32 = jnp.float32
    shp = h.shape
    t = h.reshape(-1, shp[-1])
    g_prob = jax.nn.softmax((t @ w_group + b_group).astype(f32), axis=-1)
    g_top, g_idx = lax.top_k(g_prob, 1)
    g_onehot = jax.nn.one_hot(g_idx[:, 0], MOE_GROUPS, dtype=f32)
    e_logits = (t @ w_expert + b_expert).astype(f32).reshape(-1, MOE_GROUPS, EXPERTS_PER_GROUP)
    e_in_group = jnp.einsum('tge,tg->te', e_logits, g_onehot)
    e_top, e_idx = lax.top_k(jax.nn.softmax(e_in_group, axis=-1), TOP_K)
    e_top = e_top / jnp.sum(e_top, axis=-1, keepdims=True)
    e_w = jnp.einsum('tk,tke->te', e_top, jax.nn.one_hot(e_idx, EXPERTS_PER_GROUP, dtype=f32))
    gates = ((g_onehot * g_top)[:, :, None] * e_w[:, None, :]).reshape(-1, N_EXPERTS).astype(h.dtype)
    y = jnp.zeros_like(t)
    for e in range(N_EXPERTS):
        a = jax.nn.silu(t @ w1[e]) * (t @ w3[e])
        y = y + gates[:, e:e + 1] * (a @ w2[e])
    return y.reshape(shp)


def setup_inputs(seed: int = 0) -> dict:
    key = jax.random.key(seed)
    ks = iter(jax.random.split(key, 64))
    f32 = jnp.float32

    def nrm(shape, std):
        return std * jax.random.normal(next(ks), shape, f32)

    def gain(shape):
        return 1.0 + nrm(shape, 0.02)

    inp = {}
    inp['x_prompt'] = nrm((BATCH, SEQ, D_MODEL), 1.0)
    inp['x_sample'] = nrm((DEC_BATCH, DEC_SEQ, D_MODEL), 1.0)
    inp['state_ssd_fwd'] = nrm((DEC_BATCH, N_EVEN, SSD_HEADS, SSD_HEAD_DIM, SSD_STATE), 0.1)
    inp['state_ssd_bwd'] = nrm((DEC_BATCH, N_EVEN, SSD_HEADS, SSD_HEAD_DIM, SSD_STATE), 0.1)
    inp['cache_attn_k'] = nrm((DEC_BATCH, N_ODD, PAST_LEN, N_KV_HEADS, HEAD_DIM), 1.0)
    inp['cache_attn_v'] = nrm((DEC_BATCH, N_ODD, PAST_LEN, N_KV_HEADS, HEAD_DIM), 1.0)
    inp['c'] = nrm((DEC_BATCH, D_MODEL), 1.0)
    inp['c_ctx'] = nrm((D_MODEL,), 1.0)
    inp['ada_w'] = nrm((DEPTH, D_MODEL, 6 * D_MODEL), 0.5 * D_MODEL ** -0.5)
    inp['ada_b'] = nrm((DEPTH, 6 * D_MODEL), 0.02)
    inp['norm_mix'] = gain((DEPTH, D_MODEL))
    inp['norm_ffn'] = gain((DEPTH, D_MODEL))
    inp['norm_final'] = gain((D_MODEL,))
    inp['ev_w_in'] = nrm((N_EVEN, D_MODEL, EVEN_IN_COLS), D_MODEL ** -0.5)
    inp['ev_w_out'] = nrm((N_EVEN, MIX_W, D_MODEL), MIX_W ** -0.5)
    inp['hy_conv_w'] = nrm((N_EVEN, HYENA_CONV_W, 3 * HYENA_W), HYENA_CONV_W ** -0.5)
    inp['hy_conv_b'] = nrm((N_EVEN, 3 * HYENA_W), 0.02)
    inp['hy_f_w1'] = nrm((N_EVEN, HYENA_EMB, HYENA_FFN_W), HYENA_EMB ** -0.5)
    inp['hy_f_b1'] = nrm((N_EVEN, HYENA_FFN_W), 0.5)
    inp['hy_f_freq1'] = 1.0 + nrm((N_EVEN, HYENA_FFN_W), 0.1)
    inp['hy_f_w2'] = nrm((N_EVEN, HYENA_FFN_W, HYENA_FFN_W), HYENA_FFN_W ** -0.5)
    inp['hy_f_b2'] = nrm((N_EVEN, HYENA_FFN_W), 0.5)
    inp['hy_f_freq2'] = 1.0 + nrm((N_EVEN, HYENA_FFN_W), 0.1)
    inp['hy_f_w3'] = nrm((N_EVEN, HYENA_FFN_W, 2 * HYENA_ORDER * HYENA_W), 0.005)
    inp['hy_bias'] = nrm((N_EVEN, HYENA_ORDER, HYENA_W), 0.5)
    inp['ssd_conv_w'] = nrm((N_EVEN, SSD_CONV_W, SSD_CONV_CH), SSD_CONV_W ** -0.5)
    inp['ssd_conv_b'] = nrm((N_EVEN, SSD_CONV_CH), 0.02)
    dt0 = jnp.exp(jax.random.uniform(next(ks), (N_EVEN, 2, SSD_HEADS), f32, math.log(1e-3), math.log(1e-1)))
    inp['ssd_dt_bias'] = dt0 + jnp.log(-jnp.expm1(-dt0))
    inp['ssd_a_log'] = jnp.log(jax.random.uniform(next(ks), (N_EVEN, 2, SSD_HEADS), f32, 1.0, 16.0))
    inp['ssd_d'] = 1.0 + nrm((N_EVEN, SSD_HEADS), 0.1)
    inp['ssd_norm'] = gain((N_EVEN, SSD_INNER))
    inp['at_wq'] = nrm((N_ODD, D_MODEL, N_HEADS * HEAD_DIM), D_MODEL ** -0.5)
    inp['at_wk'] = nrm((N_ODD, D_MODEL, N_KV_HEADS * HEAD_DIM), D_MODEL ** -0.5)
    inp['at_wv'] = nrm((N_ODD, D_MODEL, N_KV_HEADS * HEAD_DIM), D_MODEL ** -0.5)
    inp['at_wo'] = nrm((N_ODD, N_HEADS * HEAD_DIM, D_MODEL), (N_HEADS * HEAD_DIM) ** -0.5)
    inp['at_sink'] = nrm((N_ODD, N_HEADS), 0.5)
    inp['moe_w_group'] = nrm((DEPTH, D_MODEL, MOE_GROUPS), D_MODEL ** -0.5)
    inp['moe_b_group'] = nrm((DEPTH, MOE_GROUPS), 0.01)
    inp['moe_w_expert'] = nrm((DEPTH, D_MODEL, N_EXPERTS), D_MODEL ** -0.5)
    inp['moe_b_expert'] = nrm((DEPTH, N_EXPERTS), 0.01)
    inp['moe_w1'] = nrm((DEPTH, N_EXPERTS, D_MODEL, D_EXPERT), D_MODEL ** -0.5)
    inp['moe_w3'] = nrm((DEPTH, N_EXPERTS, D_MODEL, D_EXPERT), D_MODEL ** -0.5)
    inp['moe_w2'] = nrm((DEPTH, N_EXPERTS, D_EXPERT, D_MODEL), D_EXPERT ** -0.5)
    return inp


def reference(x_prompt, x_sample, state_ssd_fwd, state_ssd_bwd, cache_attn_k, cache_attn_v, c, c_ctx,
              ada_w, ada_b, norm_mix, norm_ffn, norm_final,
              ev_w_in, ev_w_out, hy_conv_w, hy_conv_b, hy_f_w1, hy_f_b1, hy_f_freq1, hy_f_w2, hy_f_b2,
              hy_f_freq2, hy_f_w3, hy_bias, ssd_conv_w, ssd_conv_b, ssd_dt_bias, ssd_a_log, ssd_d, ssd_norm,
              at_wq, at_wk, at_wv, at_wo, at_sink,
              moe_w_group, moe_b_group, moe_w_expert, moe_b_expert, moe_w1, moe_w3, moe_w2):
    xc, xl = x_prompt, x_sample
    bsz_c = x_prompt.shape[0]
    new_sf, new_sb, new_k, new_v = [], [], [], []
    for l in range(DEPTH):
        i = l // 2
        mod_c = jax.nn.silu(c_ctx) @ ada_w[l] + ada_b[l]
        mod_l = jax.nn.silu(c) @ ada_w[l] + ada_b[l]
        sh1c, sc1c, g1c, sh2c, sc2c, g2c = jnp.split(mod_c[None, None, :], 6, axis=-1)
        sh1l, sc1l, g1l, sh2l, sc2l, g2l = jnp.split(mod_l[:, None, :], 6, axis=-1)
        hc = _modulate(_rmsnorm(xc, norm_mix[l]), sh1c, sc1c)
        hl = _modulate(_rmsnorm(xl, norm_mix[l]), sh1l, sc1l)
        if l % 2 == 0:
            ev = (ev_w_in[i], ev_w_out[i], hy_conv_w[i], hy_conv_b[i], hy_f_w1[i], hy_f_b1[i], hy_f_freq1[i],
                  hy_f_w2[i], hy_f_b2[i], hy_f_freq2[i], hy_f_w3[i], hy_bias[i], ssd_conv_w[i], ssd_conv_b[i],
                  ssd_dt_bias[i], ssd_a_log[i], ssd_d[i], ssd_norm[i])
            zero = jnp.zeros((bsz_c, SSD_HEADS, SSD_HEAD_DIM, SSD_STATE), xc.dtype)
            oc, sf, sb = _even_mixer(hc, *ev, zero, zero)
            ol, _, _ = _even_mixer(hl, *ev, state_ssd_fwd[:, i], state_ssd_bwd[:, i])
            new_sf.append(sf)
            new_sb.append(sb)
        else:
            qc, kc, vc = _attn_qkv(hc, at_wq[i], at_wk[i], at_wv[i])
            oc = _context_attention(qc, kc, vc, at_sink[i]) @ at_wo[i]
            ql, kl, vl = _attn_qkv(hl, at_wq[i], at_wk[i], at_wv[i])
            ol = _latent_attention(_axial_rope(ql), _axial_rope(kl), vl,
                                   cache_attn_k[:, i], cache_attn_v[:, i], at_sink[i]) @ at_wo[i]
            new_k.append(kc)
            new_v.append(vc)
        xc = xc + g1c * oc
        xl = xl + g1l * ol
        mp = (moe_w_group[l], moe_b_group[l], moe_w_expert[l], moe_b_expert[l], moe_w1[l], moe_w3[l], moe_w2[l])
        xc = xc + g2c * _moe(_modulate(_rmsnorm(xc, norm_ffn[l]), sh2c, sc2c), *mp)
        xl = xl + g2l * _moe(_modulate(_rmsnorm(xl, norm_ffn[l]), sh2l, sc2l), *mp)
    y_prompt = _rmsnorm(xc, norm_final)
    y_sample = _rmsnorm(xl, norm_final)
    new_state_ssd_fwd = jnp.stack(new_sf, axis=1)
    new_state_ssd_bwd = jnp.stack(new_sb, axis=1)
    new_cache_attn_k = jnp.stack(new_k, axis=1)
    new_cache_attn_v = jnp.stack(new_v, axis=1)
    return (y_prompt, y_sample, new_state_ssd_fwd, new_state_ssd_bwd, new_cache_attn_k, new_cache_attn_v)
```

```python
import functools
import math

import jax
import jax.numpy as jnp
from jax import lax
from jax.experimental import pallas as pl
from jax.experimental.pallas import tpu as pltpu

F32 = jnp.float32
BF16 = jnp.bfloat16
HI = lax.Precision.HIGHEST

RMS_EPS = 1e-6
GRID_W = 64
HYENA_BANDS = 16
HYENA_FAST_DECAY = 0.3
HYENA_SLOW_DECAY = 1.5
HYENA_TARGET = 1e-2
SSD_HEAD_DIM = 64
SSD_STATE = 128
SSD_GROUPS = 8
SSD_CHUNK = 128
HEAD_DIM = 64
N_KV_HEADS = 8
WINDOW = 128
ROPE_BASE = 10000.0
MOE_GROUPS = 4
TOP_K = 2

LANES = 128
SUBLANES = 8
VMEM_LIMIT_BYTES = 48 * 1024 * 1024

ROW_TILE = 256
SPEC_BLOCK = 128


def _cp(*sem):
    return pltpu.CompilerParams(dimension_semantics=sem, vmem_limit_bytes=VMEM_LIMIT_BYTES)


def _tile(n, pref):
    if n <= pref:
        return n
    t = pref
    while n % t:
        t //= 2
    assert t >= LANES, (n, pref)
    return t


def _silu(x):
    return x * (1.0 / (1.0 + jnp.exp(-x)))


def _softplus(x):
    return jnp.maximum(x, 0.0) + jnp.log(1.0 + jnp.exp(-jnp.abs(x)))


def _ada_body(c_ref, w_ref, b_ref, o_ref):
    a = _silu(c_ref[...])
    o_ref[...] = lax.dot_general(a, w_ref[...], (((1,), (0,)), ((), ())), precision=HI,
                                 preferred_element_type=F32) + b_ref[...]


def ada_mod(cvec, ada_w, ada_b):
    depth, d, n = ada_w.shape
    rows = cvec.shape[0]
    tn = _tile(n, 1024)
    return pl.pallas_call(
        _ada_body,
        out_shape=jax.ShapeDtypeStruct((depth, rows, n), F32),
        grid=(depth, n // tn),
        in_specs=[pl.BlockSpec((rows, d), lambda l, j: (0, 0)),
                  pl.BlockSpec((None, d, tn), lambda l, j: (l, 0, j)),
                  pl.BlockSpec((None, 1, tn), lambda l, j: (l, 0, j))],
        out_specs=pl.BlockSpec((None, rows, tn), lambda l, j: (l, 0, j)),
        compiler_params=_cp("parallel", "parallel"),
        name="ada_mod",
    )(cvec, ada_w, ada_b.reshape(depth, 1, n))


def _norm_body(x_ref, g_ref, *rest, modulate):
    o_ref = rest[-1]
    x = x_ref[...]
    y = x * lax.rsqrt(jnp.mean(x * x, axis=-1, keepdims=True) + RMS_EPS)
    y = y * g_ref[...]
    if modulate:
        sh_ref, sc_ref = rest[0], rest[1]
        y = y * (1.0 + sc_ref[...]) + sh_ref[...]
    o_ref[...] = y.astype(o_ref.dtype)


def rms_norm(x, g, *, out_dtype, mod3=None, shift_idx=0, scale_idx=0, seg_of_tile=None,
             row_off=0, rows=None):
    t, d = x.shape
    rows = t if rows is None else rows
    tr = ROW_TILE
    off = row_off // tr
    in_specs = [pl.BlockSpec((tr, d), lambda i: (i + off, 0)),
                pl.BlockSpec((1, d), lambda i: (0, 0))]
    args = [x, g.reshape(1, d)]
    if mod3 is not None:
        in_specs += [pl.BlockSpec((None, 1, d), lambda i: (seg_of_tile(i) * 6 + shift_idx, 0, 0)),
                     pl.BlockSpec((None, 1, d), lambda i: (seg_of_tile(i) * 6 + scale_idx, 0, 0))]
        args += [mod3, mod3]
    return pl.pallas_call(
        functools.partial(_norm_body, modulate=mod3 is not None),
        out_shape=jax.ShapeDtypeStruct((rows, d), out_dtype),
        grid=(rows // tr,),
        in_specs=in_specs,
        out_specs=pl.BlockSpec((tr, d), lambda i: (i, 0)),
        compiler_params=_cp("parallel"),
        name="rms_norm",
    )(*args)


def _mm_body(*refs, nk, n_extra, epilogue):
    a_ref, b_ref = refs[0], refs[1]
    extra = refs[2:2 + n_extra]
    o_ref = refs[2 + n_extra]
    row_tile = pl.program_id(1)
    part = jnp.dot(a_ref[...].astype(b_ref.dtype), b_ref[...], preferred_element_type=F32)
    if nk == 1:
        o_ref[...] = epilogue(part, *extra, row_tile=row_tile).astype(o_ref.dtype)
        return
    acc_ref = refs[3 + n_extra]
    k = pl.program_id(3)

    @pl.when(k == 0)
    def _():
        acc_ref[...] = part

    @pl.when(k > 0)
    def _():
        acc_ref[...] += part

    @pl.when(k == nk - 1)
    def _():
        o_ref[...] = epilogue(acc_ref[...], *extra, row_tile=row_tile).astype(o_ref.dtype)


def matmul(a, b, *, grid, tm, tn, a_spec, b_spec, o_spec, out_shape, out_dtype,
           extra=(), extra_specs=(), epilogue=None, name="matmul"):
    nk = grid[3]
    epilogue = epilogue or (lambda acc, row_tile: acc)
    scratch = [pltpu.VMEM((tm, tn), F32)] if nk > 1 else []
    return pl.pallas_call(
        functools.partial(_mm_body, nk=nk, n_extra=len(extra), epilogue=epilogue),
        out_shape=jax.ShapeDtypeStruct(out_shape, out_dtype),
        grid=grid,
        in_specs=[a_spec, b_spec, *extra_specs],
        out_specs=o_spec,
        scratch_shapes=scratch,
        compiler_params=_cp("parallel", "parallel", "parallel", "arbitrary"),
        name=name,
    )(a, b, *extra)


def dense(a, w, *, out_dtype, tm=1024, tn=512, tk=2048, extra=(), extra_specs=(), epilogue=None,
          name="dense"):
    m, k = a.shape
    n = w.shape[1]
    tm, tn, tk = _tile(m, tm), _tile(n, tn), _tile(k, tk)
    return matmul(
        a, w, grid=(1, m // tm, n // tn, k // tk), tm=tm, tn=tn,
        a_spec=pl.BlockSpec((tm, tk), lambda bt, i, j, kk: (i, kk)),
        b_spec=pl.BlockSpec((tk, tn), lambda bt, i, j, kk: (kk, j)),
        o_spec=pl.BlockSpec((tm, tn), lambda bt, i, j, kk: (i, j)),
        out_shape=(m, n), out_dtype=out_dtype, extra=extra, extra_specs=extra_specs,
        epilogue=epilogue, name=name)


def dense_residual(a, w, x, mod3, gate_idx, seg_of_rows, seg_rows, *, tm=1024, tn=512, tk=2048, name="dense_res"):
    n = w.shape[1]
    tm_ = _tile(seg_rows, tm)
    tn_ = _tile(n, tn)
    return dense(
        a, w, out_dtype=F32, tm=tm_, tn=tn, tk=tk, extra=(x, mod3),
        extra_specs=(pl.BlockSpec((tm_, tn_), lambda bt, i, j, kk: (i, j)),
                     pl.BlockSpec((None, 1, tn_),
                                  lambda bt, i, j, kk: (seg_of_rows(i * tm_) * 6 + gate_idx, 0, j))),
        epilogue=lambda acc, x_ref, g_ref, row_tile: x_ref[...] + g_ref[...] * acc, name=name)


def _dwconv_body(flags_ref, x_ref, p_ref, n_ref, w_ref, b_ref, o_ref, *, n_plain):
    i = pl.program_id(0)
    j = pl.program_id(1)
    x = x_ref[...].astype(F32)
    tr = x.shape[0]
    row = lax.broadcasted_iota(jnp.int32, x.shape, 0)
    keep_prev = (flags_ref[0, i] == 0).astype(F32)
    keep_next = (flags_ref[1, i] == 0).astype(F32)
    halo_prev = p_ref[SUBLANES - 1:SUBLANES, :].astype(F32) * keep_prev
    halo_next = n_ref[0:1, :].astype(F32) * keep_next
    prev = jnp.where(row == 0, halo_prev, pltpu.roll(x, 1, 0))
    nxt = jnp.where(row == tr - 1, halo_next, pltpu.roll(x, tr - 1, 0))
    w = w_ref[...]
    y = prev * w[0:1, :] + x * w[1:2, :] + nxt * w[2:3, :] + b_ref[...]
    y = jnp.where(j >= n_plain, _silu(y), y)
    o_ref[...] = y.astype(o_ref.dtype)


def dwconv(proj, w, b, flags, *, n_plain_cols, gap_cols, out_cols):
    t = proj.shape[0]
    tr, tc = ROW_TILE, 512
    n_plain, gap = n_plain_cols // tc, gap_cols // tc
    hb = tr // SUBLANES
    last_hb = t // SUBLANES - 1

    def src(j):
        return j + jnp.where(j >= n_plain, gap, 0)

    grid_spec = pltpu.PrefetchScalarGridSpec(
        num_scalar_prefetch=1,
        grid=(t // tr, out_cols // tc),
        in_specs=[pl.BlockSpec((tr, tc), lambda i, j, f: (i, src(j))),
                  pl.BlockSpec((SUBLANES, tc), lambda i, j, f: (jnp.maximum(i * hb - 1, 0), src(j))),
                  pl.BlockSpec((SUBLANES, tc), lambda i, j, f: (jnp.minimum((i + 1) * hb, last_hb), src(j))),
                  pl.BlockSpec((3, tc), lambda i, j, f: (0, j)),
                  pl.BlockSpec((1, tc), lambda i, j, f: (0, j))],
        out_specs=pl.BlockSpec((tr, tc), lambda i, j, f: (i, j)))
    return pl.pallas_call(
        functools.partial(_dwconv_body, n_plain=n_plain),
        out_shape=jax.ShapeDtypeStruct((t, out_cols), BF16),
        grid_spec=grid_spec,
        compiler_params=_cp("parallel", "parallel"),
        name="dwconv",
    )(flags, proj, proj, proj, w, b.reshape(1, -1))


def _filter_body(feat_ref, w1_ref, b1_ref, f1_ref, w2_ref, b2_ref, f2_ref, w3_ref, dl_ref, o_ref):
    d = pl.program_id(0)
    i = pl.program_id(1)
    feat = feat_ref[...]
    dn = (((1,), (0,)), ((), ()))
    hid = jnp.sin(f1_ref[...] * (lax.dot_general(feat, w1_ref[...], dn, precision=HI,
                                                 preferred_element_type=F32) + b1_ref[...]))
    hid = jnp.sin(f2_ref[...] * (lax.dot_general(hid, w2_ref[...], dn, precision=HI,
                                                 preferred_element_type=F32) + b2_ref[...]))
    filt = lax.dot_general(hid, w3_ref[...], dn, precision=HI, preferred_element_type=F32)
    t01 = feat[:, 0:1]
    filt = filt * jnp.exp(-t01 * dl_ref[...])
    row = lax.broadcasted_iota(jnp.int32, filt.shape, 0) + i * filt.shape[0]
    filt = jnp.where((d == 1) & (row == 0), 0.0, filt)
    o_ref[...] = filt.astype(o_ref.dtype)


def hyena_filters(L, f_w1, f_b1, f_freq1, f_w2, f_b2, f_freq2, f_w3, c):
    t = jnp.arange(L, dtype=F32)
    t01 = t / (L - 1)
    bands = jnp.linspace(1e-4, HYENA_BANDS - 1, HYENA_BANDS, dtype=F32)
    ang = (2.0 * math.pi / L) * t[:, None] * bands[None, :]
    feat = jnp.concatenate([t01[:, None], jnp.cos(ang), -jnp.sin(ang)], axis=-1)
    emb = feat.shape[1]
    feat = jnp.pad(feat, ((0, 0), (0, LANES - emb)))
    w1 = jnp.pad(f_w1, ((0, LANES - emb), (0, 0)))
    ffn = f_w1.shape[1]
    order = f_w3.shape[1] // (2 * c)
    deltas = jnp.abs(jnp.linspace(math.log(HYENA_TARGET) / HYENA_SLOW_DECAY,
                                  math.log(HYENA_TARGET) / HYENA_FAST_DECAY, c, dtype=F32))
    deltas = jnp.tile(deltas, order).reshape(1, order * c)
    tl, tn = _tile(L, 512), _tile(order * c, 1024)
    nj = order * c // tn
    return pl.pallas_call(
        _filter_body,
        out_shape=jax.ShapeDtypeStruct((2, L, order * c), BF16),
        grid=(2, L // tl, nj),
        in_specs=[pl.BlockSpec((tl, LANES), lambda d, i, j: (i, 0)),
                  pl.BlockSpec((LANES, ffn), lambda d, i, j: (0, 0)),
                  pl.BlockSpec((1, ffn), lambda d, i, j: (0, 0)),
                  pl.BlockSpec((1, ffn), lambda d, i, j: (0, 0)),
                  pl.BlockSpec((ffn, ffn), lambda d, i, j: (0, 0)),
                  pl.BlockSpec((1, ffn), lambda d, i, j: (0, 0)),
                  pl.BlockSpec((1, ffn), lambda d, i, j: (0, 0)),
                  pl.BlockSpec((ffn, tn), lambda d, i, j: (0, d * nj + j)),
                  pl.BlockSpec((1, tn), lambda d, i, j: (0, j))],
        out_specs=pl.BlockSpec((None, tl, tn), lambda d, i, j: (d, i, j)),
        compiler_params=_cp("parallel", "parallel", "parallel"),
        name="hyena_filters",
    )(feat, w1, f_b1.reshape(1, ffn), f_freq1.reshape(1, ffn), f_w2, f_b2.reshape(1, ffn),
      f_freq2.reshape(1, ffn), f_w3, deltas)


def _cos_turns(m, log2_period):
    period = 1 << log2_period
    quarter = period // 4
    mm = m + quarter // 2
    q = (mm >> (log2_period - 2)) & 3
    r = (mm & (quarter - 1)) - quarter // 2
    phi = r.astype(F32) * (2.0 * math.pi / period)
    p2 = phi * phi
    cosv = 1.0 + p2 * (-1.0 / 2 + p2 * (1.0 / 24 + p2 * (-1.0 / 720 + p2 * (1.0 / 40320 - p2 / 3628800))))
    sinv = phi * (1.0 + p2 * (-1.0 / 6 + p2 * (1.0 / 120 + p2 * (-1.0 / 5040 + p2 / 362880))))
    return jnp.where(q == 0, cosv, jnp.where(q == 1, -sinv, jnp.where(q == 2, -cosv, sinv)))


def _spec_index(r, L):
    blk = r >> 8
    within = r & (2 * SPEC_BLOCK - 1)
    return blk * SPEC_BLOCK + (within & (SPEC_BLOCK - 1)), within >= SPEC_BLOCK


def _dft_body(o_ref, *, L, mode):
    log2n = int(math.log2(2 * L))
    shape = o_ref.shape
    i0 = pl.program_id(0) * shape[0]
    j0 = pl.program_id(1) * shape[1]
    rows = lax.broadcasted_iota(jnp.int32, shape, 0) + i0
    cols = lax.broadcasted_iota(jnp.int32, shape, 1) + j0
    if mode == "inv":
        r, n = cols, rows
    else:
        r, n = rows, cols
    k, im = _spec_index(r, L)
    back = n >= L
    n = jnp.where(back, n - L, n)
    phase = jnp.where(im, jnp.where(back, -(L // 2), L // 2), 0)
    val = _cos_turns((k * n + phase) & (2 * L - 1), log2n)
    nyq = jnp.where((n & 1) == 0, 1.0, -1.0)
    val = jnp.where(im & (k == 0), nyq, val)
    if mode == "inv":
        val = val * jnp.where(k == 0, 1.0 / (2 * L), 2.0 / (2 * L))
    o_ref[...] = val.astype(o_ref.dtype)


def dft_table(L, mode):
    assert L & (L - 1) == 0 and L >= 2 * SPEC_BLOCK
    shape = {"fwd": (2 * L, L), "filt": (2 * L, 2 * L), "inv": (L, 2 * L)}[mode]
    tr, tc = _tile(shape[0], 512), _tile(shape[1], 1024)
    return pl.pallas_call(
        functools.partial(_dft_body, L=L, mode=mode),
        out_shape=jax.ShapeDtypeStruct(shape, BF16),
        grid=(shape[0] // tr, shape[1] // tc),
        out_specs=pl.BlockSpec((tr, tc), lambda i, j: (i, j)),
        compiler_params=_cp("parallel", "parallel"),
        name="dft_table_" + mode,
    )()


def _spectral_epilogue(acc, kf_ref, row_tile):
    tm, tn = acc.shape
    nb = tm // (2 * SPEC_BLOCK)
    z = acc.reshape(nb, 2, SPEC_BLOCK, tn)
    kf = kf_ref[...].reshape(nb, 2, SPEC_BLOCK, tn)
    zr, zi, kr, ki = z[:, 0], z[:, 1], kf[:, 0], kf[:, 1]
    first = row_tile == 0
    blk = lax.broadcasted_iota(jnp.int32, zr.shape, 0)
    row = lax.broadcasted_iota(jnp.int32, zr.shape, 1)
    dc = first & (blk == 0) & (row == 0)
    pr = zr * kr - jnp.where(dc, 0.0, zi * ki)
    pi = jnp.where(dc, zi * ki, zr * ki + zi * kr)
    return jnp.stack([pr, pi], axis=1).reshape(tm, tn)


def hyena_stream(u, row0, nb, L, kf, f_tab, g_tab, hy_bias, c):
    tn = 512
    cb = c // tn
    tm_f = _tile(2 * L, 512)
    tk_f = _tile(L, 2048)
    tm_i = _tile(L, 512)
    tk_i = _tile(2 * L, 2048)
    z, z_rows, z_col = u, row0, 2 * cb
    for o in range(2):
        zrb = z_rows // tk_f
        p = matmul(
            f_tab, z, grid=(nb, 2 * L // tm_f, cb, L // tk_f), tm=tm_f, tn=tn,
            a_spec=pl.BlockSpec((tm_f, tk_f), lambda bt, i, j, k: (i, k)),
            b_spec=pl.BlockSpec((tk_f, tn), lambda bt, i, j, k, zrb=zrb, zc=z_col:
                                (zrb + bt * (L // tk_f) + k, zc + j)),
            o_spec=pl.BlockSpec((None, tm_f, tn), lambda bt, i, j, k: (bt, i, j)),
            out_shape=(nb, 2 * L, c), out_dtype=BF16,
            extra=(kf,), extra_specs=(pl.BlockSpec((tm_f, tn), lambda bt, i, j, k, o=o: (i, o * cb + j)),),
            epilogue=_spectral_epilogue, name="hyena_fwd")
        zrb_i = z_rows // tm_i
        urb_i = row0 // tm_i
        z_new = matmul(
            g_tab, p, grid=(nb, L // tm_i, cb, 2 * L // tk_i), tm=tm_i, tn=tn,
            a_spec=pl.BlockSpec((tm_i, tk_i), lambda bt, i, j, k: (i, k)),
            b_spec=pl.BlockSpec((None, tk_i, tn), lambda bt, i, j, k: (bt, k, j)),
            o_spec=pl.BlockSpec((tm_i, tn), lambda bt, i, j, k: (bt * (L // tm_i) + i, j)),
            out_shape=(nb * L, c), out_dtype=BF16,
            extra=(u, z, hy_bias.reshape(2, 1, c)),
            extra_specs=(pl.BlockSpec((tm_i, tn), lambda bt, i, j, k, o=o:
                                      (urb_i + bt * (L // tm_i) + i, o * cb + j)),
                         pl.BlockSpec((tm_i, tn), lambda bt, i, j, k, zrb_i=zrb_i, zc=z_col:
                                      (zrb_i + bt * (L // tm_i) + i, zc + j)),
                         pl.BlockSpec((None, 1, tn), lambda bt, i, j, k, o=o: (o, 0, j))),
            epilogue=lambda acc, g_ref, z_ref, b_ref, row_tile:
                g_ref[...].astype(F32) * (acc + b_ref[...] * z_ref[...].astype(F32)),
            name="hyena_inv")
        z, z_rows, z_col = z_new, 0, 0
    return z


def hyena_spectrum(L, filt):
    a_tab = dft_table(L, "filt")
    n = filt.shape[2]
    b = filt.reshape(2 * L, n)
    tm, tn, tk = _tile(2 * L, 512), _tile(n, 512), _tile(2 * L, 2048)
    return matmul(
        a_tab, b, grid=(1, 2 * L // tm, n // tn, 2 * L // tk), tm=tm, tn=tn,
        a_spec=pl.BlockSpec((tm, tk), lambda bt, i, j, k: (i, k)),
        b_spec=pl.BlockSpec((tk, tn), lambda bt, i, j, k: (k, j)),
        o_spec=pl.BlockSpec((tm, tn), lambda bt, i, j, k: (i, j)),
        out_shape=(2 * L, n), out_dtype=F32, name="hyena_spectrum")


def _ssd_dir(xs, bt, cm, dtc, dtr, tri, tri_t, state_ref, g, *, reverse, heads_per_group, lane0):
    q = xs.shape[0]
    p = SSD_HEAD_DIM
    da_c, dt_c = dtc
    da_r = dtr
    dn = (((1,), (0,)), ((), ()))
    cum_c = lax.dot_general(tri, da_c, dn, precision=HI, preferred_element_type=F32)
    cum_r = lax.dot_general(da_r, tri_t, dn, precision=HI, preferred_element_type=F32)
    li = lax.broadcasted_iota(jnp.int32, (q, q), 0)
    si = lax.broadcasted_iota(jnp.int32, (q, q), 1)
    mask = (li <= si) if reverse else (li >= si)
    cb = jnp.dot(cm, bt, preferred_element_type=F32)
    ys = []
    for jh in range(heads_per_group):
        lane = lane0 + g * heads_per_group + jh
        cc = _lane_col(cum_c, lane)
        cr = _sublane_row(cum_r, lane)
        dtv = _lane_col(dt_c, lane)
        total = cr[:, 0:1] if reverse else cr[:, q - 1:q]
        xd = xs[:, jh * p:(jh + 1) * p].astype(F32) * dtv
        xd_b = xd.astype(BF16)
        decay = jnp.exp(jnp.where(mask, cc - cr, -jnp.inf))
        y = jnp.dot((cb * decay).astype(BF16), xd_b, preferred_element_type=F32)
        s_prev = state_ref[jh]
        y = y + jnp.exp(cc) * jnp.dot(cm, s_prev.astype(BF16), preferred_element_type=F32)
        w_st = jnp.exp(total - cr)
        st = jnp.dot((bt.astype(F32) * w_st).astype(BF16), xd_b, preferred_element_type=F32)
        state_ref[jh] = jnp.exp(total) * s_prev + st
        ys.append(y)
    return jnp.concatenate(ys, axis=1)


def _lane_col(x, lane):
    sel = lax.broadcasted_iota(jnp.int32, x.shape, 1) == lane
    return jnp.sum(jnp.where(sel, x, 0.0), axis=1, keepdims=True)


def _sublane_row(x, row):
    sel = lax.broadcasted_iota(jnp.int32, x.shape, 0) == row
    return jnp.sum(jnp.where(sel, x, 0.0), axis=0, keepdims=True)


def _ssd_body(xf_ref, btf_ref, cf_ref, dcf_ref, drf_ref, xb_ref, btb_ref, cb_ref, dcb_ref, drb_ref,
              biasc_ref, biasr_ref, ac_ref, ar_ref, initf_ref, initb_ref,
              yf_ref, yb_ref, sf_ref, sb_ref, stf_ref, stb_ref, *, nchunks, heads_per_group, heads):
    g = pl.program_id(1)
    c = pl.program_id(2)
    q = xf_ref.shape[0]

    @pl.when(c == 0)
    def _():
        stf_ref[...] = initf_ref[...]
        stb_ref[...] = initb_ref[...]

    li = lax.broadcasted_iota(jnp.int32, (q, q), 0)
    si = lax.broadcasted_iota(jnp.int32, (q, q), 1)
    tri_f = (li >= si).astype(F32)
    tri_b = (li <= si).astype(F32)

    def prep(dc_ref, dr_ref):
        dt_c = _softplus(dc_ref[...] + biasc_ref[...])
        dt_r = _softplus(dr_ref[...] + biasr_ref[...])
        return (dt_c * ac_ref[...], dt_c), dt_r * ar_ref[...]

    dtc, dtr = prep(dcf_ref, drf_ref)
    yf_ref[...] = _ssd_dir(xf_ref[...], btf_ref[...], cf_ref[...], dtc, dtr, tri_f, tri_b, stf_ref, g,
                           reverse=False, heads_per_group=heads_per_group, lane0=0)
    dtc, dtr = prep(dcb_ref, drb_ref)
    yb_ref[...] = _ssd_dir(xb_ref[...], btb_ref[...], cb_ref[...], dtc, dtr, tri_b, tri_f, stb_ref, g,
                           reverse=True, heads_per_group=heads_per_group, lane0=heads)

    @pl.when(c == nchunks - 1)
    def _():
        sf_ref[...] = stf_ref[...]
        sb_ref[...] = stb_ref[...]


def ssd_scan(u, bt_all, dt_raw, dt_raw_t, dt_bias, a_log, init_f, init_b, *, row0, nseq, L, col0, inner):
    q, p, n, gq = SSD_CHUNK, SSD_HEAD_DIM, SSD_STATE, SSD_GROUPS
    heads = inner // p
    j = heads // gq
    jp = j * p
    nc = L // q
    rb = row0 // q
    xcb = col0 // jp
    ccb = (col0 + inner + gq * n) // n
    neg_a = -jnp.exp(a_log.astype(F32)).reshape(1, 2 * heads)
    bias = dt_bias.astype(F32).reshape(1, 2 * heads)

    def fwd(b, g, c):
        return rb + b * nc + c

    def bwd(b, g, c):
        return rb + b * nc + (nc - 1 - c)

    def specs(chunk):
        return [pl.BlockSpec((q, jp), lambda b, g, c: (chunk(b, g, c), xcb + g)),
                pl.BlockSpec((n, q), lambda b, g, c: (g, chunk(b, g, c))),
                pl.BlockSpec((q, n), lambda b, g, c: (chunk(b, g, c), ccb + g)),
                pl.BlockSpec((q, 2 * heads), lambda b, g, c: (chunk(b, g, c), 0)),
                pl.BlockSpec((2 * heads, q), lambda b, g, c: (0, chunk(b, g, c)))]

    const = lambda shape: pl.BlockSpec(shape, lambda b, g, c: (0,) * len(shape))
    st_spec = pl.BlockSpec((None, j, n, p), lambda b, g, c: (b, g, 0, 0))
    y_spec_f = pl.BlockSpec((q, jp), lambda b, g, c: (b * nc + c, g))
    y_spec_b = pl.BlockSpec((q, jp), lambda b, g, c: (b * nc + (nc - 1 - c), g))
    outs = pl.pallas_call(
        functools.partial(_ssd_body, nchunks=nc, heads_per_group=j, heads=heads),
        out_shape=(jax.ShapeDtypeStruct((nseq * L, inner), F32), jax.ShapeDtypeStruct((nseq * L, inner), F32),
                   jax.ShapeDtypeStruct((nseq, heads, n, p), F32), jax.ShapeDtypeStruct((nseq, heads, n, p), F32)),
        grid=(nseq, gq, nc),
        in_specs=[*specs(fwd), *specs(bwd),
                  const((1, 2 * heads)), const((2 * heads, 1)), const((1, 2 * heads)), const((2 * heads, 1)),
                  st_spec, st_spec],
        out_specs=(y_spec_f, y_spec_b, st_spec, st_spec),
        scratch_shapes=[pltpu.VMEM((j, n, p), F32), pltpu.VMEM((j, n, p), F32)],
        compiler_params=_cp("parallel", "parallel", "arbitrary"),
        name="ssd_scan",
    )(u, bt_all, u, dt_raw, dt_raw_t, u, bt_all, u, dt_raw, dt_raw_t,
      bias, bias.reshape(2 * heads, 1), neg_a, neg_a.reshape(2 * heads, 1), init_f, init_b)
    return outs


def _ssd_finish_body(yf_ref, yb_ref, xs_ref, z_ref, d_ref, w_ref, o_ref):
    y = yf_ref[...] + yb_ref[...] + xs_ref[...].astype(F32) * d_ref[...]
    y = y * _silu(z_ref[...].astype(F32))
    y = y * lax.rsqrt(jnp.mean(y * y, axis=-1, keepdims=True) + RMS_EPS)
    o_ref[...] = (y * w_ref[...]).astype(o_ref.dtype)


def ssd_finish(yf, yb, u, proj, d_full, norm_w, *, xs_col, z_col, inner):
    t = yf.shape[0]
    tr = ROW_TILE
    return pl.pallas_call(
        _ssd_finish_body,
        out_shape=jax.ShapeDtypeStruct((t, inner), BF16),
        grid=(t // tr,),
        in_specs=[pl.BlockSpec((tr, inner), lambda i: (i, 0)),
                  pl.BlockSpec((tr, inner), lambda i: (i, 0)),
                  pl.BlockSpec((tr, inner), lambda i: (i, xs_col // inner)),
                  pl.BlockSpec((tr, inner), lambda i: (i, z_col // inner)),
                  pl.BlockSpec((1, inner), lambda i: (0, 0)),
                  pl.BlockSpec((1, inner), lambda i: (0, 0))],
        out_specs=pl.BlockSpec((tr, inner), lambda i: (i, 0)),
        compiler_params=_cp("parallel"),
        name="ssd_finish",
    )(yf, yb, u, proj, d_full.reshape(1, inner), norm_w.reshape(1, inner))


KV_PER_STEP = LANES // HEAD_DIM


def _softmax_parts(scores, sink):
    m = sink
    for s in scores:
        m = jnp.maximum(m, jnp.max(s, axis=-1, keepdims=True))
    es = [jnp.exp(s - m) for s in scores]
    denom = jnp.exp(sink - m)
    for e in es:
        denom = denom + jnp.sum(e, axis=-1, keepdims=True)
    inv = 1.0 / denom
    return [e * inv for e in es]


def _ctx_attn_body(sink_ref, q_ref, k_ref, v_ref, o_ref, *, q_per_kv, scale):
    hb = pl.program_id(1)
    nt = (((1,), (1,)), ((), ()))
    outs = []
    for kv in range(KV_PER_STEP):
        k = k_ref[:, kv * HEAD_DIM:(kv + 1) * HEAD_DIM].astype(BF16)
        v = v_ref[:, kv * HEAD_DIM:(kv + 1) * HEAD_DIM].astype(BF16)
        for gq in range(q_per_kv):
            hl = kv * q_per_kv + gq
            qh = (q_ref[:, hl * HEAD_DIM:(hl + 1) * HEAD_DIM] * scale).astype(BF16)
            s = lax.dot_general(qh, k, nt, preferred_element_type=F32)
            sink = sink_ref[hb * KV_PER_STEP * q_per_kv + hl]
            (p,) = _softmax_parts([s], sink)
            outs.append(jnp.dot(p.astype(BF16), v, preferred_element_type=F32))
    o_ref[...] = jnp.concatenate(outs, axis=1).astype(o_ref.dtype)


def context_attention(qkv, sink, *, nseq, L, n_heads):
    q_per_kv = n_heads // N_KV_HEADS
    qw = KV_PER_STEP * q_per_kv * HEAD_DIM
    nhb = N_KV_HEADS // KV_PER_STEP
    kcb = n_heads * HEAD_DIM // LANES
    vcb = kcb + N_KV_HEADS * HEAD_DIM // LANES
    return pl.pallas_call(
        functools.partial(_ctx_attn_body, q_per_kv=q_per_kv, scale=HEAD_DIM ** -0.5),
        out_shape=jax.ShapeDtypeStruct((nseq * L, n_heads * HEAD_DIM), BF16),
        grid=(nseq, nhb),
        in_specs=[pl.BlockSpec(memory_space=pltpu.SMEM),
                  pl.BlockSpec((L, qw), lambda b, h: (b, h)),
                  pl.BlockSpec((L, LANES), lambda b, h: (b, kcb + h)),
                  pl.BlockSpec((L, LANES), lambda b, h: (b, vcb + h))],
        out_specs=pl.BlockSpec((L, qw), lambda b, h: (b, h)),
        compiler_params=_cp("parallel", "parallel"),
        name="context_attention",
    )(sink, qkv, qkv, qkv)


def _rope(x, cos, sin):
    outs = []
    lane = lax.broadcasted_iota(jnp.int32, cos.shape, 1)
    first = (lane % (HEAD_DIM // 2)) < (HEAD_DIM // 4)
    for cgrp in range(x.shape[1] // LANES):
        xc = x[:, cgrp * LANES:(cgrp + 1) * LANES]
        partner = jnp.where(first, pltpu.roll(xc, LANES - HEAD_DIM // 4, 1), pltpu.roll(xc, HEAD_DIM // 4, 1))
        outs.append(xc * cos + partner * sin)
    return outs[0] if len(outs) == 1 else jnp.concatenate(outs, axis=1)


def _lat_attn_body(sink_ref, q_ref, k_ref, v_ref, kc_ref, vc_ref, cosq_ref, sinq_ref, cos_ref, sin_ref, o_ref,
                   *, q_per_kv, scale, L, blk):
    hb = pl.program_id(1)
    i = pl.program_id(2)
    n_loc = 3 * blk
    start = jnp.clip((i - 1) * blk, 0, L - n_loc)
    start = pl.multiple_of(start, blk)
    nt = (((1,), (1,)), ((), ()))
    q_all = _rope(q_ref[...], cosq_ref[...], sinq_ref[...]) * scale
    k_loc = _rope(k_ref[pl.ds(start, n_loc), :], cos_ref[pl.ds(start, n_loc), :], sin_ref[pl.ds(start, n_loc), :])
    v_loc = v_ref[pl.ds(start, n_loc), :]
    qpos = i * blk + lax.broadcasted_iota(jnp.int32, (blk, n_loc), 0)
    kpos = start + lax.broadcasted_iota(jnp.int32, (blk, n_loc), 1)
    ok = jnp.abs(qpos - kpos) <= WINDOW
    outs = []
    for kv in range(KV_PER_STEP):
        sl = slice(kv * HEAD_DIM, (kv + 1) * HEAD_DIM)
        k = k_loc[:, sl].astype(BF16)
        v = v_loc[:, sl].astype(BF16)
        kc = kc_ref[:, sl].astype(BF16)
        vc = vc_ref[:, sl].astype(BF16)
        for gq in range(q_per_kv):
            hl = kv * q_per_kv + gq
            qh = q_all[:, hl * HEAD_DIM:(hl + 1) * HEAD_DIM].astype(BF16)
            s_loc = jnp.where(ok, lax.dot_general(qh, k, nt, preferred_element_type=F32), -jnp.inf)
            s_ctx = lax.dot_general(qh, kc, nt, preferred_element_type=F32)
            sink = sink_ref[hb * KV_PER_STEP * q_per_kv + hl]
            p_loc, p_ctx = _softmax_parts([s_loc, s_ctx], sink)
            outs.append(jnp.dot(p_loc.astype(BF16), v, preferred_element_type=F32)
                        + jnp.dot(p_ctx.astype(BF16), vc, preferred_element_type=F32))
    o_ref[...] = jnp.concatenate(outs, axis=1).astype(o_ref.dtype)


def rope_tables(L):
    rows = L // GRID_W
    row = jnp.repeat(jnp.arange(rows, dtype=F32), GRID_W)
    col = jnp.tile(jnp.arange(GRID_W, dtype=F32), rows)
    quarter = HEAD_DIM // 4
    inv = ROPE_BASE ** (-jnp.arange(quarter, dtype=F32) / quarter)
    ang_r = row[:, None] * inv[None, :]
    ang_c = col[:, None] * inv[None, :]
    ang = jnp.concatenate([ang_r, ang_r, ang_c, ang_c], axis=1)
    sign = jnp.tile(jnp.concatenate([-jnp.ones((quarter,), F32), jnp.ones((quarter,), F32)]), 2)
    cos = jnp.tile(jnp.cos(ang), (1, LANES // HEAD_DIM))
    sin = jnp.tile(jnp.sin(ang) * sign[None, :], (1, LANES // HEAD_DIM))
    return cos, sin


def latent_attention(qkv, k_cache, v_cache, sink, *, row0, nseq, L, n_heads):
    blk = WINDOW
    q_per_kv = n_heads // N_KV_HEADS
    qw = KV_PER_STEP * q_per_kv * HEAD_DIM
    nhb = N_KV_HEADS // KV_PER_STEP
    kcb = n_heads * HEAD_DIM // LANES
    vcb = kcb + N_KV_HEADS * HEAD_DIM // LANES
    nb = L // blk
    past = k_cache.shape[1]
    cos, sin = rope_tables(L)
    return pl.pallas_call(
        functools.partial(_lat_attn_body, q_per_kv=q_per_kv, scale=HEAD_DIM ** -0.5, L=L, blk=blk),
        out_shape=jax.ShapeDtypeStruct((nseq * L, n_heads * HEAD_DIM), BF16),
        grid=(nseq, nhb, nb),
        in_specs=[pl.BlockSpec(memory_space=pltpu.SMEM),
                  pl.BlockSpec((blk, qw), lambda b, h, i: (row0 // blk + b * nb + i, h)),
                  pl.BlockSpec((L, LANES), lambda b, h, i: (row0 // L + b, kcb + h)),
                  pl.BlockSpec((L, LANES), lambda b, h, i: (row0 // L + b, vcb + h)),
                  pl.BlockSpec((None, past, LANES), lambda b, h, i: (b, 0, h)),
                  pl.BlockSpec((None, past, LANES), lambda b, h, i: (b, 0, h)),
                  pl.BlockSpec((blk, LANES), lambda b, h, i: (i, 0)),
                  pl.BlockSpec((blk, LANES), lambda b, h, i: (i, 0)),
                  pl.BlockSpec((L, LANES), lambda b, h, i: (0, 0)),
                  pl.BlockSpec((L, LANES), lambda b, h, i: (0, 0))],
        out_specs=pl.BlockSpec((blk, qw), lambda b, h, i: (b * nb + i, h)),
        compiler_params=_cp("parallel", "parallel", "arbitrary"),
        name="latent_attention",
    )(sink, qkv, qkv, qkv, k_cache, v_cache, cos, sin, cos, sin)


MOE_TILE = 256


def _moe_ffn_body(texp_ref, src_ref, nused_ref, h_hbm, gate_ref, w1_ref, w3_ref, w2_ref, o_ref, xbuf, sem):
    i = pl.program_id(0)
    tm = xbuf.shape[0]

    @pl.when(i < nused_ref[0])
    def _():
        def start(r, carry):
            pltpu.make_async_copy(h_hbm.at[pl.ds(src_ref[i * tm + r], 1)], xbuf.at[pl.ds(r, 1)], sem).start()
            return carry

        lax.fori_loop(0, tm, start, 0)

        def wait(r, carry):
            pltpu.make_async_copy(h_hbm.at[pl.ds(0, 1)], xbuf.at[pl.ds(r, 1)], sem).wait()
            return carry

        lax.fori_loop(0, tm, wait, 0)
        x = xbuf[...].astype(BF16)
        a = jnp.dot(x, w1_ref[...], preferred_element_type=F32)
        b = jnp.dot(x, w3_ref[...], preferred_element_type=F32)
        act = (_silu(a) * b).astype(BF16)
        y = jnp.dot(act, w2_ref[...], preferred_element_type=F32)
        o_ref[...] = y * gate_ref[...]

    @pl.when(i >= nused_ref[0])
    def _():
        o_ref[...] = jnp.zeros_like(o_ref)


def moe_ffn(h, tile_expert, src, n_used, gate_sorted, w1, w3, w2):
    t, d = h.shape
    e, _, de = w1.shape
    tm = MOE_TILE
    n_tiles = tile_expert.shape[0]
    grid_spec = pltpu.PrefetchScalarGridSpec(
        num_scalar_prefetch=3,
        grid=(n_tiles,),
        in_specs=[pl.BlockSpec(memory_space=pl.ANY),
                  pl.BlockSpec((tm, 1), lambda i, te, s, nu: (i, 0)),
                  pl.BlockSpec((None, d, de), lambda i, te, s, nu: (te[i], 0, 0)),
                  pl.BlockSpec((None, d, de), lambda i, te, s, nu: (te[i], 0, 0)),
                  pl.BlockSpec((None, de, d), lambda i, te, s, nu: (te[i], 0, 0))],
        out_specs=pl.BlockSpec((tm, d), lambda i, te, s, nu: (i, 0)),
        scratch_shapes=[pltpu.VMEM((tm, d), F32), pltpu.SemaphoreType.DMA(())])
    return pl.pallas_call(
        _moe_ffn_body,
        out_shape=jax.ShapeDtypeStruct((n_tiles * tm, d), F32),
        grid_spec=grid_spec,
        compiler_params=_cp("arbitrary"),
        name="moe_ffn",
    )(tile_expert, src, n_used, h, gate_sorted, w1, w3, w2)


def _moe_combine_body(pos_ref, y_hbm, x_ref, g_ref, o_ref, ybuf, sem):
    i = pl.program_id(0)
    tr = x_ref.shape[0]

    def start(r, carry):
        for s in range(TOP_K):
            pltpu.make_async_copy(y_hbm.at[pl.ds(pos_ref[s, i * tr + r], 1)], ybuf.at[s, pl.ds(r, 1)], sem).start()
        return carry

    lax.fori_loop(0, tr, start, 0)

    def wait(r, carry):
        for s in range(TOP_K):
            pltpu.make_async_copy(y_hbm.at[pl.ds(0, 1)], ybuf.at[s, pl.ds(r, 1)], sem).wait()
        return carry

    lax.fori_loop(0, tr, wait, 0)
    y = ybuf[0]
    for s in range(1, TOP_K):
        y = y + ybuf[s]
    o_ref[...] = x_ref[...] + g_ref[...] * y


def moe_combine(x, y_sorted, pos, mod3, gate_idx, seg_of_tile):
    t, d = x.shape
    tr = ROW_TILE
    grid_spec = pltpu.PrefetchScalarGridSpec(
        num_scalar_prefetch=1,
        grid=(t // tr,),
        in_specs=[pl.BlockSpec(memory_space=pl.ANY),
                  pl.BlockSpec((tr, d), lambda i, p: (i, 0)),
                  pl.BlockSpec((None, 1, d), lambda i, p: (seg_of_tile(i) * 6 + gate_idx, 0, 0))],
        out_specs=pl.BlockSpec((tr, d), lambda i, p: (i, 0)),
        scratch_shapes=[pltpu.VMEM((TOP_K, tr, d), F32), pltpu.SemaphoreType.DMA(())])
    return pl.pallas_call(
        _moe_combine_body,
        out_shape=jax.ShapeDtypeStruct((t, d), F32),
        grid_spec=grid_spec,
        compiler_params=_cp("arbitrary"),
        name="moe_combine",
    )(pos, y_sorted, x, mod3)


def moe_route(logits, n_groups, n_experts):
    per = n_experts // n_groups
    g_prob = jax.nn.softmax(logits[:, :n_groups], axis=-1)
    g_top, g_idx = lax.top_k(g_prob, 1)
    e_logits = logits[:, n_groups:n_groups + n_experts].reshape(-1, n_groups, per)
    e_in = jnp.take_along_axis(e_logits, g_idx[:, :, None], axis=1)[:, 0]
    e_top, e_idx = lax.top_k(jax.nn.softmax(e_in, axis=-1), TOP_K)
    e_top = e_top / jnp.sum(e_top, axis=-1, keepdims=True)
    return g_idx * per + e_idx, g_top * e_top


def moe_plan(expert_ids, gates, n_experts):
    t = expert_ids.shape[0]
    tm = MOE_TILE
    n_tiles = (t * TOP_K) // tm + n_experts
    flat_e = expert_ids.reshape(-1)
    onehot = (flat_e[:, None] == jnp.arange(n_experts)[None, :]).astype(jnp.int32)
    rank = jnp.take_along_axis(jnp.cumsum(onehot, axis=0), flat_e[:, None], axis=1)[:, 0] - 1
    counts = jnp.sum(onehot, axis=0)
    tiles_per = (counts + tm - 1) // tm
    tile_start = jnp.cumsum(tiles_per) - tiles_per
    pos = tile_start[flat_e] * tm + rank
    n_slots = n_tiles * tm
    tok = jnp.arange(t * TOP_K, dtype=jnp.int32) // TOP_K
    src = jnp.zeros((n_slots,), jnp.int32).at[pos].set(tok)
    gate_sorted = jnp.zeros((n_slots,), F32).at[pos].set(gates.reshape(-1))
    n_used = jnp.sum(tiles_per).astype(jnp.int32)
    tile_ids = jnp.arange(n_tiles, dtype=jnp.int32)
    tile_expert = jnp.sum((tile_ids[:, None] >= tile_start[None, :]).astype(jnp.int32), axis=1) - 1
    tile_expert = jnp.where(tile_ids < n_used, tile_expert, tile_expert[jnp.maximum(n_used - 1, 0)])
    pos2 = pos.reshape(t, TOP_K).T.astype(jnp.int32)
    return tile_expert.astype(jnp.int32), src, n_used.reshape(1), gate_sorted.reshape(n_slots, 1), pos2


def kernel(x_prompt, x_sample, state_ssd_fwd, state_ssd_bwd, cache_attn_k, cache_attn_v, c, c_ctx, ada_w, ada_b, norm_mix, norm_ffn, norm_final, ev_w_in, ev_w_out, hy_conv_w, hy_conv_b, hy_f_w1, hy_f_b1, hy_f_freq1, hy_f_w2, hy_f_b2, hy_f_freq2, hy_f_w3, hy_bias, ssd_conv_w, ssd_conv_b, ssd_dt_bias, ssd_a_log, ssd_d, ssd_norm, at_wq, at_wk, at_wv, at_wo, at_sink, moe_w_group, moe_b_group, moe_w_expert, moe_b_expert, moe_w1, moe_w3, moe_w2):
    bc, lc, d = x_prompt.shape
    bl, ll, _ = x_sample.shape
    tc_rows, tl_rows = bc * lc, bl * ll
    t = tc_rows + tl_rows
    depth = ada_w.shape[0]
    n_seg = 1 + bl
    assert lc % ROW_TILE == 0 and ll % ROW_TILE == 0 and tc_rows % ll == 0
    seg_rows = math.gcd(tc_rows, ll)

    def seg_of_rows(r):
        return jnp.where(r < tc_rows, 0, 1 + (r - tc_rows) // ll)

    def seg_of_tile(i):
        return seg_of_rows(i * ROW_TILE)

    x = jnp.concatenate([x_prompt.reshape(tc_rows, d), x_sample.reshape(tl_rows, d)], axis=0)
    cvec = jnp.concatenate([c_ctx[None], c, jnp.zeros((SUBLANES - n_seg, d), F32)], axis=0)
    mod_all = ada_mod(cvec, ada_w, ada_b)

    tile_row = jnp.arange(t // ROW_TILE, dtype=jnp.int32) * ROW_TILE
    seq_len = jnp.where(tile_row < tc_rows, lc, ll)
    rel = jnp.where(tile_row < tc_rows, tile_row, tile_row - tc_rows)
    conv_flags = jnp.stack([(rel % seq_len == 0), ((rel + ROW_TILE) % seq_len == 0)]).astype(jnp.int32)

    new_sf, new_sb, new_k, new_v = [], [], [], []
    for l in range(depth):
        i = l // 2
        mod3 = mod_all[l, :n_seg].reshape(n_seg * 6, 1, d)
        h = rms_norm(x, norm_mix[l], out_dtype=BF16, mod3=mod3, shift_idx=0, scale_idx=1,
                     seg_of_tile=seg_of_tile)
        if l % 2 == 0:
            hyena_w = hy_conv_w.shape[2] // 3
            inner = ssd_norm.shape[1]
            heads = inner // SSD_HEAD_DIM
            conv_ch = ssd_conv_w.shape[2]
            main_cols = 3 * hyena_w + inner + conv_ch
            w_in = ev_w_in[i]
            proj = dense(h, w_in[:, :main_cols].astype(BF16), out_dtype=BF16, name="in_proj")
            dt_raw = dense(h, w_in[:, main_cols:].astype(BF16), out_dtype=F32, name="dt_proj")
            conv_w = jnp.concatenate([hy_conv_w[i], ssd_conv_w[i]], axis=1)
            conv_b = jnp.concatenate([hy_conv_b[i], ssd_conv_b[i]], axis=0)
            u = dwconv(proj, conv_w, conv_b, conv_flags, n_plain_cols=3 * hyena_w, gap_cols=inner,
                       out_cols=3 * hyena_w + conv_ch)
            y_parts = []
            for row0, nb, L in ((0, bc, lc), (tc_rows, bl, ll)):
                filt = hyena_filters(L, hy_f_w1[i], hy_f_b1[i], hy_f_freq1[i], hy_f_w2[i], hy_f_b2[i],
                                     hy_f_freq2[i], hy_f_w3[i], hyena_w)
                kf = hyena_spectrum(L, filt)
                y_parts.append(hyena_stream(u, row0, nb, L, kf, dft_table(L, "fwd"), dft_table(L, "inv"),
                                            hy_bias[i], hyena_w))
            y_a = jnp.concatenate(y_parts, axis=0)
            col0 = 3 * hyena_w
            gn = SSD_GROUPS * SSD_STATE
            bt_all = u[:, col0 + inner:col0 + inner + gn].T
            dt_raw_t = dt_raw.T
            zero = jnp.zeros((bc, heads, SSD_STATE, SSD_HEAD_DIM), F32)
            init_f = jnp.swapaxes(state_ssd_fwd[:, i], -1, -2)
            init_b = jnp.swapaxes(state_ssd_bwd[:, i], -1, -2)
            yf_c, yb_c, sf, sb = ssd_scan(u, bt_all, dt_raw, dt_raw_t, ssd_dt_bias[i], ssd_a_log[i], zero, zero,
                                          row0=0, nseq=bc, L=lc, col0=col0, inner=inner)
            yf_l, yb_l, _, _ = ssd_scan(u, bt_all, dt_raw, dt_raw_t, ssd_dt_bias[i], ssd_a_log[i], init_f, init_b,
                                        row0=tc_rows, nseq=bl, L=ll, col0=col0, inner=inner)
            new_sf.append(jnp.swapaxes(sf, -1, -2))
            new_sb.append(jnp.swapaxes(sb, -1, -2))
            y_b = ssd_finish(jnp.concatenate([yf_c, yf_l], axis=0), jnp.concatenate([yb_c, yb_l], axis=0),
                             u, proj, jnp.repeat(ssd_d[i], SSD_HEAD_DIM), ssd_norm[i],
                             xs_col=col0, z_col=3 * hyena_w, inner=inner)
            w_out = ev_w_out[i].astype(BF16)
            x = dense_residual(y_a, w_out[:hyena_w], x, mod3, 2, seg_of_rows, seg_rows, name="out_proj_a")
            x = dense_residual(y_b, w_out[hyena_w:], x, mod3, 2, seg_of_rows, seg_rows, name="out_proj_b")
        else:
            n_heads = at_wq.shape[2] // HEAD_DIM
            w_qkv = jnp.concatenate([at_wq[i], at_wk[i], at_wv[i]], axis=1).astype(BF16)
            qkv = dense(h, w_qkv, out_dtype=F32, name="qkv_proj")
            kvw = N_KV_HEADS * HEAD_DIM
            att_c = context_attention(qkv, at_sink[i], nseq=bc, L=lc, n_heads=n_heads)
            past = cache_attn_k.shape[2]
            att_l = latent_attention(qkv, cache_attn_k[:, i].reshape(bl, past, kvw),
                                     cache_attn_v[:, i].reshape(bl, past, kvw), at_sink[i],
                                     row0=tc_rows, nseq=bl, L=ll, n_heads=n_heads)
            att = jnp.concatenate([att_c, att_l], axis=0)
            x = dense_residual(att, at_wo[i].astype(BF16), x, mod3, 2, seg_of_rows, seg_rows, name="attn_out")
            qw = n_heads * HEAD_DIM
            new_k.append(qkv[:tc_rows, qw:qw + kvw].reshape(bc, lc, N_KV_HEADS, HEAD_DIM))
            new_v.append(qkv[:tc_rows, qw + kvw:].reshape(bc, lc, N_KV_HEADS, HEAD_DIM))
        h2 = rms_norm(x, norm_ffn[l], out_dtype=F32, mod3=mod3, shift_idx=3, scale_idx=4,
                      seg_of_tile=seg_of_tile)
        n_experts = moe_w_expert.shape[2]
        w_r = jnp.concatenate([moe_w_group[l], moe_w_expert[l]], axis=1)
        w_r = jnp.pad(w_r, ((0, 0), (0, LANES - w_r.shape[1]))).astype(BF16)
        b_r = jnp.pad(jnp.concatenate([moe_b_group[l], moe_b_expert[l]]), (0, LANES - MOE_GROUPS - n_experts))
        logits = dense(h2, w_r, out_dtype=F32, extra=(b_r.reshape(1, LANES),),
                       extra_specs=(pl.BlockSpec((1, LANES), lambda bt, ii, j, kk: (0, 0)),),
                       epilogue=lambda acc, b_ref, row_tile: acc + b_ref[...], name="router")
        expert_ids, gates = moe_route(logits, MOE_GROUPS, n_experts)
        tile_expert, src, n_used, gate_sorted, pos2 = moe_plan(expert_ids, gates, n_experts)
        y_sorted = moe_ffn(h2, tile_expert, src, n_used, gate_sorted, moe_w1[l].astype(BF16),
                           moe_w3[l].astype(BF16), moe_w2[l].astype(BF16))
        x = moe_combine(x, y_sorted, pos2, mod3, 5, seg_of_tile)

    y_prompt = rms_norm(x, norm_final, out_dtype=F32, rows=tc_rows).reshape(bc, lc, d)
    y_sample = rms_norm(x, norm_final, out_dtype=F32, row_off=tc_rows, rows=tl_rows).reshape(bl, ll, d)
    return (y_prompt, y_sample, jnp.stack(new_sf, axis=1), jnp.stack(new_sb, axis=1),
            jnp.stack(new_k, axis=1), jnp.stack(new_v, axis=1))
```

```python
import functools
import math

import jax
import jax.numpy as jnp
from jax import lax
from jax.experimental import pallas as pl
from jax.experimental.pallas import tpu as pltpu

F32 = jnp.float32
BF16 = jnp.bfloat16
HI = lax.Precision.HIGHEST

RMS_EPS = 1e-6
GRID_W = 64
HYENA_BANDS = 16
HYENA_FAST_DECAY = 0.3
HYENA_SLOW_DECAY = 1.5
HYENA_TARGET = 1e-2
SSD_HEAD_DIM = 64
SSD_STATE = 128
SSD_GROUPS = 8
SSD_CHUNK = 128
HEAD_DIM = 64
N_KV_HEADS = 8
WINDOW = 128
ROPE_BASE = 10000.0
MOE_GROUPS = 4
TOP_K = 2

LANES = 128
SUBLANES = 8
VMEM_LIMIT_BYTES = 48 * 1024 * 1024

ROW_TILE = 256
SPEC_BLOCK = 128


def _cp(*sem, vmem=VMEM_LIMIT_BYTES):
    return pltpu.CompilerParams(dimension_semantics=sem, vmem_limit_bytes=vmem)


def _tile(n, pref):
    if n <= pref:
        return n
    t = pref
    while n % t:
        t //= 2
    assert t >= LANES, (n, pref)
    return t


def _silu(x):
    return x * (1.0 / (1.0 + jnp.exp(-x)))


def _softplus(x):
    return jnp.maximum(x, 0.0) + jnp.log(1.0 + jnp.exp(-jnp.abs(x)))


def _ada_body(c_ref, w_ref, b_ref, o_ref):
    a = _silu(c_ref[...])
    o_ref[...] = lax.dot_general(a, w_ref[...], (((1,), (0,)), ((), ())), precision=HI,
                                 preferred_element_type=F32) + b_ref[...]


def ada_mod(cvec, ada_w, ada_b):
    depth, d, n = ada_w.shape
    rows = cvec.shape[0]
    tn = _tile(n, 1024)
    return pl.pallas_call(
        _ada_body,
        out_shape=jax.ShapeDtypeStruct((depth, rows, n), F32),
        grid=(depth, n // tn),
        in_specs=[pl.BlockSpec((rows, d), lambda l, j: (0, 0)),
                  pl.BlockSpec((None, d, tn), lambda l, j: (l, 0, j)),
                  pl.BlockSpec((None, 1, tn), lambda l, j: (l, 0, j))],
        out_specs=pl.BlockSpec((None, rows, tn), lambda l, j: (l, 0, j)),
        compiler_params=_cp("parallel", "parallel"),
        name="ada_mod",
    )(cvec, ada_w, ada_b.reshape(depth, 1, n))


def _norm_body(x_ref, g_ref, *rest, modulate):
    o_ref = rest[-1]
    x = x_ref[...]
    y = x * lax.rsqrt(jnp.mean(x * x, axis=-1, keepdims=True) + RMS_EPS)
    y = y * g_ref[...]
    if modulate:
        sh_ref, sc_ref = rest[0], rest[1]
        y = y * (1.0 + sc_ref[...]) + sh_ref[...]
    o_ref[...] = y.astype(o_ref.dtype)


def rms_norm(x, g, *, out_dtype, mod3=None, shift_idx=0, scale_idx=0, seg_of_tile=None,
             row_off=0, rows=None):
    t, d = x.shape
    rows = t if rows is None else rows
    tr = ROW_TILE
    off = row_off // tr
    in_specs = [pl.BlockSpec((tr, d), lambda i: (i + off, 0)),
                pl.BlockSpec((1, d), lambda i: (0, 0))]
    args = [x, g.reshape(1, d)]
    if mod3 is not None:
        in_specs += [pl.BlockSpec((None, 1, d), lambda i: (seg_of_tile(i) * 6 + shift_idx, 0, 0)),
                     pl.BlockSpec((None, 1, d), lambda i: (seg_of_tile(i) * 6 + scale_idx, 0, 0))]
        args += [mod3, mod3]
    return pl.pallas_call(
        functools.partial(_norm_body, modulate=mod3 is not None),
        out_shape=jax.ShapeDtypeStruct((rows, d), out_dtype),
        grid=(rows // tr,),
        in_specs=in_specs,
        out_specs=pl.BlockSpec((tr, d), lambda i: (i, 0)),
        compiler_params=_cp("parallel"),
        name="rms_norm",
    )(*args)


def _mm_body(*refs, nk, n_extra, n_skip, epilogue):
    a_ref, b_ref = refs[0], refs[1]
    extra = refs[2:2 + n_extra]
    o_ref = refs[2 + n_extra + n_skip]
    row_tile = pl.program_id(1)
    part = jnp.dot(a_ref[...].astype(b_ref.dtype), b_ref[...], preferred_element_type=F32)
    if nk == 1:
        o_ref[...] = epilogue(part, *extra, row_tile=row_tile).astype(o_ref.dtype)
        return
    acc_ref = refs[3 + n_extra + n_skip]
    k = pl.program_id(3)

    @pl.when(k == 0)
    def _():
        acc_ref[...] = part

    @pl.when(k > 0)
    def _():
        acc_ref[...] += part

    @pl.when(k == nk - 1)
    def _():
        o_ref[...] = epilogue(acc_ref[...], *extra, row_tile=row_tile).astype(o_ref.dtype)


def matmul(a, b, *, grid, tm, tn, a_spec, b_spec, o_spec, out_shape, out_dtype,
           extra=(), extra_specs=(), epilogue=None, fill=None, name="matmul"):
    nk = grid[3]
    epilogue = epilogue or (lambda acc, row_tile: acc)
    scratch = [pltpu.VMEM((tm, tn), F32)] if nk > 1 else []
    fills = () if fill is None else (fill,)
    return pl.pallas_call(
        functools.partial(_mm_body, nk=nk, n_extra=len(extra), n_skip=len(fills), epilogue=epilogue),
        out_shape=jax.ShapeDtypeStruct(out_shape, out_dtype),
        grid=grid,
        in_specs=[a_spec, b_spec, *extra_specs, *[pl.BlockSpec(memory_space=pl.ANY) for _ in fills]],
        out_specs=o_spec,
        scratch_shapes=scratch,
        input_output_aliases={2 + len(extra): 0} if fills else {},
        compiler_params=_cp("parallel", "parallel", "parallel", "arbitrary"),
        name=name,
    )(a, b, *extra, *fills)


def dense(a, w, *, out_dtype, tm=1024, tn=512, tk=2048, extra=(), extra_specs=(), epilogue=None,
          name="dense"):
    m, k = a.shape
    n = w.shape[1]
    tm, tn, tk = _tile(m, tm), _tile(n, tn), _tile(k, tk)
    return matmul(
        a, w, grid=(1, m // tm, n // tn, k // tk), tm=tm, tn=tn,
        a_spec=pl.BlockSpec((tm, tk), lambda bt, i, j, kk: (i, kk)),
        b_spec=pl.BlockSpec((tk, tn), lambda bt, i, j, kk: (kk, j)),
        o_spec=pl.BlockSpec((tm, tn), lambda bt, i, j, kk: (i, j)),
        out_shape=(m, n), out_dtype=out_dtype, extra=extra, extra_specs=extra_specs,
        epilogue=epilogue, name=name)


def dense_residual(a, w, x, mod3, gate_idx, seg_of_rows, seg_rows, *, tm=1024, tn=512, tk=2048, name="dense_res"):
    n = w.shape[1]
    tm_ = _tile(seg_rows, tm)
    tn_ = _tile(n, tn)
    return dense(
        a, w, out_dtype=F32, tm=tm_, tn=tn, tk=tk, extra=(x, mod3),
        extra_specs=(pl.BlockSpec((tm_, tn_), lambda bt, i, j, kk: (i, j)),
                     pl.BlockSpec((None, 1, tn_),
                                  lambda bt, i, j, kk: (seg_of_rows(i * tm_) * 6 + gate_idx, 0, j))),
        epilogue=lambda acc, x_ref, g_ref, row_tile: x_ref[...] + g_ref[...] * acc, name=name)


def _dwconv_body(flags_ref, x_ref, p_ref, n_ref, w_ref, b_ref, o_ref, *, n_plain):
    i = pl.program_id(0)
    j = pl.program_id(1)
    x = x_ref[...].astype(F32)
    tr = x.shape[0]
    row = lax.broadcasted_iota(jnp.int32, x.shape, 0)
    keep_prev = (flags_ref[0, i] == 0).astype(F32)
    keep_next = (flags_ref[1, i] == 0).astype(F32)
    halo_prev = p_ref[SUBLANES - 1:SUBLANES, :].astype(F32) * keep_prev
    halo_next = n_ref[0:1, :].astype(F32) * keep_next
    prev = jnp.where(row == 0, halo_prev, pltpu.roll(x, 1, 0))
    nxt = jnp.where(row == tr - 1, halo_next, pltpu.roll(x, tr - 1, 0))
    w = w_ref[...]
    y = prev * w[0:1, :] + x * w[1:2, :] + nxt * w[2:3, :] + b_ref[...]
    y = jnp.where(j >= n_plain, _silu(y), y)
    o_ref[...] = y.astype(o_ref.dtype)


def dwconv(proj, w, b, flags, *, n_plain_cols, gap_cols, out_cols):
    t = proj.shape[0]
    tr = ROW_TILE
    tc = _tile(math.gcd(n_plain_cols, gap_cols, out_cols), 2048)
    n_plain, gap = n_plain_cols // tc, gap_cols // tc
    hb = tr // SUBLANES
    last_hb = t // SUBLANES - 1

    def src(j):
        return j + jnp.where(j >= n_plain, gap, 0)

    grid_spec = pltpu.PrefetchScalarGridSpec(
        num_scalar_prefetch=1,
        grid=(t // tr, out_cols // tc),
        in_specs=[pl.BlockSpec((tr, tc), lambda i, j, f: (i, src(j))),
                  pl.BlockSpec((SUBLANES, tc), lambda i, j, f: (jnp.maximum(i * hb - 1, 0), src(j))),
                  pl.BlockSpec((SUBLANES, tc), lambda i, j, f: (jnp.minimum((i + 1) * hb, last_hb), src(j))),
                  pl.BlockSpec((3, tc), lambda i, j, f: (0, j)),
                  pl.BlockSpec((1, tc), lambda i, j, f: (0, j))],
        out_specs=pl.BlockSpec((tr, tc), lambda i, j, f: (i, j)))
    return pl.pallas_call(
        functools.partial(_dwconv_body, n_plain=n_plain),
        out_shape=jax.ShapeDtypeStruct((t, out_cols), BF16),
        grid_spec=grid_spec,
        compiler_params=_cp("parallel", "parallel"),
        name="dwconv",
    )(flags, proj, proj, proj, w, b.reshape(1, -1))


def _filter_body(feat_ref, w1_ref, b1_ref, f1_ref, w2_ref, b2_ref, f2_ref, w3_ref, dl_ref, o_ref):
    d = pl.program_id(0)
    i = pl.program_id(1)
    feat = feat_ref[...]
    dn = (((1,), (0,)), ((), ()))
    hid = jnp.sin(f1_ref[...] * (lax.dot_general(feat, w1_ref[...], dn, precision=HI,
                                                 preferred_element_type=F32) + b1_ref[...]))
    hid = jnp.sin(f2_ref[...] * (lax.dot_general(hid, w2_ref[...], dn, precision=HI,
                                                 preferred_element_type=F32) + b2_ref[...]))
    filt = lax.dot_general(hid, w3_ref[...], dn, precision=HI, preferred_element_type=F32)
    t01 = feat[:, 0:1]
    filt = filt * jnp.exp(-t01 * dl_ref[...])
    row = lax.broadcasted_iota(jnp.int32, filt.shape, 0) + i * filt.shape[0]
    filt = jnp.where((d == 1) & (row == 0), 0.0, filt)
    o_ref[...] = filt.astype(o_ref.dtype)


def hyena_filters(L, f_w1, f_b1, f_freq1, f_w2, f_b2, f_freq2, f_w3, c):
    t = jnp.arange(L, dtype=F32)
    t01 = t / (L - 1)
    bands = jnp.linspace(1e-4, HYENA_BANDS - 1, HYENA_BANDS, dtype=F32)
    ang = (2.0 * math.pi / L) * t[:, None] * bands[None, :]
    feat = jnp.concatenate([t01[:, None], jnp.cos(ang), -jnp.sin(ang)], axis=-1)
    emb = feat.shape[1]
    feat = jnp.pad(feat, ((0, 0), (0, LANES - emb)))
    w1 = jnp.pad(f_w1, ((0, LANES - emb), (0, 0)))
    ffn = f_w1.shape[1]
    order = f_w3.shape[1] // (2 * c)
    deltas = jnp.abs(jnp.linspace(math.log(HYENA_TARGET) / HYENA_SLOW_DECAY,
                                  math.log(HYENA_TARGET) / HYENA_FAST_DECAY, c, dtype=F32))
    deltas = jnp.tile(deltas, order).reshape(1, order * c)
    tl, tn = _tile(L, 512), _tile(order * c, 1024)
    nj = order * c // tn
    return pl.pallas_call(
        _filter_body,
        out_shape=jax.ShapeDtypeStruct((2, L, order * c), BF16),
        grid=(2, L // tl, nj),
        in_specs=[pl.BlockSpec((tl, LANES), lambda d, i, j: (i, 0)),
                  pl.BlockSpec((LANES, ffn), lambda d, i, j: (0, 0)),
                  pl.BlockSpec((1, ffn), lambda d, i, j: (0, 0)),
                  pl.BlockSpec((1, ffn), lambda d, i, j: (0, 0)),
                  pl.BlockSpec((ffn, ffn), lambda d, i, j: (0, 0)),
                  pl.BlockSpec((1, ffn), lambda d, i, j: (0, 0)),
                  pl.BlockSpec((1, ffn), lambda d, i, j: (0, 0)),
                  pl.BlockSpec((ffn, tn), lambda d, i, j: (0, d * nj + j)),
                  pl.BlockSpec((1, tn), lambda d, i, j: (0, j))],
        out_specs=pl.BlockSpec((None, tl, tn), lambda d, i, j: (d, i, j)),
        compiler_params=_cp("parallel", "parallel", "parallel"),
        name="hyena_filters",
    )(feat, w1, f_b1.reshape(1, ffn), f_freq1.reshape(1, ffn), f_w2, f_b2.reshape(1, ffn),
      f_freq2.reshape(1, ffn), f_w3, deltas)


def _cos_turns(m, log2_period):
    period = 1 << log2_period
    quarter = period // 4
    mm = m + quarter // 2
    q = (mm >> (log2_period - 2)) & 3
    r = (mm & (quarter - 1)) - quarter // 2
    phi = r.astype(F32) * (2.0 * math.pi / period)
    p2 = phi * phi
    cosv = 1.0 + p2 * (-1.0 / 2 + p2 * (1.0 / 24 + p2 * (-1.0 / 720 + p2 * (1.0 / 40320 - p2 / 3628800))))
    sinv = phi * (1.0 + p2 * (-1.0 / 6 + p2 * (1.0 / 120 + p2 * (-1.0 / 5040 + p2 / 362880))))
    return jnp.where(q == 0, cosv, jnp.where(q == 1, -sinv, jnp.where(q == 2, -cosv, sinv)))


def _spec_index(r, L):
    blk = r >> 8
    within = r & (2 * SPEC_BLOCK - 1)
    return blk * SPEC_BLOCK + (within & (SPEC_BLOCK - 1)), within >= SPEC_BLOCK


def _dft_body(o_ref, *, L, mode):
    log2n = int(math.log2(2 * L))
    shape = o_ref.shape
    i0 = pl.program_id(0) * shape[0]
    j0 = pl.program_id(1) * shape[1]
    rows = lax.broadcasted_iota(jnp.int32, shape, 0) + i0
    cols = lax.broadcasted_iota(jnp.int32, shape, 1) + j0
    if mode == "inv":
        r, n = cols, rows
    else:
        r, n = rows, cols
    k, im = _spec_index(r, L)
    back = n >= L
    n = jnp.where(back, n - L, n)
    phase = jnp.where(im, jnp.where(back, -(L // 2), L // 2), 0)
    val = _cos_turns((k * n + phase) & (2 * L - 1), log2n)
    nyq = jnp.where((n & 1) == 0, 1.0, -1.0)
    val = jnp.where(im & (k == 0), nyq, val)
    if mode == "inv":
        val = val * jnp.where(k == 0, 1.0 / (2 * L), 2.0 / (2 * L))
    o_ref[...] = val.astype(o_ref.dtype)


def dft_table(L, mode):
    assert L & (L - 1) == 0 and L >= 2 * SPEC_BLOCK
    shape = {"fwd": (2 * L, L), "filt": (2 * L, 2 * L), "inv": (L, 2 * L)}[mode]
    tr, tc = _tile(shape[0], 512), _tile(shape[1], 1024)
    return pl.pallas_call(
        functools.partial(_dft_body, L=L, mode=mode),
        out_shape=jax.ShapeDtypeStruct(shape, BF16),
        grid=(shape[0] // tr, shape[1] // tc),
        out_specs=pl.BlockSpec((tr, tc), lambda i, j: (i, j)),
        compiler_params=_cp("parallel", "parallel"),
        name="dft_table_" + mode,
    )()


def _spectral_epilogue(acc, kf_ref, row_tile):
    tm, tn = acc.shape
    nb = tm // (2 * SPEC_BLOCK)
    z = acc.reshape(nb, 2, SPEC_BLOCK, tn)
    kf = kf_ref[...].reshape(nb, 2, SPEC_BLOCK, tn)
    zr, zi, kr, ki = z[:, 0], z[:, 1], kf[:, 0], kf[:, 1]
    first = row_tile == 0
    blk = lax.broadcasted_iota(jnp.int32, zr.shape, 0)
    row = lax.broadcasted_iota(jnp.int32, zr.shape, 1)
    dc = first & (blk == 0) & (row == 0)
    pr = zr * kr - jnp.where(dc, 0.0, zi * ki)
    pi = jnp.where(dc, zi * ki, zr * ki + zi * kr)
    return jnp.stack([pr, pi], axis=1).reshape(tm, tn)


def hyena_stream(u, row0, nb, L, kf, f_tab, g_tab, hy_bias, c, fill=None):
    out_rows = u.shape[0]
    tn = 512
    cb = c // tn
    tm_f = _tile(2 * L, 512)
    tk_f = _tile(L, 2048)
    tm_i = _tile(L, 512)
    tk_i = _tile(2 * L, 2048)
    z, z_rows, z_col = u, row0, 2 * cb
    for o in range(2):
        zrb = z_rows // tk_f
        p = matmul(
            f_tab, z, grid=(nb, 2 * L // tm_f, cb, L // tk_f), tm=tm_f, tn=tn,
            a_spec=pl.BlockSpec((tm_f, tk_f), lambda bt, i, j, k: (i, k)),
            b_spec=pl.BlockSpec((tk_f, tn), lambda bt, i, j, k, zrb=zrb, zc=z_col:
                                (zrb + bt * (L // tk_f) + k, zc + j)),
            o_spec=pl.BlockSpec((None, tm_f, tn), lambda bt, i, j, k: (bt, i, j)),
            out_shape=(nb, 2 * L, c), out_dtype=BF16,
            extra=(kf,), extra_specs=(pl.BlockSpec((tm_f, tn), lambda bt, i, j, k, o=o: (i, o * cb + j)),),
            epilogue=_spectral_epilogue, name="hyena_fwd")
        zrb_i = z_rows // tm_i
        urb_i = row0 // tm_i
        last = o == 1
        orb = urb_i if last else 0
        z_new = matmul(
            g_tab, p, grid=(nb, L // tm_i, cb, 2 * L // tk_i), tm=tm_i, tn=tn,
            a_spec=pl.BlockSpec((tm_i, tk_i), lambda bt, i, j, k: (i, k)),
            b_spec=pl.BlockSpec((None, tk_i, tn), lambda bt, i, j, k: (bt, k, j)),
            o_spec=pl.BlockSpec((tm_i, tn), lambda bt, i, j, k, orb=orb: (orb + bt * (L // tm_i) + i, j)),
            out_shape=(out_rows if last else nb * L, c), out_dtype=BF16, fill=fill if last else None,
            extra=(u, z, hy_bias.reshape(2, 1, c)),
            extra_specs=(pl.BlockSpec((tm_i, tn), lambda bt, i, j, k, o=o:
                                      (urb_i + bt * (L // tm_i) + i, o * cb + j)),
                         pl.BlockSpec((tm_i, tn), lambda bt, i, j, k, zrb_i=zrb_i, zc=z_col:
                                      (zrb_i + bt * (L // tm_i) + i, zc + j)),
                         pl.BlockSpec((None, 1, tn), lambda bt, i, j, k, o=o: (o, 0, j))),
            epilogue=lambda acc, g_ref, z_ref, b_ref, row_tile:
                g_ref[...].astype(F32) * (acc + b_ref[...] * z_ref[...].astype(F32)),
            name="hyena_inv")
        z, z_rows, z_col = z_new, 0, 0
    return z


def hyena_spectrum(L, filt, a_tab):
    n = filt.shape[2]
    b = filt.reshape(2 * L, n)
    tm, tn, tk = _tile(2 * L, 512), _tile(n, 512), _tile(2 * L, 2048)
    return matmul(
        a_tab, b, grid=(1, 2 * L // tm, n // tn, 2 * L // tk), tm=tm, tn=tn,
        a_spec=pl.BlockSpec((tm, tk), lambda bt, i, j, k: (i, k)),
        b_spec=pl.BlockSpec((tk, tn), lambda bt, i, j, k: (k, j)),
        o_spec=pl.BlockSpec((tm, tn), lambda bt, i, j, k: (i, j)),
        out_shape=(2 * L, n), out_dtype=F32, name="hyena_spectrum")


def _fft_tab_body(o_ref, *, kind, n1, n2, k1_len):
    shape = o_ref.shape
    r = lax.broadcasted_iota(jnp.int32, shape, 0) + pl.program_id(0) * shape[0]
    c = lax.broadcasted_iota(jnp.int32, shape, 1)
    n = n1 * n2
    lg = lambda v: int(math.log2(v))
    if kind == "t1f":
        t = n2 * c + (r >> lg(2 * n1))
        part = (r >> lg(n1)) & 1
        val = _cos_turns(((r & (n1 - 1)) * t + part * (n // 4)) & (n - 1), lg(n))
    elif kind == "t1i":
        t = n2 * (r & (k1_len - 1)) + (r >> lg(k1_len))
        part = c >> lg(n1)
        val = _cos_turns(((c & (n1 - 1)) * t + part * (n // 4)) & (n - 1), lg(n)) * (1.0 / n)
    else:
        rp = r >= n2
        cblk = c >> lg(n2)
        cp = (cblk & 1) == 1
        sin_sign = -1 if kind == "f2i" else 1
        quarter = jnp.where(rp == cp, 0, jnp.where(cp, -sin_sign, sin_sign))
        val = _cos_turns(((r & (n2 - 1)) * (c & (n2 - 1)) + quarter * (n2 // 4)) & (n2 - 1), lg(n2))
        if kind == "f2filt":
            val = jnp.where((cblk >= 2) & rp, -val, val)
    o_ref[...] = val.astype(o_ref.dtype)


def fft_table(kind, n1, n2):
    k1_len = n1 // 2
    shape = {"t1f": (n2 * 2 * n1, k1_len), "t1i": (n2 * k1_len, 2 * n1), "f2f": (2 * n2, 2 * n2),
             "f2i": (2 * n2, 2 * n2), "f2filt": (2 * n2, 4 * n2)}[kind]
    tr = _tile(shape[0], 1024)
    out = pl.pallas_call(
        functools.partial(_fft_tab_body, kind=kind, n1=n1, n2=n2, k1_len=k1_len),
        out_shape=jax.ShapeDtypeStruct(shape, BF16),
        grid=(shape[0] // tr,),
        out_specs=pl.BlockSpec((tr, shape[1]), lambda i: (i, 0)),
        compiler_params=_cp("parallel"),
        name="fft_table_" + kind,
    )()
    if kind == "t1f":
        return out.reshape(n2, 2 * n1, k1_len)
    if kind == "t1i":
        return out.reshape(n2, k1_len, 2 * n1)
    return out


def _complex_mul_epilogue(acc, kf_ref, row_tile):
    h = acc.shape[0] // 2
    zr, zi = acc[:h], acc[h:]
    kr, ki = kf_ref[:h, :], kf_ref[h:, :]
    return jnp.concatenate([zr * kr - zi * ki, zr * ki + zi * kr], axis=0)


def hyena_spectrum_2level(L, filt, t1f, f2filt):
    n2 = SPEC_BLOCK
    n1 = 2 * L // n2
    k1 = n1 // 2
    n = filt.shape[2]
    tn = _tile(n, 2048)
    nj = n // tn
    y1 = matmul(
        t1f, filt.reshape(2 * k1, n2 * n), grid=(2, n2, nj, 1), tm=2 * n1, tn=tn,
        a_spec=pl.BlockSpec((None, 2 * n1, k1), lambda bt, i, j, k: (i, 0, 0)),
        b_spec=pl.BlockSpec((k1, tn), lambda bt, i, j, k: (bt, i * nj + j)),
        o_spec=pl.BlockSpec((None, 2 * n1, tn), lambda bt, i, j, k: (bt, 0, i * nj + j)),
        out_shape=(2, 2 * n1, n2 * n), out_dtype=BF16, name="hyena_filt_fft1")
    return matmul(
        f2filt, y1.reshape(2, 2 * n1 * n2, n), grid=(n1, 1, nj, 4), tm=2 * n2, tn=tn,
        a_spec=pl.BlockSpec((2 * n2, n2), lambda bt, i, j, k: (0, k)),
        b_spec=pl.BlockSpec((None, n2, tn), lambda bt, i, j, k: (k // 2, (k % 2) * n1 + bt, j)),
        o_spec=pl.BlockSpec((2 * n2, tn), lambda bt, i, j, k: (bt, j)),
        out_shape=(n1 * 2 * n2, n), out_dtype=F32, name="hyena_filt_fft2")


def hyena_stream_2level(u, row0, nb, L, kf, tabs, hy_bias, c, fill=None):
    t1f, f2f, f2i, t1i = tabs
    n2 = SPEC_BLOCK
    n1 = 2 * L // n2
    k1 = n1 // 2
    tn = _tile(c, 2048)
    cb = c // tn
    t, u_w = u.shape
    assert row0 % L == 0 and u_w % tn == 0
    uv = u.reshape(t // n2, n2 * u_w)
    uw = u_w // tn
    urow = row0 // L
    zv, zr, zw, zc = uv, urow, uw, 2 * cb
    bias3 = hy_bias.reshape(2, 1, c)
    for o in range(2):
        y1 = matmul(
            t1f, zv, grid=(nb, n2, cb, 1), tm=2 * n1, tn=tn,
            a_spec=pl.BlockSpec((None, 2 * n1, k1), lambda bt, i, j, k: (i, 0, 0)),
            b_spec=pl.BlockSpec((k1, tn), lambda bt, i, j, k, zr=zr, zw=zw, zc=zc: (zr + bt, i * zw + zc + j)),
            o_spec=pl.BlockSpec((None, 2 * n1, tn), lambda bt, i, j, k: (bt, 0, i * cb + j)),
            out_shape=(nb, 2 * n1, n2 * c), out_dtype=BF16, name="hyena_fft1")
        p = matmul(
            f2f, y1.reshape(nb, 2 * n1 * n2, c), grid=(n1, nb, cb, 2), tm=2 * n2, tn=tn,
            a_spec=pl.BlockSpec((2 * n2, n2), lambda bt, i, j, k: (0, k)),
            b_spec=pl.BlockSpec((None, n2, tn), lambda bt, i, j, k: (i, k * n1 + bt, j)),
            o_spec=pl.BlockSpec((None, 2 * n2, tn), lambda bt, i, j, k: (i, bt, j)),
            out_shape=(nb, n1 * 2 * n2, c), out_dtype=BF16,
            extra=(kf,), extra_specs=(pl.BlockSpec((2 * n2, tn), lambda bt, i, j, k, o=o: (bt, o * cb + j)),),
            epilogue=_complex_mul_epilogue, name="hyena_fft2")
        u1 = matmul(
            f2i, p, grid=(nb, 2 * n1, cb, 1), tm=n2, tn=tn,
            a_spec=pl.BlockSpec((n2, 2 * n2), lambda bt, i, j, k: (i % 2, 0)),
            b_spec=pl.BlockSpec((None, 2 * n2, tn), lambda bt, i, j, k: (bt, i // 2, j)),
            o_spec=pl.BlockSpec((None, n2, tn), lambda bt, i, j, k: (bt, (i % 2) * n1 + i // 2, j)),
            out_shape=(nb, 2 * n1 * n2, c), out_dtype=BF16, name="hyena_ifft1")
        last = o == 1
        orow = urow if last else 0
        z = matmul(
            t1i, u1.reshape(nb, 2 * n1, n2 * c), grid=(nb, n2, cb, 1), tm=k1, tn=tn,
            a_spec=pl.BlockSpec((None, k1, 2 * n1), lambda bt, i, j, k: (i, 0, 0)),
            b_spec=pl.BlockSpec((None, 2 * n1, tn), lambda bt, i, j, k: (bt, 0, i * cb + j)),
            o_spec=pl.BlockSpec((k1, tn), lambda bt, i, j, k, orow=orow: (orow + bt, i * cb + j)),
            out_shape=(t // n2 if last else nb * k1, n2 * c), out_dtype=BF16,
            fill=None if (fill is None or not last) else fill.reshape(t // n2, n2 * c),
            extra=(uv, zv, bias3),
            extra_specs=(pl.BlockSpec((k1, tn), lambda bt, i, j, k, o=o: (urow + bt, i * uw + o * cb + j)),
                         pl.BlockSpec((k1, tn), lambda bt, i, j, k, zr=zr, zw=zw, zc=zc:
                                      (zr + bt, i * zw + zc + j)),
                         pl.BlockSpec((None, 1, tn), lambda bt, i, j, k, o=o: (o, 0, j))),
            epilogue=lambda acc, g_ref, z_ref, b_ref, row_tile:
                g_ref[...].astype(F32) * (acc + b_ref[...] * z_ref[...].astype(F32)),
            name="hyena_ifft2")
        zv, zr, zw, zc = z, 0, cb, 0
    return zv.reshape(t, c)


def _ssd_prep_body(dc_ref, dr_ref, biasc_ref, biasr_ref, ac_ref, ar_ref, dt_ref, cumc_ref, cumr_ref, *, heads):
    q = dc_ref.shape[0]
    dn = (((1,), (0,)), ((), ()))
    li = lax.broadcasted_iota(jnp.int32, (q, q), 0)
    si = lax.broadcasted_iota(jnp.int32, (q, q), 1)
    lower = (li >= si).astype(F32)
    upper = (li <= si).astype(F32)
    dt_c = _softplus(dc_ref[...] + biasc_ref[...])
    da_c = dt_c * ac_ref[...]
    da_r = _softplus(dr_ref[...] + biasr_ref[...]) * ar_ref[...]
    pre_c = lax.dot_general(lower, da_c, dn, precision=HI, preferred_element_type=F32)
    suf_c = lax.dot_general(upper, da_c, dn, precision=HI, preferred_element_type=F32)
    pre_r = lax.dot_general(da_r, upper, dn, precision=HI, preferred_element_type=F32)
    suf_r = lax.dot_general(da_r, lower, dn, precision=HI, preferred_element_type=F32)
    dt_ref[...] = dt_c
    cumc_ref[...] = jnp.where(lax.broadcasted_iota(jnp.int32, pre_c.shape, 1) < heads, pre_c, suf_c)
    cumr_ref[...] = jnp.where(lax.broadcasted_iota(jnp.int32, pre_r.shape, 0) < heads, pre_r, suf_r)


def ssd_prep(dt_raw, dt_raw_t, dt_bias, a_log):
    t, h2 = dt_raw.shape
    q = SSD_CHUNK
    neg_a = -jnp.exp(a_log.astype(F32)).reshape(1, h2)
    bias = dt_bias.astype(F32).reshape(1, h2)
    const = lambda shape: pl.BlockSpec(shape, lambda i: (0,) * len(shape))
    col = pl.BlockSpec((q, h2), lambda i: (i, 0))
    row = pl.BlockSpec((h2, q), lambda i: (0, i))
    return pl.pallas_call(
        functools.partial(_ssd_prep_body, heads=h2 // 2),
        out_shape=(jax.ShapeDtypeStruct((t, h2), F32), jax.ShapeDtypeStruct((t, h2), F32),
                   jax.ShapeDtypeStruct((h2, t), F32)),
        grid=(t // q,),
        in_specs=[col, row, const((1, h2)), const((h2, 1)), const((1, h2)), const((h2, 1))],
        out_specs=(col, col, row),
        compiler_params=_cp("parallel"),
        name="ssd_prep",
    )(dt_raw, dt_raw_t, bias, bias.reshape(h2, 1), neg_a, neg_a.reshape(h2, 1))


def _expand_heads(x, head0, n_heads, width):
    rows = lax.broadcasted_iota(jnp.int32, (x.shape[1], n_heads * width), 0)
    lanes = lax.broadcasted_iota(jnp.int32, (x.shape[1], n_heads * width), 1)
    sel = jnp.where(rows == head0 + (lanes >> int(math.log2(width))), 1.0, 0.0).astype(BF16)
    out = None
    rest = x
    for _ in range(3):
        piece = rest.astype(BF16)
        rest = rest - piece.astype(F32)
        part = jnp.dot(piece, sel, preferred_element_type=F32)
        out = part if out is None else out + part
    return out


def _ssd_dir(xs, bt, cm, dt_c, cum_c, cumr_ref, state_ref, head0, *, reverse, heads_per_group):
    q = xs.shape[0]
    p = SSD_HEAD_DIM
    j = heads_per_group
    cc = _expand_heads(cum_c, head0, j, p)
    cc_wide = _expand_heads(cum_c, head0, j, q)
    dtv = _expand_heads(dt_c, head0, j, p)
    total = cc[0:1, :] if reverse else cc[q - 1:q, :]
    xd = xs.astype(F32) * dtv
    xd_b = xd.astype(BF16)
    li = lax.broadcasted_iota(jnp.int32, (q, q), 0)
    si = lax.broadcasted_iota(jnp.int32, (q, q), 1)
    mask = (li <= si) if reverse else (li >= si)
    cb = jnp.dot(cm, bt, preferred_element_type=F32)
    ys = []
    for jh in range(j):
        cr = cumr_ref[pl.ds(head0 + jh, 1), :]
        decay = jnp.exp(jnp.where(mask, cc_wide[:, jh * q:(jh + 1) * q] - cr, -jnp.inf))
        ys.append(jnp.dot((cb * decay).astype(BF16), xd_b[:, jh * p:(jh + 1) * p], preferred_element_type=F32))
    s_prev = state_ref[...]
    y = jnp.concatenate(ys, axis=1) + jnp.exp(cc) * jnp.dot(cm, s_prev.astype(BF16), preferred_element_type=F32)
    st = jnp.dot(bt, (xd * jnp.exp(total - cc)).astype(BF16), preferred_element_type=F32)
    state_ref[...] = jnp.exp(total) * s_prev + st
    return y


def _ssd_body(tab_ref, xf_ref, btf_ref, cf_ref, dtf_ref, ccf_ref, crf_ref, xb_ref, btb_ref, cb_ref, dtb_ref, ccb_ref,
              crb_ref, initf_ref, initb_ref, yf_ref, yb_ref, sf_ref, sb_ref, stf_ref, stb_ref,
              *, heads_per_group, heads):
    g = pl.program_id(0)
    item = pl.program_id(1)

    @pl.when(tab_ref[_SSD_FIRST, item] == 1)
    def _():
        stf_ref[...] = initf_ref[...]
        stb_ref[...] = initb_ref[...]

    yf_ref[...] = _ssd_dir(xf_ref[...], btf_ref[...], cf_ref[...], dtf_ref[...], ccf_ref[...], crf_ref, stf_ref,
                           g * heads_per_group, reverse=False, heads_per_group=heads_per_group)
    yb_ref[...] = _ssd_dir(xb_ref[...], btb_ref[...], cb_ref[...], dtb_ref[...], ccb_ref[...], crb_ref, stb_ref,
                           heads + g * heads_per_group, reverse=True, heads_per_group=heads_per_group)

    @pl.when(tab_ref[_SSD_LAST, item] == 1)
    def _():
        sf_ref[...] = stf_ref[...]
        sb_ref[...] = stb_ref[...]


_SSD_FWD, _SSD_BWD, _SSD_SEQ, _SSD_FIRST, _SSD_LAST = range(5)


def ssd_scan(u, bt_all, dt, cum_c, cum_r, init_f, init_b, *, seq_lens, col0, inner):
    q, p, n, gq = SSD_CHUNK, SSD_HEAD_DIM, SSD_STATE, SSD_GROUPS
    heads = inner // p
    j = heads // gq
    jp = j * p
    xcb = col0 // jp
    ccb = (col0 + inner + gq * n) // n
    rows, base = [], 0
    for s, length in enumerate(seq_lens):
        nc = length // q
        rows += [(base + c, base + nc - 1 - c, s, int(c == 0), int(c == nc - 1)) for c in range(nc)]
        base += nc
    table = jnp.array(rows, jnp.int32).T
    nseq, n_items = len(seq_lens), len(rows)

    def specs(which):
        return [pl.BlockSpec((q, jp), lambda g, it, tab: (tab[which, it], xcb + g)),
                pl.BlockSpec((n, q), lambda g, it, tab: (g, tab[which, it])),
                pl.BlockSpec((q, n), lambda g, it, tab: (tab[which, it], ccb + g)),
                pl.BlockSpec((q, 2 * heads), lambda g, it, tab: (tab[which, it], 0)),
                pl.BlockSpec((q, 2 * heads), lambda g, it, tab: (tab[which, it], 0)),
                pl.BlockSpec((2 * heads, q), lambda g, it, tab: (0, tab[which, it]))]

    st_spec = pl.BlockSpec((None, None, n, jp), lambda g, it, tab: (tab[_SSD_SEQ, it], g, 0, 0))
    st_shape = jax.ShapeDtypeStruct((nseq, gq, n, jp), F32)
    y_shape = jax.ShapeDtypeStruct((u.shape[0], inner), F32)
    grid_spec = pltpu.PrefetchScalarGridSpec(
        num_scalar_prefetch=1,
        grid=(gq, n_items),
        in_specs=[*specs(_SSD_FWD), *specs(_SSD_BWD), st_spec, st_spec],
        out_specs=(pl.BlockSpec((q, jp), lambda g, it, tab: (tab[_SSD_FWD, it], g)),
                   pl.BlockSpec((q, jp), lambda g, it, tab: (tab[_SSD_BWD, it], g)), st_spec, st_spec),
        scratch_shapes=[pltpu.VMEM((n, jp), F32), pltpu.VMEM((n, jp), F32)])
    return pl.pallas_call(
        functools.partial(_ssd_body, heads_per_group=j, heads=heads),
        out_shape=(y_shape, y_shape, st_shape, st_shape),
        grid_spec=grid_spec,
        compiler_params=_cp("parallel", "arbitrary"),
        name="ssd_scan",
    )(table, u, bt_all, u, dt, cum_c, cum_r, u, bt_all, u, dt, cum_c, cum_r, init_f, init_b)


def _state_to_kernel_layout(s, groups):
    b, h, p, n = s.shape
    return s.reshape(b, groups, h // groups, p, n).transpose(0, 1, 4, 2, 3).reshape(b, groups, n, (h // groups) * p)


def _state_from_kernel_layout(s, head_dim):
    b, g, n, jp = s.shape
    j = jp // head_dim
    return s.reshape(b, g, n, j, head_dim).transpose(0, 1, 3, 4, 2).reshape(b, g * j, head_dim, n)


def _ssd_finish_body(yf_ref, yb_ref, xs_ref, z_ref, d_ref, w_ref, o_ref):
    y = yf_ref[...] + yb_ref[...] + xs_ref[...].astype(F32) * d_ref[...]
    y = y * _silu(z_ref[...].astype(F32))
    y = y * lax.rsqrt(jnp.mean(y * y, axis=-1, keepdims=True) + RMS_EPS)
    o_ref[...] = (y * w_ref[...]).astype(o_ref.dtype)


def ssd_finish(yf, yb, u, proj, d_full, norm_w, *, xs_col, z_col, inner):
    t = yf.shape[0]
    tr = ROW_TILE
    return pl.pallas_call(
        _ssd_finish_body,
        out_shape=jax.ShapeDtypeStruct((t, inner), BF16),
        grid=(t // tr,),
        in_specs=[pl.BlockSpec((tr, inner), lambda i: (i, 0)),
                  pl.BlockSpec((tr, inner), lambda i: (i, 0)),
                  pl.BlockSpec((tr, inner), lambda i: (i, xs_col // inner)),
                  pl.BlockSpec((tr, inner), lambda i: (i, z_col // inner)),
                  pl.BlockSpec((1, inner), lambda i: (0, 0)),
                  pl.BlockSpec((1, inner), lambda i: (0, 0))],
        out_specs=pl.BlockSpec((tr, inner), lambda i: (i, 0)),
        compiler_params=_cp("parallel"),
        name="ssd_finish",
    )(yf, yb, u, proj, d_full.reshape(1, inner), norm_w.reshape(1, inner))


KV_PER_STEP = LANES // HEAD_DIM


def _softmax_parts(scores, sink):
    m = sink
    for s in scores:
        m = jnp.maximum(m, jnp.max(s, axis=-1, keepdims=True))
    es = [jnp.exp(s - m) for s in scores]
    denom = jnp.exp(sink - m)
    for e in es:
        denom = denom + jnp.sum(e, axis=-1, keepdims=True)
    inv = 1.0 / denom
    return [e * inv for e in es]


def _ctx_attn_body(sink_ref, q_ref, k_ref, v_ref, fill_ref, o_ref, *, q_per_kv, scale):
    del fill_ref
    hb = pl.program_id(1)
    nt = (((1,), (1,)), ((), ()))
    outs = []
    for kv in range(KV_PER_STEP):
        k = k_ref[:, kv * HEAD_DIM:(kv + 1) * HEAD_DIM].astype(BF16)
        v = v_ref[:, kv * HEAD_DIM:(kv + 1) * HEAD_DIM].astype(BF16)
        for gq in range(q_per_kv):
            hl = kv * q_per_kv + gq
            qh = (q_ref[:, hl * HEAD_DIM:(hl + 1) * HEAD_DIM] * scale).astype(BF16)
            s = lax.dot_general(qh, k, nt, preferred_element_type=F32)
            sink = sink_ref[hb * KV_PER_STEP * q_per_kv + hl]
            (p,) = _softmax_parts([s], sink)
            outs.append(jnp.dot(p.astype(BF16), v, preferred_element_type=F32))
    o_ref[...] = jnp.concatenate(outs, axis=1).astype(o_ref.dtype)


def context_attention(qkv, sink, *, nseq, L, n_heads):
    q_per_kv = n_heads // N_KV_HEADS
    qw = KV_PER_STEP * q_per_kv * HEAD_DIM
    nhb = N_KV_HEADS // KV_PER_STEP
    kcb = n_heads * HEAD_DIM // LANES
    vcb = kcb + N_KV_HEADS * HEAD_DIM // LANES
    return pl.pallas_call(
        functools.partial(_ctx_attn_body, q_per_kv=q_per_kv, scale=HEAD_DIM ** -0.5),
        out_shape=jax.ShapeDtypeStruct((qkv.shape[0], n_heads * HEAD_DIM), BF16),
        grid=(nseq, nhb),
        in_specs=[pl.BlockSpec(memory_space=pltpu.SMEM),
                  pl.BlockSpec((L, qw), lambda b, h: (b, h)),
                  pl.BlockSpec((L, LANES), lambda b, h: (b, kcb + h)),
                  pl.BlockSpec((L, LANES), lambda b, h: (b, vcb + h)),
                  pl.BlockSpec(memory_space=pl.ANY)],
        out_specs=pl.BlockSpec((L, qw), lambda b, h: (b, h)),
        input_output_aliases={4: 0},
        compiler_params=_cp("parallel", "parallel"),
        name="context_attention",
    )(sink, qkv, qkv, qkv, jnp.zeros((qkv.shape[0], n_heads * HEAD_DIM), BF16))


def _rope(x, cos, sin):
    outs = []
    lane = lax.broadcasted_iota(jnp.int32, cos.shape, 1)
    first = (lane % (HEAD_DIM // 2)) < (HEAD_DIM // 4)
    for cgrp in range(x.shape[1] // LANES):
        xc = x[:, cgrp * LANES:(cgrp + 1) * LANES]
        partner = jnp.where(first, pltpu.roll(xc, LANES - HEAD_DIM // 4, 1), pltpu.roll(xc, HEAD_DIM // 4, 1))
        outs.append(xc * cos + partner * sin)
    return outs[0] if len(outs) == 1 else jnp.concatenate(outs, axis=1)


def _lat_attn_body(sink_ref, q_ref, k_ref, v_ref, kc_ref, vc_ref, cosq_ref, sinq_ref, cos_ref, sin_ref, fill_ref,
                   o_ref, *, q_per_kv, scale, L, blk):
    del fill_ref
    hb = pl.program_id(1)
    i = pl.program_id(2)
    n_loc = 3 * blk
    start = jnp.clip((i - 1) * blk, 0, L - n_loc)
    start = pl.multiple_of(start, blk)
    nt = (((1,), (1,)), ((), ()))
    q_all = _rope(q_ref[...], cosq_ref[...], sinq_ref[...]) * scale
    k_loc = _rope(k_ref[pl.ds(start, n_loc), :], cos_ref[pl.ds(start, n_loc), :], sin_ref[pl.ds(start, n_loc), :])
    v_loc = v_ref[pl.ds(start, n_loc), :]
    qpos = i * blk + lax.broadcasted_iota(jnp.int32, (blk, n_loc), 0)
    kpos = start + lax.broadcasted_iota(jnp.int32, (blk, n_loc), 1)
    ok = jnp.abs(qpos - kpos) <= WINDOW
    outs = []
    for kv in range(KV_PER_STEP):
        sl = slice(kv * HEAD_DIM, (kv + 1) * HEAD_DIM)
        k = k_loc[:, sl].astype(BF16)
        v = v_loc[:, sl].astype(BF16)
        kc = kc_ref[:, sl].astype(BF16)
        vc = vc_ref[:, sl].astype(BF16)
        for gq in range(q_per_kv):
            hl = kv * q_per_kv + gq
            qh = q_all[:, hl * HEAD_DIM:(hl + 1) * HEAD_DIM].astype(BF16)
            s_loc = jnp.where(ok, lax.dot_general(qh, k, nt, preferred_element_type=F32), -jnp.inf)
            s_ctx = lax.dot_general(qh, kc, nt, preferred_element_type=F32)
            sink = sink_ref[hb * KV_PER_STEP * q_per_kv + hl]
            p_loc, p_ctx = _softmax_parts([s_loc, s_ctx], sink)
            outs.append(jnp.dot(p_loc.astype(BF16), v, preferred_element_type=F32)
                        + jnp.dot(p_ctx.astype(BF16), vc, preferred_element_type=F32))
    o_ref[...] = jnp.concatenate(outs, axis=1).astype(o_ref.dtype)


def rope_tables(L):
    rows = L // GRID_W
    row = jnp.repeat(jnp.arange(rows, dtype=F32), GRID_W)
    col = jnp.tile(jnp.arange(GRID_W, dtype=F32), rows)
    quarter = HEAD_DIM // 4
    inv = ROPE_BASE ** (-jnp.arange(quarter, dtype=F32) / quarter)
    ang_r = row[:, None] * inv[None, :]
    ang_c = col[:, None] * inv[None, :]
    ang = jnp.concatenate([ang_r, ang_r, ang_c, ang_c], axis=1)
    sign = jnp.tile(jnp.concatenate([-jnp.ones((quarter,), F32), jnp.ones((quarter,), F32)]), 2)
    cos = jnp.tile(jnp.cos(ang), (1, LANES // HEAD_DIM))
    sin = jnp.tile(jnp.sin(ang) * sign[None, :], (1, LANES // HEAD_DIM))
    return cos, sin


def latent_attention(qkv, k_cache, v_cache, sink, fill, *, row0, nseq, L, n_heads):
    blk = WINDOW
    q_per_kv = n_heads // N_KV_HEADS
    qw = KV_PER_STEP * q_per_kv * HEAD_DIM
    nhb = N_KV_HEADS // KV_PER_STEP
    kcb = n_heads * HEAD_DIM // LANES
    vcb = kcb + N_KV_HEADS * HEAD_DIM // LANES
    nb = L // blk
    past = k_cache.shape[1]
    cos, sin = rope_tables(L)
    return pl.pallas_call(
        functools.partial(_lat_attn_body, q_per_kv=q_per_kv, scale=HEAD_DIM ** -0.5, L=L, blk=blk),
        out_shape=jax.ShapeDtypeStruct(fill.shape, BF16),
        grid=(nseq, nhb, nb),
        in_specs=[pl.BlockSpec(memory_space=pltpu.SMEM),
                  pl.BlockSpec((blk, qw), lambda b, h, i: (row0 // blk + b * nb + i, h)),
                  pl.BlockSpec((L, LANES), lambda b, h, i: (row0 // L + b, kcb + h)),
                  pl.BlockSpec((L, LANES), lambda b, h, i: (row0 // L + b, vcb + h)),
                  pl.BlockSpec((None, past, LANES), lambda b, h, i: (b, 0, h)),
                  pl.BlockSpec((None, past, LANES), lambda b, h, i: (b, 0, h)),
                  pl.BlockSpec((blk, LANES), lambda b, h, i: (i, 0)),
                  pl.BlockSpec((blk, LANES), lambda b, h, i: (i, 0)),
                  pl.BlockSpec((L, LANES), lambda b, h, i: (0, 0)),
                  pl.BlockSpec((L, LANES), lambda b, h, i: (0, 0)),
                  pl.BlockSpec(memory_space=pl.ANY)],
        out_specs=pl.BlockSpec((blk, qw), lambda b, h, i: (row0 // blk + b * nb + i, h)),
        input_output_aliases={10: 0},
        compiler_params=_cp("parallel", "parallel", "arbitrary"),
        name="latent_attention",
    )(sink, qkv, qkv, qkv, k_cache, v_cache, cos, sin, cos, sin, fill)


MOE_TILE = 256
MOE_FFN_VMEM_BYTES = 56 * 1024 * 1024


def _moe_dispatch_body(pos_ref, h_ref, g_ref, init_hbm, xs_hbm, buf, sem):
    del init_hbm
    i = pl.program_id(0)
    tr, d = h_ref.shape
    buf[:, :d] = h_ref[...]
    buf[:, d:] = g_ref[...]

    def start(r, carry):
        pltpu.make_async_copy(buf.at[pl.ds(r, 1)], xs_hbm.at[pl.ds(pos_ref[i * tr + r], 1)], sem).start()
        return carry

    lax.fori_loop(0, tr, start, 0)

    def wait(r, carry):
        pltpu.make_async_copy(buf.at[pl.ds(r, 1)], xs_hbm.at[pl.ds(0, 1)], sem).wait()
        return carry

    lax.fori_loop(0, tr, wait, 0)


def moe_dispatch(h, gate_lanes, pos, n_slots):
    t, d = h.shape
    tr = ROW_TILE
    w = d + LANES
    grid_spec = pltpu.PrefetchScalarGridSpec(
        num_scalar_prefetch=1,
        grid=(t // tr,),
        in_specs=[pl.BlockSpec((tr, d), lambda i, p: (i, 0)),
                  pl.BlockSpec((tr, LANES), lambda i, p: (i, 0)),
                  pl.BlockSpec(memory_space=pl.ANY)],
        out_specs=pl.BlockSpec(memory_space=pl.ANY),
        scratch_shapes=[pltpu.VMEM((tr, w), F32), pltpu.SemaphoreType.DMA(())])
    return pl.pallas_call(
        _moe_dispatch_body,
        out_shape=jax.ShapeDtypeStruct((n_slots, w), F32),
        grid_spec=grid_spec,
        input_output_aliases={3: 0},
        compiler_params=_cp("arbitrary"),
        name="moe_dispatch",
    )(pos, h, gate_lanes, jnp.zeros((n_slots, w), F32))


def _moe_ffn_body(ea_ref, eb_ref, nused_ref, xs_ref, w1a_ref, w3a_ref, w2a_ref, w1b_ref, w3b_ref, w2b_ref, o_ref):
    i = pl.program_id(0)
    d = o_ref.shape[1]

    @pl.when(i < nused_ref[0])
    def _():
        x = xs_ref[:, :d].astype(BF16)

        def expert(w1_ref, w3_ref, w2_ref):
            a = jnp.dot(x, w1_ref[...], preferred_element_type=F32)
            b = jnp.dot(x, w3_ref[...], preferred_element_type=F32)
            return jnp.dot((_silu(a) * b).astype(BF16), w2_ref[...], preferred_element_type=F32)

        o_ref[...] = (xs_ref[:, d:d + 1] * expert(w1a_ref, w3a_ref, w2a_ref)
                      + xs_ref[:, d + 1:d + 2] * expert(w1b_ref, w3b_ref, w2b_ref))

    @pl.when(i >= nused_ref[0])
    def _():
        o_ref[...] = jnp.zeros_like(o_ref)


def moe_ffn(xs, tile_ea, tile_eb, n_used, w1, w3, w2):
    _, d, de = w1.shape
    tm = MOE_TILE
    n_tiles = tile_ea.shape[0]
    up_a = pl.BlockSpec((None, d, de), lambda i, ea, eb, nu: (ea[i], 0, 0))
    up_b = pl.BlockSpec((None, d, de), lambda i, ea, eb, nu: (eb[i], 0, 0))
    grid_spec = pltpu.PrefetchScalarGridSpec(
        num_scalar_prefetch=3,
        grid=(n_tiles,),
        in_specs=[pl.BlockSpec((tm, d + LANES), lambda i, ea, eb, nu: (i, 0)),
                  up_a, up_a, pl.BlockSpec((None, de, d), lambda i, ea, eb, nu: (ea[i], 0, 0)),
                  up_b, up_b, pl.BlockSpec((None, de, d), lambda i, ea, eb, nu: (eb[i], 0, 0))],
        out_specs=pl.BlockSpec((tm, d), lambda i, ea, eb, nu: (i, 0)))
    return pl.pallas_call(
        _moe_ffn_body,
        out_shape=jax.ShapeDtypeStruct((n_tiles * tm, d), F32),
        grid_spec=grid_spec,
        compiler_params=_cp("arbitrary", vmem=MOE_FFN_VMEM_BYTES),
        name="moe_ffn",
    )(tile_ea, tile_eb, n_used, xs, w1, w3, w2, w1, w3, w2)


def _moe_combine_body(pos_ref, y_hbm, x_ref, g_ref, o_ref, ybuf, sem):
    i = pl.program_id(0)
    tr = x_ref.shape[0]

    def start(r, carry):
        pltpu.make_async_copy(y_hbm.at[pl.ds(pos_ref[i * tr + r], 1)], ybuf.at[pl.ds(r, 1)], sem).start()
        return carry

    lax.fori_loop(0, tr, start, 0)

    def wait(r, carry):
        pltpu.make_async_copy(y_hbm.at[pl.ds(0, 1)], ybuf.at[pl.ds(r, 1)], sem).wait()
        return carry

    lax.fori_loop(0, tr, wait, 0)
    o_ref[...] = x_ref[...] + g_ref[...] * ybuf[...]


def moe_combine(x, y_sorted, pos, mod3, gate_idx, seg_of_tile):
    t, d = x.shape
    tr = ROW_TILE
    grid_spec = pltpu.PrefetchScalarGridSpec(
        num_scalar_prefetch=1,
        grid=(t // tr,),
        in_specs=[pl.BlockSpec(memory_space=pl.ANY),
                  pl.BlockSpec((tr, d), lambda i, p: (i, 0)),
                  pl.BlockSpec((None, 1, d), lambda i, p: (seg_of_tile(i) * 6 + gate_idx, 0, 0))],
        out_specs=pl.BlockSpec((tr, d), lambda i, p: (i, 0)),
        scratch_shapes=[pltpu.VMEM((tr, d), F32), pltpu.SemaphoreType.DMA(())])
    return pl.pallas_call(
        _moe_combine_body,
        out_shape=jax.ShapeDtypeStruct((t, d), F32),
        grid_spec=grid_spec,
        compiler_params=_cp("arbitrary"),
        name="moe_combine",
    )(pos, y_sorted, x, mod3)


def moe_route(logits, n_groups, n_experts):
    per = n_experts // n_groups
    g_prob = jax.nn.softmax(logits[:, :n_groups], axis=-1)
    g_top, g_idx = lax.top_k(g_prob, 1)
    e_logits = logits[:, n_groups:n_groups + n_experts].reshape(-1, n_groups, per)
    e_in = jnp.take_along_axis(e_logits, g_idx[:, :, None], axis=1)[:, 0]
    e_top, e_idx = lax.top_k(jax.nn.softmax(e_in, axis=-1), TOP_K)
    e_top = e_top / jnp.sum(e_top, axis=-1, keepdims=True)
    return g_idx * per + e_idx, g_top * e_top


def moe_plan(expert_ids, gates, n_groups, n_experts):
    assert TOP_K == 2
    t = expert_ids.shape[0]
    tm = MOE_TILE
    per = n_experts // n_groups
    n_pairs = per * (per - 1) // 2
    n_classes = n_groups * n_pairs
    n_tiles = t // tm + n_classes
    e0, e1 = expert_ids[:, 0], expert_ids[:, 1]
    lo, hi = jnp.minimum(e0, e1), jnp.maximum(e0, e1)
    g_lo = jnp.where(e0 < e1, gates[:, 0], gates[:, 1])
    g_hi = jnp.where(e0 < e1, gates[:, 1], gates[:, 0])
    a, b = lo % per, hi % per
    cls = (lo // per) * n_pairs + (a * (2 * per - a - 1)) // 2 + (b - a - 1)
    onehot = (cls[:, None] == jnp.arange(n_classes)[None, :]).astype(F32)
    blk = ROW_TILE
    inner = jnp.einsum("ij,bjk->bik", jnp.tril(jnp.ones((blk, blk), F32)), onehot.reshape(-1, blk, n_classes))
    bsum = inner[:, -1, :]
    running = (inner + (jnp.cumsum(bsum, axis=0) - bsum)[:, None, :]).reshape(-1, n_classes)
    rank = jnp.take_along_axis(running, cls[:, None], axis=1)[:, 0].astype(jnp.int32) - 1
    counts = jnp.sum(bsum, axis=0).astype(jnp.int32)
    tiles_per = (counts + tm - 1) // tm
    tile_start = jnp.cumsum(tiles_per) - tiles_per
    pos = (tile_start[cls] * tm + rank).astype(jnp.int32)
    n_used = jnp.sum(tiles_per).astype(jnp.int32)
    tile_ids = jnp.arange(n_tiles, dtype=jnp.int32)
    tile_cls = jnp.sum((tile_ids[:, None] >= tile_start[None, :]).astype(jnp.int32), axis=1) - 1
    tile_cls = jnp.where(tile_ids < n_used, tile_cls, tile_cls[jnp.maximum(n_used - 1, 0)])
    pairs = [(pa, pb) for pa in range(per) for pb in range(pa + 1, per)]
    cls_lo = jnp.array([g * per + pa for g in range(n_groups) for pa, _ in pairs], jnp.int32)
    cls_hi = jnp.array([g * per + pb for g in range(n_groups) for _, pb in pairs], jnp.int32)
    gate_lanes = jnp.pad(jnp.stack([g_lo, g_hi], axis=1), ((0, 0), (0, LANES - 2)))
    return pos, gate_lanes, cls_lo[tile_cls], cls_hi[tile_cls], n_used.reshape(1), n_tiles * tm


def kernel(x_prompt, x_sample, state_ssd_fwd, state_ssd_bwd, cache_attn_k, cache_attn_v, c, c_ctx, ada_w, ada_b, norm_mix, norm_ffn, norm_final, ev_w_in, ev_w_out, hy_conv_w, hy_conv_b, hy_f_w1, hy_f_b1, hy_f_freq1, hy_f_w2, hy_f_b2, hy_f_freq2, hy_f_w3, hy_bias, ssd_conv_w, ssd_conv_b, ssd_dt_bias, ssd_a_log, ssd_d, ssd_norm, at_wq, at_wk, at_wv, at_wo, at_sink, moe_w_group, moe_b_group, moe_w_expert, moe_b_expert, moe_w1, moe_w3, moe_w2):
    bc, lc, d = x_prompt.shape
    bl, ll, _ = x_sample.shape
    tc_rows, tl_rows = bc * lc, bl * ll
    t = tc_rows + tl_rows
    depth = ada_w.shape[0]
    n_seg = 1 + bl
    assert lc % ROW_TILE == 0 and ll % ROW_TILE == 0 and tc_rows % ll == 0
    seg_rows = math.gcd(tc_rows, ll)

    def seg_of_rows(r):
        return jnp.where(r < tc_rows, 0, 1 + (r - tc_rows) // ll)

    def seg_of_tile(i):
        return seg_of_rows(i * ROW_TILE)

    x = jnp.concatenate([x_prompt.reshape(tc_rows, d), x_sample.reshape(tl_rows, d)], axis=0)
    cvec = jnp.concatenate([c_ctx[None], c, jnp.zeros((SUBLANES - n_seg, d), F32)], axis=0)
    mod_all = ada_mod(cvec, ada_w, ada_b)

    tile_row = jnp.arange(t // ROW_TILE, dtype=jnp.int32) * ROW_TILE
    seq_len = jnp.where(tile_row < tc_rows, lc, ll)
    rel = jnp.where(tile_row < tc_rows, tile_row, tile_row - tc_rows)
    conv_flags = jnp.stack([(rel % seq_len == 0), ((rel + ROW_TILE) % seq_len == 0)]).astype(jnp.int32)

    new_sf, new_sb, new_k, new_v = [], [], [], []
    for l in range(depth):
        i = l // 2
        mod3 = mod_all[l, :n_seg].reshape(n_seg * 6, 1, d)
        h = rms_norm(x, norm_mix[l], out_dtype=BF16, mod3=mod3, shift_idx=0, scale_idx=1,
                     seg_of_tile=seg_of_tile)
        if l % 2 == 0:
            hyena_w = hy_conv_w.shape[2] // 3
            inner = ssd_norm.shape[1]
            heads = inner // SSD_HEAD_DIM
            conv_ch = ssd_conv_w.shape[2]
            main_cols = 3 * hyena_w + inner + conv_ch
            w_in = ev_w_in[i]
            proj = dense(h, w_in[:, :main_cols].astype(BF16), out_dtype=BF16, name="in_proj")
            dt_raw = dense(h, w_in[:, main_cols:].astype(BF16), out_dtype=F32, name="dt_proj")
            conv_w = jnp.concatenate([hy_conv_w[i], ssd_conv_w[i]], axis=1)
            conv_b = jnp.concatenate([hy_conv_b[i], ssd_conv_b[i]], axis=0)
            u = dwconv(proj, conv_w, conv_b, conv_flags, n_plain_cols=3 * hyena_w, gap_cols=inner,
                       out_cols=3 * hyena_w + conv_ch)
            y_a = jnp.zeros((t, hyena_w), BF16)
            for row0, nb, L in ((0, bc, lc), (tc_rows, bl, ll)):
                filt = hyena_filters(L, hy_f_w1[i], hy_f_b1[i], hy_f_freq1[i], hy_f_w2[i], hy_f_b2[i],
                                     hy_f_freq2[i], hy_f_w3[i], hyena_w)
                if L % (16 * SPEC_BLOCK) == 0:
                    n1 = 2 * L // SPEC_BLOCK
                    t1f = fft_table("t1f", n1, SPEC_BLOCK)
                    kf = hyena_spectrum_2level(L, filt, t1f, fft_table("f2filt", n1, SPEC_BLOCK))
                    tabs = (t1f, fft_table("f2f", n1, SPEC_BLOCK), fft_table("f2i", n1, SPEC_BLOCK),
                            fft_table("t1i", n1, SPEC_BLOCK))
                    y_a = hyena_stream_2level(u, row0, nb, L, kf, tabs, hy_bias[i], hyena_w, fill=y_a)
                else:
                    a_tab = dft_table(L, "filt")
                    kf = hyena_spectrum(L, filt, a_tab)
                    y_a = hyena_stream(u, row0, nb, L, kf, a_tab, dft_table(L, "inv"), hy_bias[i], hyena_w,
                                       fill=y_a)
            col0 = 3 * hyena_w
            gn = SSD_GROUPS * SSD_STATE
            bt_all = u[:, col0 + inner:col0 + inner + gn].T
            dt, cum_c, cum_r = ssd_prep(dt_raw, dt_raw.T, ssd_dt_bias[i], ssd_a_log[i])
            zero = jnp.zeros((bc, SSD_GROUPS, SSD_STATE, inner // SSD_GROUPS), F32)
            init_f = jnp.concatenate([zero, _state_to_kernel_layout(state_ssd_fwd[:, i], SSD_GROUPS)], axis=0)
            init_b = jnp.concatenate([zero, _state_to_kernel_layout(state_ssd_bwd[:, i], SSD_GROUPS)], axis=0)
            yf, yb, sf, sb = ssd_scan(u, bt_all, dt, cum_c, cum_r, init_f, init_b,
                                      seq_lens=[lc] * bc + [ll] * bl, col0=col0, inner=inner)
            new_sf.append(_state_from_kernel_layout(sf[:bc], SSD_HEAD_DIM))
            new_sb.append(_state_from_kernel_layout(sb[:bc], SSD_HEAD_DIM))
            y_b = ssd_finish(yf, yb, u, proj, jnp.repeat(ssd_d[i], SSD_HEAD_DIM), ssd_norm[i],
                             xs_col=col0, z_col=3 * hyena_w, inner=inner)
            w_out = ev_w_out[i].astype(BF16)
            x = dense_residual(y_a, w_out[:hyena_w], x, mod3, 2, seg_of_rows, seg_rows, name="out_proj_a")
            x = dense_residual(y_b, w_out[hyena_w:], x, mod3, 2, seg_of_rows, seg_rows, name="out_proj_b")
        else:
            n_heads = at_wq.shape[2] // HEAD_DIM
            w_qkv = jnp.concatenate([at_wq[i], at_wk[i], at_wv[i]], axis=1).astype(BF16)
            qkv = dense(h, w_qkv, out_dtype=F32, name="qkv_proj")
            kvw = N_KV_HEADS * HEAD_DIM
            att_c = context_attention(qkv, at_sink[i], nseq=bc, L=lc, n_heads=n_heads)
            past = cache_attn_k.shape[2]
            att = latent_attention(qkv, cache_attn_k[:, i].reshape(bl, past, kvw),
                                   cache_attn_v[:, i].reshape(bl, past, kvw), at_sink[i], att_c,
                                   row0=tc_rows, nseq=bl, L=ll, n_heads=n_heads)
            x = dense_residual(att, at_wo[i].astype(BF16), x, mod3, 2, seg_of_rows, seg_rows, name="attn_out")
            qw = n_heads * HEAD_DIM
            new_k.append(qkv[:tc_rows, qw:qw + kvw].reshape(bc, lc, N_KV_HEADS, HEAD_DIM))
            new_v.append(qkv[:tc_rows, qw + kvw:].reshape(bc, lc, N_KV_HEADS, HEAD_DIM))
        h2 = rms_norm(x, norm_ffn[l], out_dtype=F32, mod3=mod3, shift_idx=3, scale_idx=4,
                      seg_of_tile=seg_of_tile)
        n_experts = moe_w_expert.shape[2]
        w_r = jnp.concatenate([moe_w_group[l], moe_w_expert[l]], axis=1)
        w_r = jnp.pad(w_r, ((0, 0), (0, LANES - w_r.shape[1]))).astype(BF16)
        b_r = jnp.pad(jnp.concatenate([moe_b_group[l], moe_b_expert[l]]), (0, LANES - MOE_GROUPS - n_experts))
        logits = dense(h2, w_r, out_dtype=F32, extra=(b_r.reshape(1, LANES),),
                       extra_specs=(pl.BlockSpec((1, LANES), lambda bt, ii, j, kk: (0, 0)),),
                       epilogue=lambda acc, b_ref, row_tile: acc + b_ref[...], name="router")
        expert_ids, gates = moe_route(logits, MOE_GROUPS, n_experts)
        pos, gate_lanes, tile_lo, tile_hi, n_used, n_slots = moe_plan(expert_ids, gates, MOE_GROUPS, n_experts)
        xs = moe_dispatch(h2, gate_lanes, pos, n_slots)
        y_sorted = moe_ffn(xs, tile_lo, tile_hi, n_used, moe_w1[l].astype(BF16), moe_w3[l].astype(BF16),
                           moe_w2[l].astype(BF16))
        x = moe_combine(x, y_sorted, pos, mod3, 5, seg_of_tile)

    y_prompt = rms_norm(x, norm_final, out_dtype=F32, rows=tc_rows).reshape(bc, lc, d)
    y_sample = rms_norm(x, norm_final, out_dtype=F32, row_off=tc_rows, rows=tl_rows).reshape(bl, ll, d)
    return (y_prompt, y_sample, jnp.stack(new_sf, axis=1), jnp.stack(new_sb, axis=1),
            jnp.stack(new_k, axis=1), jnp.stack(new_v, axis=1))
```

```python
import functools
import math

import jax
import jax.numpy as jnp
from jax import lax
from jax.experimental import pallas as pl
from jax.experimental.pallas import tpu as pltpu

F32 = jnp.float32
BF16 = jnp.bfloat16
HI = lax.Precision.HIGHEST

RMS_EPS = 1e-6
GRID_W = 64
HYENA_BANDS = 16
HYENA_FAST_DECAY = 0.3
HYENA_SLOW_DECAY = 1.5
HYENA_TARGET = 1e-2
SSD_HEAD_DIM = 64
SSD_STATE = 128
SSD_GROUPS = 8
SSD_CHUNK = 128
HEAD_DIM = 64
N_KV_HEADS = 8
WINDOW = 128
ROPE_BASE = 10000.0
MOE_GROUPS = 4
TOP_K = 2

LANES = 128
SUBLANES = 8
BF16_SUBLANES = 16
VMEM_LIMIT_BYTES = 48 * 1024 * 1024

ROW_TILE = 256
SPEC_BLOCK = 128


def _cp(*sem, vmem=VMEM_LIMIT_BYTES):
    return pltpu.CompilerParams(dimension_semantics=sem, vmem_limit_bytes=vmem)


def _tile(n, pref):
    if n <= pref:
        return n
    t = pref
    while n % t:
        t //= 2
    assert t >= LANES, (n, pref)
    return t


def _silu(x):
    return x * (1.0 / (1.0 + jnp.exp(-x)))


def _softplus(x):
    return jnp.maximum(x, 0.0) + jnp.log(1.0 + jnp.exp(-jnp.abs(x)))


def _ada_body(c_ref, w_ref, b_ref, o_ref):
    a = _silu(c_ref[...])
    o_ref[...] = lax.dot_general(a, w_ref[...], (((1,), (0,)), ((), ())), precision=HI,
                                 preferred_element_type=F32) + b_ref[...]


def ada_mod(cvec, ada_w, ada_b):
    depth, d, n = ada_w.shape
    rows = cvec.shape[0]
    tn = _tile(n, 1024)
    return pl.pallas_call(
        _ada_body,
        out_shape=jax.ShapeDtypeStruct((depth, rows, n), F32),
        grid=(depth, n // tn),
        in_specs=[pl.BlockSpec((rows, d), lambda l, j: (0, 0)),
                  pl.BlockSpec((None, d, tn), lambda l, j: (l, 0, j)),
                  pl.BlockSpec((None, 1, tn), lambda l, j: (l, 0, j))],
        out_specs=pl.BlockSpec((None, rows, tn), lambda l, j: (l, 0, j)),
        compiler_params=_cp("parallel", "parallel"),
        name="ada_mod",
    )(cvec, ada_w, ada_b.reshape(depth, 1, n))


def _norm_body(x_ref, g_ref, *rest, modulate):
    o_ref = rest[-1]
    x = x_ref[...]
    y = x * lax.rsqrt(jnp.mean(x * x, axis=-1, keepdims=True) + RMS_EPS)
    y = y * g_ref[...]
    if modulate:
        sh_ref, sc_ref = rest[0], rest[1]
        y = y * (1.0 + sc_ref[...]) + sh_ref[...]
    o_ref[...] = y.astype(o_ref.dtype)


def rms_norm(x, g, *, out_dtype, mod3=None, shift_idx=0, scale_idx=0, seg_of_tile=None,
             row_off=0, rows=None):
    t, d = x.shape
    rows = t if rows is None else rows
    tr = ROW_TILE
    off = row_off // tr
    in_specs = [pl.BlockSpec((tr, d), lambda i: (i + off, 0)),
                pl.BlockSpec((1, d), lambda i: (0, 0))]
    args = [x, g.reshape(1, d)]
    if mod3 is not None:
        in_specs += [pl.BlockSpec((None, 1, d), lambda i: (seg_of_tile(i) * 6 + shift_idx, 0, 0)),
                     pl.BlockSpec((None, 1, d), lambda i: (seg_of_tile(i) * 6 + scale_idx, 0, 0))]
        args += [mod3, mod3]
    return pl.pallas_call(
        functools.partial(_norm_body, modulate=mod3 is not None),
        out_shape=jax.ShapeDtypeStruct((rows, d), out_dtype),
        grid=(rows // tr,),
        in_specs=in_specs,
        out_specs=pl.BlockSpec((tr, d), lambda i: (i, 0)),
        compiler_params=_cp("parallel"),
        name="rms_norm",
    )(*args)


def _mm_body(*refs, nk, n_extra, n_skip, epilogue):
    a_ref, b_ref = refs[0], refs[1]
    extra = refs[2:2 + n_extra]
    o_ref = refs[2 + n_extra + n_skip]
    row_tile = pl.program_id(1)
    b = b_ref[...]
    if b.ndim == 3:
        b = b.reshape(-1, b.shape[-1])
    part = jnp.dot(a_ref[...].astype(b.dtype), b, preferred_element_type=F32)
    if nk == 1:
        o_ref[...] = epilogue(part, *extra, row_tile=row_tile).astype(o_ref.dtype).reshape(o_ref.shape)
        return
    acc_ref = refs[3 + n_extra + n_skip]
    k = pl.program_id(3)

    @pl.when(k == 0)
    def _():
        acc_ref[...] = part

    @pl.when(k > 0)
    def _():
        acc_ref[...] += part

    @pl.when(k == nk - 1)
    def _():
        o_ref[...] = epilogue(acc_ref[...], *extra, row_tile=row_tile).astype(o_ref.dtype)


def matmul(a, b, *, grid, tm, tn, a_spec, b_spec, o_spec, out_shape, out_dtype,
           extra=(), extra_specs=(), epilogue=None, fill=None, name="matmul"):
    nk = grid[3]
    epilogue = epilogue or (lambda acc, row_tile: acc)
    scratch = [pltpu.VMEM((tm, tn), F32)] if nk > 1 else []
    fills = () if fill is None else (fill,)
    return pl.pallas_call(
        functools.partial(_mm_body, nk=nk, n_extra=len(extra), n_skip=len(fills), epilogue=epilogue),
        out_shape=jax.ShapeDtypeStruct(out_shape, out_dtype),
        grid=grid,
        in_specs=[a_spec, b_spec, *extra_specs, *[pl.BlockSpec(memory_space=pl.ANY) for _ in fills]],
        out_specs=o_spec,
        scratch_shapes=scratch,
        input_output_aliases={2 + len(extra): 0} if fills else {},
        compiler_params=_cp("parallel", "parallel", "parallel", "arbitrary"),
        name=name,
    )(a, b, *extra, *fills)


def dense(a, w, *, out_dtype, tm=1024, tn=512, tk=2048, extra=(), extra_specs=(), epilogue=None,
          name="dense"):
    m, k = a.shape
    n = w.shape[1]
    tm, tn, tk = _tile(m, tm), _tile(n, tn), _tile(k, tk)
    return matmul(
        a, w, grid=(1, m // tm, n // tn, k // tk), tm=tm, tn=tn,
        a_spec=pl.BlockSpec((tm, tk), lambda bt, i, j, kk: (i, kk)),
        b_spec=pl.BlockSpec((tk, tn), lambda bt, i, j, kk: (kk, j)),
        o_spec=pl.BlockSpec((tm, tn), lambda bt, i, j, kk: (i, j)),
        out_shape=(m, n), out_dtype=out_dtype, extra=extra, extra_specs=extra_specs,
        epilogue=epilogue, name=name)


def dense_residual(a, w, x, mod3, gate_idx, seg_of_rows, seg_rows, *, tm=1024, tn=512, tk=2048, name="dense_res"):
    n = w.shape[1]
    tm_ = _tile(seg_rows, tm)
    tn_ = _tile(n, tn)
    return dense(
        a, w, out_dtype=F32, tm=tm_, tn=tn, tk=tk, extra=(x, mod3),
        extra_specs=(pl.BlockSpec((tm_, tn_), lambda bt, i, j, kk: (i, j)),
                     pl.BlockSpec((None, 1, tn_),
                                  lambda bt, i, j, kk: (seg_of_rows(i * tm_) * 6 + gate_idx, 0, j))),
        epilogue=lambda acc, x_ref, g_ref, row_tile: x_ref[...] + g_ref[...] * acc, name=name)


def _dwconv_body(flags_ref, x_ref, p_ref, n_ref, w_ref, b_ref, o_ref, *, n_plain):
    i = pl.program_id(0)
    j = pl.program_id(1)
    x = x_ref[...].astype(F32)
    tr = x.shape[0]
    row = lax.broadcasted_iota(jnp.int32, x.shape, 0)
    keep_prev = (flags_ref[0, i] == 0).astype(F32)
    keep_next = (flags_ref[1, i] == 0).astype(F32)
    halo_prev = p_ref[SUBLANES - 1:SUBLANES, :].astype(F32) * keep_prev
    halo_next = n_ref[0:1, :].astype(F32) * keep_next
    prev = jnp.where(row == 0, halo_prev, pltpu.roll(x, 1, 0))
    nxt = jnp.where(row == tr - 1, halo_next, pltpu.roll(x, tr - 1, 0))
    w = w_ref[...]
    y = prev * w[0:1, :] + x * w[1:2, :] + nxt * w[2:3, :] + b_ref[...]
    y = jnp.where(j >= n_plain, _silu(y), y)
    o_ref[...] = y.astype(o_ref.dtype)


def dwconv(proj, w, b, flags, *, n_plain_cols, gap_cols, out_cols):
    t = proj.shape[0]
    tr = ROW_TILE
    tc = _tile(math.gcd(n_plain_cols, gap_cols, out_cols), 2048)
    n_plain, gap = n_plain_cols // tc, gap_cols // tc
    hb = tr // SUBLANES
    last_hb = t // SUBLANES - 1

    def src(j):
        return j + jnp.where(j >= n_plain, gap, 0)

    grid_spec = pltpu.PrefetchScalarGridSpec(
        num_scalar_prefetch=1,
        grid=(t // tr, out_cols // tc),
        in_specs=[pl.BlockSpec((tr, tc), lambda i, j, f: (i, src(j))),
                  pl.BlockSpec((SUBLANES, tc), lambda i, j, f: (jnp.maximum(i * hb - 1, 0), src(j))),
                  pl.BlockSpec((SUBLANES, tc), lambda i, j, f: (jnp.minimum((i + 1) * hb, last_hb), src(j))),
                  pl.BlockSpec((3, tc), lambda i, j, f: (0, j)),
                  pl.BlockSpec((1, tc), lambda i, j, f: (0, j))],
        out_specs=pl.BlockSpec((tr, tc), lambda i, j, f: (i, j)))
    return pl.pallas_call(
        functools.partial(_dwconv_body, n_plain=n_plain),
        out_shape=jax.ShapeDtypeStruct((t, out_cols), BF16),
        grid_spec=grid_spec,
        compiler_params=_cp("parallel", "parallel"),
        name="dwconv",
    )(flags, proj, proj, proj, w, b.reshape(1, -1))


def _filter_body(feat_ref, w1_ref, b1_ref, f1_ref, w2_ref, b2_ref, f2_ref, w3_ref, dl_ref, o_ref):
    d = pl.program_id(0)
    i = pl.program_id(1)
    feat = feat_ref[...]
    dn = (((1,), (0,)), ((), ()))
    hid = jnp.sin(f1_ref[...] * (lax.dot_general(feat, w1_ref[...], dn, precision=HI,
                                                 preferred_element_type=F32) + b1_ref[...]))
    hid = jnp.sin(f2_ref[...] * (lax.dot_general(hid, w2_ref[...], dn, precision=HI,
                                                 preferred_element_type=F32) + b2_ref[...]))
    filt = lax.dot_general(hid, w3_ref[...], dn, precision=HI, preferred_element_type=F32)
    t01 = feat[:, 0:1]
    filt = filt * jnp.exp(-t01 * dl_ref[...])
    row = lax.broadcasted_iota(jnp.int32, filt.shape, 0) + i * filt.shape[0]
    filt = jnp.where((d == 1) & (row == 0), 0.0, filt)
    o_ref[...] = filt.astype(o_ref.dtype)


def hyena_filters(L, f_w1, f_b1, f_freq1, f_w2, f_b2, f_freq2, f_w3, c):
    t = jnp.arange(L, dtype=F32)
    t01 = t / (L - 1)
    bands = jnp.linspace(1e-4, HYENA_BANDS - 1, HYENA_BANDS, dtype=F32)
    ang = (2.0 * math.pi / L) * t[:, None] * bands[None, :]
    feat = jnp.concatenate([t01[:, None], jnp.cos(ang), -jnp.sin(ang)], axis=-1)
    emb = feat.shape[1]
    feat = jnp.pad(feat, ((0, 0), (0, LANES - emb)))
    w1 = jnp.pad(f_w1, ((0, LANES - emb), (0, 0)))
    ffn = f_w1.shape[1]
    order = f_w3.shape[1] // (2 * c)
    deltas = jnp.abs(jnp.linspace(math.log(HYENA_TARGET) / HYENA_SLOW_DECAY,
                                  math.log(HYENA_TARGET) / HYENA_FAST_DECAY, c, dtype=F32))
    deltas = jnp.tile(deltas, order).reshape(1, order * c)
    tl, tn = _tile(L, 512), _tile(order * c, 1024)
    nj = order * c // tn
    return pl.pallas_call(
        _filter_body,
        out_shape=jax.ShapeDtypeStruct((2, L, order * c), BF16),
        grid=(2, L // tl, nj),
        in_specs=[pl.BlockSpec((tl, LANES), lambda d, i, j: (i, 0)),
                  pl.BlockSpec((LANES, ffn), lambda d, i, j: (0, 0)),
                  pl.BlockSpec((1, ffn), lambda d, i, j: (0, 0)),
                  pl.BlockSpec((1, ffn), lambda d, i, j: (0, 0)),
                  pl.BlockSpec((ffn, ffn), lambda d, i, j: (0, 0)),
                  pl.BlockSpec((1, ffn), lambda d, i, j: (0, 0)),
                  pl.BlockSpec((1, ffn), lambda d, i, j: (0, 0)),
                  pl.BlockSpec((ffn, tn), lambda d, i, j: (0, d * nj + j)),
                  pl.BlockSpec((1, tn), lambda d, i, j: (0, j))],
        out_specs=pl.BlockSpec((None, tl, tn), lambda d, i, j: (d, i, j)),
        compiler_params=_cp("parallel", "parallel", "parallel"),
        name="hyena_filters",
    )(feat, w1, f_b1.reshape(1, ffn), f_freq1.reshape(1, ffn), f_w2, f_b2.reshape(1, ffn),
      f_freq2.reshape(1, ffn), f_w3, deltas)


def _cos_turns(m, log2_period):
    period = 1 << log2_period
    quarter = period // 4
    mm = m + quarter // 2
    q = (mm >> (log2_period - 2)) & 3
    r = (mm & (quarter - 1)) - quarter // 2
    phi = r.astype(F32) * (2.0 * math.pi / period)
    p2 = phi * phi
    cosv = 1.0 + p2 * (-1.0 / 2 + p2 * (1.0 / 24 + p2 * (-1.0 / 720 + p2 * (1.0 / 40320 - p2 / 3628800))))
    sinv = phi * (1.0 + p2 * (-1.0 / 6 + p2 * (1.0 / 120 + p2 * (-1.0 / 5040 + p2 / 362880))))
    return jnp.where(q == 0, cosv, jnp.where(q == 1, -sinv, jnp.where(q == 2, -cosv, sinv)))


def _spec_index(r, L):
    blk = r >> 8
    within = r & (2 * SPEC_BLOCK - 1)
    return blk * SPEC_BLOCK + (within & (SPEC_BLOCK - 1)), within >= SPEC_BLOCK


def _dft_body(o_ref, *, L, mode):
    log2n = int(math.log2(2 * L))
    shape = o_ref.shape
    i0 = pl.program_id(0) * shape[0]
    j0 = pl.program_id(1) * shape[1]
    rows = lax.broadcasted_iota(jnp.int32, shape, 0) + i0
    cols = lax.broadcasted_iota(jnp.int32, shape, 1) + j0
    if mode == "inv":
        r, n = cols, rows
    else:
        r, n = rows, cols
    k, im = _spec_index(r, L)
    back = n >= L
    n = jnp.where(back, n - L, n)
    phase = jnp.where(im, jnp.where(back, -(L // 2), L // 2), 0)
    val = _cos_turns((k * n + phase) & (2 * L - 1), log2n)
    nyq = jnp.where((n & 1) == 0, 1.0, -1.0)
    val = jnp.where(im & (k == 0), nyq, val)
    if mode == "inv":
        val = val * jnp.where(k == 0, 1.0 / (2 * L), 2.0 / (2 * L))
    o_ref[...] = val.astype(o_ref.dtype)


def dft_table(L, mode):
    assert L & (L - 1) == 0 and L >= 2 * SPEC_BLOCK
    shape = {"fwd": (2 * L, L), "filt": (2 * L, 2 * L), "inv": (L, 2 * L)}[mode]
    tr, tc = _tile(shape[0], 512), _tile(shape[1], 1024)
    return pl.pallas_call(
        functools.partial(_dft_body, L=L, mode=mode),
        out_shape=jax.ShapeDtypeStruct(shape, BF16),
        grid=(shape[0] // tr, shape[1] // tc),
        out_specs=pl.BlockSpec((tr, tc), lambda i, j: (i, j)),
        compiler_params=_cp("parallel", "parallel"),
        name="dft_table_" + mode,
    )()


def _spectral_epilogue(acc, kf_ref, row_tile):
    tm, tn = acc.shape
    nb = tm // (2 * SPEC_BLOCK)
    z = acc.reshape(nb, 2, SPEC_BLOCK, tn)
    kf = kf_ref[...].reshape(nb, 2, SPEC_BLOCK, tn)
    zr, zi, kr, ki = z[:, 0], z[:, 1], kf[:, 0], kf[:, 1]
    first = row_tile == 0
    blk = lax.broadcasted_iota(jnp.int32, zr.shape, 0)
    row = lax.broadcasted_iota(jnp.int32, zr.shape, 1)
    dc = first & (blk == 0) & (row == 0)
    pr = zr * kr - jnp.where(dc, 0.0, zi * ki)
    pi = jnp.where(dc, zi * ki, zr * ki + zi * kr)
    return jnp.stack([pr, pi], axis=1).reshape(tm, tn)


def hyena_stream(u, row0, nb, L, kf, f_tab, g_tab, hy_bias, c, fill=None):
    out_rows = u.shape[0]
    tn = 512
    cb = c // tn
    tm_f = _tile(2 * L, 512)
    tk_f = _tile(L, 2048)
    tm_i = _tile(L, 512)
    tk_i = _tile(2 * L, 2048)
    z, z_rows, z_col = u, row0, 2 * cb
    for o in range(2):
        zrb = z_rows // tk_f
        p = matmul(
            f_tab, z, grid=(nb, 2 * L // tm_f, cb, L // tk_f), tm=tm_f, tn=tn,
            a_spec=pl.BlockSpec((tm_f, tk_f), lambda bt, i, j, k: (i, k)),
            b_spec=pl.BlockSpec((tk_f, tn), lambda bt, i, j, k, zrb=zrb, zc=z_col:
                                (zrb + bt * (L // tk_f) + k, zc + j)),
            o_spec=pl.BlockSpec((None, tm_f, tn), lambda bt, i, j, k: (bt, i, j)),
            out_shape=(nb, 2 * L, c), out_dtype=BF16,
            extra=(kf,), extra_specs=(pl.BlockSpec((tm_f, tn), lambda bt, i, j, k, o=o: (i, o * cb + j)),),
            epilogue=_spectral_epilogue, name="hyena_fwd")
        zrb_i = z_rows // tm_i
        urb_i = row0 // tm_i
        last = o == 1
        orb = urb_i if last else 0
        z_new = matmul(
            g_tab, p, grid=(nb, L // tm_i, cb, 2 * L // tk_i), tm=tm_i, tn=tn,
            a_spec=pl.BlockSpec((tm_i, tk_i), lambda bt, i, j, k: (i, k)),
            b_spec=pl.BlockSpec((None, tk_i, tn), lambda bt, i, j, k: (bt, k, j)),
            o_spec=pl.BlockSpec((tm_i, tn), lambda bt, i, j, k, orb=orb: (orb + bt * (L // tm_i) + i, j)),
            out_shape=(out_rows if last else nb * L, c), out_dtype=BF16, fill=fill if last else None,
            extra=(u, z, hy_bias.reshape(2, 1, c)),
            extra_specs=(pl.BlockSpec((tm_i, tn), lambda bt, i, j, k, o=o:
                                      (urb_i + bt * (L // tm_i) + i, o * cb + j)),
                         pl.BlockSpec((tm_i, tn), lambda bt, i, j, k, zrb_i=zrb_i, zc=z_col:
                                      (zrb_i + bt * (L // tm_i) + i, zc + j)),
                         pl.BlockSpec((None, 1, tn), lambda bt, i, j, k, o=o: (o, 0, j))),
            epilogue=lambda acc, g_ref, z_ref, b_ref, row_tile:
                g_ref[...].astype(F32) * (acc + b_ref[...] * z_ref[...].astype(F32)),
            name="hyena_inv")
        z, z_rows, z_col = z_new, 0, 0
    return z


def hyena_spectrum(L, filt, a_tab):
    n = filt.shape[2]
    b = filt.reshape(2 * L, n)
    tm, tn, tk = _tile(2 * L, 512), _tile(n, 512), _tile(2 * L, 2048)
    return matmul(
        a_tab, b, grid=(1, 2 * L // tm, n // tn, 2 * L // tk), tm=tm, tn=tn,
        a_spec=pl.BlockSpec((tm, tk), lambda bt, i, j, k: (i, k)),
        b_spec=pl.BlockSpec((tk, tn), lambda bt, i, j, k: (k, j)),
        o_spec=pl.BlockSpec((tm, tn), lambda bt, i, j, k: (i, j)),
        out_shape=(2 * L, n), out_dtype=F32, name="hyena_spectrum")


def _fft_tab_body(o_ref, *, kind, n1, n2, k1_len):
    shape = o_ref.shape
    r = lax.broadcasted_iota(jnp.int32, shape, 0) + pl.program_id(0) * shape[0]
    c = lax.broadcasted_iota(jnp.int32, shape, 1)
    n = n1 * n2
    lg = lambda v: int(math.log2(v))
    s8 = BF16_SUBLANES
    if kind in ("t1f", "t1i"):
        if kind == "t1f":
            grp, kp, s, n1i, s2 = r >> lg(2 * n1 * s8), (r >> lg(s8)) & (2 * n1 - 1), r & (s8 - 1), c >> lg(s8), c & (s8 - 1)
        else:
            grp, n1i, s, kp, s2 = r >> lg(k1_len * s8), (r >> lg(s8)) & (k1_len - 1), r & (s8 - 1), c >> lg(s8), c & (s8 - 1)
        t = n2 * n1i + grp * s8 + s
        val = _cos_turns(((kp & (n1 - 1)) * t + (kp >> lg(n1)) * (n // 4)) & (n - 1), lg(n))
        val = jnp.where(s == s2, val * (1.0 / n if kind == "t1i" else 1.0), 0.0)
    else:
        rp = r >= n2
        cblk = c >> lg(n2)
        cp = (cblk & 1) == 1
        sin_sign = -1 if kind == "f2i" else 1
        quarter = jnp.where(rp == cp, 0, jnp.where(cp, -sin_sign, sin_sign))
        val = _cos_turns(((r & (n2 - 1)) * (c & (n2 - 1)) + quarter * (n2 // 4)) & (n2 - 1), lg(n2))
        if kind == "f2filt":
            val = jnp.where((cblk >= 2) & rp, -val, val)
    o_ref[...] = val.astype(o_ref.dtype)


def fft_table(kind, n1, n2):
    k1_len = n1 // 2
    s8 = BF16_SUBLANES
    groups = n2 // s8
    shape = {"t1f": (groups * 2 * n1 * s8, k1_len * s8), "t1i": (groups * k1_len * s8, 2 * n1 * s8),
             "f2f": (2 * n2, 2 * n2), "f2i": (2 * n2, 2 * n2), "f2filt": (2 * n2, 4 * n2)}[kind]
    tr = _tile(shape[0], 1024)
    out = pl.pallas_call(
        functools.partial(_fft_tab_body, kind=kind, n1=n1, n2=n2, k1_len=k1_len),
        out_shape=jax.ShapeDtypeStruct(shape, BF16),
        grid=(shape[0] // tr,),
        out_specs=pl.BlockSpec((tr, shape[1]), lambda i: (i, 0)),
        compiler_params=_cp("parallel"),
        name="fft_table_" + kind,
    )()
    if kind in ("t1f", "t1i"):
        return out.reshape(groups, shape[0] // groups, shape[1])
    return out


FFT_K1_PER_STEP = 4


def _fft_mid_body(f_ref, *refs, mode, n2):
    f = f_ref[...]
    if mode == "inv":
        p_ref, o_ref = refs
        for k in range(FFT_K1_PER_STEP):
            res = jnp.dot(f, p_ref[k * 2 * n2:(k + 1) * 2 * n2, :], preferred_element_type=F32)
            o_ref[0, k * n2:(k + 1) * n2, :] = res[:n2].astype(o_ref.dtype)
            o_ref[1, k * n2:(k + 1) * n2, :] = res[n2:].astype(o_ref.dtype)
        return
    if mode == "fwd":
        y_ref, kf_ref, o_ref = refs
        srcs = (y_ref,)
    else:
        y0_ref, y1_ref, o_ref = refs
        srcs = (y0_ref, y1_ref)
    for k in range(FFT_K1_PER_STEP):
        rows = slice(k * n2, (k + 1) * n2)
        stack = jnp.concatenate([s[part, rows, :] for s in srcs for part in range(2)], axis=0)
        acc = jnp.dot(f, stack, preferred_element_type=F32)
        out_rows = slice(k * 2 * n2, (k + 1) * 2 * n2)
        if mode == "fwd":
            zr, zi = acc[:n2], acc[n2:]
            kr = kf_ref[k * 2 * n2:k * 2 * n2 + n2, :]
            ki = kf_ref[k * 2 * n2 + n2:(k + 1) * 2 * n2, :]
            acc = jnp.concatenate([zr * kr - zi * ki, zr * ki + zi * kr], axis=0)
        o_ref[out_rows, :] = acc.astype(o_ref.dtype)


def fft_mid(mode, f_tab, src, n1, n2, *, kf=None, kf_col0=0, out_dtype=BF16, name):
    kb = FFT_K1_PER_STEP
    c = src.shape[-1]
    tn = _tile(c, 1024)
    cb = c // tn
    nb = 1 if mode == "filt" else src.shape[0]
    y_block = (None, 2, kb * n2, tn)
    p_block = (None, kb * 2 * n2, tn)
    f_spec = pl.BlockSpec(f_tab.shape, lambda i, b, j: (0, 0))
    if mode == "fwd":
        in_specs = [f_spec, pl.BlockSpec(y_block, lambda i, b, j: (b, 0, i, j)),
                    pl.BlockSpec((kb * 2 * n2, tn), lambda i, b, j: (i, kf_col0 // tn + j))]
        args = (f_tab, src, kf)
        out_shape, out_spec = (nb, n1 * 2 * n2, c), pl.BlockSpec(p_block, lambda i, b, j: (b, i, j))
    elif mode == "filt":
        in_specs = [f_spec, pl.BlockSpec(y_block, lambda i, b, j: (0, 0, i, j)),
                    pl.BlockSpec(y_block, lambda i, b, j: (1, 0, i, j))]
        args = (f_tab, src, src)
        out_shape, out_spec = (n1 * 2 * n2, c), pl.BlockSpec((kb * 2 * n2, tn), lambda i, b, j: (i, j))
    else:
        in_specs = [f_spec, pl.BlockSpec(p_block, lambda i, b, j: (b, i, j))]
        args = (f_tab, src)
        out_shape, out_spec = (nb, 2, n1 * n2, c), pl.BlockSpec(y_block, lambda i, b, j: (b, 0, i, j))
    return pl.pallas_call(
        functools.partial(_fft_mid_body, mode=mode, n2=n2),
        out_shape=jax.ShapeDtypeStruct(out_shape, out_dtype),
        grid=(n1 // kb, nb, cb),
        in_specs=in_specs,
        out_specs=out_spec,
        compiler_params=_cp("parallel", "parallel", "parallel"),
        name=name,
    )(*args)


def hyena_spectrum_2level(L, filt, t1f, f2filt):
    n2 = SPEC_BLOCK
    n1 = 2 * L // n2
    k1 = n1 // 2
    n = filt.shape[2]
    tn = _tile(n, 1024)
    nj = n // tn
    sl = BF16_SUBLANES
    y1 = matmul(
        t1f, filt.reshape(2 * k1, n2, n), grid=(2, n2 // sl, nj, 1), tm=2 * n1 * sl, tn=tn,
        a_spec=pl.BlockSpec((None, 2 * n1 * sl, k1 * sl), lambda bt, i, j, k: (i, 0, 0)),
        b_spec=pl.BlockSpec((k1, sl, tn), lambda bt, i, j, k: (bt, i, j)),
        o_spec=pl.BlockSpec((None, 2 * n1, sl, tn), lambda bt, i, j, k: (bt, 0, i, j)),
        out_shape=(2, 2 * n1, n2, n), out_dtype=BF16, name="hyena_filt_fft1")
    return fft_mid("filt", f2filt, y1.reshape(2, 2, n1 * n2, n), n1, n2, out_dtype=F32, name="hyena_filt_fft2")


def hyena_stream_2level(u, row0, nb, L, kf, tabs, hy_bias, c, fill=None):
    t1f, f2f, f2i, t1i = tabs
    n2 = SPEC_BLOCK
    n1 = 2 * L // n2
    k1 = n1 // 2
    sl = BF16_SUBLANES
    ng = n2 // sl
    t1n = _tile(c, 1024)
    c1b = c // t1n
    tn = _tile(c, 2048)
    cb = c // tn
    t, u_w = u.shape
    assert row0 % L == 0 and u_w % tn == 0
    uv = u.reshape(t // n2, n2, u_w)
    urow = row0 // L
    zv, zr, zc = uv, urow, 2 * c // t1n
    bias3 = hy_bias.reshape(2, 1, c)
    for o in range(2):
        y1 = matmul(
            t1f, zv, grid=(nb, ng, c1b, 1), tm=2 * n1 * sl, tn=t1n,
            a_spec=pl.BlockSpec((None, 2 * n1 * sl, k1 * sl), lambda bt, i, j, k: (i, 0, 0)),
            b_spec=pl.BlockSpec((k1, sl, t1n), lambda bt, i, j, k, zr=zr, zc=zc: (zr + bt, i, zc + j)),
            o_spec=pl.BlockSpec((None, 2 * n1, sl, t1n), lambda bt, i, j, k: (bt, 0, i, j)),
            out_shape=(nb, 2 * n1, n2, c), out_dtype=BF16, name="hyena_fft1")
        p = fft_mid("fwd", f2f, y1.reshape(nb, 2, n1 * n2, c), n1, n2, kf=kf, kf_col0=o * c, name="hyena_fft2")
        u1 = fft_mid("inv", f2i, p, n1, n2, name="hyena_ifft1")
        last = o == 1
        orow = urow if last else 0

        def gate_epilogue(acc, g_ref, z_ref, b_ref, row_tile):
            g = g_ref[...].reshape(acc.shape).astype(F32)
            return g * (acc + b_ref[...] * z_ref[...].reshape(acc.shape).astype(F32))

        z = matmul(
            t1i, u1.reshape(nb, 2 * n1, n2, c), grid=(nb, ng, c1b, 1), tm=k1 * sl, tn=t1n,
            a_spec=pl.BlockSpec((None, k1 * sl, 2 * n1 * sl), lambda bt, i, j, k: (i, 0, 0)),
            b_spec=pl.BlockSpec((None, 2 * n1, sl, t1n), lambda bt, i, j, k: (bt, 0, i, j)),
            o_spec=pl.BlockSpec((k1, sl, t1n), lambda bt, i, j, k, orow=orow: (orow + bt, i, j)),
            out_shape=(t // n2 if last else nb * k1, n2, c), out_dtype=BF16,
            fill=None if (fill is None or not last) else fill.reshape(t // n2, n2, c),
            extra=(uv, zv, bias3),
            extra_specs=(pl.BlockSpec((k1, sl, t1n), lambda bt, i, j, k, o=o: (urow + bt, i, o * c1b + j)),
                         pl.BlockSpec((k1, sl, t1n), lambda bt, i, j, k, zr=zr, zc=zc: (zr + bt, i, zc + j)),
                         pl.BlockSpec((None, 1, t1n), lambda bt, i, j, k, o=o: (o, 0, j))),
            epilogue=gate_epilogue, name="hyena_ifft2")
        zv, zr, zc = z, 0, 0
    return zv.reshape(t, c)


def _ssd_prep_body(dc_ref, dr_ref, biasc_ref, biasr_ref, ac_ref, ar_ref, dt_ref, cumc_ref, cumr_ref, *, heads):
    q = dc_ref.shape[0]
    dn = (((1,), (0,)), ((), ()))
    li = lax.broadcasted_iota(jnp.int32, (q, q), 0)
    si = lax.broadcasted_iota(jnp.int32, (q, q), 1)
    lower = (li >= si).astype(F32)
    upper = (li <= si).astype(F32)
    dt_c = _softplus(dc_ref[...] + biasc_ref[...])
    da_c = dt_c * ac_ref[...]
    da_r = _softplus(dr_ref[...] + biasr_ref[...]) * ar_ref[...]
    pre_c = lax.dot_general(lower, da_c, dn, precision=HI, preferred_element_type=F32)
    suf_c = lax.dot_general(upper, da_c, dn, precision=HI, preferred_element_type=F32)
    pre_r = lax.dot_general(da_r, upper, dn, precision=HI, preferred_element_type=F32)
    suf_r = lax.dot_general(da_r, lower, dn, precision=HI, preferred_element_type=F32)
    dt_ref[...] = dt_c
    cumc_ref[...] = jnp.where(lax.broadcasted_iota(jnp.int32, pre_c.shape, 1) < heads, pre_c, suf_c)
    cumr_ref[...] = jnp.where(lax.broadcasted_iota(jnp.int32, pre_r.shape, 0) < heads, pre_r, suf_r)


def ssd_prep(dt_raw, dt_raw_t, dt_bias, a_log):
    t, h2 = dt_raw.shape
    q = SSD_CHUNK
    neg_a = -jnp.exp(a_log.astype(F32)).reshape(1, h2)
    bias = dt_bias.astype(F32).reshape(1, h2)
    const = lambda shape: pl.BlockSpec(shape, lambda i: (0,) * len(shape))
    col = pl.BlockSpec((q, h2), lambda i: (i, 0))
    row = pl.BlockSpec((h2, q), lambda i: (0, i))
    return pl.pallas_call(
        functools.partial(_ssd_prep_body, heads=h2 // 2),
        out_shape=(jax.ShapeDtypeStruct((t, h2), F32), jax.ShapeDtypeStruct((t, h2), F32),
                   jax.ShapeDtypeStruct((h2, t), F32)),
        grid=(t // q,),
        in_specs=[col, row, const((1, h2)), const((h2, 1)), const((1, h2)), const((h2, 1))],
        out_specs=(col, col, row),
        compiler_params=_cp("parallel"),
        name="ssd_prep",
    )(dt_raw, dt_raw_t, bias, bias.reshape(h2, 1), neg_a, neg_a.reshape(h2, 1))


def _expand_heads(x, head0, n_heads, width):
    rows = lax.broadcasted_iota(jnp.int32, (x.shape[1], n_heads * width), 0)
    lanes = lax.broadcasted_iota(jnp.int32, (x.shape[1], n_heads * width), 1)
    sel = jnp.where(rows == head0 + (lanes >> int(math.log2(width))), 1.0, 0.0).astype(BF16)
    out = None
    rest = x
    for _ in range(3):
        piece = rest.astype(BF16)
        rest = rest - piece.astype(F32)
        part = jnp.dot(piece, sel, preferred_element_type=F32)
        out = part if out is None else out + part
    return out


def _ssd_dir(xs, bt, cm, dt_c, cum_c, cumr_ref, state_ref, head0, *, reverse, heads_per_group):
    q = xs.shape[0]
    p = SSD_HEAD_DIM
    j = heads_per_group
    cc = _expand_heads(cum_c, head0, j, p)
    cc_wide = _expand_heads(cum_c, head0, j, q)
    dtv = _expand_heads(dt_c, head0, j, p)
    total = cc[0:1, :] if reverse else cc[q - 1:q, :]
    xd = xs.astype(F32) * dtv
    xd_b = xd.astype(BF16)
    li = lax.broadcasted_iota(jnp.int32, (q, q), 0)
    si = lax.broadcasted_iota(jnp.int32, (q, q), 1)
    mask = (li <= si) if reverse else (li >= si)
    cb = jnp.dot(cm, bt, preferred_element_type=F32)
    ys = []
    for jh in range(j):
        cr = cumr_ref[pl.ds(head0 + jh, 1), :]
        decay = jnp.exp(jnp.where(mask, cc_wide[:, jh * q:(jh + 1) * q] - cr, -jnp.inf))
        ys.append(jnp.dot((cb * decay).astype(BF16), xd_b[:, jh * p:(jh + 1) * p], preferred_element_type=F32))
    s_prev = state_ref[...]
    y = jnp.concatenate(ys, axis=1) + jnp.exp(cc) * jnp.dot(cm, s_prev.astype(BF16), preferred_element_type=F32)
    st = jnp.dot(bt, (xd * jnp.exp(total - cc)).astype(BF16), preferred_element_type=F32)
    state_ref[...] = jnp.exp(total) * s_prev + st
    return y


def _ssd_body(tab_ref, xf_ref, btf_ref, cf_ref, dtf_ref, ccf_ref, crf_ref, xb_ref, btb_ref, cb_ref, dtb_ref, ccb_ref,
              crb_ref, initf_ref, initb_ref, yf_ref, yb_ref, sf_ref, sb_ref, stf_ref, stb_ref,
              *, heads_per_group, heads):
    g = pl.program_id(0)
    item = pl.program_id(1)

    @pl.when(tab_ref[_SSD_FIRST, item] == 1)
    def _():
        stf_ref[...] = initf_ref[...]
        stb_ref[...] = initb_ref[...]

    yf_ref[...] = _ssd_dir(xf_ref[...], btf_ref[...], cf_ref[...], dtf_ref[...], ccf_ref[...], crf_ref, stf_ref,
                           g * heads_per_group, reverse=False, heads_per_group=heads_per_group)
    yb_ref[...] = _ssd_dir(xb_ref[...], btb_ref[...], cb_ref[...], dtb_ref[...], ccb_ref[...], crb_ref, stb_ref,
                           heads + g * heads_per_group, reverse=True, heads_per_group=heads_per_group)

    @pl.when(tab_ref[_SSD_LAST, item] == 1)
    def _():
        sf_ref[...] = stf_ref[...]
        sb_ref[...] = stb_ref[...]


_SSD_FWD, _SSD_BWD, _SSD_SEQ, _SSD_FIRST, _SSD_LAST = range(5)


def ssd_scan(u, bt_all, dt, cum_c, cum_r, init_f, init_b, *, seq_lens, col0, inner):
    q, p, n, gq = SSD_CHUNK, SSD_HEAD_DIM, SSD_STATE, SSD_GROUPS
    heads = inner // p
    j = heads // gq
    jp = j * p
    xcb = col0 // jp
    ccb = (col0 + inner + gq * n) // n
    rows, base = [], 0
    for s, length in enumerate(seq_lens):
        nc = length // q
        rows += [(base + c, base + nc - 1 - c, s, int(c == 0), int(c == nc - 1)) for c in range(nc)]
        base += nc
    table = jnp.array(rows, jnp.int32).T
    nseq, n_items = len(seq_lens), len(rows)

    def specs(which):
        return [pl.BlockSpec((q, jp), lambda g, it, tab: (tab[which, it], xcb + g)),
                pl.BlockSpec((n, q), lambda g, it, tab: (g, tab[which, it])),
                pl.BlockSpec((q, n), lambda g, it, tab: (tab[which, it], ccb + g)),
                pl.BlockSpec((q, 2 * heads), lambda g, it, tab: (tab[which, it], 0)),
                pl.BlockSpec((q, 2 * heads), lambda g, it, tab: (tab[which, it], 0)),
                pl.BlockSpec((2 * heads, q), lambda g, it, tab: (0, tab[which, it]))]

    st_spec = pl.BlockSpec((None, None, n, jp), lambda g, it, tab: (tab[_SSD_SEQ, it], g, 0, 0))
    st_shape = jax.ShapeDtypeStruct((nseq, gq, n, jp), F32)
    y_shape = jax.ShapeDtypeStruct((u.shape[0], inner), F32)
    grid_spec = pltpu.PrefetchScalarGridSpec(
        num_scalar_prefetch=1,
        grid=(gq, n_items),
        in_specs=[*specs(_SSD_FWD), *specs(_SSD_BWD), st_spec, st_spec],
        out_specs=(pl.BlockSpec((q, jp), lambda g, it, tab: (tab[_SSD_FWD, it], g)),
                   pl.BlockSpec((q, jp), lambda g, it, tab: (tab[_SSD_BWD, it], g)), st_spec, st_spec),
        scratch_shapes=[pltpu.VMEM((n, jp), F32), pltpu.VMEM((n, jp), F32)])
    return pl.pallas_call(
        functools.partial(_ssd_body, heads_per_group=j, heads=heads),
        out_shape=(y_shape, y_shape, st_shape, st_shape),
        grid_spec=grid_spec,
        compiler_params=_cp("parallel", "arbitrary"),
        name="ssd_scan",
    )(table, u, bt_all, u, dt, cum_c, cum_r, u, bt_all, u, dt, cum_c, cum_r, init_f, init_b)


def _state_to_kernel_layout(s, groups):
    b, h, p, n = s.shape
    return s.reshape(b, groups, h // groups, p, n).transpose(0, 1, 4, 2, 3).reshape(b, groups, n, (h // groups) * p)


def _state_from_kernel_layout(s, head_dim):
    b, g, n, jp = s.shape
    j = jp // head_dim
    return s.reshape(b, g, n, j, head_dim).transpose(0, 1, 3, 4, 2).reshape(b, g * j, head_dim, n)


def _ssd_finish_body(yf_ref, yb_ref, xs_ref, z_ref, d_ref, w_ref, o_ref):
    y = yf_ref[...] + yb_ref[...] + xs_ref[...].astype(F32) * d_ref[...]
    y = y * _silu(z_ref[...].astype(F32))
    y = y * lax.rsqrt(jnp.mean(y * y, axis=-1, keepdims=True) + RMS_EPS)
    o_ref[...] = (y * w_ref[...]).astype(o_ref.dtype)


def ssd_finish(yf, yb, u, proj, d_full, norm_w, *, xs_col, z_col, inner):
    t = yf.shape[0]
    tr = ROW_TILE
    return pl.pallas_call(
        _ssd_finish_body,
        out_shape=jax.ShapeDtypeStruct((t, inner), BF16),
        grid=(t // tr,),
        in_specs=[pl.BlockSpec((tr, inner), lambda i: (i, 0)),
                  pl.BlockSpec((tr, inner), lambda i: (i, 0)),
                  pl.BlockSpec((tr, inner), lambda i: (i, xs_col // inner)),
                  pl.BlockSpec((tr, inner), lambda i: (i, z_col // inner)),
                  pl.BlockSpec((1, inner), lambda i: (0, 0)),
                  pl.BlockSpec((1, inner), lambda i: (0, 0))],
        out_specs=pl.BlockSpec((tr, inner), lambda i: (i, 0)),
        compiler_params=_cp("parallel"),
        name="ssd_finish",
    )(yf, yb, u, proj, d_full.reshape(1, inner), norm_w.reshape(1, inner))


KV_PER_STEP = LANES // HEAD_DIM


def _softmax_parts(scores, sink):
    m = sink
    for s in scores:
        m = jnp.maximum(m, jnp.max(s, axis=-1, keepdims=True))
    es = [jnp.exp(s - m) for s in scores]
    denom = jnp.exp(sink - m)
    for e in es:
        denom = denom + jnp.sum(e, axis=-1, keepdims=True)
    inv = 1.0 / denom
    return [e * inv for e in es]


def _ctx_attn_body(sink_ref, q_ref, k_ref, v_ref, fill_ref, o_ref, *, q_per_kv, scale):
    del fill_ref
    hb = pl.program_id(1)
    nt = (((1,), (1,)), ((), ()))
    outs = []
    for kv in range(KV_PER_STEP):
        k = k_ref[:, kv * HEAD_DIM:(kv + 1) * HEAD_DIM].astype(BF16)
        v = v_ref[:, kv * HEAD_DIM:(kv + 1) * HEAD_DIM].astype(BF16)
        for gq in range(q_per_kv):
            hl = kv * q_per_kv + gq
            qh = (q_ref[:, hl * HEAD_DIM:(hl + 1) * HEAD_DIM] * scale).astype(BF16)
            s = lax.dot_general(qh, k, nt, preferred_element_type=F32)
            sink = sink_ref[hb * KV_PER_STEP * q_per_kv + hl]
            (p,) = _softmax_parts([s], sink)
            outs.append(jnp.dot(p.astype(BF16), v, preferred_element_type=F32))
    o_ref[...] = jnp.concatenate(outs, axis=1).astype(o_ref.dtype)


def context_attention(qkv, sink, *, nseq, L, n_heads):
    q_per_kv = n_heads // N_KV_HEADS
    qw = KV_PER_STEP * q_per_kv * HEAD_DIM
    nhb = N_KV_HEADS // KV_PER_STEP
    kcb = n_heads * HEAD_DIM // LANES
    vcb = kcb + N_KV_HEADS * HEAD_DIM // LANES
    return pl.pallas_call(
        functools.partial(_ctx_attn_body, q_per_kv=q_per_kv, scale=HEAD_DIM ** -0.5),
        out_shape=jax.ShapeDtypeStruct((qkv.shape[0], n_heads * HEAD_DIM), BF16),
        grid=(nseq, nhb),
        in_specs=[pl.BlockSpec(memory_space=pltpu.SMEM),
                  pl.BlockSpec((L, qw), lambda b, h: (b, h)),
                  pl.BlockSpec((L, LANES), lambda b, h: (b, kcb + h)),
                  pl.BlockSpec((L, LANES), lambda b, h: (b, vcb + h)),
                  pl.BlockSpec(memory_space=pl.ANY)],
        out_specs=pl.BlockSpec((L, qw), lambda b, h: (b, h)),
        input_output_aliases={4: 0},
        compiler_params=_cp("parallel", "parallel"),
        name="context_attention",
    )(sink, qkv, qkv, qkv, jnp.zeros((qkv.shape[0], n_heads * HEAD_DIM), BF16))


def _rope(x, cos, sin):
    outs = []
    lane = lax.broadcasted_iota(jnp.int32, cos.shape, 1)
    first = (lane % (HEAD_DIM // 2)) < (HEAD_DIM // 4)
    for cgrp in range(x.shape[1] // LANES):
        xc = x[:, cgrp * LANES:(cgrp + 1) * LANES]
        partner = jnp.where(first, pltpu.roll(xc, LANES - HEAD_DIM // 4, 1), pltpu.roll(xc, HEAD_DIM // 4, 1))
        outs.append(xc * cos + partner * sin)
    return outs[0] if len(outs) == 1 else jnp.concatenate(outs, axis=1)


def _lat_attn_body(sink_ref, q_ref, k_ref, v_ref, kc_ref, vc_ref, cosq_ref, sinq_ref, cos_ref, sin_ref, fill_ref,
                   o_ref, *, q_per_kv, scale, L, blk):
    del fill_ref
    hb = pl.program_id(1)
    i = pl.program_id(2)
    n_loc = 3 * blk
    start = jnp.clip((i - 1) * blk, 0, L - n_loc)
    start = pl.multiple_of(start, blk)
    nt = (((1,), (1,)), ((), ()))
    q_all = _rope(q_ref[...], cosq_ref[...], sinq_ref[...]) * scale
    k_loc = _rope(k_ref[pl.ds(start, n_loc), :], cos_ref[pl.ds(start, n_loc), :], sin_ref[pl.ds(start, n_loc), :])
    v_loc = v_ref[pl.ds(start, n_loc), :]
    qpos = i * blk + lax.broadcasted_iota(jnp.int32, (blk, n_loc), 0)
    kpos = start + lax.broadcasted_iota(jnp.int32, (blk, n_loc), 1)
    ok = jnp.abs(qpos - kpos) <= WINDOW
    outs = []
    for kv in range(KV_PER_STEP):
        sl = slice(kv * HEAD_DIM, (kv + 1) * HEAD_DIM)
        k = k_loc[:, sl].astype(BF16)
        v = v_loc[:, sl].astype(BF16)
        kc = kc_ref[:, sl].astype(BF16)
        vc = vc_ref[:, sl].astype(BF16)
        for gq in range(q_per_kv):
            hl = kv * q_per_kv + gq
            qh = q_all[:, hl * HEAD_DIM:(hl + 1) * HEAD_DIM].astype(BF16)
            s_loc = jnp.where(ok, lax.dot_general(qh, k, nt, preferred_element_type=F32), -jnp.inf)
            s_ctx = lax.dot_general(qh, kc, nt, preferred_element_type=F32)
            sink = sink_ref[hb * KV_PER_STEP * q_per_kv + hl]
            p_loc, p_ctx = _softmax_parts([s_loc, s_ctx], sink)
            outs.append(jnp.dot(p_loc.astype(BF16), v, preferred_element_type=F32)
                        + jnp.dot(p_ctx.astype(BF16), vc, preferred_element_type=F32))
    o_ref[...] = jnp.concatenate(outs, axis=1).astype(o_ref.dtype)


def rope_tables(L):
    rows = L // GRID_W
    row = jnp.repeat(jnp.arange(rows, dtype=F32), GRID_W)
    col = jnp.tile(jnp.arange(GRID_W, dtype=F32), rows)
    quarter = HEAD_DIM // 4
    inv = ROPE_BASE ** (-jnp.arange(quarter, dtype=F32) / quarter)
    ang_r = row[:, None] * inv[None, :]
    ang_c = col[:, None] * inv[None, :]
    ang = jnp.concatenate([ang_r, ang_r, ang_c, ang_c], axis=1)
    sign = jnp.tile(jnp.concatenate([-jnp.ones((quarter,), F32), jnp.ones((quarter,), F32)]), 2)
    cos = jnp.tile(jnp.cos(ang), (1, LANES // HEAD_DIM))
    sin = jnp.tile(jnp.sin(ang) * sign[None, :], (1, LANES // HEAD_DIM))
    return cos, sin


def latent_attention(qkv, k_cache, v_cache, sink, fill, *, row0, nseq, L, n_heads):
    blk = WINDOW
    q_per_kv = n_heads // N_KV_HEADS
    qw = KV_PER_STEP * q_per_kv * HEAD_DIM
    nhb = N_KV_HEADS // KV_PER_STEP
    kcb = n_heads * HEAD_DIM // LANES
    vcb = kcb + N_KV_HEADS * HEAD_DIM // LANES
    nb = L // blk
    past = k_cache.shape[1]
    cos, sin = rope_tables(L)
    return pl.pallas_call(
        functools.partial(_lat_attn_body, q_per_kv=q_per_kv, scale=HEAD_DIM ** -0.5, L=L, blk=blk),
        out_shape=jax.ShapeDtypeStruct(fill.shape, BF16),
        grid=(nseq, nhb, nb),
        in_specs=[pl.BlockSpec(memory_space=pltpu.SMEM),
                  pl.BlockSpec((blk, qw), lambda b, h, i: (row0 // blk + b * nb + i, h)),
                  pl.BlockSpec((L, LANES), lambda b, h, i: (row0 // L + b, kcb + h)),
                  pl.BlockSpec((L, LANES), lambda b, h, i: (row0 // L + b, vcb + h)),
                  pl.BlockSpec((None, past, LANES), lambda b, h, i: (b, 0, h)),
                  pl.BlockSpec((None, past, LANES), lambda b, h, i: (b, 0, h)),
                  pl.BlockSpec((blk, LANES), lambda b, h, i: (i, 0)),
                  pl.BlockSpec((blk, LANES), lambda b, h, i: (i, 0)),
                  pl.BlockSpec((L, LANES), lambda b, h, i: (0, 0)),
                  pl.BlockSpec((L, LANES), lambda b, h, i: (0, 0)),
                  pl.BlockSpec(memory_space=pl.ANY)],
        out_specs=pl.BlockSpec((blk, qw), lambda b, h, i: (row0 // blk + b * nb + i, h)),
        input_output_aliases={10: 0},
        compiler_params=_cp("parallel", "parallel", "arbitrary"),
        name="latent_attention",
    )(sink, qkv, qkv, qkv, k_cache, v_cache, cos, sin, cos, sin, fill)


MOE_TILE = 256
MOE_FFN_VMEM_BYTES = 56 * 1024 * 1024


def _moe_dispatch_body(pos_ref, h_ref, g_ref, init_hbm, xs_hbm, buf, sem):
    del init_hbm
    i = pl.program_id(0)
    tr, d = h_ref.shape
    buf[:, :d] = h_ref[...]
    buf[:, d:] = g_ref[...]

    def start(r, carry):
        pltpu.make_async_copy(buf.at[pl.ds(r, 1)], xs_hbm.at[pl.ds(pos_ref[i * tr + r], 1)], sem).start()
        return carry

    lax.fori_loop(0, tr, start, 0)

    def wait(r, carry):
        pltpu.make_async_copy(buf.at[pl.ds(r, 1)], xs_hbm.at[pl.ds(0, 1)], sem).wait()
        return carry

    lax.fori_loop(0, tr, wait, 0)


def moe_dispatch(h, gate_lanes, pos, n_slots):
    t, d = h.shape
    tr = ROW_TILE
    w = d + LANES
    grid_spec = pltpu.PrefetchScalarGridSpec(
        num_scalar_prefetch=1,
        grid=(t // tr,),
        in_specs=[pl.BlockSpec((tr, d), lambda i, p: (i, 0)),
                  pl.BlockSpec((tr, LANES), lambda i, p: (i, 0)),
                  pl.BlockSpec(memory_space=pl.ANY)],
        out_specs=pl.BlockSpec(memory_space=pl.ANY),
        scratch_shapes=[pltpu.VMEM((tr, w), F32), pltpu.SemaphoreType.DMA(())])
    return pl.pallas_call(
        _moe_dispatch_body,
        out_shape=jax.ShapeDtypeStruct((n_slots, w), F32),
        grid_spec=grid_spec,
        input_output_aliases={3: 0},
        compiler_params=_cp("arbitrary"),
        name="moe_dispatch",
    )(pos, h, gate_lanes, jnp.zeros((n_slots, w), F32))


def _moe_ffn_body(ea_ref, eb_ref, nused_ref, xs_ref, w1a_ref, w3a_ref, w2a_ref, w1b_ref, w3b_ref, w2b_ref, o_ref):
    i = pl.program_id(0)
    d = o_ref.shape[1]

    @pl.when(i < nused_ref[0])
    def _():
        x = xs_ref[:, :d].astype(BF16)

        def expert(w1_ref, w3_ref, w2_ref):
            a = jnp.dot(x, w1_ref[...], preferred_element_type=F32)
            b = jnp.dot(x, w3_ref[...], preferred_element_type=F32)
            return jnp.dot((_silu(a) * b).astype(BF16), w2_ref[...], preferred_element_type=F32)

        o_ref[...] = (xs_ref[:, d:d + 1] * expert(w1a_ref, w3a_ref, w2a_ref)
                      + xs_ref[:, d + 1:d + 2] * expert(w1b_ref, w3b_ref, w2b_ref))

    @pl.when(i >= nused_ref[0])
    def _():
        o_ref[...] = jnp.zeros_like(o_ref)


def moe_ffn(xs, tile_ea, tile_eb, n_used, w1, w3, w2):
    _, d, de = w1.shape
    tm = MOE_TILE
    n_tiles = tile_ea.shape[0]
    up_a = pl.BlockSpec((None, d, de), lambda i, ea, eb, nu: (ea[i], 0, 0))
    up_b = pl.BlockSpec((None, d, de), lambda i, ea, eb, nu: (eb[i], 0, 0))
    grid_spec = pltpu.PrefetchScalarGridSpec(
        num_scalar_prefetch=3,
        grid=(n_tiles,),
        in_specs=[pl.BlockSpec((tm, d + LANES), lambda i, ea, eb, nu: (i, 0)),
                  up_a, up_a, pl.BlockSpec((None, de, d), lambda i, ea, eb, nu: (ea[i], 0, 0)),
                  up_b, up_b, pl.BlockSpec((None, de, d), lambda i, ea, eb, nu: (eb[i], 0, 0))],
        out_specs=pl.BlockSpec((tm, d), lambda i, ea, eb, nu: (i, 0)))
    return pl.pallas_call(
        _moe_ffn_body,
        out_shape=jax.ShapeDtypeStruct((n_tiles * tm, d), F32),
        grid_spec=grid_spec,
        compiler_params=_cp("arbitrary", vmem=MOE_FFN_VMEM_BYTES),
        name="moe_ffn",
    )(tile_ea, tile_eb, n_used, xs, w1, w3, w2, w1, w3, w2)


def _moe_combine_body(pos_ref, y_hbm, x_ref, g_ref, o_ref, ybuf, sem):
    i = pl.program_id(0)
    tr = x_ref.shape[0]

    def start(r, carry):
        pltpu.make_async_copy(y_hbm.at[pl.ds(pos_ref[i * tr + r], 1)], ybuf.at[pl.ds(r, 1)], sem).start()
        return carry

    lax.fori_loop(0, tr, start, 0)

    def wait(r, carry):
        pltpu.make_async_copy(y_hbm.at[pl.ds(0, 1)], ybuf.at[pl.ds(r, 1)], sem).wait()
        return carry

    lax.fori_loop(0, tr, wait, 0)
    o_ref[...] = x_ref[...] + g_ref[...] * ybuf[...]


def moe_combine(x, y_sorted, pos, mod3, gate_idx, seg_of_tile):
    t, d = x.shape
    tr = ROW_TILE
    grid_spec = pltpu.PrefetchScalarGridSpec(
        num_scalar_prefetch=1,
        grid=(t // tr,),
        in_specs=[pl.BlockSpec(memory_space=pl.ANY),
                  pl.BlockSpec((tr, d), lambda i, p: (i, 0)),
                  pl.BlockSpec((None, 1, d), lambda i, p: (seg_of_tile(i) * 6 + gate_idx, 0, 0))],
        out_specs=pl.BlockSpec((tr, d), lambda i, p: (i, 0)),
        scratch_shapes=[pltpu.VMEM((tr, d), F32), pltpu.SemaphoreType.DMA(())])
    return pl.pallas_call(
        _moe_combine_body,
        out_shape=jax.ShapeDtypeStruct((t, d), F32),
        grid_spec=grid_spec,
        compiler_params=_cp("arbitrary"),
        name="moe_combine",
    )(pos, y_sorted, x, mod3)


def moe_route(logits, n_groups, n_experts):
    per = n_experts // n_groups
    assert TOP_K == 2
    g_prob = jax.nn.softmax(logits[:, :n_groups], axis=-1)
    g_top = jnp.max(g_prob, axis=-1, keepdims=True)
    g_idx = jnp.argmax(g_prob, axis=-1, keepdims=True).astype(jnp.int32)
    e_logits = logits[:, n_groups:n_groups + n_experts].reshape(-1, n_groups, per)
    g_onehot = g_idx == jnp.arange(n_groups)[None, :]
    e_in = jnp.sum(jnp.where(g_onehot[:, :, None], e_logits, 0.0), axis=1)
    e_prob = jax.nn.softmax(e_in, axis=-1)
    i0 = jnp.argmax(e_prob, axis=-1, keepdims=True).astype(jnp.int32)
    rest = jnp.where(jnp.arange(per)[None, :] == i0, -jnp.inf, e_prob)
    i1 = jnp.argmax(rest, axis=-1, keepdims=True).astype(jnp.int32)
    e_idx = jnp.concatenate([i0, i1], axis=1)
    e_top = jnp.concatenate([jnp.max(e_prob, axis=-1, keepdims=True), jnp.max(rest, axis=-1, keepdims=True)], axis=1)
    e_top = e_top / jnp.sum(e_top, axis=-1, keepdims=True)
    return g_idx * per + e_idx, g_top * e_top


def moe_plan(expert_ids, gates, n_groups, n_experts):
    assert TOP_K == 2
    t = expert_ids.shape[0]
    tm = MOE_TILE
    per = n_experts // n_groups
    n_pairs = per * (per - 1) // 2
    n_classes = n_groups * n_pairs
    n_tiles = t // tm + n_classes
    e0, e1 = expert_ids[:, 0], expert_ids[:, 1]
    lo, hi = jnp.minimum(e0, e1), jnp.maximum(e0, e1)
    g_lo = jnp.where(e0 < e1, gates[:, 0], gates[:, 1])
    g_hi = jnp.where(e0 < e1, gates[:, 1], gates[:, 0])
    a, b = lo % per, hi % per
    cls = (lo // per) * n_pairs + (a * (2 * per - a - 1)) // 2 + (b - a - 1)
    onehot = (cls[:, None] == jnp.arange(n_classes)[None, :]).astype(F32)
    blk = ROW_TILE
    inner = jnp.einsum("ij,bjk->bik", jnp.tril(jnp.ones((blk, blk), F32)), onehot.reshape(-1, blk, n_classes))
    bsum = inner[:, -1, :]
    running = (inner + (jnp.cumsum(bsum, axis=0) - bsum)[:, None, :]).reshape(-1, n_classes)
    counts = jnp.sum(bsum, axis=0).astype(jnp.int32)
    tiles_per = (counts + tm - 1) // tm
    tile_start = jnp.cumsum(tiles_per) - tiles_per
    slot = running - 1.0 + (tile_start * tm).astype(F32)[None, :]
    pos = jnp.sum(jnp.where(onehot > 0, slot, 0.0), axis=1).astype(jnp.int32)
    n_used = jnp.sum(tiles_per).astype(jnp.int32)
    tile_ids = jnp.arange(n_tiles, dtype=jnp.int32)
    tile_cls = jnp.sum((tile_ids[:, None] >= tile_start[None, :]).astype(jnp.int32), axis=1) - 1
    tile_cls = jnp.where(tile_ids < n_used, tile_cls, tile_cls[jnp.maximum(n_used - 1, 0)])
    pairs = [(pa, pb) for pa in range(per) for pb in range(pa + 1, per)]
    cls_lo = jnp.array([g * per + pa for g in range(n_groups) for pa, _ in pairs], jnp.int32)
    cls_hi = jnp.array([g * per + pb for g in range(n_groups) for _, pb in pairs], jnp.int32)
    gate_lanes = jnp.pad(jnp.stack([g_lo, g_hi], axis=1), ((0, 0), (0, LANES - 2)))
    tile_onehot = tile_cls[:, None] == jnp.arange(n_classes)[None, :]
    tile_lo = jnp.sum(jnp.where(tile_onehot, cls_lo[None, :], 0), axis=1)
    tile_hi = jnp.sum(jnp.where(tile_onehot, cls_hi[None, :], 0), axis=1)
    return pos, gate_lanes, tile_lo, tile_hi, n_used.reshape(1), n_tiles * tm


def kernel(x_prompt, x_sample, state_ssd_fwd, state_ssd_bwd, cache_attn_k, cache_attn_v, c, c_ctx, ada_w, ada_b, norm_mix, norm_ffn, norm_final, ev_w_in, ev_w_out, hy_conv_w, hy_conv_b, hy_f_w1, hy_f_b1, hy_f_freq1, hy_f_w2, hy_f_b2, hy_f_freq2, hy_f_w3, hy_bias, ssd_conv_w, ssd_conv_b, ssd_dt_bias, ssd_a_log, ssd_d, ssd_norm, at_wq, at_wk, at_wv, at_wo, at_sink, moe_w_group, moe_b_group, moe_w_expert, moe_b_expert, moe_w1, moe_w3, moe_w2):
    bc, lc, d = x_prompt.shape
    bl, ll, _ = x_sample.shape
    tc_rows, tl_rows = bc * lc, bl * ll
    t = tc_rows + tl_rows
    depth = ada_w.shape[0]
    n_seg = 1 + bl
    assert lc % ROW_TILE == 0 and ll % ROW_TILE == 0 and tc_rows % ll == 0
    seg_rows = math.gcd(tc_rows, ll)

    def seg_of_rows(r):
        return jnp.where(r < tc_rows, 0, 1 + (r - tc_rows) // ll)

    def seg_of_tile(i):
        return seg_of_rows(i * ROW_TILE)

    x = jnp.concatenate([x_prompt.reshape(tc_rows, d), x_sample.reshape(tl_rows, d)], axis=0)
    cvec = jnp.concatenate([c_ctx[None], c, jnp.zeros((SUBLANES - n_seg, d), F32)], axis=0)
    mod_all = ada_mod(cvec, ada_w, ada_b)

    tile_row = jnp.arange(t // ROW_TILE, dtype=jnp.int32) * ROW_TILE
    seq_len = jnp.where(tile_row < tc_rows, lc, ll)
    rel = jnp.where(tile_row < tc_rows, tile_row, tile_row - tc_rows)
    conv_flags = jnp.stack([(rel % seq_len == 0), ((rel + ROW_TILE) % seq_len == 0)]).astype(jnp.int32)

    new_sf, new_sb, new_k, new_v = [], [], [], []
    for l in range(depth):
        i = l // 2
        mod3 = mod_all[l, :n_seg].reshape(n_seg * 6, 1, d)
        h = rms_norm(x, norm_mix[l], out_dtype=BF16, mod3=mod3, shift_idx=0, scale_idx=1,
                     seg_of_tile=seg_of_tile)
        if l % 2 == 0:
            hyena_w = hy_conv_w.shape[2] // 3
            inner = ssd_norm.shape[1]
            heads = inner // SSD_HEAD_DIM
            conv_ch = ssd_conv_w.shape[2]
            main_cols = 3 * hyena_w + inner + conv_ch
            w_in = ev_w_in[i]
            proj = dense(h, w_in[:, :main_cols].astype(BF16), out_dtype=BF16, name="in_proj")
            dt_raw = dense(h, w_in[:, main_cols:].astype(BF16), out_dtype=F32, name="dt_proj")
            conv_w = jnp.concatenate([hy_conv_w[i], ssd_conv_w[i]], axis=1)
            conv_b = jnp.concatenate([hy_conv_b[i], ssd_conv_b[i]], axis=0)
            u = dwconv(proj, conv_w, conv_b, conv_flags, n_plain_cols=3 * hyena_w, gap_cols=inner,
                       out_cols=3 * hyena_w + conv_ch)
            y_a = jnp.zeros((t, hyena_w), BF16)
            for row0, nb, L in ((0, bc, lc), (tc_rows, bl, ll)):
                filt = hyena_filters(L, hy_f_w1[i], hy_f_b1[i], hy_f_freq1[i], hy_f_w2[i], hy_f_b2[i],
                                     hy_f_freq2[i], hy_f_w3[i], hyena_w)
                if L % (16 * SPEC_BLOCK) == 0:
                    n1 = 2 * L // SPEC_BLOCK
                    t1f = fft_table("t1f", n1, SPEC_BLOCK)
                    kf = hyena_spectrum_2level(L, filt, t1f, fft_table("f2filt", n1, SPEC_BLOCK))
                    tabs = (t1f, fft_table("f2f", n1, SPEC_BLOCK), fft_table("f2i", n1, SPEC_BLOCK),
                            fft_table("t1i", n1, SPEC_BLOCK))
                    y_a = hyena_stream_2level(u, row0, nb, L, kf, tabs, hy_bias[i], hyena_w, fill=y_a)
                else:
                    a_tab = dft_table(L, "filt")
                    kf = hyena_spectrum(L, filt, a_tab)
                    y_a = hyena_stream(u, row0, nb, L, kf, a_tab, dft_table(L, "inv"), hy_bias[i], hyena_w,
                                       fill=y_a)
            col0 = 3 * hyena_w
            gn = SSD_GROUPS * SSD_STATE
            bt_all = u[:, col0 + inner:col0 + inner + gn].T
            dt, cum_c, cum_r = ssd_prep(dt_raw, dt_raw.T, ssd_dt_bias[i], ssd_a_log[i])
            zero = jnp.zeros((bc, SSD_GROUPS, SSD_STATE, inner // SSD_GROUPS), F32)
            init_f = jnp.concatenate([zero, _state_to_kernel_layout(state_ssd_fwd[:, i], SSD_GROUPS)], axis=0)
            init_b = jnp.concatenate([zero, _state_to_kernel_layout(state_ssd_bwd[:, i], SSD_GROUPS)], axis=0)
            yf, yb, sf, sb = ssd_scan(u, bt_all, dt, cum_c, cum_r, init_f, init_b,
                                      seq_lens=[lc] * bc + [ll] * bl, col0=col0, inner=inner)
            new_sf.append(_state_from_kernel_layout(sf[:bc], SSD_HEAD_DIM))
            new_sb.append(_state_from_kernel_layout(sb[:bc], SSD_HEAD_DIM))
            y_b = ssd_finish(yf, yb, u, proj, jnp.repeat(ssd_d[i], SSD_HEAD_DIM), ssd_norm[i],
                             xs_col=col0, z_col=3 * hyena_w, inner=inner)
            w_out = ev_w_out[i].astype(BF16)
            x = dense_residual(y_a, w_out[:hyena_w], x, mod3, 2, seg_of_rows, seg_rows, name="out_proj_a")
            x = dense_residual(y_b, w_out[hyena_w:], x, mod3, 2, seg_of_rows, seg_rows, name="out_proj_b")
        else:
            n_heads = at_wq.shape[2] // HEAD_DIM
            w_qkv = jnp.concatenate([at_wq[i], at_wk[i], at_wv[i]], axis=1).astype(BF16)
            qkv = dense(h, w_qkv, out_dtype=F32, name="qkv_proj")
            kvw = N_KV_HEADS * HEAD_DIM
            att_c = context_attention(qkv, at_sink[i], nseq=bc, L=lc, n_heads=n_heads)
            past = cache_attn_k.shape[2]
            att = latent_attention(qkv, cache_attn_k[:, i].reshape(bl, past, kvw),
                                   cache_attn_v[:, i].reshape(bl, past, kvw), at_sink[i], att_c,
                                   row0=tc_rows, nseq=bl, L=ll, n_heads=n_heads)
            x = dense_residual(att, at_wo[i].astype(BF16), x, mod3, 2, seg_of_rows, seg_rows, name="attn_out")
            qw = n_heads * HEAD_DIM
            new_k.append(qkv[:tc_rows, qw:qw + kvw].reshape(bc, lc, N_KV_HEADS, HEAD_DIM))
            new_v.append(qkv[:tc_rows, qw + kvw:].reshape(bc, lc, N_KV_HEADS, HEAD_DIM))
        h2 = rms_norm(x, norm_ffn[l], out_dtype=F32, mod3=mod3, shift_idx=3, scale_idx=4,
                      seg_of_tile=seg_of_tile)
        n_experts = moe_w_expert.shape[2]
        w_r = jnp.concatenate([moe_w_group[l], moe_w_expert[l]], axis=1)
        w_r = jnp.pad(w_r, ((0, 0), (0, LANES - w_r.shape[1]))).astype(BF16)
        b_r = jnp.pad(jnp.concatenate([moe_b_group[l], moe_b_expert[l]]), (0, LANES - MOE_GROUPS - n_experts))
        logits = dense(h2, w_r, out_dtype=F32, extra=(b_r.reshape(1, LANES),),
                       extra_specs=(pl.BlockSpec((1, LANES), lambda bt, ii, j, kk: (0, 0)),),
                       epilogue=lambda acc, b_ref, row_tile: acc + b_ref[...], name="router")
        expert_ids, gates = moe_route(logits, MOE_GROUPS, n_experts)
        pos, gate_lanes, tile_lo, tile_hi, n_used, n_slots = moe_plan(expert_ids, gates, MOE_GROUPS, n_experts)
        xs = moe_dispatch(h2, gate_lanes, pos, n_slots)
        y_sorted = moe_ffn(xs, tile_lo, tile_hi, n_used, moe_w1[l].astype(BF16), moe_w3[l].astype(BF16),
                           moe_w2[l].astype(BF16))
        x = moe_combine(x, y_sorted, pos, mod3, 5, seg_of_tile)

    y_prompt = rms_norm(x, norm_final, out_dtype=F32, rows=tc_rows).reshape(bc, lc, d)
    y_sample = rms_norm(x, norm_final, out_dtype=F32, row_off=tc_rows, rows=tl_rows).reshape(bl, ll, d)
    return (y_prompt, y_sample, jnp.stack(new_sf, axis=1), jnp.stack(new_sb, axis=1),
            jnp.stack(new_k, axis=1), jnp.stack(new_v, axis=1))
```

```python
import functools
import math

import jax
import jax.numpy as jnp
from jax import lax
from jax.experimental import pallas as pl
from jax.experimental.pallas import tpu as pltpu

F32 = jnp.float32
BF16 = jnp.bfloat16
HI = lax.Precision.HIGHEST

RMS_EPS = 1e-6
GRID_W = 64
HYENA_BANDS = 16
HYENA_FAST_DECAY = 0.3
HYENA_SLOW_DECAY = 1.5
HYENA_TARGET = 1e-2
SSD_HEAD_DIM = 64
SSD_STATE = 128
SSD_GROUPS = 8
SSD_CHUNK = 128
HEAD_DIM = 64
N_KV_HEADS = 8
WINDOW = 128
ROPE_BASE = 10000.0
MOE_GROUPS = 4
TOP_K = 2

LANES = 128
SUBLANES = 8
BF16_SUBLANES = 16
DMA_PRIORITIES = 2
VMEM_LIMIT_BYTES = 48 * 1024 * 1024

ROW_TILE = 256
SPEC_BLOCK = 128


def _cp(*sem, vmem=VMEM_LIMIT_BYTES):
    return pltpu.CompilerParams(dimension_semantics=sem, vmem_limit_bytes=vmem)


def _tile(n, pref):
    if n <= pref:
        return n
    t = pref
    while n % t:
        t //= 2
    assert t >= LANES, (n, pref)
    return t


def _silu(x):
    return x * (1.0 / (1.0 + jnp.exp(-x)))


def _softplus(x):
    return jnp.maximum(x, 0.0) + jnp.log(1.0 + jnp.exp(-jnp.abs(x)))


def _ada_body(c_ref, w_ref, b_ref, o_ref):
    a = _silu(c_ref[...])
    o_ref[...] = lax.dot_general(a, w_ref[...], (((1,), (0,)), ((), ())), precision=HI,
                                 preferred_element_type=F32) + b_ref[...]


def ada_mod(cvec, ada_w, ada_b):
    depth, d, n = ada_w.shape
    rows = cvec.shape[0]
    tn = _tile(n, 1024)
    return pl.pallas_call(
        _ada_body,
        out_shape=jax.ShapeDtypeStruct((depth, rows, n), F32),
        grid=(depth, n // tn),
        in_specs=[pl.BlockSpec((rows, d), lambda l, j: (0, 0)),
                  pl.BlockSpec((None, d, tn), lambda l, j: (l, 0, j)),
                  pl.BlockSpec((None, 1, tn), lambda l, j: (l, 0, j))],
        out_specs=pl.BlockSpec((None, rows, tn), lambda l, j: (l, 0, j)),
        compiler_params=_cp("parallel", "parallel"),
        name="ada_mod",
    )(cvec, ada_w, ada_b.reshape(depth, 1, n))


def _norm_body(x_ref, g_ref, *rest, modulate):
    o_ref = rest[-1]
    x = x_ref[...]
    y = x * lax.rsqrt(jnp.mean(x * x, axis=-1, keepdims=True) + RMS_EPS)
    y = y * g_ref[...]
    if modulate:
        sh_ref, sc_ref = rest[0], rest[1]
        y = y * (1.0 + sc_ref[...]) + sh_ref[...]
    o_ref[...] = y.astype(o_ref.dtype)


def rms_norm(x, g, *, out_dtype, mod3=None, shift_idx=0, scale_idx=0, seg_of_tile=None,
             row_off=0, rows=None):
    t, d = x.shape
    rows = t if rows is None else rows
    tr = ROW_TILE
    off = row_off // tr
    in_specs = [pl.BlockSpec((tr, d), lambda i: (i + off, 0)),
                pl.BlockSpec((1, d), lambda i: (0, 0))]
    args = [x, g.reshape(1, d)]
    if mod3 is not None:
        in_specs += [pl.BlockSpec((None, 1, d), lambda i: (seg_of_tile(i) * 6 + shift_idx, 0, 0)),
                     pl.BlockSpec((None, 1, d), lambda i: (seg_of_tile(i) * 6 + scale_idx, 0, 0))]
        args += [mod3, mod3]
    return pl.pallas_call(
        functools.partial(_norm_body, modulate=mod3 is not None),
        out_shape=jax.ShapeDtypeStruct((rows, d), out_dtype),
        grid=(rows // tr,),
        in_specs=in_specs,
        out_specs=pl.BlockSpec((tr, d), lambda i: (i, 0)),
        compiler_params=_cp("parallel"),
        name="rms_norm",
    )(*args)


def _mm_body(*refs, nk, n_extra, n_skip, epilogue):
    a_ref, b_ref = refs[0], refs[1]
    extra = refs[2:2 + n_extra]
    o_ref = refs[2 + n_extra + n_skip]
    row_tile = pl.program_id(1)
    b = b_ref[...]
    if b.ndim == 3:
        b = b.reshape(-1, b.shape[-1])
    part = jnp.dot(a_ref[...].astype(b.dtype), b, preferred_element_type=F32)
    if nk == 1:
        o_ref[...] = epilogue(part, *extra, row_tile=row_tile).astype(o_ref.dtype).reshape(o_ref.shape)
        return
    acc_ref = refs[3 + n_extra + n_skip]
    k = pl.program_id(3)

    @pl.when(k == 0)
    def _():
        acc_ref[...] = part

    @pl.when(k > 0)
    def _():
        acc_ref[...] += part

    @pl.when(k == nk - 1)
    def _():
        o_ref[...] = epilogue(acc_ref[...], *extra, row_tile=row_tile).astype(o_ref.dtype)


def matmul(a, b, *, grid, tm, tn, a_spec, b_spec, o_spec, out_shape, out_dtype,
           extra=(), extra_specs=(), epilogue=None, fill=None, name="matmul"):
    nk = grid[3]
    epilogue = epilogue or (lambda acc, row_tile: acc)
    scratch = [pltpu.VMEM((tm, tn), F32)] if nk > 1 else []
    fills = () if fill is None else (fill,)
    return pl.pallas_call(
        functools.partial(_mm_body, nk=nk, n_extra=len(extra), n_skip=len(fills), epilogue=epilogue),
        out_shape=jax.ShapeDtypeStruct(out_shape, out_dtype),
        grid=grid,
        in_specs=[a_spec, b_spec, *extra_specs, *[pl.BlockSpec(memory_space=pl.ANY) for _ in fills]],
        out_specs=o_spec,
        scratch_shapes=scratch,
        input_output_aliases={2 + len(extra): 0} if fills else {},
        compiler_params=_cp("parallel", "parallel", "parallel", "arbitrary"),
        name=name,
    )(a, b, *extra, *fills)


def dense(a, w, *, out_dtype, tm=1024, tn=512, tk=2048, extra=(), extra_specs=(), epilogue=None,
          name="dense"):
    m, k = a.shape
    n = w.shape[1]
    tm, tn, tk = _tile(m, tm), _tile(n, tn), _tile(k, tk)
    return matmul(
        a, w, grid=(1, m // tm, n // tn, k // tk), tm=tm, tn=tn,
        a_spec=pl.BlockSpec((tm, tk), lambda bt, i, j, kk: (i, kk)),
        b_spec=pl.BlockSpec((tk, tn), lambda bt, i, j, kk: (kk, j)),
        o_spec=pl.BlockSpec((tm, tn), lambda bt, i, j, kk: (i, j)),
        out_shape=(m, n), out_dtype=out_dtype, extra=extra, extra_specs=extra_specs,
        epilogue=epilogue, name=name)


def dense_residual(a, w, x, mod3, gate_idx, seg_of_rows, seg_rows, *, tm=1024, tn=512, tk=2048, name="dense_res"):
    n = w.shape[1]
    tm_ = _tile(seg_rows, tm)
    tn_ = _tile(n, tn)
    return dense(
        a, w, out_dtype=F32, tm=tm_, tn=tn, tk=tk, extra=(x, mod3),
        extra_specs=(pl.BlockSpec((tm_, tn_), lambda bt, i, j, kk: (i, j)),
                     pl.BlockSpec((None, 1, tn_),
                                  lambda bt, i, j, kk: (seg_of_rows(i * tm_) * 6 + gate_idx, 0, j))),
        epilogue=lambda acc, x_ref, g_ref, row_tile: x_ref[...] + g_ref[...] * acc, name=name)


def _dwconv_body(flags_ref, x_ref, p_ref, n_ref, w_ref, b_ref, o_ref, *, n_plain):
    i = pl.program_id(0)
    j = pl.program_id(1)
    x = x_ref[...].astype(F32)
    tr = x.shape[0]
    row = lax.broadcasted_iota(jnp.int32, x.shape, 0)
    keep_prev = (flags_ref[0, i] == 0).astype(F32)
    keep_next = (flags_ref[1, i] == 0).astype(F32)
    halo_prev = p_ref[SUBLANES - 1:SUBLANES, :].astype(F32) * keep_prev
    halo_next = n_ref[0:1, :].astype(F32) * keep_next
    prev = jnp.where(row == 0, halo_prev, pltpu.roll(x, 1, 0))
    nxt = jnp.where(row == tr - 1, halo_next, pltpu.roll(x, tr - 1, 0))
    w = w_ref[...]
    y = prev * w[0:1, :] + x * w[1:2, :] + nxt * w[2:3, :] + b_ref[...]
    @pl.when(j < n_plain)
    def _():
        o_ref[...] = y.astype(o_ref.dtype)

    @pl.when(j >= n_plain)
    def _():
        o_ref[...] = _silu(y).astype(o_ref.dtype)


def dwconv(proj, w, b, flags, *, n_plain_cols, gap_cols, out_cols):
    t = proj.shape[0]
    tr = ROW_TILE
    tc = _tile(math.gcd(n_plain_cols, gap_cols, out_cols), 2048)
    n_plain, gap = n_plain_cols // tc, gap_cols // tc
    hb = tr // SUBLANES
    last_hb = t // SUBLANES - 1

    def src(j):
        return j + jnp.where(j >= n_plain, gap, 0)

    grid_spec = pltpu.PrefetchScalarGridSpec(
        num_scalar_prefetch=1,
        grid=(t // tr, out_cols // tc),
        in_specs=[pl.BlockSpec((tr, tc), lambda i, j, f: (i, src(j))),
                  pl.BlockSpec((SUBLANES, tc), lambda i, j, f: (jnp.maximum(i * hb - 1, 0), src(j))),
                  pl.BlockSpec((SUBLANES, tc), lambda i, j, f: (jnp.minimum((i + 1) * hb, last_hb), src(j))),
                  pl.BlockSpec((3, tc), lambda i, j, f: (0, j)),
                  pl.BlockSpec((1, tc), lambda i, j, f: (0, j))],
        out_specs=pl.BlockSpec((tr, tc), lambda i, j, f: (i, j)))
    return pl.pallas_call(
        functools.partial(_dwconv_body, n_plain=n_plain),
        out_shape=jax.ShapeDtypeStruct((t, out_cols), BF16),
        grid_spec=grid_spec,
        compiler_params=_cp("parallel", "parallel"),
        name="dwconv",
    )(flags, proj, proj, proj, w, b.reshape(1, -1))


def _filter_body(feat_ref, w1_ref, b1_ref, f1_ref, w2_ref, b2_ref, f2_ref, w3_ref, dl_ref, o_ref):
    d = pl.program_id(0)
    i = pl.program_id(1)
    feat = feat_ref[...]
    dn = (((1,), (0,)), ((), ()))
    hid = jnp.sin(f1_ref[...] * (lax.dot_general(feat, w1_ref[...], dn, precision=HI,
                                                 preferred_element_type=F32) + b1_ref[...]))
    hid = jnp.sin(f2_ref[...] * (lax.dot_general(hid, w2_ref[...], dn, precision=HI,
                                                 preferred_element_type=F32) + b2_ref[...]))
    filt = lax.dot_general(hid, w3_ref[...], dn, precision=HI, preferred_element_type=F32)
    t01 = feat[:, 0:1]
    filt = filt * jnp.exp(-t01 * dl_ref[...])
    row = lax.broadcasted_iota(jnp.int32, filt.shape, 0) + i * filt.shape[0]
    filt = jnp.where((d == 1) & (row == 0), 0.0, filt)
    o_ref[...] = filt.astype(o_ref.dtype)


def hyena_filters(L, f_w1, f_b1, f_freq1, f_w2, f_b2, f_freq2, f_w3, c):
    t = jnp.arange(L, dtype=F32)
    t01 = t / (L - 1)
    bands = jnp.linspace(1e-4, HYENA_BANDS - 1, HYENA_BANDS, dtype=F32)
    ang = (2.0 * math.pi / L) * t[:, None] * bands[None, :]
    feat = jnp.concatenate([t01[:, None], jnp.cos(ang), -jnp.sin(ang)], axis=-1)
    emb = feat.shape[1]
    feat = jnp.pad(feat, ((0, 0), (0, LANES - emb)))
    w1 = jnp.pad(f_w1, ((0, LANES - emb), (0, 0)))
    ffn = f_w1.shape[1]
    order = f_w3.shape[1] // (2 * c)
    deltas = jnp.abs(jnp.linspace(math.log(HYENA_TARGET) / HYENA_SLOW_DECAY,
                                  math.log(HYENA_TARGET) / HYENA_FAST_DECAY, c, dtype=F32))
    deltas = jnp.tile(deltas, order).reshape(1, order * c)
    tl, tn = _tile(L, 512), _tile(order * c, 1024)
    nj = order * c // tn
    return pl.pallas_call(
        _filter_body,
        out_shape=jax.ShapeDtypeStruct((2, L, order * c), BF16),
        grid=(2, L // tl, nj),
        in_specs=[pl.BlockSpec((tl, LANES), lambda d, i, j: (i, 0)),
                  pl.BlockSpec((LANES, ffn), lambda d, i, j: (0, 0)),
                  pl.BlockSpec((1, ffn), lambda d, i, j: (0, 0)),
                  pl.BlockSpec((1, ffn), lambda d, i, j: (0, 0)),
                  pl.BlockSpec((ffn, ffn), lambda d, i, j: (0, 0)),
                  pl.BlockSpec((1, ffn), lambda d, i, j: (0, 0)),
                  pl.BlockSpec((1, ffn), lambda d, i, j: (0, 0)),
                  pl.BlockSpec((ffn, tn), lambda d, i, j: (0, d * nj + j)),
                  pl.BlockSpec((1, tn), lambda d, i, j: (0, j))],
        out_specs=pl.BlockSpec((None, tl, tn), lambda d, i, j: (d, i, j)),
        compiler_params=_cp("parallel", "parallel", "parallel"),
        name="hyena_filters",
    )(feat, w1, f_b1.reshape(1, ffn), f_freq1.reshape(1, ffn), f_w2, f_b2.reshape(1, ffn),
      f_freq2.reshape(1, ffn), f_w3, deltas)


def _cos_turns(m, log2_period):
    period = 1 << log2_period
    quarter = period // 4
    mm = m + quarter // 2
    q = (mm >> (log2_period - 2)) & 3
    r = (mm & (quarter - 1)) - quarter // 2
    phi = r.astype(F32) * (2.0 * math.pi / period)
    p2 = phi * phi
    cosv = 1.0 + p2 * (-1.0 / 2 + p2 * (1.0 / 24 + p2 * (-1.0 / 720 + p2 * (1.0 / 40320 - p2 / 3628800))))
    sinv = phi * (1.0 + p2 * (-1.0 / 6 + p2 * (1.0 / 120 + p2 * (-1.0 / 5040 + p2 / 362880))))
    return jnp.where(q == 0, cosv, jnp.where(q == 1, -sinv, jnp.where(q == 2, -cosv, sinv)))


def _spec_index(r, L):
    blk = r >> 8
    within = r & (2 * SPEC_BLOCK - 1)
    return blk * SPEC_BLOCK + (within & (SPEC_BLOCK - 1)), within >= SPEC_BLOCK


def _dft_body(o_ref, *, L, mode):
    log2n = int(math.log2(2 * L))
    shape = o_ref.shape
    i0 = pl.program_id(0) * shape[0]
    j0 = pl.program_id(1) * shape[1]
    rows = lax.broadcasted_iota(jnp.int32, shape, 0) + i0
    cols = lax.broadcasted_iota(jnp.int32, shape, 1) + j0
    if mode == "inv":
        r, n = cols, rows
    else:
        r, n = rows, cols
    k, im = _spec_index(r, L)
    back = n >= L
    n = jnp.where(back, n - L, n)
    phase = jnp.where(im, jnp.where(back, -(L // 2), L // 2), 0)
    val = _cos_turns((k * n + phase) & (2 * L - 1), log2n)
    nyq = jnp.where((n & 1) == 0, 1.0, -1.0)
    val = jnp.where(im & (k == 0), nyq, val)
    if mode == "inv":
        val = val * jnp.where(k == 0, 1.0 / (2 * L), 2.0 / (2 * L))
    o_ref[...] = val.astype(o_ref.dtype)


def dft_table(L, mode):
    assert L & (L - 1) == 0 and L >= 2 * SPEC_BLOCK
    shape = {"fwd": (2 * L, L), "filt": (2 * L, 2 * L), "inv": (L, 2 * L)}[mode]
    tr, tc = _tile(shape[0], 512), _tile(shape[1], 1024)
    return pl.pallas_call(
        functools.partial(_dft_body, L=L, mode=mode),
        out_shape=jax.ShapeDtypeStruct(shape, BF16),
        grid=(shape[0] // tr, shape[1] // tc),
        out_specs=pl.BlockSpec((tr, tc), lambda i, j: (i, j)),
        compiler_params=_cp("parallel", "parallel"),
        name="dft_table_" + mode,
    )()


def _spectral_epilogue(acc, kf_ref, row_tile):
    tm, tn = acc.shape
    nb = tm // (2 * SPEC_BLOCK)
    z = acc.reshape(nb, 2, SPEC_BLOCK, tn)
    kf = kf_ref[...].reshape(nb, 2, SPEC_BLOCK, tn)
    zr, zi, kr, ki = z[:, 0], z[:, 1], kf[:, 0], kf[:, 1]
    first = row_tile == 0
    blk = lax.broadcasted_iota(jnp.int32, zr.shape, 0)
    row = lax.broadcasted_iota(jnp.int32, zr.shape, 1)
    dc = first & (blk == 0) & (row == 0)
    pr = zr * kr - jnp.where(dc, 0.0, zi * ki)
    pi = jnp.where(dc, zi * ki, zr * ki + zi * kr)
    return jnp.stack([pr, pi], axis=1).reshape(tm, tn)


def hyena_stream(u, row0, nb, L, kf, f_tab, g_tab, hy_bias, c, fill=None):
    out_rows = u.shape[0]
    tn = 512
    cb = c // tn
    tm_f = _tile(2 * L, 512)
    tk_f = _tile(L, 2048)
    tm_i = _tile(L, 512)
    tk_i = _tile(2 * L, 2048)
    z, z_rows, z_col = u, row0, 2 * cb
    for o in range(2):
        zrb = z_rows // tk_f
        p = matmul(
            f_tab, z, grid=(nb, 2 * L // tm_f, cb, L // tk_f), tm=tm_f, tn=tn,
            a_spec=pl.BlockSpec((tm_f, tk_f), lambda bt, i, j, k: (i, k)),
            b_spec=pl.BlockSpec((tk_f, tn), lambda bt, i, j, k, zrb=zrb, zc=z_col:
                                (zrb + bt * (L // tk_f) + k, zc + j)),
            o_spec=pl.BlockSpec((None, tm_f, tn), lambda bt, i, j, k: (bt, i, j)),
            out_shape=(nb, 2 * L, c), out_dtype=BF16,
            extra=(kf,), extra_specs=(pl.BlockSpec((tm_f, tn), lambda bt, i, j, k, o=o: (i, o * cb + j)),),
            epilogue=_spectral_epilogue, name="hyena_fwd")
        zrb_i = z_rows // tm_i
        urb_i = row0 // tm_i
        last = o == 1
        orb = urb_i if last else 0
        z_new = matmul(
            g_tab, p, grid=(nb, L // tm_i, cb, 2 * L // tk_i), tm=tm_i, tn=tn,
            a_spec=pl.BlockSpec((tm_i, tk_i), lambda bt, i, j, k: (i, k)),
            b_spec=pl.BlockSpec((None, tk_i, tn), lambda bt, i, j, k: (bt, k, j)),
            o_spec=pl.BlockSpec((tm_i, tn), lambda bt, i, j, k, orb=orb: (orb + bt * (L // tm_i) + i, j)),
            out_shape=(out_rows if last else nb * L, c), out_dtype=BF16, fill=fill if last else None,
            extra=(u, z, hy_bias.reshape(2, 1, c)),
            extra_specs=(pl.BlockSpec((tm_i, tn), lambda bt, i, j, k, o=o:
                                      (urb_i + bt * (L // tm_i) + i, o * cb + j)),
                         pl.BlockSpec((tm_i, tn), lambda bt, i, j, k, zrb_i=zrb_i, zc=z_col:
                                      (zrb_i + bt * (L // tm_i) + i, zc + j)),
                         pl.BlockSpec((None, 1, tn), lambda bt, i, j, k, o=o: (o, 0, j))),
            epilogue=lambda acc, g_ref, z_ref, b_ref, row_tile:
                g_ref[...].astype(F32) * (acc + b_ref[...] * z_ref[...].astype(F32)),
            name="hyena_inv")
        z, z_rows, z_col = z_new, 0, 0
    return z


def hyena_spectrum(L, filt, a_tab):
    n = filt.shape[2]
    b = filt.reshape(2 * L, n)
    tm, tn, tk = _tile(2 * L, 512), _tile(n, 512), _tile(2 * L, 2048)
    return matmul(
        a_tab, b, grid=(1, 2 * L // tm, n // tn, 2 * L // tk), tm=tm, tn=tn,
        a_spec=pl.BlockSpec((tm, tk), lambda bt, i, j, k: (i, k)),
        b_spec=pl.BlockSpec((tk, tn), lambda bt, i, j, k: (k, j)),
        o_spec=pl.BlockSpec((tm, tn), lambda bt, i, j, k: (i, j)),
        out_shape=(2 * L, n), out_dtype=F32, name="hyena_spectrum")


def _fft_tab_body(o_ref, *, kind, n1, n2, k1_len):
    shape = o_ref.shape
    r = lax.broadcasted_iota(jnp.int32, shape, 0) + pl.program_id(0) * shape[0]
    c = lax.broadcasted_iota(jnp.int32, shape, 1)
    n = n1 * n2
    lg = lambda v: int(math.log2(v))
    s8 = BF16_SUBLANES
    if kind in ("t1f", "t1i"):
        if kind == "t1f":
            grp, kp, s, n1i, s2 = r >> lg(2 * n1 * s8), (r >> lg(s8)) & (2 * n1 - 1), r & (s8 - 1), c >> lg(s8), c & (s8 - 1)
        else:
            grp, n1i, s, kp, s2 = r >> lg(k1_len * s8), (r >> lg(s8)) & (k1_len - 1), r & (s8 - 1), c >> lg(s8), c & (s8 - 1)
        t = n2 * n1i + grp * s8 + s
        val = _cos_turns(((kp & (n1 - 1)) * t + (kp >> lg(n1)) * (n // 4)) & (n - 1), lg(n))
        val = jnp.where(s == s2, val * (1.0 / n if kind == "t1i" else 1.0), 0.0)
    else:
        rp = r >= n2
        cblk = c >> lg(n2)
        cp = (cblk & 1) == 1
        sin_sign = -1 if kind == "f2i" else 1
        quarter = jnp.where(rp == cp, 0, jnp.where(cp, -sin_sign, sin_sign))
        val = _cos_turns(((r & (n2 - 1)) * (c & (n2 - 1)) + quarter * (n2 // 4)) & (n2 - 1), lg(n2))
        if kind == "f2filt":
            val = jnp.where((cblk >= 2) & rp, -val, val)
    o_ref[...] = val.astype(o_ref.dtype)


def fft_table(kind, n1, n2):
    k1_len = n1 // 2
    s8 = BF16_SUBLANES
    groups = n2 // s8
    shape = {"t1f": (groups * 2 * n1 * s8, k1_len * s8), "t1i": (groups * k1_len * s8, 2 * n1 * s8),
             "f2f": (2 * n2, 2 * n2), "f2i": (2 * n2, 2 * n2), "f2filt": (2 * n2, 4 * n2)}[kind]
    tr = _tile(shape[0], 1024)
    out = pl.pallas_call(
        functools.partial(_fft_tab_body, kind=kind, n1=n1, n2=n2, k1_len=k1_len),
        out_shape=jax.ShapeDtypeStruct(shape, BF16),
        grid=(shape[0] // tr,),
        out_specs=pl.BlockSpec((tr, shape[1]), lambda i: (i, 0)),
        compiler_params=_cp("parallel"),
        name="fft_table_" + kind,
    )()
    if kind in ("t1f", "t1i"):
        return out.reshape(groups, shape[0] // groups, shape[1])
    return out


FFT_K1_PER_STEP = 4


def _fft_mid_body(f_ref, *refs, mode, n2):
    f = f_ref[...]
    if mode == "inv":
        p_ref, o_ref = refs
        for k in range(FFT_K1_PER_STEP):
            res = jnp.dot(f, p_ref[k * 2 * n2:(k + 1) * 2 * n2, :], preferred_element_type=F32)
            o_ref[0, k * n2:(k + 1) * n2, :] = res[:n2].astype(o_ref.dtype)
            o_ref[1, k * n2:(k + 1) * n2, :] = res[n2:].astype(o_ref.dtype)
        return
    if mode == "fwd":
        y_ref, kf_ref, o_ref = refs
        srcs = (y_ref,)
    else:
        y0_ref, y1_ref, o_ref = refs
        srcs = (y0_ref, y1_ref)
    for k in range(FFT_K1_PER_STEP):
        rows = slice(k * n2, (k + 1) * n2)
        stack = jnp.concatenate([s[part, rows, :] for s in srcs for part in range(2)], axis=0)
        acc = jnp.dot(f, stack, preferred_element_type=F32)
        out_rows = slice(k * 2 * n2, (k + 1) * 2 * n2)
        if mode == "fwd":
            zr, zi = acc[:n2], acc[n2:]
            kr = kf_ref[k * 2 * n2:k * 2 * n2 + n2, :]
            ki = kf_ref[k * 2 * n2 + n2:(k + 1) * 2 * n2, :]
            acc = jnp.concatenate([zr * kr - zi * ki, zr * ki + zi * kr], axis=0)
        o_ref[out_rows, :] = acc.astype(o_ref.dtype)


def fft_mid(mode, f_tab, src, n1, n2, *, kf=None, kf_col0=0, out_dtype=BF16, name):
    kb = FFT_K1_PER_STEP
    c = src.shape[-1]
    tn = _tile(c, 1024)
    cb = c // tn
    nb = 1 if mode == "filt" else src.shape[0]
    y_block = (None, 2, kb * n2, tn)
    p_block = (None, kb * 2 * n2, tn)
    f_spec = pl.BlockSpec(f_tab.shape, lambda i, b, j: (0, 0))
    if mode == "fwd":
        in_specs = [f_spec, pl.BlockSpec(y_block, lambda i, b, j: (b, 0, i, j)),
                    pl.BlockSpec((kb * 2 * n2, tn), lambda i, b, j: (i, kf_col0 // tn + j))]
        args = (f_tab, src, kf)
        out_shape, out_spec = (nb, n1 * 2 * n2, c), pl.BlockSpec(p_block, lambda i, b, j: (b, i, j))
    elif mode == "filt":
        in_specs = [f_spec, pl.BlockSpec(y_block, lambda i, b, j: (0, 0, i, j)),
                    pl.BlockSpec(y_block, lambda i, b, j: (1, 0, i, j))]
        args = (f_tab, src, src)
        out_shape, out_spec = (n1 * 2 * n2, c), pl.BlockSpec((kb * 2 * n2, tn), lambda i, b, j: (i, j))
    else:
        in_specs = [f_spec, pl.BlockSpec(p_block, lambda i, b, j: (b, i, j))]
        args = (f_tab, src)
        out_shape, out_spec = (nb, 2, n1 * n2, c), pl.BlockSpec(y_block, lambda i, b, j: (b, 0, i, j))
    return pl.pallas_call(
        functools.partial(_fft_mid_body, mode=mode, n2=n2),
        out_shape=jax.ShapeDtypeStruct(out_shape, out_dtype),
        grid=(n1 // kb, nb, cb),
        in_specs=in_specs,
        out_specs=out_spec,
        compiler_params=_cp("parallel", "parallel", "parallel"),
        name=name,
    )(*args)


def hyena_spectrum_2level(L, filt, t1f, f2filt):
    n2 = SPEC_BLOCK
    n1 = 2 * L // n2
    k1 = n1 // 2
    n = filt.shape[2]
    tn = _tile(n, 1024)
    nj = n // tn
    sl = BF16_SUBLANES
    y1 = matmul(
        t1f, filt.reshape(2 * k1, n2, n), grid=(2, n2 // sl, nj, 1), tm=2 * n1 * sl, tn=tn,
        a_spec=pl.BlockSpec((None, 2 * n1 * sl, k1 * sl), lambda bt, i, j, k: (i, 0, 0)),
        b_spec=pl.BlockSpec((k1, sl, tn), lambda bt, i, j, k: (bt, i, j)),
        o_spec=pl.BlockSpec((None, 2 * n1, sl, tn), lambda bt, i, j, k: (bt, 0, i, j)),
        out_shape=(2, 2 * n1, n2, n), out_dtype=BF16, name="hyena_filt_fft1")
    return fft_mid("filt", f2filt, y1.reshape(2, 2, n1 * n2, n), n1, n2, out_dtype=F32, name="hyena_filt_fft2")


def hyena_stream_2level(u, row0, nb, L, kf, tabs, hy_bias, c, fill=None):
    t1f, f2f, f2i, t1i = tabs
    n2 = SPEC_BLOCK
    n1 = 2 * L // n2
    k1 = n1 // 2
    sl = BF16_SUBLANES
    ng = n2 // sl
    t1n = _tile(c, 1024)
    c1b = c // t1n
    tn = _tile(c, 2048)
    cb = c // tn
    t, u_w = u.shape
    assert row0 % L == 0 and u_w % tn == 0
    uv = u.reshape(t // n2, n2, u_w)
    urow = row0 // L
    zv, zr, zc = uv, urow, 2 * c // t1n
    bias3 = hy_bias.reshape(2, 1, c)
    for o in range(2):
        y1 = matmul(
            t1f, zv, grid=(nb, ng, c1b, 1), tm=2 * n1 * sl, tn=t1n,
            a_spec=pl.BlockSpec((None, 2 * n1 * sl, k1 * sl), lambda bt, i, j, k: (i, 0, 0)),
            b_spec=pl.BlockSpec((k1, sl, t1n), lambda bt, i, j, k, zr=zr, zc=zc: (zr + bt, i, zc + j)),
            o_spec=pl.BlockSpec((None, 2 * n1, sl, t1n), lambda bt, i, j, k: (bt, 0, i, j)),
            out_shape=(nb, 2 * n1, n2, c), out_dtype=BF16, name="hyena_fft1")
        p = fft_mid("fwd", f2f, y1.reshape(nb, 2, n1 * n2, c), n1, n2, kf=kf, kf_col0=o * c, name="hyena_fft2")
        u1 = fft_mid("inv", f2i, p, n1, n2, name="hyena_ifft1")
        last = o == 1
        orow = urow if last else 0

        def gate_epilogue(acc, g_ref, z_ref, b_ref, row_tile):
            g = g_ref[...].reshape(acc.shape).astype(F32)
            return g * (acc + b_ref[...] * z_ref[...].reshape(acc.shape).astype(F32))

        z = matmul(
            t1i, u1.reshape(nb, 2 * n1, n2, c), grid=(nb, ng, c1b, 1), tm=k1 * sl, tn=t1n,
            a_spec=pl.BlockSpec((None, k1 * sl, 2 * n1 * sl), lambda bt, i, j, k: (i, 0, 0)),
            b_spec=pl.BlockSpec((None, 2 * n1, sl, t1n), lambda bt, i, j, k: (bt, 0, i, j)),
            o_spec=pl.BlockSpec((k1, sl, t1n), lambda bt, i, j, k, orow=orow: (orow + bt, i, j)),
            out_shape=(t // n2 if last else nb * k1, n2, c), out_dtype=BF16,
            fill=None if (fill is None or not last) else fill.reshape(t // n2, n2, c),
            extra=(uv, zv, bias3),
            extra_specs=(pl.BlockSpec((k1, sl, t1n), lambda bt, i, j, k, o=o: (urow + bt, i, o * c1b + j)),
                         pl.BlockSpec((k1, sl, t1n), lambda bt, i, j, k, zr=zr, zc=zc: (zr + bt, i, zc + j)),
                         pl.BlockSpec((None, 1, t1n), lambda bt, i, j, k, o=o: (o, 0, j))),
            epilogue=gate_epilogue, name="hyena_ifft2")
        zv, zr, zc = z, 0, 0
    return zv.reshape(t, c)


def _ssd_prep_body(dc_ref, dr_ref, biasc_ref, biasr_ref, ac_ref, ar_ref, dt_ref, cumc_ref, cumr_ref, *, heads):
    q = dc_ref.shape[0]
    dn = (((1,), (0,)), ((), ()))
    li = lax.broadcasted_iota(jnp.int32, (q, q), 0)
    si = lax.broadcasted_iota(jnp.int32, (q, q), 1)
    lower = (li >= si).astype(F32)
    upper = (li <= si).astype(F32)
    dt_c = _softplus(dc_ref[...] + biasc_ref[...])
    da_c = dt_c * ac_ref[...]
    da_r = _softplus(dr_ref[...] + biasr_ref[...]) * ar_ref[...]
    pre_c = lax.dot_general(lower, da_c, dn, precision=HI, preferred_element_type=F32)
    suf_c = lax.dot_general(upper, da_c, dn, precision=HI, preferred_element_type=F32)
    pre_r = lax.dot_general(da_r, upper, dn, precision=HI, preferred_element_type=F32)
    suf_r = lax.dot_general(da_r, lower, dn, precision=HI, preferred_element_type=F32)
    dt_ref[...] = dt_c
    cumc_ref[...] = jnp.where(lax.broadcasted_iota(jnp.int32, pre_c.shape, 1) < heads, pre_c, suf_c)
    cumr_ref[...] = jnp.where(lax.broadcasted_iota(jnp.int32, pre_r.shape, 0) < heads, pre_r, suf_r)


def ssd_prep(dt_raw, dt_raw_t, dt_bias, a_log):
    t, h2 = dt_raw.shape
    q = SSD_CHUNK
    neg_a = -jnp.exp(a_log.astype(F32)).reshape(1, h2)
    bias = dt_bias.astype(F32).reshape(1, h2)
    const = lambda shape: pl.BlockSpec(shape, lambda i: (0,) * len(shape))
    col = pl.BlockSpec((q, h2), lambda i: (i, 0))
    row = pl.BlockSpec((h2, q), lambda i: (0, i))
    return pl.pallas_call(
        functools.partial(_ssd_prep_body, heads=h2 // 2),
        out_shape=(jax.ShapeDtypeStruct((t, h2), F32), jax.ShapeDtypeStruct((t, h2), F32),
                   jax.ShapeDtypeStruct((h2, t), F32)),
        grid=(t // q,),
        in_specs=[col, row, const((1, h2)), const((h2, 1)), const((1, h2)), const((h2, 1))],
        out_specs=(col, col, row),
        compiler_params=_cp("parallel"),
        name="ssd_prep",
    )(dt_raw, dt_raw_t, bias, bias.reshape(h2, 1), neg_a, neg_a.reshape(h2, 1))


def _expand_heads(x, head0, n_heads, width):
    rows = lax.broadcasted_iota(jnp.int32, (x.shape[1], n_heads * width), 0)
    lanes = lax.broadcasted_iota(jnp.int32, (x.shape[1], n_heads * width), 1)
    sel = jnp.where(rows == head0 + (lanes >> int(math.log2(width))), 1.0, 0.0).astype(BF16)
    out = None
    rest = x
    for _ in range(3):
        piece = rest.astype(BF16)
        rest = rest - piece.astype(F32)
        part = jnp.dot(piece, sel, preferred_element_type=F32)
        out = part if out is None else out + part
    return out


def _ssd_dir(xs, bt, cm, dt_c, cum_c, cumr_ref, state_ref, head0, *, reverse, heads_per_group):
    q = xs.shape[0]
    p = SSD_HEAD_DIM
    j = heads_per_group
    cc = _expand_heads(cum_c, head0, j, p)
    cc_wide = _expand_heads(cum_c, head0, j, q)
    dtv = _expand_heads(dt_c, head0, j, p)
    total = cc[0:1, :] if reverse else cc[q - 1:q, :]
    xd = xs.astype(F32) * dtv
    xd_b = xd.astype(BF16)
    li = lax.broadcasted_iota(jnp.int32, (q, q), 0)
    si = lax.broadcasted_iota(jnp.int32, (q, q), 1)
    mask = (li <= si) if reverse else (li >= si)
    cb = jnp.dot(cm, bt, preferred_element_type=F32)
    ys = []
    for jh in range(j):
        cr = cumr_ref[pl.ds(head0 + jh, 1), :]
        decay = jnp.exp(jnp.where(mask, cc_wide[:, jh * q:(jh + 1) * q] - cr, -jnp.inf))
        ys.append(jnp.dot((cb * decay).astype(BF16), xd_b[:, jh * p:(jh + 1) * p], preferred_element_type=F32))
    s_prev = state_ref[...]
    y = jnp.concatenate(ys, axis=1) + jnp.exp(cc) * jnp.dot(cm, s_prev.astype(BF16), preferred_element_type=F32)
    st = jnp.dot(bt, (xd * jnp.exp(total - cc)).astype(BF16), preferred_element_type=F32)
    state_ref[...] = jnp.exp(total) * s_prev + st
    return y


def _ssd_body(tab_ref, xf_ref, btf_ref, cf_ref, dtf_ref, ccf_ref, crf_ref, xb_ref, btb_ref, cb_ref, dtb_ref, ccb_ref,
              crb_ref, initf_ref, initb_ref, yf_ref, yb_ref, sf_ref, sb_ref, stf_ref, stb_ref,
              *, heads_per_group, heads):
    g = pl.program_id(0)
    item = pl.program_id(1)

    @pl.when(tab_ref[_SSD_FIRST, item] == 1)
    def _():
        stf_ref[...] = initf_ref[...]
        stb_ref[...] = initb_ref[...]

    jp = heads_per_group * SSD_HEAD_DIM
    n = SSD_STATE
    for gi in range(SSD_GROUPS_PER_STEP):
        head0 = (g * SSD_GROUPS_PER_STEP + gi) * heads_per_group
        cols, srow = slice(gi * jp, (gi + 1) * jp), slice(gi * n, (gi + 1) * n)
        yf_ref[:, cols] = _ssd_dir(xf_ref[:, cols], btf_ref[srow, :], cf_ref[:, srow], dtf_ref[...], ccf_ref[...],
                                   crf_ref, stf_ref.at[gi], head0, reverse=False, heads_per_group=heads_per_group)
        yb_ref[:, cols] = _ssd_dir(xb_ref[:, cols], btb_ref[srow, :], cb_ref[:, srow], dtb_ref[...], ccb_ref[...],
                                   crb_ref, stb_ref.at[gi], heads + head0, reverse=True,
                                   heads_per_group=heads_per_group)

    @pl.when(tab_ref[_SSD_LAST, item] == 1)
    def _():
        sf_ref[...] = stf_ref[...]
        sb_ref[...] = stb_ref[...]


_SSD_FWD, _SSD_BWD, _SSD_SEQ, _SSD_FIRST, _SSD_LAST = range(5)
SSD_GROUPS_PER_STEP = 2


def ssd_scan(u, bt_all, dt, cum_c, cum_r, init_f, init_b, *, seq_lens, col0, inner):
    q, p, n, gq = SSD_CHUNK, SSD_HEAD_DIM, SSD_STATE, SSD_GROUPS
    heads = inner // p
    j = heads // gq
    jp = j * p
    xcb = col0 // jp
    ccb = (col0 + inner + gq * n) // n
    rows, base = [], 0
    for s, length in enumerate(seq_lens):
        nc = length // q
        rows += [(base + c, base + nc - 1 - c, s, int(c == 0), int(c == nc - 1)) for c in range(nc)]
        base += nc
    table = jnp.array(rows, jnp.int32).T
    nseq, n_items = len(seq_lens), len(rows)

    gs = SSD_GROUPS_PER_STEP
    assert xcb % gs == 0 and ccb % gs == 0 and gq % gs == 0

    def specs(which):
        return [pl.BlockSpec((q, gs * jp), lambda g, it, tab: (tab[which, it], xcb // gs + g)),
                pl.BlockSpec((gs * n, q), lambda g, it, tab: (g, tab[which, it])),
                pl.BlockSpec((q, gs * n), lambda g, it, tab: (tab[which, it], ccb // gs + g)),
                pl.BlockSpec((q, 2 * heads), lambda g, it, tab: (tab[which, it], 0)),
                pl.BlockSpec((q, 2 * heads), lambda g, it, tab: (tab[which, it], 0)),
                pl.BlockSpec((2 * heads, q), lambda g, it, tab: (0, tab[which, it]))]

    st_spec = pl.BlockSpec((None, gs, n, jp), lambda g, it, tab: (tab[_SSD_SEQ, it], g, 0, 0))
    st_shape = jax.ShapeDtypeStruct((nseq, gq, n, jp), F32)
    y_shape = jax.ShapeDtypeStruct((u.shape[0], inner), F32)
    grid_spec = pltpu.PrefetchScalarGridSpec(
        num_scalar_prefetch=1,
        grid=(gq // gs, n_items),
        in_specs=[*specs(_SSD_FWD), *specs(_SSD_BWD), st_spec, st_spec],
        out_specs=(pl.BlockSpec((q, gs * jp), lambda g, it, tab: (tab[_SSD_FWD, it], g)),
                   pl.BlockSpec((q, gs * jp), lambda g, it, tab: (tab[_SSD_BWD, it], g)), st_spec, st_spec),
        scratch_shapes=[pltpu.VMEM((gs, n, jp), F32), pltpu.VMEM((gs, n, jp), F32)])
    return pl.pallas_call(
        functools.partial(_ssd_body, heads_per_group=j, heads=heads),
        out_shape=(y_shape, y_shape, st_shape, st_shape),
        grid_spec=grid_spec,
        compiler_params=_cp("parallel", "arbitrary"),
        name="ssd_scan",
    )(table, u, bt_all, u, dt, cum_c, cum_r, u, bt_all, u, dt, cum_c, cum_r, init_f, init_b)


def _state_to_kernel_layout(s, groups):
    b, h, p, n = s.shape
    return s.reshape(b, groups, h // groups, p, n).transpose(0, 1, 4, 2, 3).reshape(b, groups, n, (h // groups) * p)


def _state_from_kernel_layout(s, head_dim):
    b, g, n, jp = s.shape
    j = jp // head_dim
    return s.reshape(b, g, n, j, head_dim).transpose(0, 1, 3, 4, 2).reshape(b, g * j, head_dim, n)


def _ssd_finish_body(yf_ref, yb_ref, xs_ref, z_ref, d_ref, w_ref, o_ref):
    y = yf_ref[...] + yb_ref[...] + xs_ref[...].astype(F32) * d_ref[...]
    y = y * _silu(z_ref[...].astype(F32))
    y = y * lax.rsqrt(jnp.mean(y * y, axis=-1, keepdims=True) + RMS_EPS)
    o_ref[...] = (y * w_ref[...]).astype(o_ref.dtype)


def ssd_finish(yf, yb, u, proj, d_full, norm_w, *, xs_col, z_col, inner):
    t = yf.shape[0]
    tr = ROW_TILE
    return pl.pallas_call(
        _ssd_finish_body,
        out_shape=jax.ShapeDtypeStruct((t, inner), BF16),
        grid=(t // tr,),
        in_specs=[pl.BlockSpec((tr, inner), lambda i: (i, 0)),
                  pl.BlockSpec((tr, inner), lambda i: (i, 0)),
                  pl.BlockSpec((tr, inner), lambda i: (i, xs_col // inner)),
                  pl.BlockSpec((tr, inner), lambda i: (i, z_col // inner)),
                  pl.BlockSpec((1, inner), lambda i: (0, 0)),
                  pl.BlockSpec((1, inner), lambda i: (0, 0))],
        out_specs=pl.BlockSpec((tr, inner), lambda i: (i, 0)),
        compiler_params=_cp("parallel"),
        name="ssd_finish",
    )(yf, yb, u, proj, d_full.reshape(1, inner), norm_w.reshape(1, inner))


KV_PER_STEP = LANES // HEAD_DIM


_NT = (((1,), (1,)), ((), ()))


def _attend(scores, values, sink):
    m = sink
    for s in scores:
        m = jnp.maximum(m, jnp.max(s, axis=-1, keepdims=True))
    denom = jnp.exp(sink - m)
    acc = None
    for s, v in zip(scores, values):
        e = jnp.exp(s - m)
        denom = denom + jnp.sum(e, axis=-1, keepdims=True)
        part = jnp.dot(e.astype(BF16), v, preferred_element_type=F32)
        acc = part if acc is None else acc + part
    return acc * (1.0 / denom)


def _stack_heads(q, first_head, n, rows):
    return jnp.concatenate([q[:, (first_head + g) * HEAD_DIM:(first_head + g + 1) * HEAD_DIM] for g in range(n)],
                           axis=0)


def _stacked_sinks(sink_ref, first_head, n, rows):
    row = lax.broadcasted_iota(jnp.int32, (n * rows, 1), 0)
    col = jnp.full((n * rows, 1), sink_ref[first_head + n - 1], F32)
    for g in reversed(range(n - 1)):
        col = jnp.where(row < (g + 1) * rows, sink_ref[first_head + g], col)
    return col


def _ctx_attn_body(sink_ref, q_ref, k_ref, v_ref, fill_ref, o_ref, *, q_per_kv, scale):
    del fill_ref
    hb = pl.program_id(1)
    rows = q_ref.shape[0]
    q_all = q_ref[...] * scale
    outs = []
    for kv in range(KV_PER_STEP):
        k = k_ref[:, kv * HEAD_DIM:(kv + 1) * HEAD_DIM].astype(BF16)
        v = v_ref[:, kv * HEAD_DIM:(kv + 1) * HEAD_DIM].astype(BF16)
        qs = _stack_heads(q_all, kv * q_per_kv, q_per_kv, rows).astype(BF16)
        s = lax.dot_general(qs, k, _NT, preferred_element_type=F32)
        sink = _stacked_sinks(sink_ref, (hb * KV_PER_STEP + kv) * q_per_kv, q_per_kv, rows)
        o = _attend([s], [v], sink)
        outs += [o[g * rows:(g + 1) * rows] for g in range(q_per_kv)]
    o_ref[...] = jnp.concatenate(outs, axis=1).astype(o_ref.dtype)


def context_attention(qkv, sink, *, nseq, L, n_heads):
    q_per_kv = n_heads // N_KV_HEADS
    qw = KV_PER_STEP * q_per_kv * HEAD_DIM
    nhb = N_KV_HEADS // KV_PER_STEP
    kcb = n_heads * HEAD_DIM // LANES
    vcb = kcb + N_KV_HEADS * HEAD_DIM // LANES
    return pl.pallas_call(
        functools.partial(_ctx_attn_body, q_per_kv=q_per_kv, scale=HEAD_DIM ** -0.5),
        out_shape=jax.ShapeDtypeStruct((qkv.shape[0], n_heads * HEAD_DIM), BF16),
        grid=(nseq, nhb),
        in_specs=[pl.BlockSpec(memory_space=pltpu.SMEM),
                  pl.BlockSpec((L, qw), lambda b, h: (b, h)),
                  pl.BlockSpec((L, LANES), lambda b, h: (b, kcb + h)),
                  pl.BlockSpec((L, LANES), lambda b, h: (b, vcb + h)),
                  pl.BlockSpec(memory_space=pl.ANY)],
        out_specs=pl.BlockSpec((L, qw), lambda b, h: (b, h)),
        input_output_aliases={4: 0},
        compiler_params=_cp("parallel", "parallel"),
        name="context_attention",
    )(sink, qkv, qkv, qkv, jnp.zeros((qkv.shape[0], n_heads * HEAD_DIM), BF16))


def _rope(x, cos, sin):
    outs = []
    lane = lax.broadcasted_iota(jnp.int32, cos.shape, 1)
    first = (lane % (HEAD_DIM // 2)) < (HEAD_DIM // 4)
    for cgrp in range(x.shape[1] // LANES):
        xc = x[:, cgrp * LANES:(cgrp + 1) * LANES]
        partner = jnp.where(first, pltpu.roll(xc, LANES - HEAD_DIM // 4, 1), pltpu.roll(xc, HEAD_DIM // 4, 1))
        outs.append(xc * cos + partner * sin)
    return outs[0] if len(outs) == 1 else jnp.concatenate(outs, axis=1)


def _lat_attn_body(sink_ref, q_ref, k_ref, v_ref, kc_ref, vc_ref, cosq_ref, sinq_ref, cos_ref, sin_ref, fill_ref,
                   o_ref, k_scr, v_scr, kc_scr, vc_scr, *, q_per_kv, scale, L, blk):
    del fill_ref
    hb = pl.program_id(1)
    i = pl.program_id(2)

    @pl.when(i == 0)
    def _():
        k_scr[...] = _rope(k_ref[...], cos_ref[...], sin_ref[...]).astype(BF16)
        v_scr[...] = v_ref[...].astype(BF16)
        kc_scr[...] = kc_ref[...].astype(BF16)
        vc_scr[...] = vc_ref[...].astype(BF16)

    n_loc = 3 * blk
    start = jnp.clip((i - 1) * blk, 0, L - n_loc)
    start = pl.multiple_of(start, blk)
    q_all = _rope(q_ref[...], cosq_ref[...], sinq_ref[...]) * scale
    k_loc = k_scr[pl.ds(start, n_loc), :]
    v_loc = v_scr[pl.ds(start, n_loc), :]
    rows = q_per_kv * blk
    qpos = i * blk + (lax.broadcasted_iota(jnp.int32, (rows, n_loc), 0) & (blk - 1))
    kpos = start + lax.broadcasted_iota(jnp.int32, (rows, n_loc), 1)
    ok = jnp.abs(qpos - kpos) <= WINDOW
    outs = []
    for kv in range(KV_PER_STEP):
        sl = slice(kv * HEAD_DIM, (kv + 1) * HEAD_DIM)
        qs = _stack_heads(q_all, kv * q_per_kv, q_per_kv, blk).astype(BF16)
        s_loc = jnp.where(ok, lax.dot_general(qs, k_loc[:, sl], _NT, preferred_element_type=F32), -jnp.inf)
        s_ctx = lax.dot_general(qs, kc_scr[:, sl], _NT, preferred_element_type=F32)
        sink = _stacked_sinks(sink_ref, (hb * KV_PER_STEP + kv) * q_per_kv, q_per_kv, blk)
        o = _attend([s_loc, s_ctx], [v_loc[:, sl], vc_scr[:, sl]], sink)
        outs += [o[g * blk:(g + 1) * blk] for g in range(q_per_kv)]
    o_ref[...] = jnp.concatenate(outs, axis=1).astype(o_ref.dtype)


def rope_tables(L):
    rows = L // GRID_W
    row = jnp.repeat(jnp.arange(rows, dtype=F32), GRID_W)
    col = jnp.tile(jnp.arange(GRID_W, dtype=F32), rows)
    quarter = HEAD_DIM // 4
    inv = ROPE_BASE ** (-jnp.arange(quarter, dtype=F32) / quarter)
    ang_r = row[:, None] * inv[None, :]
    ang_c = col[:, None] * inv[None, :]
    ang = jnp.concatenate([ang_r, ang_r, ang_c, ang_c], axis=1)
    sign = jnp.tile(jnp.concatenate([-jnp.ones((quarter,), F32), jnp.ones((quarter,), F32)]), 2)
    cos = jnp.tile(jnp.cos(ang), (1, LANES // HEAD_DIM))
    sin = jnp.tile(jnp.sin(ang) * sign[None, :], (1, LANES // HEAD_DIM))
    return cos, sin


def latent_attention(qkv, k_cache, v_cache, sink, fill, *, row0, nseq, L, n_heads):
    blk = WINDOW
    q_per_kv = n_heads // N_KV_HEADS
    qw = KV_PER_STEP * q_per_kv * HEAD_DIM
    nhb = N_KV_HEADS // KV_PER_STEP
    kcb = n_heads * HEAD_DIM // LANES
    vcb = kcb + N_KV_HEADS * HEAD_DIM // LANES
    nb = L // blk
    past = k_cache.shape[1]
    cos, sin = rope_tables(L)
    return pl.pallas_call(
        functools.partial(_lat_attn_body, q_per_kv=q_per_kv, scale=HEAD_DIM ** -0.5, L=L, blk=blk),
        out_shape=jax.ShapeDtypeStruct(fill.shape, BF16),
        grid=(nseq, nhb, nb),
        in_specs=[pl.BlockSpec(memory_space=pltpu.SMEM),
                  pl.BlockSpec((blk, qw), lambda b, h, i: (row0 // blk + b * nb + i, h)),
                  pl.BlockSpec((L, LANES), lambda b, h, i: (row0 // L + b, kcb + h)),
                  pl.BlockSpec((L, LANES), lambda b, h, i: (row0 // L + b, vcb + h)),
                  pl.BlockSpec((None, past, LANES), lambda b, h, i: (b, 0, h)),
                  pl.BlockSpec((None, past, LANES), lambda b, h, i: (b, 0, h)),
                  pl.BlockSpec((blk, LANES), lambda b, h, i: (i, 0)),
                  pl.BlockSpec((blk, LANES), lambda b, h, i: (i, 0)),
                  pl.BlockSpec((L, LANES), lambda b, h, i: (0, 0)),
                  pl.BlockSpec((L, LANES), lambda b, h, i: (0, 0)),
                  pl.BlockSpec(memory_space=pl.ANY)],
        out_specs=pl.BlockSpec((blk, qw), lambda b, h, i: (row0 // blk + b * nb + i, h)),
        scratch_shapes=[pltpu.VMEM((L, LANES), BF16), pltpu.VMEM((L, LANES), BF16),
                        pltpu.VMEM((past, LANES), BF16), pltpu.VMEM((past, LANES), BF16)],
        input_output_aliases={10: 0},
        compiler_params=_cp("parallel", "parallel", "arbitrary"),
        name="latent_attention",
    )(sink, qkv, qkv, qkv, k_cache, v_cache, cos, sin, cos, sin, fill)


MOE_TILE = 256
MOE_FFN_VMEM_BYTES = 56 * 1024 * 1024


def _moe_dispatch_body(pos_ref, h_ref, g_ref, init_hbm, xs_hbm, buf, sem):
    del init_hbm
    i = pl.program_id(0)
    tr, d = h_ref.shape
    buf[:, :d] = h_ref[...]
    buf[:, d:] = g_ref[...]

    def start(r2, carry):
        for prio in range(DMA_PRIORITIES):
            r = r2 * DMA_PRIORITIES + prio
            pltpu.make_async_copy(buf.at[pl.ds(r, 1)], xs_hbm.at[pl.ds(pos_ref[i * tr + r], 1)],
                                  sem).start(priority=prio)
        return carry

    lax.fori_loop(0, tr // DMA_PRIORITIES, start, 0)

    def wait(r, carry):
        pltpu.make_async_copy(buf.at[pl.ds(r, 1)], xs_hbm.at[pl.ds(0, 1)], sem).wait()
        return carry

    lax.fori_loop(0, tr, wait, 0)


def moe_dispatch(h, gate_lanes, pos, n_slots):
    t, d = h.shape
    tr = ROW_TILE
    w = d + LANES
    grid_spec = pltpu.PrefetchScalarGridSpec(
        num_scalar_prefetch=1,
        grid=(t // tr,),
        in_specs=[pl.BlockSpec((tr, d), lambda i, p: (i, 0)),
                  pl.BlockSpec((tr, LANES), lambda i, p: (i, 0)),
                  pl.BlockSpec(memory_space=pl.ANY)],
        out_specs=pl.BlockSpec(memory_space=pl.ANY),
        scratch_shapes=[pltpu.VMEM((tr, w), F32), pltpu.SemaphoreType.DMA(())])
    return pl.pallas_call(
        _moe_dispatch_body,
        out_shape=jax.ShapeDtypeStruct((n_slots, w), F32),
        grid_spec=grid_spec,
        input_output_aliases={3: 0},
        compiler_params=_cp("arbitrary"),
        name="moe_dispatch",
    )(pos, h, gate_lanes, jnp.zeros((n_slots, w), F32))


def _moe_ffn_body(ea_ref, eb_ref, nused_ref, xs_ref, w1a_ref, w3a_ref, w2a_ref, w1b_ref, w3b_ref, w2b_ref, o_ref):
    i = pl.program_id(0)
    d = o_ref.shape[1]

    @pl.when(i < nused_ref[0])
    def _():
        x = xs_ref[:, :d].astype(BF16)

        def expert(w1_ref, w3_ref, w2_ref):
            a = jnp.dot(x, w1_ref[...], preferred_element_type=F32)
            b = jnp.dot(x, w3_ref[...], preferred_element_type=F32)
            return jnp.dot((_silu(a) * b).astype(BF16), w2_ref[...], preferred_element_type=F32)

        o_ref[...] = (xs_ref[:, d:d + 1] * expert(w1a_ref, w3a_ref, w2a_ref)
                      + xs_ref[:, d + 1:d + 2] * expert(w1b_ref, w3b_ref, w2b_ref))

    @pl.when(i >= nused_ref[0])
    def _():
        o_ref[...] = jnp.zeros_like(o_ref)


def moe_ffn(xs, tile_ea, tile_eb, n_used, w1, w3, w2):
    _, d, de = w1.shape
    tm = MOE_TILE
    n_tiles = tile_ea.shape[0]
    up_a = pl.BlockSpec((None, d, de), lambda i, ea, eb, nu: (ea[i], 0, 0))
    up_b = pl.BlockSpec((None, d, de), lambda i, ea, eb, nu: (eb[i], 0, 0))
    grid_spec = pltpu.PrefetchScalarGridSpec(
        num_scalar_prefetch=3,
        grid=(n_tiles,),
        in_specs=[pl.BlockSpec((tm, d + LANES), lambda i, ea, eb, nu: (i, 0)),
                  up_a, up_a, pl.BlockSpec((None, de, d), lambda i, ea, eb, nu: (ea[i], 0, 0)),
                  up_b, up_b, pl.BlockSpec((None, de, d), lambda i, ea, eb, nu: (eb[i], 0, 0))],
        out_specs=pl.BlockSpec((tm, d), lambda i, ea, eb, nu: (i, 0)))
    return pl.pallas_call(
        _moe_ffn_body,
        out_shape=jax.ShapeDtypeStruct((n_tiles * tm, d), F32),
        grid_spec=grid_spec,
        compiler_params=_cp("arbitrary", vmem=MOE_FFN_VMEM_BYTES),
        name="moe_ffn",
    )(tile_ea, tile_eb, n_used, xs, w1, w3, w2, w1, w3, w2)


def _moe_combine_body(pos_ref, y_hbm, x_ref, g_ref, o_ref, ybuf, sem):
    i = pl.program_id(0)
    tr = x_ref.shape[0]

    def start(r2, carry):
        for prio in range(DMA_PRIORITIES):
            r = r2 * DMA_PRIORITIES + prio
            pltpu.make_async_copy(y_hbm.at[pl.ds(pos_ref[i * tr + r], 1)], ybuf.at[pl.ds(r, 1)],
                                  sem).start(priority=prio)
        return carry

    lax.fori_loop(0, tr // DMA_PRIORITIES, start, 0)

    def wait(r, carry):
        pltpu.make_async_copy(y_hbm.at[pl.ds(0, 1)], ybuf.at[pl.ds(r, 1)], sem).wait()
        return carry

    lax.fori_loop(0, tr, wait, 0)
    o_ref[...] = x_ref[...] + g_ref[...] * ybuf[...]


def moe_combine(x, y_sorted, pos, mod3, gate_idx, seg_of_tile):
    t, d = x.shape
    tr = ROW_TILE
    grid_spec = pltpu.PrefetchScalarGridSpec(
        num_scalar_prefetch=1,
        grid=(t // tr,),
        in_specs=[pl.BlockSpec(memory_space=pl.ANY),
                  pl.BlockSpec((tr, d), lambda i, p: (i, 0)),
                  pl.BlockSpec((None, 1, d), lambda i, p: (seg_of_tile(i) * 6 + gate_idx, 0, 0))],
        out_specs=pl.BlockSpec((tr, d), lambda i, p: (i, 0)),
        scratch_shapes=[pltpu.VMEM((tr, d), F32), pltpu.SemaphoreType.DMA(())])
    return pl.pallas_call(
        _moe_combine_body,
        out_shape=jax.ShapeDtypeStruct((t, d), F32),
        grid_spec=grid_spec,
        compiler_params=_cp("arbitrary"),
        name="moe_combine",
    )(pos, y_sorted, x, mod3)


def moe_route(logits, n_groups, n_experts):
    per = n_experts // n_groups
    assert TOP_K == 2
    g_prob = jax.nn.softmax(logits[:, :n_groups], axis=-1)
    g_top = jnp.max(g_prob, axis=-1, keepdims=True)
    g_idx = jnp.argmax(g_prob, axis=-1, keepdims=True).astype(jnp.int32)
    e_logits = logits[:, n_groups:n_groups + n_experts].reshape(-1, n_groups, per)
    g_onehot = g_idx == jnp.arange(n_groups)[None, :]
    e_in = jnp.sum(jnp.where(g_onehot[:, :, None], e_logits, 0.0), axis=1)
    e_prob = jax.nn.softmax(e_in, axis=-1)
    i0 = jnp.argmax(e_prob, axis=-1, keepdims=True).astype(jnp.int32)
    rest = jnp.where(jnp.arange(per)[None, :] == i0, -jnp.inf, e_prob)
    i1 = jnp.argmax(rest, axis=-1, keepdims=True).astype(jnp.int32)
    e_idx = jnp.concatenate([i0, i1], axis=1)
    e_top = jnp.concatenate([jnp.max(e_prob, axis=-1, keepdims=True), jnp.max(rest, axis=-1, keepdims=True)], axis=1)
    e_top = e_top / jnp.sum(e_top, axis=-1, keepdims=True)
    return g_idx * per + e_idx, g_top * e_top


def moe_plan(expert_ids, gates, n_groups, n_experts):
    assert TOP_K == 2
    t = expert_ids.shape[0]
    tm = MOE_TILE
    per = n_experts // n_groups
    n_pairs = per * (per - 1) // 2
    n_classes = n_groups * n_pairs
    n_tiles = t // tm + n_classes
    e0, e1 = expert_ids[:, 0], expert_ids[:, 1]
    lo, hi = jnp.minimum(e0, e1), jnp.maximum(e0, e1)
    g_lo = jnp.where(e0 < e1, gates[:, 0], gates[:, 1])
    g_hi = jnp.where(e0 < e1, gates[:, 1], gates[:, 0])
    a, b = lo % per, hi % per
    cls = (lo // per) * n_pairs + (a * (2 * per - a - 1)) // 2 + (b - a - 1)
    onehot = (cls[:, None] == jnp.arange(n_classes)[None, :]).astype(F32)
    blk = ROW_TILE
    inner = jnp.einsum("ij,bjk->bik", jnp.tril(jnp.ones((blk, blk), F32)), onehot.reshape(-1, blk, n_classes))
    bsum = inner[:, -1, :]
    running = (inner + (jnp.cumsum(bsum, axis=0) - bsum)[:, None, :]).reshape(-1, n_classes)
    counts = jnp.sum(bsum, axis=0).astype(jnp.int32)
    tiles_per = (counts + tm - 1) // tm
    tile_start = jnp.cumsum(tiles_per) - tiles_per
    slot = running - 1.0 + (tile_start * tm).astype(F32)[None, :]
    pos = jnp.sum(jnp.where(onehot > 0, slot, 0.0), axis=1).astype(jnp.int32)
    n_used = jnp.sum(tiles_per).astype(jnp.int32)
    tile_ids = jnp.arange(n_tiles, dtype=jnp.int32)
    tile_cls = jnp.sum((tile_ids[:, None] >= tile_start[None, :]).astype(jnp.int32), axis=1) - 1
    tile_cls = jnp.where(tile_ids < n_used, tile_cls, tile_cls[jnp.maximum(n_used - 1, 0)])
    pairs = [(pa, pb) for pa in range(per) for pb in range(pa + 1, per)]
    cls_lo = jnp.array([g * per + pa for g in range(n_groups) for pa, _ in pairs], jnp.int32)
    cls_hi = jnp.array([g * per + pb for g in range(n_groups) for _, pb in pairs], jnp.int32)
    gate_lanes = jnp.pad(jnp.stack([g_lo, g_hi], axis=1), ((0, 0), (0, LANES - 2)))
    tile_onehot = tile_cls[:, None] == jnp.arange(n_classes)[None, :]
    tile_lo = jnp.sum(jnp.where(tile_onehot, cls_lo[None, :], 0), axis=1)
    tile_hi = jnp.sum(jnp.where(tile_onehot, cls_hi[None, :], 0), axis=1)
    return pos, gate_lanes, tile_lo, tile_hi, n_used.reshape(1), n_tiles * tm


def kernel(x_prompt, x_sample, state_ssd_fwd, state_ssd_bwd, cache_attn_k, cache_attn_v, c, c_ctx, ada_w, ada_b, norm_mix, norm_ffn, norm_final, ev_w_in, ev_w_out, hy_conv_w, hy_conv_b, hy_f_w1, hy_f_b1, hy_f_freq1, hy_f_w2, hy_f_b2, hy_f_freq2, hy_f_w3, hy_bias, ssd_conv_w, ssd_conv_b, ssd_dt_bias, ssd_a_log, ssd_d, ssd_norm, at_wq, at_wk, at_wv, at_wo, at_sink, moe_w_group, moe_b_group, moe_w_expert, moe_b_expert, moe_w1, moe_w3, moe_w2):
    bc, lc, d = x_prompt.shape
    bl, ll, _ = x_sample.shape
    tc_rows, tl_rows = bc * lc, bl * ll
    t = tc_rows + tl_rows
    depth = ada_w.shape[0]
    n_seg = 1 + bl
    assert lc % ROW_TILE == 0 and ll % ROW_TILE == 0 and tc_rows % ll == 0
    seg_rows = math.gcd(tc_rows, ll)

    def seg_of_rows(r):
        return jnp.where(r < tc_rows, 0, 1 + (r - tc_rows) // ll)

    def seg_of_tile(i):
        return seg_of_rows(i * ROW_TILE)

    x = jnp.concatenate([x_prompt.reshape(tc_rows, d), x_sample.reshape(tl_rows, d)], axis=0)
    cvec = jnp.concatenate([c_ctx[None], c, jnp.zeros((SUBLANES - n_seg, d), F32)], axis=0)
    mod_all = ada_mod(cvec, ada_w, ada_b)

    tile_row = jnp.arange(t // ROW_TILE, dtype=jnp.int32) * ROW_TILE
    seq_len = jnp.where(tile_row < tc_rows, lc, ll)
    rel = jnp.where(tile_row < tc_rows, tile_row, tile_row - tc_rows)
    conv_flags = jnp.stack([(rel % seq_len == 0), ((rel + ROW_TILE) % seq_len == 0)]).astype(jnp.int32)

    new_sf, new_sb, new_k, new_v = [], [], [], []
    for l in range(depth):
        i = l // 2
        mod3 = mod_all[l, :n_seg].reshape(n_seg * 6, 1, d)
        h = rms_norm(x, norm_mix[l], out_dtype=BF16, mod3=mod3, shift_idx=0, scale_idx=1,
                     seg_of_tile=seg_of_tile)
        if l % 2 == 0:
            hyena_w = hy_conv_w.shape[2] // 3
            inner = ssd_norm.shape[1]
            heads = inner // SSD_HEAD_DIM
            conv_ch = ssd_conv_w.shape[2]
            main_cols = 3 * hyena_w + inner + conv_ch
            w_in = ev_w_in[i]
            proj = dense(h, w_in[:, :main_cols].astype(BF16), out_dtype=BF16, name="in_proj")
            dt_raw = dense(h, w_in[:, main_cols:].astype(BF16), out_dtype=F32, name="dt_proj")
            conv_w = jnp.concatenate([hy_conv_w[i], ssd_conv_w[i]], axis=1)
            conv_b = jnp.concatenate([hy_conv_b[i], ssd_conv_b[i]], axis=0)
            u = dwconv(proj, conv_w, conv_b, conv_flags, n_plain_cols=3 * hyena_w, gap_cols=inner,
                       out_cols=3 * hyena_w + conv_ch)
            y_a = jnp.zeros((t, hyena_w), BF16)
            for row0, nb, L in ((0, bc, lc), (tc_rows, bl, ll)):
                filt = hyena_filters(L, hy_f_w1[i], hy_f_b1[i], hy_f_freq1[i], hy_f_w2[i], hy_f_b2[i],
                                     hy_f_freq2[i], hy_f_w3[i], hyena_w)
                if L % (16 * SPEC_BLOCK) == 0:
                    n1 = 2 * L // SPEC_BLOCK
                    t1f = fft_table("t1f", n1, SPEC_BLOCK)
                    kf = hyena_spectrum_2level(L, filt, t1f, fft_table("f2filt", n1, SPEC_BLOCK))
                    tabs = (t1f, fft_table("f2f", n1, SPEC_BLOCK), fft_table("f2i", n1, SPEC_BLOCK),
                            fft_table("t1i", n1, SPEC_BLOCK))
                    y_a = hyena_stream_2level(u, row0, nb, L, kf, tabs, hy_bias[i], hyena_w, fill=y_a)
                else:
                    a_tab = dft_table(L, "filt")
                    kf = hyena_spectrum(L, filt, a_tab)
                    y_a = hyena_stream(u, row0, nb, L, kf, a_tab, dft_table(L, "inv"), hy_bias[i], hyena_w,
                                       fill=y_a)
            col0 = 3 * hyena_w
            gn = SSD_GROUPS * SSD_STATE
            bt_all = u[:, col0 + inner:col0 + inner + gn].T
            dt, cum_c, cum_r = ssd_prep(dt_raw, dt_raw.T, ssd_dt_bias[i], ssd_a_log[i])
            zero = jnp.zeros((bc, SSD_GROUPS, SSD_STATE, inner // SSD_GROUPS), F32)
            init_f = jnp.concatenate([zero, _state_to_kernel_layout(state_ssd_fwd[:, i], SSD_GROUPS)], axis=0)
            init_b = jnp.concatenate([zero, _state_to_kernel_layout(state_ssd_bwd[:, i], SSD_GROUPS)], axis=0)
            yf, yb, sf, sb = ssd_scan(u, bt_all, dt, cum_c, cum_r, init_f, init_b,
                                      seq_lens=[lc] * bc + [ll] * bl, col0=col0, inner=inner)
            new_sf.append(_state_from_kernel_layout(sf[:bc], SSD_HEAD_DIM))
            new_sb.append(_state_from_kernel_layout(sb[:bc], SSD_HEAD_DIM))
            y_b = ssd_finish(yf, yb, u, proj, jnp.repeat(ssd_d[i], SSD_HEAD_DIM), ssd_norm[i],
                             xs_col=col0, z_col=3 * hyena_w, inner=inner)
            w_out = ev_w_out[i].astype(BF16)
            x = dense_residual(y_a, w_out[:hyena_w], x, mod3, 2, seg_of_rows, seg_rows, name="out_proj_a")
            x = dense_residual(y_b, w_out[hyena_w:], x, mod3, 2, seg_of_rows, seg_rows, name="out_proj_b")
        else:
            n_heads = at_wq.shape[2] // HEAD_DIM
            w_qkv = jnp.concatenate([at_wq[i], at_wk[i], at_wv[i]], axis=1).astype(BF16)
            qkv = dense(h, w_qkv, out_dtype=F32, name="qkv_proj")
            kvw = N_KV_HEADS * HEAD_DIM
            att_c = context_attention(qkv, at_sink[i], nseq=bc, L=lc, n_heads=n_heads)
            past = cache_attn_k.shape[2]
            att = latent_attention(qkv, cache_attn_k[:, i].reshape(bl, past, kvw),
                                   cache_attn_v[:, i].reshape(bl, past, kvw), at_sink[i], att_c,
                                   row0=tc_rows, nseq=bl, L=ll, n_heads=n_heads)
            x = dense_residual(att, at_wo[i].astype(BF16), x, mod3, 2, seg_of_rows, seg_rows, name="attn_out")
            qw = n_heads * HEAD_DIM
            new_k.append(qkv[:tc_rows, qw:qw + kvw].reshape(bc, lc, N_KV_HEADS, HEAD_DIM))
            new_v.append(qkv[:tc_rows, qw + kvw:].reshape(bc, lc, N_KV_HEADS, HEAD_DIM))
        h2 = rms_norm(x, norm_ffn[l], out_dtype=F32, mod3=mod3, shift_idx=3, scale_idx=4,
                      seg_of_tile=seg_of_tile)
        n_experts = moe_w_expert.shape[2]
        w_r = jnp.concatenate([moe_w_group[l], moe_w_expert[l]], axis=1)
        w_r = jnp.pad(w_r, ((0, 0), (0, LANES - w_r.shape[1]))).astype(BF16)
        b_r = jnp.pad(jnp.concatenate([moe_b_group[l], moe_b_expert[l]]), (0, LANES - MOE_GROUPS - n_experts))
        logits = dense(h2, w_r, out_dtype=F32, extra=(b_r.reshape(1, LANES),),
                       extra_specs=(pl.BlockSpec((1, LANES), lambda bt, ii, j, kk: (0, 0)),),
                       epilogue=lambda acc, b_ref, row_tile: acc + b_ref[...], name="router")
        expert_ids, gates = moe_route(logits, MOE_GROUPS, n_experts)
        pos, gate_lanes, tile_lo, tile_hi, n_used, n_slots = moe_plan(expert_ids, gates, MOE_GROUPS, n_experts)
        xs = moe_dispatch(h2, gate_lanes, pos, n_slots)
        y_sorted = moe_ffn(xs, tile_lo, tile_hi, n_used, moe_w1[l].astype(BF16), moe_w3[l].astype(BF16),
                           moe_w2[l].astype(BF16))
        x = moe_combine(x, y_sorted, pos, mod3, 5, seg_of_tile)

    y_prompt = rms_norm(x, norm_final, out_dtype=F32, rows=tc_rows).reshape(bc, lc, d)
    y_sample = rms_norm(x, norm_final, out_dtype=F32, row_off=tc_rows, rows=tl_rows).reshape(bl, ll, d)
    return (y_prompt, y_sample, jnp.stack(new_sf, axis=1), jnp.stack(new_sb, axis=1),
            jnp.stack(new_k, axis=1), jnp.stack(new_v, axis=1))
```

```python
import functools
import math

import jax
import jax.numpy as jnp
from jax import lax
from jax.experimental import pallas as pl
from jax.experimental.pallas import tpu as pltpu

F32 = jnp.float32
BF16 = jnp.bfloat16
HI = lax.Precision.HIGHEST

RMS_EPS = 1e-6
GRID_W = 64
HYENA_BANDS = 16
HYENA_FAST_DECAY = 0.3
HYENA_SLOW_DECAY = 1.5
HYENA_TARGET = 1e-2
SSD_HEAD_DIM = 64
SSD_STATE = 128
SSD_GROUPS = 8
SSD_CHUNK = 128
HEAD_DIM = 64
N_KV_HEADS = 8
WINDOW = 128
ROPE_BASE = 10000.0
MOE_GROUPS = 4
TOP_K = 2

LANES = 128
SUBLANES = 8
BF16_SUBLANES = 16
DMA_PRIORITIES = 2
VMEM_LIMIT_BYTES = 48 * 1024 * 1024

ROW_TILE = 256
SPEC_BLOCK = 128


def _cp(*sem, vmem=VMEM_LIMIT_BYTES):
    return pltpu.CompilerParams(dimension_semantics=sem, vmem_limit_bytes=vmem)


def _tile(n, pref):
    if n <= pref:
        return n
    t = pref
    while n % t:
        t //= 2
    assert t >= LANES, (n, pref)
    return t


def _silu(x):
    return x * (1.0 / (1.0 + jnp.exp(-x)))


def _softplus(x):
    return jnp.maximum(x, 0.0) + jnp.log(1.0 + jnp.exp(-jnp.abs(x)))


def _ada_body(c_ref, w_ref, b_ref, o_ref):
    a = _silu(c_ref[...])
    o_ref[...] = lax.dot_general(a, w_ref[...], (((1,), (0,)), ((), ())), precision=HI,
                                 preferred_element_type=F32) + b_ref[...]


def ada_mod(cvec, ada_w, ada_b):
    depth, d, n = ada_w.shape
    rows = cvec.shape[0]
    tn = _tile(n, 1024)
    return pl.pallas_call(
        _ada_body,
        out_shape=jax.ShapeDtypeStruct((depth, rows, n), F32),
        grid=(depth, n // tn),
        in_specs=[pl.BlockSpec((rows, d), lambda l, j: (0, 0)),
                  pl.BlockSpec((None, d, tn), lambda l, j: (l, 0, j)),
                  pl.BlockSpec((None, 1, tn), lambda l, j: (l, 0, j))],
        out_specs=pl.BlockSpec((None, rows, tn), lambda l, j: (l, 0, j)),
        compiler_params=_cp("parallel", "parallel"),
        name="ada_mod",
    )(cvec, ada_w, ada_b.reshape(depth, 1, n))


def _norm_body(x_ref, g_ref, *rest, modulate):
    o_ref = rest[-1]
    x = x_ref[...]
    y = x * lax.rsqrt(jnp.mean(x * x, axis=-1, keepdims=True) + RMS_EPS)
    y = y * g_ref[...]
    if modulate:
        sh_ref, sc_ref = rest[0], rest[1]
        y = y * (1.0 + sc_ref[...]) + sh_ref[...]
    o_ref[...] = y.astype(o_ref.dtype)


def rms_norm(x, g, *, out_dtype, mod3=None, shift_idx=0, scale_idx=0, seg_of_tile=None,
             row_off=0, rows=None):
    t, d = x.shape
    rows = t if rows is None else rows
    tr = ROW_TILE
    off = row_off // tr
    in_specs = [pl.BlockSpec((tr, d), lambda i: (i + off, 0)),
                pl.BlockSpec((1, d), lambda i: (0, 0))]
    args = [x, g.reshape(1, d)]
    if mod3 is not None:
        in_specs += [pl.BlockSpec((None, 1, d), lambda i: (seg_of_tile(i) * 6 + shift_idx, 0, 0)),
                     pl.BlockSpec((None, 1, d), lambda i: (seg_of_tile(i) * 6 + scale_idx, 0, 0))]
        args += [mod3, mod3]
    return pl.pallas_call(
        functools.partial(_norm_body, modulate=mod3 is not None),
        out_shape=jax.ShapeDtypeStruct((rows, d), out_dtype),
        grid=(rows // tr,),
        in_specs=in_specs,
        out_specs=pl.BlockSpec((tr, d), lambda i: (i, 0)),
        compiler_params=_cp("parallel"),
        name="rms_norm",
    )(*args)


def _mm_body(*refs, nk, n_extra, n_skip, epilogue):
    a_ref, b_ref = refs[0], refs[1]
    extra = refs[2:2 + n_extra]
    o_ref = refs[2 + n_extra + n_skip]
    row_tile = pl.program_id(1)
    b = b_ref[...]
    if b.ndim == 3:
        b = b.reshape(-1, b.shape[-1])
    part = jnp.dot(a_ref[...].astype(b.dtype), b, preferred_element_type=F32)
    if nk == 1:
        o_ref[...] = epilogue(part, *extra, row_tile=row_tile).astype(o_ref.dtype).reshape(o_ref.shape)
        return
    acc_ref = refs[3 + n_extra + n_skip]
    k = pl.program_id(3)

    @pl.when(k == 0)
    def _():
        acc_ref[...] = part

    @pl.when(k > 0)
    def _():
        acc_ref[...] += part

    @pl.when(k == nk - 1)
    def _():
        o_ref[...] = epilogue(acc_ref[...], *extra, row_tile=row_tile).astype(o_ref.dtype)


def matmul(a, b, *, grid, tm, tn, a_spec, b_spec, o_spec, out_shape, out_dtype,
           extra=(), extra_specs=(), epilogue=None, fill=None, name="matmul"):
    nk = grid[3]
    epilogue = epilogue or (lambda acc, row_tile: acc)
    scratch = [pltpu.VMEM((tm, tn), F32)] if nk > 1 else []
    fills = () if fill is None else (fill,)
    return pl.pallas_call(
        functools.partial(_mm_body, nk=nk, n_extra=len(extra), n_skip=len(fills), epilogue=epilogue),
        out_shape=jax.ShapeDtypeStruct(out_shape, out_dtype),
        grid=grid,
        in_specs=[a_spec, b_spec, *extra_specs, *[pl.BlockSpec(memory_space=pl.ANY) for _ in fills]],
        out_specs=o_spec,
        scratch_shapes=scratch,
        input_output_aliases={2 + len(extra): 0} if fills else {},
        compiler_params=_cp("parallel", "parallel", "parallel", "arbitrary"),
        name=name,
    )(a, b, *extra, *fills)


def dense(a, w, *, out_dtype, tm=1024, tn=512, tk=2048, extra=(), extra_specs=(), epilogue=None,
          w_row0=0, n_cols=None, name="dense"):
    m, k = a.shape
    n = w.shape[1] if n_cols is None else n_cols
    tm, tn, tk = _tile(m, tm), _tile(n, tn), _tile(k, tk)
    assert w_row0 % tk == 0
    kb0 = w_row0 // tk
    return matmul(
        a, w, grid=(1, m // tm, n // tn, k // tk), tm=tm, tn=tn,
        a_spec=pl.BlockSpec((tm, tk), lambda bt, i, j, kk: (i, kk)),
        b_spec=pl.BlockSpec((tk, tn), lambda bt, i, j, kk: (kb0 + kk, j)),
        o_spec=pl.BlockSpec((tm, tn), lambda bt, i, j, kk: (i, j)),
        out_shape=(m, n), out_dtype=out_dtype, extra=extra, extra_specs=extra_specs,
        epilogue=epilogue, name=name)


def dense_residual(a, w, x, mod3, gate_idx, seg_of_rows, seg_rows, *, tm=1024, tn=512, tk=2048, w_row0=0,
                   name="dense_res"):
    n = w.shape[1]
    tm_ = _tile(seg_rows, tm)
    tn_ = _tile(n, tn)
    return dense(
        a, w, out_dtype=F32, tm=tm_, tn=tn, tk=tk, w_row0=w_row0, extra=(x, mod3),
        extra_specs=(pl.BlockSpec((tm_, tn_), lambda bt, i, j, kk: (i, j)),
                     pl.BlockSpec((None, 1, tn_),
                                  lambda bt, i, j, kk: (seg_of_rows(i * tm_) * 6 + gate_idx, 0, j))),
        epilogue=lambda acc, x_ref, g_ref, row_tile: x_ref[...] + g_ref[...] * acc, name=name)


def _dwconv_body(flags_ref, x_ref, p_ref, n_ref, w_ref, b_ref, o_ref, *, n_plain):
    i = pl.program_id(0)
    j = pl.program_id(1)
    x = x_ref[...].astype(F32)
    tr = x.shape[0]
    row = lax.broadcasted_iota(jnp.int32, x.shape, 0)
    keep_prev = (flags_ref[0, i] == 0).astype(F32)
    keep_next = (flags_ref[1, i] == 0).astype(F32)
    halo_prev = p_ref[SUBLANES - 1:SUBLANES, :].astype(F32) * keep_prev
    halo_next = n_ref[0:1, :].astype(F32) * keep_next
    prev = jnp.where(row == 0, halo_prev, pltpu.roll(x, 1, 0))
    nxt = jnp.where(row == tr - 1, halo_next, pltpu.roll(x, tr - 1, 0))
    w = w_ref[...]
    y = prev * w[0:1, :] + x * w[1:2, :] + nxt * w[2:3, :] + b_ref[...]
    y = jnp.where(j >= n_plain, _silu(y), y)
    o_ref[...] = y.astype(o_ref.dtype)


def dwconv(proj, w, b, flags, *, n_plain_cols, gap_cols, out_cols):
    t = proj.shape[0]
    tr = ROW_TILE
    tc = _tile(math.gcd(n_plain_cols, gap_cols, out_cols), 2048)
    n_plain, gap = n_plain_cols // tc, gap_cols // tc
    hb = tr // SUBLANES
    last_hb = t // SUBLANES - 1

    def src(j):
        return j + jnp.where(j >= n_plain, gap, 0)

    grid_spec = pltpu.PrefetchScalarGridSpec(
        num_scalar_prefetch=1,
        grid=(t // tr, out_cols // tc),
        in_specs=[pl.BlockSpec((tr, tc), lambda i, j, f: (i, src(j))),
                  pl.BlockSpec((SUBLANES, tc), lambda i, j, f: (jnp.maximum(i * hb - 1, 0), src(j))),
                  pl.BlockSpec((SUBLANES, tc), lambda i, j, f: (jnp.minimum((i + 1) * hb, last_hb), src(j))),
                  pl.BlockSpec((3, tc), lambda i, j, f: (0, j)),
                  pl.BlockSpec((1, tc), lambda i, j, f: (0, j))],
        out_specs=pl.BlockSpec((tr, tc), lambda i, j, f: (i, j)))
    return pl.pallas_call(
        functools.partial(_dwconv_body, n_plain=n_plain),
        out_shape=jax.ShapeDtypeStruct((t, out_cols), BF16),
        grid_spec=grid_spec,
        compiler_params=_cp("parallel", "parallel"),
        name="dwconv",
    )(flags, proj, proj, proj, w, b.reshape(1, -1))


def _filter_body(feat_ref, w1_ref, b1_ref, f1_ref, w2_ref, b2_ref, f2_ref, w3_ref, dl_ref, o_ref):
    d = pl.program_id(0)
    i = pl.program_id(1)
    feat = feat_ref[...]
    dn = (((1,), (0,)), ((), ()))
    hid = jnp.sin(f1_ref[...] * (lax.dot_general(feat, w1_ref[...], dn, precision=HI,
                                                 preferred_element_type=F32) + b1_ref[...]))
    hid = jnp.sin(f2_ref[...] * (lax.dot_general(hid, w2_ref[...], dn, precision=HI,
                                                 preferred_element_type=F32) + b2_ref[...]))
    filt = jnp.dot(hid.astype(BF16), w3_ref[...].astype(BF16), preferred_element_type=F32)
    t01 = feat[:, 0:1]
    filt = filt * jnp.exp(-t01 * dl_ref[...])
    row = lax.broadcasted_iota(jnp.int32, filt.shape, 0) + i * filt.shape[0]
    filt = jnp.where((d == 1) & (row == 0), 0.0, filt)
    o_ref[...] = filt.astype(o_ref.dtype)


def hyena_filters(L, f_w1, f_b1, f_freq1, f_w2, f_b2, f_freq2, f_w3, c):
    t = jnp.arange(L, dtype=F32)
    t01 = t / (L - 1)
    bands = jnp.linspace(1e-4, HYENA_BANDS - 1, HYENA_BANDS, dtype=F32)
    ang = (2.0 * math.pi / L) * t[:, None] * bands[None, :]
    feat = jnp.concatenate([t01[:, None], jnp.cos(ang), -jnp.sin(ang)], axis=-1)
    emb = feat.shape[1]
    feat = jnp.pad(feat, ((0, 0), (0, LANES - emb)))
    w1 = jnp.pad(f_w1, ((0, LANES - emb), (0, 0)))
    ffn = f_w1.shape[1]
    order = f_w3.shape[1] // (2 * c)
    deltas = jnp.abs(jnp.linspace(math.log(HYENA_TARGET) / HYENA_SLOW_DECAY,
                                  math.log(HYENA_TARGET) / HYENA_FAST_DECAY, c, dtype=F32))
    deltas = jnp.tile(deltas, order).reshape(1, order * c)
    tl, tn = _tile(L, 512), _tile(order * c, 4096)
    nj = order * c // tn
    return pl.pallas_call(
        _filter_body,
        out_shape=jax.ShapeDtypeStruct((2, L, order * c), BF16),
        grid=(2, L // tl, nj),
        in_specs=[pl.BlockSpec((tl, LANES), lambda d, i, j: (i, 0)),
                  pl.BlockSpec((LANES, ffn), lambda d, i, j: (0, 0)),
                  pl.BlockSpec((1, ffn), lambda d, i, j: (0, 0)),
                  pl.BlockSpec((1, ffn), lambda d, i, j: (0, 0)),
                  pl.BlockSpec((ffn, ffn), lambda d, i, j: (0, 0)),
                  pl.BlockSpec((1, ffn), lambda d, i, j: (0, 0)),
                  pl.BlockSpec((1, ffn), lambda d, i, j: (0, 0)),
                  pl.BlockSpec((ffn, tn), lambda d, i, j: (0, d * nj + j)),
                  pl.BlockSpec((1, tn), lambda d, i, j: (0, j))],
        out_specs=pl.BlockSpec((None, tl, tn), lambda d, i, j: (d, i, j)),
        compiler_params=_cp("parallel", "parallel", "parallel"),
        name="hyena_filters",
    )(feat, w1, f_b1.reshape(1, ffn), f_freq1.reshape(1, ffn), f_w2, f_b2.reshape(1, ffn),
      f_freq2.reshape(1, ffn), f_w3, deltas)


def _cos_turns(m, log2_period):
    period = 1 << log2_period
    quarter = period // 4
    mm = m + quarter // 2
    q = (mm >> (log2_period - 2)) & 3
    r = (mm & (quarter - 1)) - quarter // 2
    phi = r.astype(F32) * (2.0 * math.pi / period)
    p2 = phi * phi
    cosv = 1.0 + p2 * (-1.0 / 2 + p2 * (1.0 / 24 + p2 * (-1.0 / 720 + p2 * (1.0 / 40320 - p2 / 3628800))))
    sinv = phi * (1.0 + p2 * (-1.0 / 6 + p2 * (1.0 / 120 + p2 * (-1.0 / 5040 + p2 / 362880))))
    return jnp.where(q == 0, cosv, jnp.where(q == 1, -sinv, jnp.where(q == 2, -cosv, sinv)))


def _spec_index(r, L):
    blk = r >> 8
    within = r & (2 * SPEC_BLOCK - 1)
    return blk * SPEC_BLOCK + (within & (SPEC_BLOCK - 1)), within >= SPEC_BLOCK


def _dft_body(o_ref, *, L, mode):
    log2n = int(math.log2(2 * L))
    shape = o_ref.shape
    i0 = pl.program_id(0) * shape[0]
    j0 = pl.program_id(1) * shape[1]
    rows = lax.broadcasted_iota(jnp.int32, shape, 0) + i0
    cols = lax.broadcasted_iota(jnp.int32, shape, 1) + j0
    if mode == "inv":
        r, n = cols, rows
    else:
        r, n = rows, cols
    k, im = _spec_index(r, L)
    back = n >= L
    n = jnp.where(back, n - L, n)
    phase = jnp.where(im, jnp.where(back, -(L // 2), L // 2), 0)
    val = _cos_turns((k * n + phase) & (2 * L - 1), log2n)
    nyq = jnp.where((n & 1) == 0, 1.0, -1.0)
    val = jnp.where(im & (k == 0), nyq, val)
    if mode == "inv":
        val = val * jnp.where(k == 0, 1.0 / (2 * L), 2.0 / (2 * L))
    o_ref[...] = val.astype(o_ref.dtype)


def dft_table(L, mode):
    assert L & (L - 1) == 0 and L >= 2 * SPEC_BLOCK
    shape = {"fwd": (2 * L, L), "filt": (2 * L, 2 * L), "inv": (L, 2 * L)}[mode]
    tr, tc = _tile(shape[0], 512), _tile(shape[1], 1024)
    return pl.pallas_call(
        functools.partial(_dft_body, L=L, mode=mode),
        out_shape=jax.ShapeDtypeStruct(shape, BF16),
        grid=(shape[0] // tr, shape[1] // tc),
        out_specs=pl.BlockSpec((tr, tc), lambda i, j: (i, j)),
        compiler_params=_cp("parallel", "parallel"),
        name="dft_table_" + mode,
    )()


def _spectral_epilogue(acc, kf_ref, row_tile):
    tm, tn = acc.shape
    nb = tm // (2 * SPEC_BLOCK)
    z = acc.reshape(nb, 2, SPEC_BLOCK, tn)
    kf = kf_ref[...].reshape(nb, 2, SPEC_BLOCK, tn)
    zr, zi, kr, ki = z[:, 0], z[:, 1], kf[:, 0], kf[:, 1]
    first = row_tile == 0
    blk = lax.broadcasted_iota(jnp.int32, zr.shape, 0)
    row = lax.broadcasted_iota(jnp.int32, zr.shape, 1)
    dc = first & (blk == 0) & (row == 0)
    pr = zr * kr - jnp.where(dc, 0.0, zi * ki)
    pi = jnp.where(dc, zi * ki, zr * ki + zi * kr)
    return jnp.stack([pr, pi], axis=1).reshape(tm, tn)


def hyena_stream(u, row0, nb, L, kf, f_tab, g_tab, hy_bias, c, fill=None):
    out_rows = u.shape[0]
    tn = _tile(c, 2048)
    cb = c // tn
    tm_f = _tile(2 * L, 512)
    tk_f = _tile(L, 2048)
    tm_i = _tile(L, 512)
    tk_i = _tile(2 * L, 2048)
    z, z_rows, z_col = u, row0, 2 * cb
    for o in range(2):
        zrb = z_rows // tk_f
        p = matmul(
            f_tab, z, grid=(nb, 2 * L // tm_f, cb, L // tk_f), tm=tm_f, tn=tn,
            a_spec=pl.BlockSpec((tm_f, tk_f), lambda bt, i, j, k: (i, k)),
            b_spec=pl.BlockSpec((tk_f, tn), lambda bt, i, j, k, zrb=zrb, zc=z_col:
                                (zrb + bt * (L // tk_f) + k, zc + j)),
            o_spec=pl.BlockSpec((None, tm_f, tn), lambda bt, i, j, k: (bt, i, j)),
            out_shape=(nb, 2 * L, c), out_dtype=BF16,
            extra=(kf,), extra_specs=(pl.BlockSpec((tm_f, tn), lambda bt, i, j, k, o=o: (i, o * cb + j)),),
            epilogue=_spectral_epilogue, name="hyena_fwd")
        zrb_i = z_rows // tm_i
        urb_i = row0 // tm_i
        last = o == 1
        orb = urb_i if last else 0
        z_new = matmul(
            g_tab, p, grid=(nb, L // tm_i, cb, 2 * L // tk_i), tm=tm_i, tn=tn,
            a_spec=pl.BlockSpec((tm_i, tk_i), lambda bt, i, j, k: (i, k)),
            b_spec=pl.BlockSpec((None, tk_i, tn), lambda bt, i, j, k: (bt, k, j)),
            o_spec=pl.BlockSpec((tm_i, tn), lambda bt, i, j, k, orb=orb: (orb + bt * (L // tm_i) + i, j)),
            out_shape=(out_rows if last else nb * L, c), out_dtype=BF16, fill=fill if last else None,
            extra=(u, z, hy_bias.reshape(2, 1, c)),
            extra_specs=(pl.BlockSpec((tm_i, tn), lambda bt, i, j, k, o=o:
                                      (urb_i + bt * (L // tm_i) + i, o * cb + j)),
                         pl.BlockSpec((tm_i, tn), lambda bt, i, j, k, zrb_i=zrb_i, zc=z_col:
                                      (zrb_i + bt * (L // tm_i) + i, zc + j)),
                         pl.BlockSpec((None, 1, tn), lambda bt, i, j, k, o=o: (o, 0, j))),
            epilogue=lambda acc, g_ref, z_ref, b_ref, row_tile:
                g_ref[...].astype(F32) * (acc + b_ref[...] * z_ref[...].astype(F32)),
            name="hyena_inv")
        z, z_rows, z_col = z_new, 0, 0
    return z


def hyena_spectrum(L, filt, a_tab):
    n = filt.shape[2]
    b = filt.reshape(2 * L, n)
    tm, tn, tk = _tile(2 * L, 512), _tile(n, 512), _tile(2 * L, 2048)
    return matmul(
        a_tab, b, grid=(1, 2 * L // tm, n // tn, 2 * L // tk), tm=tm, tn=tn,
        a_spec=pl.BlockSpec((tm, tk), lambda bt, i, j, k: (i, k)),
        b_spec=pl.BlockSpec((tk, tn), lambda bt, i, j, k: (k, j)),
        o_spec=pl.BlockSpec((tm, tn), lambda bt, i, j, k: (i, j)),
        out_shape=(2 * L, n), out_dtype=F32, name="hyena_spectrum")


def _fft_tab_body(o_ref, *, kind, n1, n2, k1_len):
    shape = o_ref.shape
    r = lax.broadcasted_iota(jnp.int32, shape, 0) + pl.program_id(0) * shape[0]
    c = lax.broadcasted_iota(jnp.int32, shape, 1)
    n = n1 * n2
    lg = lambda v: int(math.log2(v))
    s8 = BF16_SUBLANES
    if kind in ("t1f", "t1i"):
        if kind == "t1f":
            grp, kp, s, n1i, s2 = r >> lg(2 * n1 * s8), (r >> lg(s8)) & (2 * n1 - 1), r & (s8 - 1), c >> lg(s8), c & (s8 - 1)
        else:
            grp, n1i, s, kp, s2 = r >> lg(k1_len * s8), (r >> lg(s8)) & (k1_len - 1), r & (s8 - 1), c >> lg(s8), c & (s8 - 1)
        t = n2 * n1i + grp * s8 + s
        val = _cos_turns(((kp & (n1 - 1)) * t + (kp >> lg(n1)) * (n // 4)) & (n - 1), lg(n))
        val = jnp.where(s == s2, val * (1.0 / n if kind == "t1i" else 1.0), 0.0)
    else:
        rp = r >= n2
        cblk = c >> lg(n2)
        cp = (cblk & 1) == 1
        sin_sign = -1 if kind == "f2i" else 1
        quarter = jnp.where(rp == cp, 0, jnp.where(cp, -sin_sign, sin_sign))
        val = _cos_turns(((r & (n2 - 1)) * (c & (n2 - 1)) + quarter * (n2 // 4)) & (n2 - 1), lg(n2))
        if kind == "f2filt":
            val = jnp.where((cblk >= 2) & rp, -val, val)
    o_ref[...] = val.astype(o_ref.dtype)


def fft_table(kind, n1, n2):
    k1_len = n1 // 2
    s8 = BF16_SUBLANES
    groups = n2 // s8
    shape = {"t1f": (groups * 2 * n1 * s8, k1_len * s8), "t1i": (groups * k1_len * s8, 2 * n1 * s8),
             "f2f": (2 * n2, 2 * n2), "f2i": (2 * n2, 2 * n2), "f2filt": (2 * n2, 4 * n2)}[kind]
    tr = _tile(shape[0], 1024)
    out = pl.pallas_call(
        functools.partial(_fft_tab_body, kind=kind, n1=n1, n2=n2, k1_len=k1_len),
        out_shape=jax.ShapeDtypeStruct(shape, BF16),
        grid=(shape[0] // tr,),
        out_specs=pl.BlockSpec((tr, shape[1]), lambda i: (i, 0)),
        compiler_params=_cp("parallel"),
        name="fft_table_" + kind,
    )()
    if kind in ("t1f", "t1i"):
        return out.reshape(groups, shape[0] // groups, shape[1])
    return out


FFT_K1_PER_STEP = 4


def _fft_mid_body(f_ref, *refs, mode, n2):
    f = f_ref[...]
    if mode == "inv":
        p_ref, o_ref = refs
        for k in range(FFT_K1_PER_STEP):
            res = jnp.dot(f, p_ref[k * 2 * n2:(k + 1) * 2 * n2, :], preferred_element_type=F32)
            o_ref[0, k * n2:(k + 1) * n2, :] = res[:n2].astype(o_ref.dtype)
            o_ref[1, k * n2:(k + 1) * n2, :] = res[n2:].astype(o_ref.dtype)
        return
    if mode == "fwd":
        y_ref, kf_ref, o_ref = refs
        srcs = (y_ref,)
    else:
        y0_ref, y1_ref, o_ref = refs
        srcs = (y0_ref, y1_ref)
    for k in range(FFT_K1_PER_STEP):
        rows = slice(k * n2, (k + 1) * n2)
        stack = jnp.concatenate([s[part, rows, :] for s in srcs for part in range(2)], axis=0)
        acc = jnp.dot(f, stack, preferred_element_type=F32)
        out_rows = slice(k * 2 * n2, (k + 1) * 2 * n2)
        if mode == "fwd":
            zr, zi = acc[:n2], acc[n2:]
            kr = kf_ref[k * 2 * n2:k * 2 * n2 + n2, :]
            ki = kf_ref[k * 2 * n2 + n2:(k + 1) * 2 * n2, :]
            acc = jnp.concatenate([zr * kr - zi * ki, zr * ki + zi * kr], axis=0)
        o_ref[out_rows, :] = acc.astype(o_ref.dtype)


def fft_mid(mode, f_tab, src, n1, n2, *, kf=None, kf_col0=0, out_dtype=BF16, name):
    kb = FFT_K1_PER_STEP
    c = src.shape[-1]
    tn = _tile(c, 1024)
    cb = c // tn
    nb = 1 if mode == "filt" else src.shape[0]
    y_block = (None, 2, kb * n2, tn)
    p_block = (None, kb * 2 * n2, tn)
    f_spec = pl.BlockSpec(f_tab.shape, lambda i, b, j: (0, 0))
    if mode == "fwd":
        in_specs = [f_spec, pl.BlockSpec(y_block, lambda i, b, j: (b, 0, i, j)),
                    pl.BlockSpec((kb * 2 * n2, tn), lambda i, b, j: (i, kf_col0 // tn + j))]
        args = (f_tab, src, kf)
        out_shape, out_spec = (nb, n1 * 2 * n2, c), pl.BlockSpec(p_block, lambda i, b, j: (b, i, j))
    elif mode == "filt":
        in_specs = [f_spec, pl.BlockSpec(y_block, lambda i, b, j: (0, 0, i, j)),
                    pl.BlockSpec(y_block, lambda i, b, j: (1, 0, i, j))]
        args = (f_tab, src, src)
        out_shape, out_spec = (n1 * 2 * n2, c), pl.BlockSpec((kb * 2 * n2, tn), lambda i, b, j: (i, j))
    else:
        in_specs = [f_spec, pl.BlockSpec(p_block, lambda i, b, j: (b, i, j))]
        args = (f_tab, src)
        out_shape, out_spec = (nb, 2, n1 * n2, c), pl.BlockSpec(y_block, lambda i, b, j: (b, 0, i, j))
    return pl.pallas_call(
        functools.partial(_fft_mid_body, mode=mode, n2=n2),
        out_shape=jax.ShapeDtypeStruct(out_shape, out_dtype),
        grid=(n1 // kb, nb, cb),
        in_specs=in_specs,
        out_specs=out_spec,
        compiler_params=_cp("parallel", "parallel", "parallel"),
        name=name,
    )(*args)


def hyena_spectrum_2level(L, filt, t1f, f2filt):
    n2 = SPEC_BLOCK
    n1 = 2 * L // n2
    k1 = n1 // 2
    n = filt.shape[2]
    tn = _tile(n, 1024)
    nj = n // tn
    sl = BF16_SUBLANES
    y1 = matmul(
        t1f, filt.reshape(2 * k1, n2, n), grid=(2, n2 // sl, nj, 1), tm=2 * n1 * sl, tn=tn,
        a_spec=pl.BlockSpec((None, 2 * n1 * sl, k1 * sl), lambda bt, i, j, k: (i, 0, 0)),
        b_spec=pl.BlockSpec((k1, sl, tn), lambda bt, i, j, k: (bt, i, j)),
        o_spec=pl.BlockSpec((None, 2 * n1, sl, tn), lambda bt, i, j, k: (bt, 0, i, j)),
        out_shape=(2, 2 * n1, n2, n), out_dtype=BF16, name="hyena_filt_fft1")
    return fft_mid("filt", f2filt, y1.reshape(2, 2, n1 * n2, n), n1, n2, out_dtype=F32, name="hyena_filt_fft2")


def hyena_stream_2level(u, row0, nb, L, kf, tabs, hy_bias, c, fill=None):
    t1f, f2f, f2i, t1i = tabs
    n2 = SPEC_BLOCK
    n1 = 2 * L // n2
    k1 = n1 // 2
    sl = BF16_SUBLANES
    ng = n2 // sl
    t1n = _tile(c, 1024)
    c1b = c // t1n
    tn = _tile(c, 2048)
    cb = c // tn
    t, u_w = u.shape
    assert row0 % L == 0 and u_w % tn == 0
    uv = u.reshape(t // n2, n2, u_w)
    urow = row0 // L
    zv, zr, zc = uv, urow, 2 * c // t1n
    bias3 = hy_bias.reshape(2, 1, c)
    for o in range(2):
        y1 = matmul(
            t1f, zv, grid=(nb, ng, c1b, 1), tm=2 * n1 * sl, tn=t1n,
            a_spec=pl.BlockSpec((None, 2 * n1 * sl, k1 * sl), lambda bt, i, j, k: (i, 0, 0)),
            b_spec=pl.BlockSpec((k1, sl, t1n), lambda bt, i, j, k, zr=zr, zc=zc: (zr + bt, i, zc + j)),
            o_spec=pl.BlockSpec((None, 2 * n1, sl, t1n), lambda bt, i, j, k: (bt, 0, i, j)),
            out_shape=(nb, 2 * n1, n2, c), out_dtype=BF16, name="hyena_fft1")
        p = fft_mid("fwd", f2f, y1.reshape(nb, 2, n1 * n2, c), n1, n2, kf=kf, kf_col0=o * c, name="hyena_fft2")
        u1 = fft_mid("inv", f2i, p, n1, n2, name="hyena_ifft1")
        last = o == 1
        orow = urow if last else 0

        def gate_epilogue(acc, g_ref, z_ref, b_ref, row_tile):
            g = g_ref[...].reshape(acc.shape).astype(F32)
            return g * (acc + b_ref[...] * z_ref[...].reshape(acc.shape).astype(F32))

        z = matmul(
            t1i, u1.reshape(nb, 2 * n1, n2, c), grid=(nb, ng, c1b, 1), tm=k1 * sl, tn=t1n,
            a_spec=pl.BlockSpec((None, k1 * sl, 2 * n1 * sl), lambda bt, i, j, k: (i, 0, 0)),
            b_spec=pl.BlockSpec((None, 2 * n1, sl, t1n), lambda bt, i, j, k: (bt, 0, i, j)),
            o_spec=pl.BlockSpec((k1, sl, t1n), lambda bt, i, j, k, orow=orow: (orow + bt, i, j)),
            out_shape=(t // n2 if last else nb * k1, n2, c), out_dtype=BF16,
            fill=None if (fill is None or not last) else fill.reshape(t // n2, n2, c),
            extra=(uv, zv, bias3),
            extra_specs=(pl.BlockSpec((k1, sl, t1n), lambda bt, i, j, k, o=o: (urow + bt, i, o * c1b + j)),
                         pl.BlockSpec((k1, sl, t1n), lambda bt, i, j, k, zr=zr, zc=zc: (zr + bt, i, zc + j)),
                         pl.BlockSpec((None, 1, t1n), lambda bt, i, j, k, o=o: (o, 0, j))),
            epilogue=gate_epilogue, name="hyena_ifft2")
        zv, zr, zc = z, 0, 0
    return zv.reshape(t, c)


SSD_CUM_TERMS = 3


def _ssd_prep_body(dc_ref, dr_ref, biasc_ref, biasr_ref, ac_ref, ar_ref, fac_ref, cumr_ref, *, heads):
    q = dc_ref.shape[0]
    dn = (((1,), (0,)), ((), ()))
    li = lax.broadcasted_iota(jnp.int32, (q, q), 0)
    si = lax.broadcasted_iota(jnp.int32, (q, q), 1)
    lower = (li >= si).astype(F32)
    upper = (li <= si).astype(F32)
    dt_c = _softplus(dc_ref[...] + biasc_ref[...])
    da_c = dt_c * ac_ref[...]
    da_r = _softplus(dr_ref[...] + biasr_ref[...]) * ar_ref[...]
    pre_c = lax.dot_general(lower, da_c, dn, precision=HI, preferred_element_type=F32)
    suf_c = lax.dot_general(upper, da_c, dn, precision=HI, preferred_element_type=F32)
    pre_r = lax.dot_general(da_r, upper, dn, precision=HI, preferred_element_type=F32)
    suf_r = lax.dot_general(da_r, lower, dn, precision=HI, preferred_element_type=F32)
    cum_c = jnp.where(lax.broadcasted_iota(jnp.int32, pre_c.shape, 1) < heads, pre_c, suf_c)
    terms, rest = [], cum_c
    for _ in range(SSD_CUM_TERMS):
        term = rest.astype(BF16)
        rest = rest - term.astype(F32)
        terms.append(term)
    fac_ref[...] = jnp.concatenate(terms + [dt_c.astype(BF16)], axis=1)
    cumr_ref[...] = jnp.where(lax.broadcasted_iota(jnp.int32, pre_r.shape, 0) < heads, pre_r, suf_r)


def ssd_prep(dt_raw, dt_raw_t, dt_bias, a_log):
    t, h2 = dt_raw.shape
    q = SSD_CHUNK
    neg_a = -jnp.exp(a_log.astype(F32)).reshape(1, h2)
    bias = dt_bias.astype(F32).reshape(1, h2)
    const = lambda shape: pl.BlockSpec(shape, lambda i: (0,) * len(shape))
    col = pl.BlockSpec((q, h2), lambda i: (i, 0))
    row = pl.BlockSpec((h2, q), lambda i: (0, i))
    nf = (SSD_CUM_TERMS + 1) * h2
    return pl.pallas_call(
        functools.partial(_ssd_prep_body, heads=h2 // 2),
        out_shape=(jax.ShapeDtypeStruct((t, nf), BF16), jax.ShapeDtypeStruct((h2, t), F32)),
        grid=(t // q,),
        in_specs=[col, row, const((1, h2)), const((h2, 1)), const((1, h2)), const((h2, 1))],
        out_specs=(pl.BlockSpec((q, nf), lambda i: (i, 0)), row),
        compiler_params=_cp("parallel"),
        name="ssd_prep",
    )(dt_raw, dt_raw_t, bias, bias.reshape(h2, 1), neg_a, neg_a.reshape(h2, 1))


def ssd_selectors(h2, j, p, q):
    w_sum, w_wide = j * p, j * q
    k = (SSD_CUM_TERMS + 1) * h2
    rows = jnp.arange(k, dtype=jnp.int32)[None, :, None]
    cols = jnp.arange(w_sum + w_wide + j * p, dtype=jnp.int32)[None, None, :]
    head0 = j * jnp.arange(h2 // j, dtype=jnp.int32)[:, None, None]
    col_head = jnp.where(cols < w_sum, cols // p, jnp.where(cols < w_sum + w_wide, (cols - w_sum) // q,
                                                              (cols - w_sum - w_wide) // p))
    is_dt_row = rows // h2 == SSD_CUM_TERMS
    is_dt_col = cols >= w_sum + w_wide
    sel = (rows % h2 == head0 + col_head) & (is_dt_row == is_dt_col)
    return sel.astype(BF16)


def _ssd_dir(xs, bt, cm, fac, sel, diag_mask, cumr_ref, state_ref, head0, *, reverse, heads_per_group):
    q = xs.shape[0]
    p = SSD_HEAD_DIM
    j = heads_per_group
    spread = jnp.dot(fac, sel, preferred_element_type=F32)
    cc, cc_wide, dtv = spread[:, :j * p], spread[:, j * p:j * p + j * q], spread[:, j * p + j * q:]
    total = cc[0:1, :] if reverse else cc[q - 1:q, :]
    xd = xs.astype(F32) * dtv
    xd_b = xd.astype(BF16)
    li = lax.broadcasted_iota(jnp.int32, (q, q), 0)
    si = lax.broadcasted_iota(jnp.int32, (q, q), 1)
    mask = (li <= si) if reverse else (li >= si)
    cb = jnp.dot(cm, bt, preferred_element_type=F32)
    weights = []
    for jh in range(j):
        cr = cumr_ref[pl.ds(head0 + jh, 1), :]
        decay = jnp.exp(jnp.where(mask, cc_wide[:, jh * q:(jh + 1) * q] - cr, -jnp.inf))
        weights.append((cb * decay).astype(BF16))
    xd_diag = jnp.concatenate([xd_b] * j, axis=0) * diag_mask
    y = jnp.dot(jnp.concatenate(weights, axis=1), xd_diag, preferred_element_type=F32)
    s_prev = state_ref[...]
    y = y + jnp.exp(cc) * jnp.dot(cm, s_prev.astype(BF16), preferred_element_type=F32)
    st = jnp.dot(bt, (xd * jnp.exp(total - cc)).astype(BF16), preferred_element_type=F32)
    state_ref[...] = jnp.exp(total) * s_prev + st
    return y


def _ssd_body(tab_ref, xf_ref, btf_ref, cf_ref, facf_ref, crf_ref, xb_ref, btb_ref, cb_ref, facb_ref, crb_ref,
              self_ref, selb_ref, mask_ref, initf_ref, initb_ref, yf_ref, yb_ref, sf_ref, sb_ref, stf_ref, stb_ref,
              *, heads_per_group, heads):
    g = pl.program_id(0)
    item = pl.program_id(1)

    @pl.when(tab_ref[_SSD_FIRST, item] == 1)
    def _():
        stf_ref[...] = initf_ref[...]
        stb_ref[...] = initb_ref[...]

    jp = heads_per_group * SSD_HEAD_DIM
    n = SSD_STATE
    for gi in range(SSD_GROUPS_PER_STEP):
        head0 = (g * SSD_GROUPS_PER_STEP + gi) * heads_per_group
        cols, srow = slice(gi * jp, (gi + 1) * jp), slice(gi * n, (gi + 1) * n)
        yf_ref[:, cols] = _ssd_dir(xf_ref[:, cols], btf_ref[srow, :], cf_ref[:, srow], facf_ref[...], self_ref[gi],
                                   mask_ref[...], crf_ref, stf_ref.at[gi], head0, reverse=False,
                                   heads_per_group=heads_per_group)
        yb_ref[:, cols] = _ssd_dir(xb_ref[:, cols], btb_ref[srow, :], cb_ref[:, srow], facb_ref[...], selb_ref[gi],
                                   mask_ref[...], crb_ref, stb_ref.at[gi], heads + head0, reverse=True,
                                   heads_per_group=heads_per_group)

    @pl.when(tab_ref[_SSD_LAST, item] == 1)
    def _():
        sf_ref[...] = stf_ref[...]
        sb_ref[...] = stb_ref[...]


_SSD_FWD, _SSD_BWD, _SSD_SEQ, _SSD_FIRST, _SSD_LAST = range(5)
SSD_GROUPS_PER_STEP = 2


def ssd_scan(u, bt_all, fac, cum_r, init_f, init_b, *, seq_lens, col0, inner):
    q, p, n, gq = SSD_CHUNK, SSD_HEAD_DIM, SSD_STATE, SSD_GROUPS
    heads = inner // p
    j = heads // gq
    jp = j * p
    xcb = col0 // jp
    ccb = (col0 + inner + gq * n) // n
    rows, base = [], 0
    for s, length in enumerate(seq_lens):
        nc = length // q
        rows += [(base + c, base + nc - 1 - c, s, int(c == 0), int(c == nc - 1)) for c in range(nc)]
        base += nc
    table = jnp.array(rows, jnp.int32).T
    nseq, n_items = len(seq_lens), len(rows)

    gs = SSD_GROUPS_PER_STEP
    assert xcb % gs == 0 and ccb % gs == 0 and gq % gs == 0

    def specs(which):
        return [pl.BlockSpec((q, gs * jp), lambda g, it, tab: (tab[which, it], xcb // gs + g)),
                pl.BlockSpec((gs * n, q), lambda g, it, tab: (g, tab[which, it])),
                pl.BlockSpec((q, gs * n), lambda g, it, tab: (tab[which, it], ccb // gs + g)),
                pl.BlockSpec((q, fac.shape[1]), lambda g, it, tab: (tab[which, it], 0)),
                pl.BlockSpec((2 * heads, q), lambda g, it, tab: (0, tab[which, it]))]

    st_spec = pl.BlockSpec((None, gs, n, jp), lambda g, it, tab: (tab[_SSD_SEQ, it], g, 0, 0))
    st_shape = jax.ShapeDtypeStruct((nseq, gq, n, jp), F32)
    y_shape = jax.ShapeDtypeStruct((u.shape[0], inner), F32)
    sel_all = ssd_selectors(2 * heads, j, p, q)
    sel_block = (gs,) + sel_all.shape[1:]
    diag_mask = (jnp.arange(j * q)[:, None] // q == jnp.arange(jp)[None, :] // p).astype(BF16)
    grid_spec = pltpu.PrefetchScalarGridSpec(
        num_scalar_prefetch=1,
        grid=(gq // gs, n_items),
        in_specs=[*specs(_SSD_FWD), *specs(_SSD_BWD),
                  pl.BlockSpec(sel_block, lambda g, it, tab: (g, 0, 0)),
                  pl.BlockSpec(sel_block, lambda g, it, tab: (gq // gs + g, 0, 0)),
                  pl.BlockSpec(diag_mask.shape, lambda g, it, tab: (0, 0)),
                  st_spec, st_spec],
        out_specs=(pl.BlockSpec((q, gs * jp), lambda g, it, tab: (tab[_SSD_FWD, it], g)),
                   pl.BlockSpec((q, gs * jp), lambda g, it, tab: (tab[_SSD_BWD, it], g)), st_spec, st_spec),
        scratch_shapes=[pltpu.VMEM((gs, n, jp), F32), pltpu.VMEM((gs, n, jp), F32)])
    return pl.pallas_call(
        functools.partial(_ssd_body, heads_per_group=j, heads=heads),
        out_shape=(y_shape, y_shape, st_shape, st_shape),
        grid_spec=grid_spec,
        compiler_params=_cp("parallel", "arbitrary"),
        name="ssd_scan",
    )(table, u, bt_all, u, fac, cum_r, u, bt_all, u, fac, cum_r, sel_all, sel_all, diag_mask, init_f, init_b)


def _state_to_kernel_layout(s, groups):
    b, h, p, n = s.shape
    return s.reshape(b, groups, h // groups, p, n).transpose(0, 1, 4, 2, 3).reshape(b, groups, n, (h // groups) * p)


def _state_from_kernel_layout(s, head_dim):
    b, g, n, jp = s.shape
    j = jp // head_dim
    return s.reshape(b, g, n, j, head_dim).transpose(0, 1, 3, 4, 2).reshape(b, g * j, head_dim, n)


def _ssd_finish_body(yf_ref, yb_ref, xs_ref, z_ref, d_ref, w_ref, o_ref):
    y = yf_ref[...] + yb_ref[...] + xs_ref[...].astype(F32) * d_ref[...]
    y = y * _silu(z_ref[...].astype(F32))
    y = y * lax.rsqrt(jnp.mean(y * y, axis=-1, keepdims=True) + RMS_EPS)
    o_ref[...] = (y * w_ref[...]).astype(o_ref.dtype)


def ssd_finish(yf, yb, u, proj, d_full, norm_w, *, xs_col, z_col, inner):
    t = yf.shape[0]
    tr = ROW_TILE
    return pl.pallas_call(
        _ssd_finish_body,
        out_shape=jax.ShapeDtypeStruct((t, inner), BF16),
        grid=(t // tr,),
        in_specs=[pl.BlockSpec((tr, inner), lambda i: (i, 0)),
                  pl.BlockSpec((tr, inner), lambda i: (i, 0)),
                  pl.BlockSpec((tr, inner), lambda i: (i, xs_col // inner)),
                  pl.BlockSpec((tr, inner), lambda i: (i, z_col // inner)),
                  pl.BlockSpec((1, inner), lambda i: (0, 0)),
                  pl.BlockSpec((1, inner), lambda i: (0, 0))],
        out_specs=pl.BlockSpec((tr, inner), lambda i: (i, 0)),
        compiler_params=_cp("parallel"),
        name="ssd_finish",
    )(yf, yb, u, proj, d_full.reshape(1, inner), norm_w.reshape(1, inner))


KV_PER_STEP = LANES // HEAD_DIM


_NT = (((1,), (1,)), ((), ()))


def _attend(scores, values, sink):
    m = sink
    for s in scores:
        m = jnp.maximum(m, jnp.max(s, axis=-1, keepdims=True))
    denom = jnp.exp(sink - m)
    acc = None
    for s, v in zip(scores, values):
        e = jnp.exp(s - m)
        denom = denom + jnp.sum(e, axis=-1, keepdims=True)
        part = jnp.dot(e.astype(BF16), v, preferred_element_type=F32)
        acc = part if acc is None else acc + part
    return acc * (1.0 / denom)


def _stack_heads(q, first_head, n, rows):
    return jnp.concatenate([q[:, (first_head + g) * HEAD_DIM:(first_head + g + 1) * HEAD_DIM] for g in range(n)],
                           axis=0)


def _stacked_sinks(sink_ref, first_head, n, rows):
    row = lax.broadcasted_iota(jnp.int32, (n * rows, 1), 0)
    col = jnp.full((n * rows, 1), sink_ref[first_head + n - 1], F32)
    for g in reversed(range(n - 1)):
        col = jnp.where(row < (g + 1) * rows, sink_ref[first_head + g], col)
    return col


def _ctx_attn_body(sink_ref, q_ref, k_ref, v_ref, fill_ref, o_ref, *, q_per_kv, scale):
    del fill_ref
    hb = pl.program_id(1)
    rows = q_ref.shape[0]
    q_all = q_ref[...] * scale
    outs = []
    for kv in range(KV_PER_STEP):
        k = k_ref[:, kv * HEAD_DIM:(kv + 1) * HEAD_DIM].astype(BF16)
        v = v_ref[:, kv * HEAD_DIM:(kv + 1) * HEAD_DIM].astype(BF16)
        qs = _stack_heads(q_all, kv * q_per_kv, q_per_kv, rows).astype(BF16)
        s = lax.dot_general(qs, k, _NT, preferred_element_type=F32)
        sink = _stacked_sinks(sink_ref, (hb * KV_PER_STEP + kv) * q_per_kv, q_per_kv, rows)
        o = _attend([s], [v], sink)
        outs += [o[g * rows:(g + 1) * rows] for g in range(q_per_kv)]
    o_ref[...] = jnp.concatenate(outs, axis=1).astype(o_ref.dtype)


def context_attention(qkv, sink, *, nseq, L, n_heads):
    q_per_kv = n_heads // N_KV_HEADS
    qw = KV_PER_STEP * q_per_kv * HEAD_DIM
    nhb = N_KV_HEADS // KV_PER_STEP
    kcb = n_heads * HEAD_DIM // LANES
    vcb = kcb + N_KV_HEADS * HEAD_DIM // LANES
    return pl.pallas_call(
        functools.partial(_ctx_attn_body, q_per_kv=q_per_kv, scale=HEAD_DIM ** -0.5),
        out_shape=jax.ShapeDtypeStruct((qkv.shape[0], n_heads * HEAD_DIM), BF16),
        grid=(nseq, nhb),
        in_specs=[pl.BlockSpec(memory_space=pltpu.SMEM),
                  pl.BlockSpec((L, qw), lambda b, h: (b, h)),
                  pl.BlockSpec((L, LANES), lambda b, h: (b, kcb + h)),
                  pl.BlockSpec((L, LANES), lambda b, h: (b, vcb + h)),
                  pl.BlockSpec(memory_space=pl.ANY)],
        out_specs=pl.BlockSpec((L, qw), lambda b, h: (b, h)),
        input_output_aliases={4: 0},
        compiler_params=_cp("parallel", "parallel"),
        name="context_attention",
    )(sink, qkv, qkv, qkv, jnp.zeros((qkv.shape[0], n_heads * HEAD_DIM), BF16))


def _rope(x, cos, sin):
    outs = []
    lane = lax.broadcasted_iota(jnp.int32, cos.shape, 1)
    first = (lane % (HEAD_DIM // 2)) < (HEAD_DIM // 4)
    for cgrp in range(x.shape[1] // LANES):
        xc = x[:, cgrp * LANES:(cgrp + 1) * LANES]
        partner = jnp.where(first, pltpu.roll(xc, LANES - HEAD_DIM // 4, 1), pltpu.roll(xc, HEAD_DIM // 4, 1))
        outs.append(xc * cos + partner * sin)
    return outs[0] if len(outs) == 1 else jnp.concatenate(outs, axis=1)


def _lat_attn_body(sink_ref, q_ref, k_ref, v_ref, kc_ref, vc_ref, cosq_ref, sinq_ref, cos_ref, sin_ref, fill_ref,
                   o_ref, k_scr, v_scr, kc_scr, vc_scr, *, q_per_kv, scale, L, blk):
    del fill_ref
    hb = pl.program_id(1)
    i = pl.program_id(2)

    @pl.when(i == 0)
    def _():
        k_scr[...] = _rope(k_ref[...], cos_ref[...], sin_ref[...]).astype(BF16)
        v_scr[...] = v_ref[...].astype(BF16)
        kc_scr[...] = kc_ref[...].astype(BF16)
        vc_scr[...] = vc_ref[...].astype(BF16)

    n_loc = 3 * blk
    start = jnp.clip((i - 1) * blk, 0, L - n_loc)
    start = pl.multiple_of(start, blk)
    q_all = _rope(q_ref[...], cosq_ref[...], sinq_ref[...]) * scale
    k_loc = k_scr[pl.ds(start, n_loc), :]
    v_loc = v_scr[pl.ds(start, n_loc), :]
    rows = q_per_kv * blk
    qpos = i * blk + (lax.broadcasted_iota(jnp.int32, (rows, n_loc), 0) & (blk - 1))
    kpos = start + lax.broadcasted_iota(jnp.int32, (rows, n_loc), 1)
    ok = jnp.abs(qpos - kpos) <= WINDOW
    outs = []
    for kv in range(KV_PER_STEP):
        sl = slice(kv * HEAD_DIM, (kv + 1) * HEAD_DIM)
        qs = _stack_heads(q_all, kv * q_per_kv, q_per_kv, blk).astype(BF16)
        s_loc = jnp.where(ok, lax.dot_general(qs, k_loc[:, sl], _NT, preferred_element_type=F32), -jnp.inf)
        s_ctx = lax.dot_general(qs, kc_scr[:, sl], _NT, preferred_element_type=F32)
        sink = _stacked_sinks(sink_ref, (hb * KV_PER_STEP + kv) * q_per_kv, q_per_kv, blk)
        o = _attend([s_loc, s_ctx], [v_loc[:, sl], vc_scr[:, sl]], sink)
        outs += [o[g * blk:(g + 1) * blk] for g in range(q_per_kv)]
    o_ref[...] = jnp.concatenate(outs, axis=1).astype(o_ref.dtype)


def rope_tables(L):
    rows = L // GRID_W
    row = jnp.repeat(jnp.arange(rows, dtype=F32), GRID_W)
    col = jnp.tile(jnp.arange(GRID_W, dtype=F32), rows)
    quarter = HEAD_DIM // 4
    inv = ROPE_BASE ** (-jnp.arange(quarter, dtype=F32) / quarter)
    ang_r = row[:, None] * inv[None, :]
    ang_c = col[:, None] * inv[None, :]
    ang = jnp.concatenate([ang_r, ang_r, ang_c, ang_c], axis=1)
    sign = jnp.tile(jnp.concatenate([-jnp.ones((quarter,), F32), jnp.ones((quarter,), F32)]), 2)
    cos = jnp.tile(jnp.cos(ang), (1, LANES // HEAD_DIM))
    sin = jnp.tile(jnp.sin(ang) * sign[None, :], (1, LANES // HEAD_DIM))
    return cos, sin


def latent_attention(qkv, k_cache, v_cache, sink, fill, *, row0, nseq, L, n_heads):
    blk = WINDOW
    q_per_kv = n_heads // N_KV_HEADS
    qw = KV_PER_STEP * q_per_kv * HEAD_DIM
    nhb = N_KV_HEADS // KV_PER_STEP
    kcb = n_heads * HEAD_DIM // LANES
    vcb = kcb + N_KV_HEADS * HEAD_DIM // LANES
    nb = L // blk
    past = k_cache.shape[1]
    cos, sin = rope_tables(L)
    return pl.pallas_call(
        functools.partial(_lat_attn_body, q_per_kv=q_per_kv, scale=HEAD_DIM ** -0.5, L=L, blk=blk),
        out_shape=jax.ShapeDtypeStruct(fill.shape, BF16),
        grid=(nseq, nhb, nb),
        in_specs=[pl.BlockSpec(memory_space=pltpu.SMEM),
                  pl.BlockSpec((blk, qw), lambda b, h, i: (row0 // blk + b * nb + i, h)),
                  pl.BlockSpec((L, LANES), lambda b, h, i: (row0 // L + b, kcb + h)),
                  pl.BlockSpec((L, LANES), lambda b, h, i: (row0 // L + b, vcb + h)),
                  pl.BlockSpec((None, past, LANES), lambda b, h, i: (b, 0, h)),
                  pl.BlockSpec((None, past, LANES), lambda b, h, i: (b, 0, h)),
                  pl.BlockSpec((blk, LANES), lambda b, h, i: (i, 0)),
                  pl.BlockSpec((blk, LANES), lambda b, h, i: (i, 0)),
                  pl.BlockSpec((L, LANES), lambda b, h, i: (0, 0)),
                  pl.BlockSpec((L, LANES), lambda b, h, i: (0, 0)),
                  pl.BlockSpec(memory_space=pl.ANY)],
        out_specs=pl.BlockSpec((blk, qw), lambda b, h, i: (row0 // blk + b * nb + i, h)),
        scratch_shapes=[pltpu.VMEM((L, LANES), BF16), pltpu.VMEM((L, LANES), BF16),
                        pltpu.VMEM((past, LANES), BF16), pltpu.VMEM((past, LANES), BF16)],
        input_output_aliases={10: 0},
        compiler_params=_cp("parallel", "parallel", "arbitrary"),
        name="latent_attention",
    )(sink, qkv, qkv, qkv, k_cache, v_cache, cos, sin, cos, sin, fill)


MOE_TILE = 256
MOE_FFN_VMEM_BYTES = 56 * 1024 * 1024


def _moe_dispatch_body(pos_ref, h_ref, g_ref, init_hbm, xs_hbm, buf, sem):
    del init_hbm
    i = pl.program_id(0)
    tr, d = h_ref.shape
    buf[:, :d] = h_ref[...]
    buf[:, d:] = g_ref[...]

    def start(r2, carry):
        for prio in range(DMA_PRIORITIES):
            r = r2 * DMA_PRIORITIES + prio
            pltpu.make_async_copy(buf.at[pl.ds(r, 1)], xs_hbm.at[pl.ds(pos_ref[i * tr + r], 1)],
                                  sem).start(priority=prio)
        return carry

    lax.fori_loop(0, tr // DMA_PRIORITIES, start, 0)

    def wait(r, carry):
        pltpu.make_async_copy(buf.at[pl.ds(r, 1)], xs_hbm.at[pl.ds(0, 1)], sem).wait()
        return carry

    lax.fori_loop(0, tr, wait, 0)


def moe_dispatch(h, gate_lanes, pos, n_slots):
    t, d = h.shape
    tr = ROW_TILE
    w = d + LANES
    grid_spec = pltpu.PrefetchScalarGridSpec(
        num_scalar_prefetch=1,
        grid=(t // tr,),
        in_specs=[pl.BlockSpec((tr, d), lambda i, p: (i, 0)),
                  pl.BlockSpec((tr, LANES), lambda i, p: (i, 0)),
                  pl.BlockSpec(memory_space=pl.ANY)],
        out_specs=pl.BlockSpec(memory_space=pl.ANY),
        scratch_shapes=[pltpu.VMEM((tr, w), F32), pltpu.SemaphoreType.DMA(())])
    return pl.pallas_call(
        _moe_dispatch_body,
        out_shape=jax.ShapeDtypeStruct((n_slots, w), F32),
        grid_spec=grid_spec,
        input_output_aliases={3: 0},
        compiler_params=_cp("arbitrary"),
        name="moe_dispatch",
    )(pos, h, gate_lanes, jnp.zeros((n_slots, w), F32))


def _moe_ffn_body(ea_ref, eb_ref, nused_ref, xs_ref, w1a_ref, w3a_ref, w2a_ref, w1b_ref, w3b_ref, w2b_ref, o_ref):
    i = pl.program_id(0)
    d = o_ref.shape[1]

    @pl.when(i < nused_ref[0])
    def _():
        x = xs_ref[:, :d].astype(BF16)

        def expert(w1_ref, w3_ref, w2_ref):
            a = jnp.dot(x, w1_ref[...], preferred_element_type=F32)
            b = jnp.dot(x, w3_ref[...], preferred_element_type=F32)
            return jnp.dot((_silu(a) * b).astype(BF16), w2_ref[...], preferred_element_type=F32)

        o_ref[...] = (xs_ref[:, d:d + 1] * expert(w1a_ref, w3a_ref, w2a_ref)
                      + xs_ref[:, d + 1:d + 2] * expert(w1b_ref, w3b_ref, w2b_ref))

    @pl.when(i >= nused_ref[0])
    def _():
        o_ref[...] = jnp.zeros_like(o_ref)


def moe_ffn(xs, tile_ea, tile_eb, n_used, w1, w3, w2):
    _, d, de = w1.shape
    tm = MOE_TILE
    n_tiles = tile_ea.shape[0]
    up_a = pl.BlockSpec((None, d, de), lambda i, ea, eb, nu: (ea[i], 0, 0))
    up_b = pl.BlockSpec((None, d, de), lambda i, ea, eb, nu: (eb[i], 0, 0))
    grid_spec = pltpu.PrefetchScalarGridSpec(
        num_scalar_prefetch=3,
        grid=(n_tiles,),
        in_specs=[pl.BlockSpec((tm, d + LANES), lambda i, ea, eb, nu: (i, 0)),
                  up_a, up_a, pl.BlockSpec((None, de, d), lambda i, ea, eb, nu: (ea[i], 0, 0)),
                  up_b, up_b, pl.BlockSpec((None, de, d), lambda i, ea, eb, nu: (eb[i], 0, 0))],
        out_specs=pl.BlockSpec((tm, d), lambda i, ea, eb, nu: (i, 0)))
    return pl.pallas_call(
        _moe_ffn_body,
        out_shape=jax.ShapeDtypeStruct((n_tiles * tm, d), F32),
        grid_spec=grid_spec,
        compiler_params=_cp("arbitrary", vmem=MOE_FFN_VMEM_BYTES),
        name="moe_ffn",
    )(tile_ea, tile_eb, n_used, xs, w1, w3, w2, w1, w3, w2)


def _moe_combine_body(pos_ref, y_hbm, x_ref, g_ref, o_ref, ybuf, sem):
    i = pl.program_id(0)
    tr = x_ref.shape[0]

    def start(r2, carry):
        for prio in range(DMA_PRIORITIES):
            r = r2 * DMA_PRIORITIES + prio
            pltpu.make_async_copy(y_hbm.at[pl.ds(pos_ref[i * tr + r], 1)], ybuf.at[pl.ds(r, 1)],
                                  sem).start(priority=prio)
        return carry

    lax.fori_loop(0, tr // DMA_PRIORITIES, start, 0)

    def wait(r, carry):
        pltpu.make_async_copy(y_hbm.at[pl.ds(0, 1)], ybuf.at[pl.ds(r, 1)], sem).wait()
        return carry

    lax.fori_loop(0, tr, wait, 0)
    o_ref[...] = x_ref[...] + g_ref[...] * ybuf[...]


def moe_combine(x, y_sorted, pos, mod3, gate_idx, seg_of_tile):
    t, d = x.shape
    tr = ROW_TILE
    grid_spec = pltpu.PrefetchScalarGridSpec(
        num_scalar_prefetch=1,
        grid=(t // tr,),
        in_specs=[pl.BlockSpec(memory_space=pl.ANY),
                  pl.BlockSpec((tr, d), lambda i, p: (i, 0)),
                  pl.BlockSpec((None, 1, d), lambda i, p: (seg_of_tile(i) * 6 + gate_idx, 0, 0))],
        out_specs=pl.BlockSpec((tr, d), lambda i, p: (i, 0)),
        scratch_shapes=[pltpu.VMEM((tr, d), F32), pltpu.SemaphoreType.DMA(())])
    return pl.pallas_call(
        _moe_combine_body,
        out_shape=jax.ShapeDtypeStruct((t, d), F32),
        grid_spec=grid_spec,
        compiler_params=_cp("arbitrary"),
        name="moe_combine",
    )(pos, y_sorted, x, mod3)


def moe_route(logits, n_groups, n_experts):
    per = n_experts // n_groups
    assert TOP_K == 2
    g_prob = jax.nn.softmax(logits[:, :n_groups], axis=-1)
    g_top = jnp.max(g_prob, axis=-1, keepdims=True)
    g_idx = jnp.argmax(g_prob, axis=-1, keepdims=True).astype(jnp.int32)
    e_logits = logits[:, n_groups:n_groups + n_experts].reshape(-1, n_groups, per)
    g_onehot = g_idx == jnp.arange(n_groups)[None, :]
    e_in = jnp.sum(jnp.where(g_onehot[:, :, None], e_logits, 0.0), axis=1)
    e_prob = jax.nn.softmax(e_in, axis=-1)
    i0 = jnp.argmax(e_prob, axis=-1, keepdims=True).astype(jnp.int32)
    rest = jnp.where(jnp.arange(per)[None, :] == i0, -jnp.inf, e_prob)
    i1 = jnp.argmax(rest, axis=-1, keepdims=True).astype(jnp.int32)
    e_idx = jnp.concatenate([i0, i1], axis=1)
    e_top = jnp.concatenate([jnp.max(e_prob, axis=-1, keepdims=True), jnp.max(rest, axis=-1, keepdims=True)], axis=1)
    e_top = e_top / jnp.sum(e_top, axis=-1, keepdims=True)
    return g_idx * per + e_idx, g_top * e_top


def moe_plan(expert_ids, gates, n_groups, n_experts):
    assert TOP_K == 2
    t = expert_ids.shape[0]
    tm = MOE_TILE
    per = n_experts // n_groups
    n_pairs = per * (per - 1) // 2
    n_classes = n_groups * n_pairs
    n_tiles = t // tm + n_classes
    e0, e1 = expert_ids[:, 0], expert_ids[:, 1]
    lo, hi = jnp.minimum(e0, e1), jnp.maximum(e0, e1)
    g_lo = jnp.where(e0 < e1, gates[:, 0], gates[:, 1])
    g_hi = jnp.where(e0 < e1, gates[:, 1], gates[:, 0])
    a, b = lo % per, hi % per
    cls = (lo // per) * n_pairs + (a * (2 * per - a - 1)) // 2 + (b - a - 1)
    onehot = (cls[:, None] == jnp.arange(n_classes)[None, :]).astype(F32)
    blk = ROW_TILE
    inner = jnp.einsum("ij,bjk->bik", jnp.tril(jnp.ones((blk, blk), F32)), onehot.reshape(-1, blk, n_classes))
    bsum = inner[:, -1, :]
    running = (inner + (jnp.cumsum(bsum, axis=0) - bsum)[:, None, :]).reshape(-1, n_classes)
    counts = jnp.sum(bsum, axis=0).astype(jnp.int32)
    tiles_per = (counts + tm - 1) // tm
    tile_start = jnp.cumsum(tiles_per) - tiles_per
    slot = running - 1.0 + (tile_start * tm).astype(F32)[None, :]
    pos = jnp.sum(jnp.where(onehot > 0, slot, 0.0), axis=1).astype(jnp.int32)
    n_used = jnp.sum(tiles_per).astype(jnp.int32)
    tile_ids = jnp.arange(n_tiles, dtype=jnp.int32)
    tile_cls = jnp.sum((tile_ids[:, None] >= tile_start[None, :]).astype(jnp.int32), axis=1) - 1
    tile_cls = jnp.where(tile_ids < n_used, tile_cls, tile_cls[jnp.maximum(n_used - 1, 0)])
    pairs = [(pa, pb) for pa in range(per) for pb in range(pa + 1, per)]
    cls_lo = jnp.array([g * per + pa for g in range(n_groups) for pa, _ in pairs], jnp.int32)
    cls_hi = jnp.array([g * per + pb for g in range(n_groups) for _, pb in pairs], jnp.int32)
    gate_lanes = jnp.pad(jnp.stack([g_lo, g_hi], axis=1), ((0, 0), (0, LANES - 2)))
    tile_onehot = tile_cls[:, None] == jnp.arange(n_classes)[None, :]
    tile_lo = jnp.sum(jnp.where(tile_onehot, cls_lo[None, :], 0), axis=1)
    tile_hi = jnp.sum(jnp.where(tile_onehot, cls_hi[None, :], 0), axis=1)
    return pos, gate_lanes, tile_lo, tile_hi, n_used.reshape(1), n_tiles * tm


def kernel(x_prompt, x_sample, state_ssd_fwd, state_ssd_bwd, cache_attn_k, cache_attn_v, c, c_ctx, ada_w, ada_b, norm_mix, norm_ffn, norm_final, ev_w_in, ev_w_out, hy_conv_w, hy_conv_b, hy_f_w1, hy_f_b1, hy_f_freq1, hy_f_w2, hy_f_b2, hy_f_freq2, hy_f_w3, hy_bias, ssd_conv_w, ssd_conv_b, ssd_dt_bias, ssd_a_log, ssd_d, ssd_norm, at_wq, at_wk, at_wv, at_wo, at_sink, moe_w_group, moe_b_group, moe_w_expert, moe_b_expert, moe_w1, moe_w3, moe_w2):
    bc, lc, d = x_prompt.shape
    bl, ll, _ = x_sample.shape
    tc_rows, tl_rows = bc * lc, bl * ll
    t = tc_rows + tl_rows
    depth = ada_w.shape[0]
    n_seg = 1 + bl
    assert lc % ROW_TILE == 0 and ll % ROW_TILE == 0 and tc_rows % ll == 0
    seg_rows = math.gcd(tc_rows, ll)

    def seg_of_rows(r):
        return jnp.where(r < tc_rows, 0, 1 + (r - tc_rows) // ll)

    def seg_of_tile(i):
        return seg_of_rows(i * ROW_TILE)

    x = jnp.concatenate([x_prompt.reshape(tc_rows, d), x_sample.reshape(tl_rows, d)], axis=0)
    cvec = jnp.concatenate([c_ctx[None], c, jnp.zeros((SUBLANES - n_seg, d), F32)], axis=0)
    mod_all = ada_mod(cvec, ada_w, ada_b)

    tile_row = jnp.arange(t // ROW_TILE, dtype=jnp.int32) * ROW_TILE
    seq_len = jnp.where(tile_row < tc_rows, lc, ll)
    rel = jnp.where(tile_row < tc_rows, tile_row, tile_row - tc_rows)
    conv_flags = jnp.stack([(rel % seq_len == 0), ((rel + ROW_TILE) % seq_len == 0)]).astype(jnp.int32)

    new_sf, new_sb, new_k, new_v = [], [], [], []
    for l in range(depth):
        i = l // 2
        mod3 = mod_all[l, :n_seg].reshape(n_seg * 6, 1, d)
        h = rms_norm(x, norm_mix[l], out_dtype=BF16, mod3=mod3, shift_idx=0, scale_idx=1,
                     seg_of_tile=seg_of_tile)
        if l % 2 == 0:
            hyena_w = hy_conv_w.shape[2] // 3
            inner = ssd_norm.shape[1]
            heads = inner // SSD_HEAD_DIM
            conv_ch = ssd_conv_w.shape[2]
            main_cols = 3 * hyena_w + inner + conv_ch
            w_in = ev_w_in[i].astype(BF16)
            proj = dense(h, w_in, n_cols=main_cols, out_dtype=BF16, name="in_proj")
            dt_raw = dense(h, w_in[:, main_cols:], out_dtype=F32, name="dt_proj")
            conv_w = jnp.concatenate([hy_conv_w[i], ssd_conv_w[i]], axis=1)
            conv_b = jnp.concatenate([hy_conv_b[i], ssd_conv_b[i]], axis=0)
            u = dwconv(proj, conv_w, conv_b, conv_flags, n_plain_cols=3 * hyena_w, gap_cols=inner,
                       out_cols=3 * hyena_w + conv_ch)
            y_a = jnp.zeros((t, hyena_w), BF16)
            for row0, nb, L in ((0, bc, lc), (tc_rows, bl, ll)):
                filt = hyena_filters(L, hy_f_w1[i], hy_f_b1[i], hy_f_freq1[i], hy_f_w2[i], hy_f_b2[i],
                                     hy_f_freq2[i], hy_f_w3[i], hyena_w)
                if L % (16 * SPEC_BLOCK) == 0:
                    n1 = 2 * L // SPEC_BLOCK
                    t1f = fft_table("t1f", n1, SPEC_BLOCK)
                    kf = hyena_spectrum_2level(L, filt, t1f, fft_table("f2filt", n1, SPEC_BLOCK))
                    tabs = (t1f, fft_table("f2f", n1, SPEC_BLOCK), fft_table("f2i", n1, SPEC_BLOCK),
                            fft_table("t1i", n1, SPEC_BLOCK))
                    y_a = hyena_stream_2level(u, row0, nb, L, kf, tabs, hy_bias[i], hyena_w, fill=y_a)
                else:
                    a_tab = dft_table(L, "filt")
                    kf = hyena_spectrum(L, filt, a_tab)
                    y_a = hyena_stream(u, row0, nb, L, kf, a_tab, dft_table(L, "inv"), hy_bias[i], hyena_w,
                                       fill=y_a)
            col0 = 3 * hyena_w
            gn = SSD_GROUPS * SSD_STATE
            bt_all = u[:, col0 + inner:col0 + inner + gn].T
            fac, cum_r = ssd_prep(dt_raw, dt_raw.T, ssd_dt_bias[i], ssd_a_log[i])
            zero = jnp.zeros((bc, SSD_GROUPS, SSD_STATE, inner // SSD_GROUPS), F32)
            init_f = jnp.concatenate([zero, _state_to_kernel_layout(state_ssd_fwd[:, i], SSD_GROUPS)], axis=0)
            init_b = jnp.concatenate([zero, _state_to_kernel_layout(state_ssd_bwd[:, i], SSD_GROUPS)], axis=0)
            yf, yb, sf, sb = ssd_scan(u, bt_all, fac, cum_r, init_f, init_b,
                                      seq_lens=[lc] * bc + [ll] * bl, col0=col0, inner=inner)
            new_sf.append(_state_from_kernel_layout(sf[:bc], SSD_HEAD_DIM))
            new_sb.append(_state_from_kernel_layout(sb[:bc], SSD_HEAD_DIM))
            y_b = ssd_finish(yf, yb, u, proj, jnp.repeat(ssd_d[i], SSD_HEAD_DIM), ssd_norm[i],
                             xs_col=col0, z_col=3 * hyena_w, inner=inner)
            w_out = ev_w_out[i].astype(BF16)
            x = dense_residual(y_a, w_out, x, mod3, 2, seg_of_rows, seg_rows, name="out_proj_a")
            x = dense_residual(y_b, w_out, x, mod3, 2, seg_of_rows, seg_rows, w_row0=hyena_w, name="out_proj_b")
        else:
            n_heads = at_wq.shape[2] // HEAD_DIM
            w_qkv = jnp.concatenate([at_wq[i], at_wk[i], at_wv[i]], axis=1).astype(BF16)
            qkv = dense(h, w_qkv, out_dtype=F32, name="qkv_proj")
            kvw = N_KV_HEADS * HEAD_DIM
            att_c = context_attention(qkv, at_sink[i], nseq=bc, L=lc, n_heads=n_heads)
            past = cache_attn_k.shape[2]
            att = latent_attention(qkv, cache_attn_k[:, i].reshape(bl, past, kvw),
                                   cache_attn_v[:, i].reshape(bl, past, kvw), at_sink[i], att_c,
                                   row0=tc_rows, nseq=bl, L=ll, n_heads=n_heads)
            x = dense_residual(att, at_wo[i].astype(BF16), x, mod3, 2, seg_of_rows, seg_rows, name="attn_out")
            qw = n_heads * HEAD_DIM
            new_k.append(qkv[:tc_rows, qw:qw + kvw].reshape(bc, lc, N_KV_HEADS, HEAD_DIM))
            new_v.append(qkv[:tc_rows, qw + kvw:].reshape(bc, lc, N_KV_HEADS, HEAD_DIM))
        h2 = rms_norm(x, norm_ffn[l], out_dtype=F32, mod3=mod3, shift_idx=3, scale_idx=4,
                      seg_of_tile=seg_of_tile)
        n_experts = moe_w_expert.shape[2]
        w_r = jnp.concatenate([moe_w_group[l], moe_w_expert[l]], axis=1)
        w_r = jnp.pad(w_r, ((0, 0), (0, LANES - w_r.shape[1]))).astype(BF16)
        b_r = jnp.pad(jnp.concatenate([moe_b_group[l], moe_b_expert[l]]), (0, LANES - MOE_GROUPS - n_experts))
        logits = dense(h2, w_r, out_dtype=F32, extra=(b_r.reshape(1, LANES),),
                       extra_specs=(pl.BlockSpec((1, LANES), lambda bt, ii, j, kk: (0, 0)),),
                       epilogue=lambda acc, b_ref, row_tile: acc + b_ref[...], name="router")
        expert_ids, gates = moe_route(logits, MOE_GROUPS, n_experts)
        pos, gate_lanes, tile_lo, tile_hi, n_used, n_slots = moe_plan(expert_ids, gates, MOE_GROUPS, n_experts)
        xs = moe_dispatch(h2, gate_lanes, pos, n_slots)
        y_sorted = moe_ffn(xs, tile_lo, tile_hi, n_used, moe_w1[l].astype(BF16), moe_w3[l].astype(BF16),
                           moe_w2[l].astype(BF16))
        x = moe_combine(x, y_sorted, pos, mod3, 5, seg_of_tile)

    y_prompt = rms_norm(x, norm_final, out_dtype=F32, rows=tc_rows).reshape(bc, lc, d)
    y_sample = rms_norm(x, norm_final, out_dtype=F32, row_off=tc_rows, rows=tl_rows).reshape(bl, ll, d)
    return (y_prompt, y_sample, jnp.stack(new_sf, axis=1), jnp.stack(new_sb, axis=1),
            jnp.stack(new_k, axis=1), jnp.stack(new_v, axis=1))
```

```python
import functools
import math

import jax
import jax.numpy as jnp
from jax import lax
from jax.experimental import pallas as pl
from jax.experimental.pallas import tpu as pltpu

F32 = jnp.float32
BF16 = jnp.bfloat16
HI = lax.Precision.HIGHEST

RMS_EPS = 1e-6
GRID_W = 64
HYENA_BANDS = 16
HYENA_FAST_DECAY = 0.3
HYENA_SLOW_DECAY = 1.5
HYENA_TARGET = 1e-2
SSD_HEAD_DIM = 64
SSD_STATE = 128
SSD_GROUPS = 8
SSD_CHUNK = 128
HEAD_DIM = 64
N_KV_HEADS = 8
WINDOW = 128
ROPE_BASE = 10000.0
MOE_GROUPS = 4
TOP_K = 2

LANES = 128
SUBLANES = 8
BF16_SUBLANES = 16
DMA_PRIORITIES = 2
VMEM_LIMIT_BYTES = 48 * 1024 * 1024

ROW_TILE = 256
SPEC_BLOCK = 128


def _cp(*sem, vmem=VMEM_LIMIT_BYTES):
    return pltpu.CompilerParams(dimension_semantics=sem, vmem_limit_bytes=vmem)


def _tile(n, pref):
    if n <= pref:
        return n
    t = pref
    while n % t:
        t //= 2
    assert t >= LANES, (n, pref)
    return t


def _silu(x):
    return x * (1.0 / (1.0 + jnp.exp(-x)))


def _softplus(x):
    return jnp.maximum(x, 0.0) + jnp.log(1.0 + jnp.exp(-jnp.abs(x)))


def _ada_body(c_ref, w_ref, b_ref, o_ref):
    a = _silu(c_ref[...])
    o_ref[...] = lax.dot_general(a, w_ref[...], (((1,), (0,)), ((), ())), precision=HI,
                                 preferred_element_type=F32) + b_ref[...]


def ada_mod(cvec, ada_w, ada_b):
    depth, d, n = ada_w.shape
    rows = cvec.shape[0]
    tn = _tile(n, 1024)
    return pl.pallas_call(
        _ada_body,
        out_shape=jax.ShapeDtypeStruct((depth, rows, n), F32),
        grid=(depth, n // tn),
        in_specs=[pl.BlockSpec((rows, d), lambda l, j: (0, 0)),
                  pl.BlockSpec((None, d, tn), lambda l, j: (l, 0, j)),
                  pl.BlockSpec((None, 1, tn), lambda l, j: (l, 0, j))],
        out_specs=pl.BlockSpec((None, rows, tn), lambda l, j: (l, 0, j)),
        compiler_params=_cp("parallel", "parallel"),
        name="ada_mod",
    )(cvec, ada_w, ada_b.reshape(depth, 1, n))


def _norm_body(x_ref, g_ref, *rest, modulate):
    o_ref = rest[-1]
    x = x_ref[...]
    y = x * lax.rsqrt(jnp.mean(x * x, axis=-1, keepdims=True) + RMS_EPS)
    y = y * g_ref[...]
    if modulate:
        sh_ref, sc_ref = rest[0], rest[1]
        y = y * (1.0 + sc_ref[...]) + sh_ref[...]
    o_ref[...] = y.astype(o_ref.dtype)


def rms_norm(x, g, *, out_dtype, mod3=None, shift_idx=0, scale_idx=0, seg_of_tile=None,
             row_off=0, rows=None):
    t, d = x.shape
    rows = t if rows is None else rows
    tr = ROW_TILE
    off = row_off // tr
    in_specs = [pl.BlockSpec((tr, d), lambda i: (i + off, 0)),
                pl.BlockSpec((1, d), lambda i: (0, 0))]
    args = [x, g.reshape(1, d)]
    if mod3 is not None:
        in_specs += [pl.BlockSpec((None, 1, d), lambda i: (seg_of_tile(i) * 6 + shift_idx, 0, 0)),
                     pl.BlockSpec((None, 1, d), lambda i: (seg_of_tile(i) * 6 + scale_idx, 0, 0))]
        args += [mod3, mod3]
    return pl.pallas_call(
        functools.partial(_norm_body, modulate=mod3 is not None),
        out_shape=jax.ShapeDtypeStruct((rows, d), out_dtype),
        grid=(rows // tr,),
        in_specs=in_specs,
        out_specs=pl.BlockSpec((tr, d), lambda i: (i, 0)),
        compiler_params=_cp("parallel"),
        name="rms_norm",
    )(*args)


def _mm_body(*refs, nk, n_extra, n_skip, epilogue):
    a_ref, b_ref = refs[0], refs[1]
    extra = refs[2:2 + n_extra]
    o_ref = refs[2 + n_extra + n_skip]
    row_tile = pl.program_id(1)
    b = b_ref[...]
    if b.ndim == 3:
        b = b.reshape(-1, b.shape[-1])
    part = jnp.dot(a_ref[...].astype(b.dtype), b, preferred_element_type=F32)
    if nk == 1:
        o_ref[...] = epilogue(part, *extra, row_tile=row_tile).astype(o_ref.dtype).reshape(o_ref.shape)
        return
    acc_ref = refs[3 + n_extra + n_skip]
    k = pl.program_id(3)

    @pl.when(k == 0)
    def _():
        acc_ref[...] = part

    @pl.when(k > 0)
    def _():
        acc_ref[...] += part

    @pl.when(k == nk - 1)
    def _():
        o_ref[...] = epilogue(acc_ref[...], *extra, row_tile=row_tile).astype(o_ref.dtype)


def matmul(a, b, *, grid, tm, tn, a_spec, b_spec, o_spec, out_shape, out_dtype,
           extra=(), extra_specs=(), epilogue=None, fill=None, name="matmul"):
    nk = grid[3]
    epilogue = epilogue or (lambda acc, row_tile: acc)
    scratch = [pltpu.VMEM((tm, tn), F32)] if nk > 1 else []
    fills = () if fill is None else (fill,)
    return pl.pallas_call(
        functools.partial(_mm_body, nk=nk, n_extra=len(extra), n_skip=len(fills), epilogue=epilogue),
        out_shape=jax.ShapeDtypeStruct(out_shape, out_dtype),
        grid=grid,
        in_specs=[a_spec, b_spec, *extra_specs, *[pl.BlockSpec(memory_space=pl.ANY) for _ in fills]],
        out_specs=o_spec,
        scratch_shapes=scratch,
        input_output_aliases={2 + len(extra): 0} if fills else {},
        compiler_params=_cp("parallel", "parallel", "parallel", "arbitrary"),
        name=name,
    )(a, b, *extra, *fills)


def dense(a, w, *, out_dtype, tm=1024, tn=512, tk=2048, extra=(), extra_specs=(), epilogue=None,
          w_row0=0, n_cols=None, name="dense"):
    m, k = a.shape
    n = w.shape[1] if n_cols is None else n_cols
    tm, tn, tk = _tile(m, tm), _tile(n, tn), _tile(k, tk)
    assert w_row0 % tk == 0
    kb0 = w_row0 // tk
    return matmul(
        a, w, grid=(1, m // tm, n // tn, k // tk), tm=tm, tn=tn,
        a_spec=pl.BlockSpec((tm, tk), lambda bt, i, j, kk: (i, kk)),
        b_spec=pl.BlockSpec((tk, tn), lambda bt, i, j, kk: (kb0 + kk, j)),
        o_spec=pl.BlockSpec((tm, tn), lambda bt, i, j, kk: (i, j)),
        out_shape=(m, n), out_dtype=out_dtype, extra=extra, extra_specs=extra_specs,
        epilogue=epilogue, name=name)


def dense_residual(a, w, x, mod3, gate_idx, seg_of_rows, seg_rows, *, tm=1024, tn=512, tk=2048, w_row0=0,
                   name="dense_res"):
    n = w.shape[1]
    tm_ = _tile(seg_rows, tm)
    tn_ = _tile(n, tn)
    return dense(
        a, w, out_dtype=F32, tm=tm_, tn=tn, tk=tk, w_row0=w_row0, extra=(x, mod3),
        extra_specs=(pl.BlockSpec((tm_, tn_), lambda bt, i, j, kk: (i, j)),
                     pl.BlockSpec((None, 1, tn_),
                                  lambda bt, i, j, kk: (seg_of_rows(i * tm_) * 6 + gate_idx, 0, j))),
        epilogue=lambda acc, x_ref, g_ref, row_tile: x_ref[...] + g_ref[...] * acc, name=name)


def _dwconv_body(flags_ref, x_ref, p_ref, n_ref, w_ref, b_ref, o_ref, *, silu):
    i = pl.program_id(0)
    act = _silu if silu else (lambda v: v)
    x = x_ref[...].astype(F32)
    tr = x.shape[0]
    w = w_ref[...]
    y = pltpu.roll(x, 1, 0) * w[0:1, :] + x * w[1:2, :] + pltpu.roll(x, tr - 1, 0) * w[2:3, :] + b_ref[...]
    o_ref[...] = act(y).astype(o_ref.dtype)
    keep_prev = (flags_ref[0, i] == 0).astype(F32)
    keep_next = (flags_ref[1, i] == 0).astype(F32)
    halo_prev = p_ref[SUBLANES - 1:SUBLANES, :].astype(F32) * keep_prev
    halo_next = n_ref[0:1, :].astype(F32) * keep_next
    slab = BF16_SUBLANES
    row = lax.broadcasted_iota(jnp.int32, (slab, x.shape[1]), 0)
    fix_top = (halo_prev - x[tr - 1:tr, :]) * w[0:1, :]
    fix_bot = (halo_next - x[0:1, :]) * w[2:3, :]
    o_ref[0:slab, :] = act(y[0:slab] + jnp.where(row == 0, fix_top, 0.0)).astype(o_ref.dtype)
    o_ref[tr - slab:tr, :] = act(y[tr - slab:tr] + jnp.where(row == slab - 1, fix_bot, 0.0)).astype(o_ref.dtype)


def dwconv(proj, w, b, flags, *, col0, silu):
    t = proj.shape[0]
    cols = w.shape[1]
    tr = ROW_TILE
    tc = _tile(math.gcd(col0, cols), 2048)
    cb0 = col0 // tc
    hb = tr // SUBLANES
    last_hb = t // SUBLANES - 1
    grid_spec = pltpu.PrefetchScalarGridSpec(
        num_scalar_prefetch=1,
        grid=(t // tr, cols // tc),
        in_specs=[pl.BlockSpec((tr, tc), lambda i, j, f: (i, cb0 + j)),
                  pl.BlockSpec((SUBLANES, tc), lambda i, j, f: (jnp.maximum(i * hb - 1, 0), cb0 + j)),
                  pl.BlockSpec((SUBLANES, tc), lambda i, j, f: (jnp.minimum((i + 1) * hb, last_hb), cb0 + j)),
                  pl.BlockSpec((3, tc), lambda i, j, f: (0, j)),
                  pl.BlockSpec((1, tc), lambda i, j, f: (0, j))],
        out_specs=pl.BlockSpec((tr, tc), lambda i, j, f: (i, j)))
    return pl.pallas_call(
        functools.partial(_dwconv_body, silu=silu),
        out_shape=jax.ShapeDtypeStruct((t, cols), BF16),
        grid_spec=grid_spec,
        compiler_params=_cp("parallel", "parallel"),
        name="dwconv_silu" if silu else "dwconv",
    )(flags, proj, proj, proj, w, b.reshape(1, -1))


def _filter_body(feat_ref, w1_ref, b1_ref, f1_ref, w2_ref, b2_ref, f2_ref, w3_ref, dl_ref, o_ref):
    d = pl.program_id(0)
    i = pl.program_id(1)
    feat = feat_ref[...]
    dn = (((1,), (0,)), ((), ()))
    hid = jnp.sin(f1_ref[...] * (lax.dot_general(feat, w1_ref[...], dn, precision=HI,
                                                 preferred_element_type=F32) + b1_ref[...]))
    hid = jnp.sin(f2_ref[...] * (lax.dot_general(hid, w2_ref[...], dn, precision=HI,
                                                 preferred_element_type=F32) + b2_ref[...]))
    filt = jnp.dot(hid.astype(BF16), w3_ref[...].astype(BF16), preferred_element_type=F32)
    t01 = feat[:, 0:1]
    filt = filt * jnp.exp(-t01 * dl_ref[...])
    row = lax.broadcasted_iota(jnp.int32, filt.shape, 0) + i * filt.shape[0]
    filt = jnp.where((d == 1) & (row == 0), 0.0, filt)
    o_ref[...] = filt.astype(o_ref.dtype)


def hyena_filters(L, f_w1, f_b1, f_freq1, f_w2, f_b2, f_freq2, f_w3, c):
    t = jnp.arange(L, dtype=F32)
    t01 = t / (L - 1)
    bands = jnp.linspace(1e-4, HYENA_BANDS - 1, HYENA_BANDS, dtype=F32)
    ang = (2.0 * math.pi / L) * t[:, None] * bands[None, :]
    feat = jnp.concatenate([t01[:, None], jnp.cos(ang), -jnp.sin(ang)], axis=-1)
    emb = feat.shape[1]
    feat = jnp.pad(feat, ((0, 0), (0, LANES - emb)))
    w1 = jnp.pad(f_w1, ((0, LANES - emb), (0, 0)))
    ffn = f_w1.shape[1]
    order = f_w3.shape[1] // (2 * c)
    deltas = jnp.abs(jnp.linspace(math.log(HYENA_TARGET) / HYENA_SLOW_DECAY,
                                  math.log(HYENA_TARGET) / HYENA_FAST_DECAY, c, dtype=F32))
    deltas = jnp.tile(deltas, order).reshape(1, order * c)
    tl, tn = _tile(L, 512), _tile(order * c, 4096)
    nj = order * c // tn
    return pl.pallas_call(
        _filter_body,
        out_shape=jax.ShapeDtypeStruct((2, L, order * c), BF16),
        grid=(2, L // tl, nj),
        in_specs=[pl.BlockSpec((tl, LANES), lambda d, i, j: (i, 0)),
                  pl.BlockSpec((LANES, ffn), lambda d, i, j: (0, 0)),
                  pl.BlockSpec((1, ffn), lambda d, i, j: (0, 0)),
                  pl.BlockSpec((1, ffn), lambda d, i, j: (0, 0)),
                  pl.BlockSpec((ffn, ffn), lambda d, i, j: (0, 0)),
                  pl.BlockSpec((1, ffn), lambda d, i, j: (0, 0)),
                  pl.BlockSpec((1, ffn), lambda d, i, j: (0, 0)),
                  pl.BlockSpec((ffn, tn), lambda d, i, j: (0, d * nj + j)),
                  pl.BlockSpec((1, tn), lambda d, i, j: (0, j))],
        out_specs=pl.BlockSpec((None, tl, tn), lambda d, i, j: (d, i, j)),
        compiler_params=_cp("parallel", "parallel", "parallel"),
        name="hyena_filters",
    )(feat, w1, f_b1.reshape(1, ffn), f_freq1.reshape(1, ffn), f_w2, f_b2.reshape(1, ffn),
      f_freq2.reshape(1, ffn), f_w3, deltas)


def _cos_turns(m, log2_period):
    period = 1 << log2_period
    quarter = period // 4
    mm = m + quarter // 2
    q = (mm >> (log2_period - 2)) & 3
    r = (mm & (quarter - 1)) - quarter // 2
    phi = r.astype(F32) * (2.0 * math.pi / period)
    p2 = phi * phi
    cosv = 1.0 + p2 * (-1.0 / 2 + p2 * (1.0 / 24 + p2 * (-1.0 / 720 + p2 * (1.0 / 40320 - p2 / 3628800))))
    sinv = phi * (1.0 + p2 * (-1.0 / 6 + p2 * (1.0 / 120 + p2 * (-1.0 / 5040 + p2 / 362880))))
    return jnp.where(q == 0, cosv, jnp.where(q == 1, -sinv, jnp.where(q == 2, -cosv, sinv)))


def _spec_index(r, L):
    blk = r >> 8
    within = r & (2 * SPEC_BLOCK - 1)
    return blk * SPEC_BLOCK + (within & (SPEC_BLOCK - 1)), within >= SPEC_BLOCK


def _dft_body(o_ref, *, L, mode):
    log2n = int(math.log2(2 * L))
    shape = o_ref.shape
    i0 = pl.program_id(0) * shape[0]
    j0 = pl.program_id(1) * shape[1]
    rows = lax.broadcasted_iota(jnp.int32, shape, 0) + i0
    cols = lax.broadcasted_iota(jnp.int32, shape, 1) + j0
    if mode == "inv":
        r, n = cols, rows
    else:
        r, n = rows, cols
    k, im = _spec_index(r, L)
    back = n >= L
    n = jnp.where(back, n - L, n)
    phase = jnp.where(im, jnp.where(back, -(L // 2), L // 2), 0)
    val = _cos_turns((k * n + phase) & (2 * L - 1), log2n)
    nyq = jnp.where((n & 1) == 0, 1.0, -1.0)
    val = jnp.where(im & (k == 0), nyq, val)
    if mode == "inv":
        val = val * jnp.where(k == 0, 1.0 / (2 * L), 2.0 / (2 * L))
    o_ref[...] = val.astype(o_ref.dtype)


def dft_table(L, mode):
    assert L & (L - 1) == 0 and L >= 2 * SPEC_BLOCK
    shape = {"fwd": (2 * L, L), "filt": (2 * L, 2 * L), "inv": (L, 2 * L)}[mode]
    tr, tc = _tile(shape[0], 512), _tile(shape[1], 1024)
    return pl.pallas_call(
        functools.partial(_dft_body, L=L, mode=mode),
        out_shape=jax.ShapeDtypeStruct(shape, BF16),
        grid=(shape[0] // tr, shape[1] // tc),
        out_specs=pl.BlockSpec((tr, tc), lambda i, j: (i, j)),
        compiler_params=_cp("parallel", "parallel"),
        name="dft_table_" + mode,
    )()


def _spectral_epilogue(acc, kf_ref, row_tile):
    tm, tn = acc.shape
    nb = tm // (2 * SPEC_BLOCK)
    z = acc.reshape(nb, 2, SPEC_BLOCK, tn)
    kf = kf_ref[...].reshape(nb, 2, SPEC_BLOCK, tn)
    zr, zi, kr, ki = z[:, 0], z[:, 1], kf[:, 0], kf[:, 1]
    first = row_tile == 0
    blk = lax.broadcasted_iota(jnp.int32, zr.shape, 0)
    row = lax.broadcasted_iota(jnp.int32, zr.shape, 1)
    dc = first & (blk == 0) & (row == 0)
    pr = zr * kr - jnp.where(dc, 0.0, zi * ki)
    pi = jnp.where(dc, zi * ki, zr * ki + zi * kr)
    return jnp.stack([pr, pi], axis=1).reshape(tm, tn)


def hyena_stream(u, row0, nb, L, kf, f_tab, g_tab, hy_bias, c, fill=None):
    out_rows = u.shape[0]
    tn = _tile(c, 2048)
    cb = c // tn
    tm_f = _tile(2 * L, 512)
    tk_f = _tile(L, 2048)
    tm_i = _tile(L, 512)
    tk_i = _tile(2 * L, 2048)
    z, z_rows, z_col = u, row0, 2 * cb
    for o in range(2):
        zrb = z_rows // tk_f
        p = matmul(
            f_tab, z, grid=(nb, 2 * L // tm_f, cb, L // tk_f), tm=tm_f, tn=tn,
            a_spec=pl.BlockSpec((tm_f, tk_f), lambda bt, i, j, k: (i, k)),
            b_spec=pl.BlockSpec((tk_f, tn), lambda bt, i, j, k, zrb=zrb, zc=z_col:
                                (zrb + bt * (L // tk_f) + k, zc + j)),
            o_spec=pl.BlockSpec((None, tm_f, tn), lambda bt, i, j, k: (bt, i, j)),
            out_shape=(nb, 2 * L, c), out_dtype=BF16,
            extra=(kf,), extra_specs=(pl.BlockSpec((tm_f, tn), lambda bt, i, j, k, o=o: (i, o * cb + j)),),
            epilogue=_spectral_epilogue, name="hyena_fwd")
        zrb_i = z_rows // tm_i
        urb_i = row0 // tm_i
        last = o == 1
        orb = urb_i if last else 0
        z_new = matmul(
            g_tab, p, grid=(nb, L // tm_i, cb, 2 * L // tk_i), tm=tm_i, tn=tn,
            a_spec=pl.BlockSpec((tm_i, tk_i), lambda bt, i, j, k: (i, k)),
            b_spec=pl.BlockSpec((None, tk_i, tn), lambda bt, i, j, k: (bt, k, j)),
            o_spec=pl.BlockSpec((tm_i, tn), lambda bt, i, j, k, orb=orb: (orb + bt * (L // tm_i) + i, j)),
            out_shape=(out_rows if last else nb * L, c), out_dtype=BF16, fill=fill if last else None,
            extra=(u, z, hy_bias.reshape(2, 1, c)),
            extra_specs=(pl.BlockSpec((tm_i, tn), lambda bt, i, j, k, o=o:
                                      (urb_i + bt * (L // tm_i) + i, o * cb + j)),
                         pl.BlockSpec((tm_i, tn), lambda bt, i, j, k, zrb_i=zrb_i, zc=z_col:
                                      (zrb_i + bt * (L // tm_i) + i, zc + j)),
                         pl.BlockSpec((None, 1, tn), lambda bt, i, j, k, o=o: (o, 0, j))),
            epilogue=lambda acc, g_ref, z_ref, b_ref, row_tile:
                g_ref[...].astype(F32) * (acc + b_ref[...] * z_ref[...].astype(F32)),
            name="hyena_inv")
        z, z_rows, z_col = z_new, 0, 0
    return z


def hyena_spectrum(L, filt, a_tab):
    n = filt.shape[2]
    b = filt.reshape(2 * L, n)
    tm, tn, tk = _tile(2 * L, 512), _tile(n, 512), _tile(2 * L, 2048)
    return matmul(
        a_tab, b, grid=(1, 2 * L // tm, n // tn, 2 * L // tk), tm=tm, tn=tn,
        a_spec=pl.BlockSpec((tm, tk), lambda bt, i, j, k: (i, k)),
        b_spec=pl.BlockSpec((tk, tn), lambda bt, i, j, k: (k, j)),
        o_spec=pl.BlockSpec((tm, tn), lambda bt, i, j, k: (i, j)),
        out_shape=(2 * L, n), out_dtype=F32, name="hyena_spectrum")


def _fft_tab_body(o_ref, *, kind, n1, n2, k1_len):
    shape = o_ref.shape
    r = lax.broadcasted_iota(jnp.int32, shape, 0) + pl.program_id(0) * shape[0]
    c = lax.broadcasted_iota(jnp.int32, shape, 1)
    n = n1 * n2
    lg = lambda v: int(math.log2(v))
    s8 = BF16_SUBLANES
    if kind in ("t1f", "t1i"):
        if kind == "t1f":
            grp, kp, s, n1i, s2 = r >> lg(2 * n1 * s8), (r >> lg(s8)) & (2 * n1 - 1), r & (s8 - 1), c >> lg(s8), c & (s8 - 1)
        else:
            grp, n1i, s, kp, s2 = r >> lg(k1_len * s8), (r >> lg(s8)) & (k1_len - 1), r & (s8 - 1), c >> lg(s8), c & (s8 - 1)
        t = n2 * n1i + grp * s8 + s
        val = _cos_turns(((kp & (n1 - 1)) * t + (kp >> lg(n1)) * (n // 4)) & (n - 1), lg(n))
        val = jnp.where(s == s2, val * (1.0 / n if kind == "t1i" else 1.0), 0.0)
    else:
        rp = r >= n2
        cblk = c >> lg(n2)
        cp = (cblk & 1) == 1
        sin_sign = -1 if kind == "f2i" else 1
        quarter = jnp.where(rp == cp, 0, jnp.where(cp, -sin_sign, sin_sign))
        val = _cos_turns(((r & (n2 - 1)) * (c & (n2 - 1)) + quarter * (n2 // 4)) & (n2 - 1), lg(n2))
        if kind == "f2filt":
            val = jnp.where((cblk >= 2) & rp, -val, val)
    o_ref[...] = val.astype(o_ref.dtype)


def fft_table(kind, n1, n2):
    k1_len = n1 // 2
    s8 = BF16_SUBLANES
    groups = n2 // s8
    shape = {"t1f": (groups * 2 * n1 * s8, k1_len * s8), "t1i": (groups * k1_len * s8, 2 * n1 * s8),
             "f2f": (2 * n2, 2 * n2), "f2i": (2 * n2, 2 * n2), "f2filt": (2 * n2, 4 * n2)}[kind]
    tr = _tile(shape[0], 1024)
    out = pl.pallas_call(
        functools.partial(_fft_tab_body, kind=kind, n1=n1, n2=n2, k1_len=k1_len),
        out_shape=jax.ShapeDtypeStruct(shape, BF16),
        grid=(shape[0] // tr,),
        out_specs=pl.BlockSpec((tr, shape[1]), lambda i: (i, 0)),
        compiler_params=_cp("parallel"),
        name="fft_table_" + kind,
    )()
    if kind in ("t1f", "t1i"):
        return out.reshape(groups, shape[0] // groups, shape[1])
    return out


FFT_K1_PER_STEP = 4


def _fft_mid_body(f_ref, *refs, mode, n2):
    f = f_ref[...]
    if mode == "inv":
        p_ref, o_ref = refs
        for k in range(FFT_K1_PER_STEP):
            res = jnp.dot(f, p_ref[k * 2 * n2:(k + 1) * 2 * n2, :], preferred_element_type=F32)
            o_ref[0, k * n2:(k + 1) * n2, :] = res[:n2].astype(o_ref.dtype)
            o_ref[1, k * n2:(k + 1) * n2, :] = res[n2:].astype(o_ref.dtype)
        return
    if mode == "fwd":
        y_ref, kf_ref, o_ref = refs
        srcs = (y_ref,)
    else:
        y0_ref, y1_ref, o_ref = refs
        srcs = (y0_ref, y1_ref)
    for k in range(FFT_K1_PER_STEP):
        rows = slice(k * n2, (k + 1) * n2)
        stack = jnp.concatenate([s[part, rows, :] for s in srcs for part in range(2)], axis=0)
        acc = jnp.dot(f, stack, preferred_element_type=F32)
        out_rows = slice(k * 2 * n2, (k + 1) * 2 * n2)
        if mode == "fwd":
            zr, zi = acc[:n2], acc[n2:]
            kr = kf_ref[k * 2 * n2:k * 2 * n2 + n2, :].astype(F32)
            ki = kf_ref[k * 2 * n2 + n2:(k + 1) * 2 * n2, :].astype(F32)
            acc = jnp.concatenate([zr * kr - zi * ki, zr * ki + zi * kr], axis=0)
        o_ref[out_rows, :] = acc.astype(o_ref.dtype)


def fft_mid(mode, f_tab, src, n1, n2, *, kf=None, kf_col0=0, out_dtype=BF16, name):
    kb = FFT_K1_PER_STEP
    c = src.shape[-1]
    tn = _tile(c, 1024)
    cb = c // tn
    nb = 1 if mode == "filt" else src.shape[0]
    y_block = (None, 2, kb * n2, tn)
    p_block = (None, kb * 2 * n2, tn)
    f_spec = pl.BlockSpec(f_tab.shape, lambda i, b, j: (0, 0))
    if mode == "fwd":
        in_specs = [f_spec, pl.BlockSpec(y_block, lambda i, b, j: (b, 0, i, j)),
                    pl.BlockSpec((kb * 2 * n2, tn), lambda i, b, j: (i, kf_col0 // tn + j))]
        args = (f_tab, src, kf)
        out_shape, out_spec = (nb, n1 * 2 * n2, c), pl.BlockSpec(p_block, lambda i, b, j: (b, i, j))
    elif mode == "filt":
        in_specs = [f_spec, pl.BlockSpec(y_block, lambda i, b, j: (0, 0, i, j)),
                    pl.BlockSpec(y_block, lambda i, b, j: (1, 0, i, j))]
        args = (f_tab, src, src)
        out_shape, out_spec = (n1 * 2 * n2, c), pl.BlockSpec((kb * 2 * n2, tn), lambda i, b, j: (i, j))
    else:
        in_specs = [f_spec, pl.BlockSpec(p_block, lambda i, b, j: (b, i, j))]
        args = (f_tab, src)
        out_shape, out_spec = (nb, 2, n1 * n2, c), pl.BlockSpec(y_block, lambda i, b, j: (b, 0, i, j))
    return pl.pallas_call(
        functools.partial(_fft_mid_body, mode=mode, n2=n2),
        out_shape=jax.ShapeDtypeStruct(out_shape, out_dtype),
        grid=(n1 // kb, nb, cb),
        in_specs=in_specs,
        out_specs=out_spec,
        compiler_params=_cp("parallel", "parallel", "parallel"),
        name=name,
    )(*args)


def hyena_spectrum_2level(L, filt, t1f, f2filt):
    n2 = SPEC_BLOCK
    n1 = 2 * L // n2
    k1 = n1 // 2
    n = filt.shape[2]
    tn = _tile(n, 1024)
    nj = n // tn
    sl = BF16_SUBLANES
    y1 = matmul(
        t1f, filt.reshape(2 * k1, n2, n), grid=(2, n2 // sl, nj, 1), tm=2 * n1 * sl, tn=tn,
        a_spec=pl.BlockSpec((None, 2 * n1 * sl, k1 * sl), lambda bt, i, j, k: (i, 0, 0)),
        b_spec=pl.BlockSpec((k1, sl, tn), lambda bt, i, j, k: (bt, i, j)),
        o_spec=pl.BlockSpec((None, 2 * n1, sl, tn), lambda bt, i, j, k: (bt, 0, i, j)),
        out_shape=(2, 2 * n1, n2, n), out_dtype=BF16, name="hyena_filt_fft1")
    return fft_mid("filt", f2filt, y1.reshape(2, 2, n1 * n2, n), n1, n2, name="hyena_filt_fft2")


def hyena_stream_2level(u, row0, nb, L, kf, tabs, hy_bias, c, fill=None):
    t1f, f2f, f2i, t1i = tabs
    n2 = SPEC_BLOCK
    n1 = 2 * L // n2
    k1 = n1 // 2
    sl = BF16_SUBLANES
    ng = n2 // sl
    t1n = _tile(c, 1024)
    c1b = c // t1n
    tn = _tile(c, 2048)
    cb = c // tn
    t, u_w = u.shape
    assert row0 % L == 0 and u_w % tn == 0
    uv = u.reshape(t // n2, n2, u_w)
    urow = row0 // L
    zv, zr, zc = uv, urow, 2 * c // t1n
    bias3 = hy_bias.reshape(2, 1, c)
    for o in range(2):
        y1 = matmul(
            t1f, zv, grid=(nb, ng, c1b, 1), tm=2 * n1 * sl, tn=t1n,
            a_spec=pl.BlockSpec((None, 2 * n1 * sl, k1 * sl), lambda bt, i, j, k: (i, 0, 0)),
            b_spec=pl.BlockSpec((k1, sl, t1n), lambda bt, i, j, k, zr=zr, zc=zc: (zr + bt, i, zc + j)),
            o_spec=pl.BlockSpec((None, 2 * n1, sl, t1n), lambda bt, i, j, k: (bt, 0, i, j)),
            out_shape=(nb, 2 * n1, n2, c), out_dtype=BF16, name="hyena_fft1")
        p = fft_mid("fwd", f2f, y1.reshape(nb, 2, n1 * n2, c), n1, n2, kf=kf, kf_col0=o * c, name="hyena_fft2")
        u1 = fft_mid("inv", f2i, p, n1, n2, name="hyena_ifft1")
        last = o == 1
        orow = urow if last else 0

        def gate_epilogue(acc, g_ref, z_ref, b_ref, row_tile):
            g = g_ref[...].reshape(acc.shape).astype(F32)
            return g * (acc + b_ref[...] * z_ref[...].reshape(acc.shape).astype(F32))

        z = matmul(
            t1i, u1.reshape(nb, 2 * n1, n2, c), grid=(nb, ng, c1b, 1), tm=k1 * sl, tn=t1n,
            a_spec=pl.BlockSpec((None, k1 * sl, 2 * n1 * sl), lambda bt, i, j, k: (i, 0, 0)),
            b_spec=pl.BlockSpec((None, 2 * n1, sl, t1n), lambda bt, i, j, k: (bt, 0, i, j)),
            o_spec=pl.BlockSpec((k1, sl, t1n), lambda bt, i, j, k, orow=orow: (orow + bt, i, j)),
            out_shape=(t // n2 if last else nb * k1, n2, c), out_dtype=BF16,
            fill=None if (fill is None or not last) else fill.reshape(t // n2, n2, c),
            extra=(uv, zv, bias3),
            extra_specs=(pl.BlockSpec((k1, sl, t1n), lambda bt, i, j, k, o=o: (urow + bt, i, o * c1b + j)),
                         pl.BlockSpec((k1, sl, t1n), lambda bt, i, j, k, zr=zr, zc=zc: (zr + bt, i, zc + j)),
                         pl.BlockSpec((None, 1, t1n), lambda bt, i, j, k, o=o: (o, 0, j))),
            epilogue=gate_epilogue, name="hyena_ifft2")
        zv, zr, zc = z, 0, 0
    return zv.reshape(t, c)


SSD_CUM_TERMS = 3


def _ssd_prep_body(dc_ref, dr_ref, biasc_ref, biasr_ref, ac_ref, ar_ref, fac_ref, cumr_ref, *, heads):
    q = dc_ref.shape[0]
    dn = (((1,), (0,)), ((), ()))
    li = lax.broadcasted_iota(jnp.int32, (q, q), 0)
    si = lax.broadcasted_iota(jnp.int32, (q, q), 1)
    lower = (li >= si).astype(F32)
    upper = (li <= si).astype(F32)
    dt_c = _softplus(dc_ref[...] + biasc_ref[...])
    da_c = dt_c * ac_ref[...]
    da_r = _softplus(dr_ref[...] + biasr_ref[...]) * ar_ref[...]
    pre_c = lax.dot_general(lower, da_c, dn, precision=HI, preferred_element_type=F32)
    suf_c = lax.dot_general(upper, da_c, dn, precision=HI, preferred_element_type=F32)
    pre_r = lax.dot_general(da_r, upper, dn, precision=HI, preferred_element_type=F32)
    suf_r = lax.dot_general(da_r, lower, dn, precision=HI, preferred_element_type=F32)
    cum_c = jnp.where(lax.broadcasted_iota(jnp.int32, pre_c.shape, 1) < heads, pre_c, suf_c)
    terms, rest = [], cum_c
    for _ in range(SSD_CUM_TERMS):
        term = rest.astype(BF16)
        rest = rest - term.astype(F32)
        terms.append(term)
    fac_ref[...] = jnp.concatenate(terms + [dt_c.astype(BF16)], axis=1)
    cumr_ref[...] = jnp.where(lax.broadcasted_iota(jnp.int32, pre_r.shape, 0) < heads, pre_r, suf_r)


def ssd_prep(dt_raw, dt_raw_t, dt_bias, a_log):
    t, h2 = dt_raw.shape
    q = SSD_CHUNK
    neg_a = -jnp.exp(a_log.astype(F32)).reshape(1, h2)
    bias = dt_bias.astype(F32).reshape(1, h2)
    const = lambda shape: pl.BlockSpec(shape, lambda i: (0,) * len(shape))
    col = pl.BlockSpec((q, h2), lambda i: (i, 0))
    row = pl.BlockSpec((h2, q), lambda i: (0, i))
    nf = (SSD_CUM_TERMS + 1) * h2
    return pl.pallas_call(
        functools.partial(_ssd_prep_body, heads=h2 // 2),
        out_shape=(jax.ShapeDtypeStruct((t, nf), BF16), jax.ShapeDtypeStruct((h2, t), F32)),
        grid=(t // q,),
        in_specs=[col, row, const((1, h2)), const((h2, 1)), const((1, h2)), const((h2, 1))],
        out_specs=(pl.BlockSpec((q, nf), lambda i: (i, 0)), row),
        compiler_params=_cp("parallel"),
        name="ssd_prep",
    )(dt_raw, dt_raw_t, bias, bias.reshape(h2, 1), neg_a, neg_a.reshape(h2, 1))


def ssd_selectors(h2, j, p, q):
    w_sum, w_wide = j * p, j * q
    k = (SSD_CUM_TERMS + 1) * h2
    rows = jnp.arange(k, dtype=jnp.int32)[None, :, None]
    cols = jnp.arange(w_sum + w_wide + j * p, dtype=jnp.int32)[None, None, :]
    head0 = j * jnp.arange(h2 // j, dtype=jnp.int32)[:, None, None]
    col_head = jnp.where(cols < w_sum, cols // p, jnp.where(cols < w_sum + w_wide, (cols - w_sum) // q,
                                                              (cols - w_sum - w_wide) // p))
    is_dt_row = rows // h2 == SSD_CUM_TERMS
    is_dt_col = cols >= w_sum + w_wide
    sel = (rows % h2 == head0 + col_head) & (is_dt_row == is_dt_col)
    return sel.astype(BF16)


def _ssd_dir(xs, bt, cm, fac, sel, diag_mask, cumr_ref, state_ref, head0, *, reverse, heads_per_group):
    q = xs.shape[0]
    p = SSD_HEAD_DIM
    j = heads_per_group
    spread = jnp.dot(fac, sel, preferred_element_type=F32)
    cc, cc_wide, dtv = spread[:, :j * p], spread[:, j * p:j * p + j * q], spread[:, j * p + j * q:]
    total = cc[0:1, :] if reverse else cc[q - 1:q, :]
    xd = xs.astype(F32) * dtv
    xd_b = xd.astype(BF16)
    li = lax.broadcasted_iota(jnp.int32, (q, q), 0)
    si = lax.broadcasted_iota(jnp.int32, (q, q), 1)
    mask = (li <= si) if reverse else (li >= si)
    cb = jnp.dot(cm, bt, preferred_element_type=F32)
    weights = []
    for jh in range(j):
        cr = cumr_ref[pl.ds(head0 + jh, 1), :]
        decay = jnp.exp(jnp.where(mask, cc_wide[:, jh * q:(jh + 1) * q] - cr, -jnp.inf))
        weights.append((cb * decay).astype(BF16))
    xd_diag = jnp.concatenate([xd_b] * j, axis=0) * diag_mask
    y = jnp.dot(jnp.concatenate(weights, axis=1), xd_diag, preferred_element_type=F32)
    s_prev = state_ref[...]
    y = y + jnp.exp(cc) * jnp.dot(cm, s_prev.astype(BF16), preferred_element_type=F32)
    st = jnp.dot(bt, (xd * jnp.exp(total - cc)).astype(BF16), preferred_element_type=F32)
    state_ref[...] = jnp.exp(total) * s_prev + st
    return y


def _ssd_body(tab_ref, xf_ref, btf_ref, cf_ref, facf_ref, crf_ref, xb_ref, btb_ref, cb_ref, facb_ref, crb_ref,
              self_ref, selb_ref, mask_ref, initf_ref, initb_ref, yf_ref, yb_ref, sf_ref, sb_ref, stf_ref, stb_ref,
              *, heads_per_group, heads):
    g = pl.program_id(0)
    item = pl.program_id(1)

    @pl.when(tab_ref[_SSD_FIRST, item] == 1)
    def _():
        stf_ref[...] = initf_ref[...]
        stb_ref[...] = initb_ref[...]

    jp = heads_per_group * SSD_HEAD_DIM
    n = SSD_STATE
    for gi in range(SSD_GROUPS_PER_STEP):
        head0 = (g * SSD_GROUPS_PER_STEP + gi) * heads_per_group
        cols, srow = slice(gi * jp, (gi + 1) * jp), slice(gi * n, (gi + 1) * n)
        yf_ref[:, cols] = _ssd_dir(xf_ref[:, cols], btf_ref[srow, :], cf_ref[:, srow], facf_ref[...], self_ref[gi],
                                   mask_ref[...], crf_ref, stf_ref.at[gi], head0, reverse=False,
                                   heads_per_group=heads_per_group)
        yb_ref[:, cols] = _ssd_dir(xb_ref[:, cols], btb_ref[srow, :], cb_ref[:, srow], facb_ref[...], selb_ref[gi],
                                   mask_ref[...], crb_ref, stb_ref.at[gi], heads + head0, reverse=True,
                                   heads_per_group=heads_per_group)

    @pl.when(tab_ref[_SSD_LAST, item] == 1)
    def _():
        sf_ref[...] = stf_ref[...]
        sb_ref[...] = stb_ref[...]


_SSD_FWD, _SSD_BWD, _SSD_SEQ, _SSD_FIRST, _SSD_LAST = range(5)
SSD_GROUPS_PER_STEP = 4


def ssd_scan(u, bt_all, fac, cum_r, init_f, init_b, *, seq_lens, col0, inner):
    q, p, n, gq = SSD_CHUNK, SSD_HEAD_DIM, SSD_STATE, SSD_GROUPS
    heads = inner // p
    j = heads // gq
    jp = j * p
    xcb = col0 // jp
    ccb = (col0 + inner + gq * n) // n
    rows, base = [], 0
    for s, length in enumerate(seq_lens):
        nc = length // q
        rows += [(base + c, base + nc - 1 - c, s, int(c == 0), int(c == nc - 1)) for c in range(nc)]
        base += nc
    table = jnp.array(rows, jnp.int32).T
    nseq, n_items = len(seq_lens), len(rows)

    gs = SSD_GROUPS_PER_STEP
    assert xcb % gs == 0 and ccb % gs == 0 and gq % gs == 0

    def specs(which):
        return [pl.BlockSpec((q, gs * jp), lambda g, it, tab: (tab[which, it], xcb // gs + g)),
                pl.BlockSpec((gs * n, q), lambda g, it, tab: (g, tab[which, it])),
                pl.BlockSpec((q, gs * n), lambda g, it, tab: (tab[which, it], ccb // gs + g)),
                pl.BlockSpec((q, fac.shape[1]), lambda g, it, tab: (tab[which, it], 0)),
                pl.BlockSpec((2 * heads, q), lambda g, it, tab: (0, tab[which, it]))]

    st_spec = pl.BlockSpec((None, gs, n, jp), lambda g, it, tab: (tab[_SSD_SEQ, it], g, 0, 0))
    st_shape = jax.ShapeDtypeStruct((nseq, gq, n, jp), F32)
    y_shape = jax.ShapeDtypeStruct((u.shape[0], inner), F32)
    sel_all = ssd_selectors(2 * heads, j, p, q)
    sel_block = (gs,) + sel_all.shape[1:]
    diag_mask = (jnp.arange(j * q)[:, None] // q == jnp.arange(jp)[None, :] // p).astype(BF16)
    grid_spec = pltpu.PrefetchScalarGridSpec(
        num_scalar_prefetch=1,
        grid=(gq // gs, n_items),
        in_specs=[*specs(_SSD_FWD), *specs(_SSD_BWD),
                  pl.BlockSpec(sel_block, lambda g, it, tab: (g, 0, 0)),
                  pl.BlockSpec(sel_block, lambda g, it, tab: (gq // gs + g, 0, 0)),
                  pl.BlockSpec(diag_mask.shape, lambda g, it, tab: (0, 0)),
                  st_spec, st_spec],
        out_specs=(pl.BlockSpec((q, gs * jp), lambda g, it, tab: (tab[_SSD_FWD, it], g)),
                   pl.BlockSpec((q, gs * jp), lambda g, it, tab: (tab[_SSD_BWD, it], g)), st_spec, st_spec),
        scratch_shapes=[pltpu.VMEM((gs, n, jp), F32), pltpu.VMEM((gs, n, jp), F32)])
    return pl.pallas_call(
        functools.partial(_ssd_body, heads_per_group=j, heads=heads),
        out_shape=(y_shape, y_shape, st_shape, st_shape),
        grid_spec=grid_spec,
        compiler_params=_cp("parallel", "arbitrary"),
        name="ssd_scan",
    )(table, u, bt_all, u, fac, cum_r, u, bt_all, u, fac, cum_r, sel_all, sel_all, diag_mask, init_f, init_b)


def _state_to_kernel_layout(s, groups):
    b, h, p, n = s.shape
    return s.reshape(b, groups, h // groups, p, n).transpose(0, 1, 4, 2, 3).reshape(b, groups, n, (h // groups) * p)


def _state_from_kernel_layout(s, head_dim):
    b, g, n, jp = s.shape
    j = jp // head_dim
    return s.reshape(b, g, n, j, head_dim).transpose(0, 1, 3, 4, 2).reshape(b, g * j, head_dim, n)


def _ssd_finish_body(yf_ref, yb_ref, xs_ref, z_ref, d_ref, w_ref, o_ref):
    y = yf_ref[...] + yb_ref[...] + xs_ref[...].astype(F32) * d_ref[...]
    y = y * _silu(z_ref[...].astype(F32))
    y = y * lax.rsqrt(jnp.mean(y * y, axis=-1, keepdims=True) + RMS_EPS)
    o_ref[...] = (y * w_ref[...]).astype(o_ref.dtype)


def ssd_finish(yf, yb, u, proj, d_full, norm_w, *, xs_col, z_col, inner):
    t = yf.shape[0]
    tr = ROW_TILE
    return pl.pallas_call(
        _ssd_finish_body,
        out_shape=jax.ShapeDtypeStruct((t, inner), BF16),
        grid=(t // tr,),
        in_specs=[pl.BlockSpec((tr, inner), lambda i: (i, 0)),
                  pl.BlockSpec((tr, inner), lambda i: (i, 0)),
                  pl.BlockSpec((tr, inner), lambda i: (i, xs_col // inner)),
                  pl.BlockSpec((tr, inner), lambda i: (i, z_col // inner)),
                  pl.BlockSpec((1, inner), lambda i: (0, 0)),
                  pl.BlockSpec((1, inner), lambda i: (0, 0))],
        out_specs=pl.BlockSpec((tr, inner), lambda i: (i, 0)),
        compiler_params=_cp("parallel"),
        name="ssd_finish",
    )(yf, yb, u, proj, d_full.reshape(1, inner), norm_w.reshape(1, inner))


KV_PER_STEP = LANES // HEAD_DIM


_NT = (((1,), (1,)), ((), ()))


def _attend(scores, values, sink):
    m = sink
    for s in scores:
        m = jnp.maximum(m, jnp.max(s, axis=-1, keepdims=True))
    denom = jnp.exp(sink - m)
    acc = None
    for s, v in zip(scores, values):
        e = jnp.exp(s - m)
        denom = denom + jnp.sum(e, axis=-1, keepdims=True)
        part = jnp.dot(e.astype(BF16), v, preferred_element_type=F32)
        acc = part if acc is None else acc + part
    return acc * (1.0 / denom)


def _stack_heads(q, first_head, n, rows):
    return jnp.concatenate([q[:, (first_head + g) * HEAD_DIM:(first_head + g + 1) * HEAD_DIM] for g in range(n)],
                           axis=0)


def _stacked_sinks(sink_ref, first_head, n, rows):
    row = lax.broadcasted_iota(jnp.int32, (n * rows, 1), 0)
    col = jnp.full((n * rows, 1), sink_ref[first_head + n - 1], F32)
    for g in reversed(range(n - 1)):
        col = jnp.where(row < (g + 1) * rows, sink_ref[first_head + g], col)
    return col


def _ctx_attn_body(sink_ref, q_ref, k_ref, v_ref, fill_ref, o_ref, *, q_per_kv, scale):
    del fill_ref
    hb = pl.program_id(1)
    rows = q_ref.shape[0]
    q_all = q_ref[...] * scale
    outs = []
    for kv in range(KV_PER_STEP):
        k = k_ref[:, kv * HEAD_DIM:(kv + 1) * HEAD_DIM].astype(BF16)
        v = v_ref[:, kv * HEAD_DIM:(kv + 1) * HEAD_DIM].astype(BF16)
        qs = _stack_heads(q_all, kv * q_per_kv, q_per_kv, rows).astype(BF16)
        s = lax.dot_general(qs, k, _NT, preferred_element_type=F32)
        sink = _stacked_sinks(sink_ref, (hb * KV_PER_STEP + kv) * q_per_kv, q_per_kv, rows)
        o = _attend([s], [v], sink)
        outs += [o[g * rows:(g + 1) * rows] for g in range(q_per_kv)]
    o_ref[...] = jnp.concatenate(outs, axis=1).astype(o_ref.dtype)


def context_attention(qkv, sink, *, nseq, L, n_heads):
    q_per_kv = n_heads // N_KV_HEADS
    qw = KV_PER_STEP * q_per_kv * HEAD_DIM
    nhb = N_KV_HEADS // KV_PER_STEP
    kcb = n_heads * HEAD_DIM // LANES
    vcb = kcb + N_KV_HEADS * HEAD_DIM // LANES
    return pl.pallas_call(
        functools.partial(_ctx_attn_body, q_per_kv=q_per_kv, scale=HEAD_DIM ** -0.5),
        out_shape=jax.ShapeDtypeStruct((qkv.shape[0], n_heads * HEAD_DIM), BF16),
        grid=(nseq, nhb),
        in_specs=[pl.BlockSpec(memory_space=pltpu.SMEM),
                  pl.BlockSpec((L, qw), lambda b, h: (b, h)),
                  pl.BlockSpec((L, LANES), lambda b, h: (b, kcb + h)),
                  pl.BlockSpec((L, LANES), lambda b, h: (b, vcb + h)),
                  pl.BlockSpec(memory_space=pl.ANY)],
        out_specs=pl.BlockSpec((L, qw), lambda b, h: (b, h)),
        input_output_aliases={4: 0},
        compiler_params=_cp("parallel", "parallel"),
        name="context_attention",
    )(sink, qkv, qkv, qkv, jnp.zeros((qkv.shape[0], n_heads * HEAD_DIM), BF16))


def _rope(x, cos, sin):
    outs = []
    lane = lax.broadcasted_iota(jnp.int32, cos.shape, 1)
    first = (lane % (HEAD_DIM // 2)) < (HEAD_DIM // 4)
    for cgrp in range(x.shape[1] // LANES):
        xc = x[:, cgrp * LANES:(cgrp + 1) * LANES]
        partner = jnp.where(first, pltpu.roll(xc, LANES - HEAD_DIM // 4, 1), pltpu.roll(xc, HEAD_DIM // 4, 1))
        outs.append(xc * cos + partner * sin)
    return outs[0] if len(outs) == 1 else jnp.concatenate(outs, axis=1)


def _lat_attn_body(sink_ref, q_ref, k_ref, v_ref, kc_ref, vc_ref, cosq_ref, sinq_ref, cos_ref, sin_ref, fill_ref,
                   o_ref, k_scr, v_scr, kc_scr, vc_scr, *, q_per_kv, scale, L, blk):
    del fill_ref
    hb = pl.program_id(1)
    i = pl.program_id(2)

    @pl.when(i == 0)
    def _():
        k_scr[...] = _rope(k_ref[...], cos_ref[...], sin_ref[...]).astype(BF16)
        v_scr[...] = v_ref[...].astype(BF16)
        kc_scr[...] = kc_ref[...].astype(BF16)
        vc_scr[...] = vc_ref[...].astype(BF16)

    n_loc = 3 * blk
    start = jnp.clip((i - 1) * blk, 0, L - n_loc)
    start = pl.multiple_of(start, blk)
    q_all = _rope(q_ref[...], cosq_ref[...], sinq_ref[...]) * scale
    k_loc = k_scr[pl.ds(start, n_loc), :]
    v_loc = v_scr[pl.ds(start, n_loc), :]
    rows = q_per_kv * blk
    qpos = i * blk + (lax.broadcasted_iota(jnp.int32, (rows, n_loc), 0) & (blk - 1))
    kpos = start + lax.broadcasted_iota(jnp.int32, (rows, n_loc), 1)
    ok = jnp.abs(qpos - kpos) <= WINDOW
    outs = []
    for kv in range(KV_PER_STEP):
        sl = slice(kv * HEAD_DIM, (kv + 1) * HEAD_DIM)
        qs = _stack_heads(q_all, kv * q_per_kv, q_per_kv, blk).astype(BF16)
        s_loc = jnp.where(ok, lax.dot_general(qs, k_loc[:, sl], _NT, preferred_element_type=F32), -jnp.inf)
        s_ctx = lax.dot_general(qs, kc_scr[:, sl], _NT, preferred_element_type=F32)
        sink = _stacked_sinks(sink_ref, (hb * KV_PER_STEP + kv) * q_per_kv, q_per_kv, blk)
        o = _attend([s_loc, s_ctx], [v_loc[:, sl], vc_scr[:, sl]], sink)
        outs += [o[g * blk:(g + 1) * blk] for g in range(q_per_kv)]
    o_ref[...] = jnp.concatenate(outs, axis=1).astype(o_ref.dtype)


def rope_tables(L):
    rows = L // GRID_W
    row = jnp.repeat(jnp.arange(rows, dtype=F32), GRID_W)
    col = jnp.tile(jnp.arange(GRID_W, dtype=F32), rows)
    quarter = HEAD_DIM // 4
    inv = ROPE_BASE ** (-jnp.arange(quarter, dtype=F32) / quarter)
    ang_r = row[:, None] * inv[None, :]
    ang_c = col[:, None] * inv[None, :]
    ang = jnp.concatenate([ang_r, ang_r, ang_c, ang_c], axis=1)
    sign = jnp.tile(jnp.concatenate([-jnp.ones((quarter,), F32), jnp.ones((quarter,), F32)]), 2)
    cos = jnp.tile(jnp.cos(ang), (1, LANES // HEAD_DIM))
    sin = jnp.tile(jnp.sin(ang) * sign[None, :], (1, LANES // HEAD_DIM))
    return cos, sin


def latent_attention(qkv, k_cache, v_cache, sink, fill, *, row0, nseq, L, n_heads):
    blk = WINDOW
    q_per_kv = n_heads // N_KV_HEADS
    qw = KV_PER_STEP * q_per_kv * HEAD_DIM
    nhb = N_KV_HEADS // KV_PER_STEP
    kcb = n_heads * HEAD_DIM // LANES
    vcb = kcb + N_KV_HEADS * HEAD_DIM // LANES
    nb = L // blk
    past = k_cache.shape[1]
    cos, sin = rope_tables(L)
    return pl.pallas_call(
        functools.partial(_lat_attn_body, q_per_kv=q_per_kv, scale=HEAD_DIM ** -0.5, L=L, blk=blk),
        out_shape=jax.ShapeDtypeStruct(fill.shape, BF16),
        grid=(nseq, nhb, nb),
        in_specs=[pl.BlockSpec(memory_space=pltpu.SMEM),
                  pl.BlockSpec((blk, qw), lambda b, h, i: (row0 // blk + b * nb + i, h)),
                  pl.BlockSpec((L, LANES), lambda b, h, i: (row0 // L + b, kcb + h)),
                  pl.BlockSpec((L, LANES), lambda b, h, i: (row0 // L + b, vcb + h)),
                  pl.BlockSpec((None, past, LANES), lambda b, h, i: (b, 0, h)),
                  pl.BlockSpec((None, past, LANES), lambda b, h, i: (b, 0, h)),
                  pl.BlockSpec((blk, LANES), lambda b, h, i: (i, 0)),
                  pl.BlockSpec((blk, LANES), lambda b, h, i: (i, 0)),
                  pl.BlockSpec((L, LANES), lambda b, h, i: (0, 0)),
                  pl.BlockSpec((L, LANES), lambda b, h, i: (0, 0)),
                  pl.BlockSpec(memory_space=pl.ANY)],
        out_specs=pl.BlockSpec((blk, qw), lambda b, h, i: (row0 // blk + b * nb + i, h)),
        scratch_shapes=[pltpu.VMEM((L, LANES), BF16), pltpu.VMEM((L, LANES), BF16),
                        pltpu.VMEM((past, LANES), BF16), pltpu.VMEM((past, LANES), BF16)],
        input_output_aliases={10: 0},
        compiler_params=_cp("parallel", "parallel", "arbitrary"),
        name="latent_attention",
    )(sink, qkv, qkv, qkv, k_cache, v_cache, cos, sin, cos, sin, fill)


MOE_TILE = 256
MOE_FFN_VMEM_BYTES = 56 * 1024 * 1024


def _moe_dispatch_body(pos_ref, h_ref, g_ref, init_hbm, xs_hbm, buf, sem):
    del init_hbm
    i = pl.program_id(0)
    tr, d = h_ref.shape
    buf[:, :d] = h_ref[...]
    buf[:, d:] = g_ref[...]

    def start(r2, carry):
        for prio in range(DMA_PRIORITIES):
            r = r2 * DMA_PRIORITIES + prio
            pltpu.make_async_copy(buf.at[pl.ds(r, 1)], xs_hbm.at[pl.ds(pos_ref[i * tr + r], 1)],
                                  sem).start(priority=prio)
        return carry

    lax.fori_loop(0, tr // DMA_PRIORITIES, start, 0)

    def wait(r, carry):
        pltpu.make_async_copy(buf.at[pl.ds(r, 1)], xs_hbm.at[pl.ds(0, 1)], sem).wait()
        return carry

    lax.fori_loop(0, tr, wait, 0)


def moe_dispatch(h, gate_lanes, pos, n_slots):
    t, d = h.shape
    tr = ROW_TILE
    w = d + LANES
    grid_spec = pltpu.PrefetchScalarGridSpec(
        num_scalar_prefetch=1,
        grid=(t // tr,),
        in_specs=[pl.BlockSpec((tr, d), lambda i, p: (i, 0)),
                  pl.BlockSpec((tr, LANES), lambda i, p: (i, 0)),
                  pl.BlockSpec(memory_space=pl.ANY)],
        out_specs=pl.BlockSpec(memory_space=pl.ANY),
        scratch_shapes=[pltpu.VMEM((tr, w), F32), pltpu.SemaphoreType.DMA(())])
    return pl.pallas_call(
        _moe_dispatch_body,
        out_shape=jax.ShapeDtypeStruct((n_slots, w), F32),
        grid_spec=grid_spec,
        input_output_aliases={3: 0},
        compiler_params=_cp("arbitrary"),
        name="moe_dispatch",
    )(pos, h, gate_lanes, jnp.zeros((n_slots, w), F32))


def _moe_ffn_body(ea_ref, eb_ref, nused_ref, xs_ref, w1a_ref, w3a_ref, w2a_ref, w1b_ref, w3b_ref, w2b_ref, o_ref):
    i = pl.program_id(0)
    d = o_ref.shape[1]

    @pl.when(i < nused_ref[0])
    def _():
        x = xs_ref[:, :d].astype(BF16)

        def expert(w1_ref, w3_ref, w2_ref):
            a = jnp.dot(x, w1_ref[...], preferred_element_type=F32)
            b = jnp.dot(x, w3_ref[...], preferred_element_type=F32)
            return jnp.dot((_silu(a) * b).astype(BF16), w2_ref[...], preferred_element_type=F32)

        o_ref[...] = (xs_ref[:, d:d + 1] * expert(w1a_ref, w3a_ref, w2a_ref)
                      + xs_ref[:, d + 1:d + 2] * expert(w1b_ref, w3b_ref, w2b_ref))

    @pl.when(i >= nused_ref[0])
    def _():
        o_ref[...] = jnp.zeros_like(o_ref)


def moe_ffn(xs, tile_ea, tile_eb, n_used, w1, w3, w2, layer):
    _, _, d, de = w1.shape
    tm = MOE_TILE
    n_tiles = tile_ea.shape[0]
    up_a = pl.BlockSpec((None, None, d, de), lambda i, ea, eb, nu: (layer, ea[i], 0, 0))
    up_b = pl.BlockSpec((None, None, d, de), lambda i, ea, eb, nu: (layer, eb[i], 0, 0))
    grid_spec = pltpu.PrefetchScalarGridSpec(
        num_scalar_prefetch=3,
        grid=(n_tiles,),
        in_specs=[pl.BlockSpec((tm, d + LANES), lambda i, ea, eb, nu: (i, 0)),
                  up_a, up_a, pl.BlockSpec((None, None, de, d), lambda i, ea, eb, nu: (layer, ea[i], 0, 0)),
                  up_b, up_b, pl.BlockSpec((None, None, de, d), lambda i, ea, eb, nu: (layer, eb[i], 0, 0))],
        out_specs=pl.BlockSpec((tm, d), lambda i, ea, eb, nu: (i, 0)))
    return pl.pallas_call(
        _moe_ffn_body,
        out_shape=jax.ShapeDtypeStruct((n_tiles * tm, d), F32),
        grid_spec=grid_spec,
        compiler_params=_cp("arbitrary", vmem=MOE_FFN_VMEM_BYTES),
        name="moe_ffn",
    )(tile_ea, tile_eb, n_used, xs, w1, w3, w2, w1, w3, w2)


def _moe_combine_body(pos_ref, y_hbm, x_ref, g_ref, o_ref, ybuf, sem):
    i = pl.program_id(0)
    tr = x_ref.shape[0]

    def start(r2, carry):
        for prio in range(DMA_PRIORITIES):
            r = r2 * DMA_PRIORITIES + prio
            pltpu.make_async_copy(y_hbm.at[pl.ds(pos_ref[i * tr + r], 1)], ybuf.at[pl.ds(r, 1)],
                                  sem).start(priority=prio)
        return carry

    lax.fori_loop(0, tr // DMA_PRIORITIES, start, 0)

    def wait(r, carry):
        pltpu.make_async_copy(y_hbm.at[pl.ds(0, 1)], ybuf.at[pl.ds(r, 1)], sem).wait()
        return carry

    lax.fori_loop(0, tr, wait, 0)
    o_ref[...] = x_ref[...] + g_ref[...] * ybuf[...]


def moe_combine(x, y_sorted, pos, mod3, gate_idx, seg_of_tile):
    t, d = x.shape
    tr = ROW_TILE
    grid_spec = pltpu.PrefetchScalarGridSpec(
        num_scalar_prefetch=1,
        grid=(t // tr,),
        in_specs=[pl.BlockSpec(memory_space=pl.ANY),
                  pl.BlockSpec((tr, d), lambda i, p: (i, 0)),
                  pl.BlockSpec((None, 1, d), lambda i, p: (seg_of_tile(i) * 6 + gate_idx, 0, 0))],
        out_specs=pl.BlockSpec((tr, d), lambda i, p: (i, 0)),
        scratch_shapes=[pltpu.VMEM((tr, d), F32), pltpu.SemaphoreType.DMA(())])
    return pl.pallas_call(
        _moe_combine_body,
        out_shape=jax.ShapeDtypeStruct((t, d), F32),
        grid_spec=grid_spec,
        compiler_params=_cp("arbitrary"),
        name="moe_combine",
    )(pos, y_sorted, x, mod3)


def moe_route(logits, n_groups, n_experts):
    per = n_experts // n_groups
    assert TOP_K == 2
    g_prob = jax.nn.softmax(logits[:, :n_groups], axis=-1)
    g_top = jnp.max(g_prob, axis=-1, keepdims=True)
    g_idx = jnp.argmax(g_prob, axis=-1, keepdims=True).astype(jnp.int32)
    e_logits = logits[:, n_groups:n_groups + n_experts].reshape(-1, n_groups, per)
    g_onehot = g_idx == jnp.arange(n_groups)[None, :]
    e_in = jnp.sum(jnp.where(g_onehot[:, :, None], e_logits, 0.0), axis=1)
    e_prob = jax.nn.softmax(e_in, axis=-1)
    i0 = jnp.argmax(e_prob, axis=-1, keepdims=True).astype(jnp.int32)
    rest = jnp.where(jnp.arange(per)[None, :] == i0, -jnp.inf, e_prob)
    i1 = jnp.argmax(rest, axis=-1, keepdims=True).astype(jnp.int32)
    e_idx = jnp.concatenate([i0, i1], axis=1)
    e_top = jnp.concatenate([jnp.max(e_prob, axis=-1, keepdims=True), jnp.max(rest, axis=-1, keepdims=True)], axis=1)
    e_top = e_top / jnp.sum(e_top, axis=-1, keepdims=True)
    return g_idx * per + e_idx, g_top * e_top


def moe_plan(expert_ids, gates, n_groups, n_experts):
    assert TOP_K == 2
    t = expert_ids.shape[0]
    tm = MOE_TILE
    per = n_experts // n_groups
    n_pairs = per * (per - 1) // 2
    n_classes = n_groups * n_pairs
    n_tiles = t // tm + n_classes
    e0, e1 = expert_ids[:, 0], expert_ids[:, 1]
    lo, hi = jnp.minimum(e0, e1), jnp.maximum(e0, e1)
    g_lo = jnp.where(e0 < e1, gates[:, 0], gates[:, 1])
    g_hi = jnp.where(e0 < e1, gates[:, 1], gates[:, 0])
    a, b = lo % per, hi % per
    cls = (lo // per) * n_pairs + (a * (2 * per - a - 1)) // 2 + (b - a - 1)
    onehot = (cls[:, None] == jnp.arange(n_classes)[None, :]).astype(F32)
    blk = ROW_TILE
    inner = jnp.einsum("ij,bjk->bik", jnp.tril(jnp.ones((blk, blk), F32)), onehot.reshape(-1, blk, n_classes))
    bsum = inner[:, -1, :]
    running = (inner + (jnp.cumsum(bsum, axis=0) - bsum)[:, None, :]).reshape(-1, n_classes)
    counts = jnp.sum(bsum, axis=0).astype(jnp.int32)
    tiles_per = (counts + tm - 1) // tm
    tile_start = jnp.cumsum(tiles_per) - tiles_per
    slot = running - 1.0 + (tile_start * tm).astype(F32)[None, :]
    pos = jnp.sum(jnp.where(onehot > 0, slot, 0.0), axis=1).astype(jnp.int32)
    n_used = jnp.sum(tiles_per).astype(jnp.int32)
    tile_ids = jnp.arange(n_tiles, dtype=jnp.int32)
    tile_cls = jnp.sum((tile_ids[:, None] >= tile_start[None, :]).astype(jnp.int32), axis=1) - 1
    tile_cls = jnp.where(tile_ids < n_used, tile_cls, tile_cls[jnp.maximum(n_used - 1, 0)])
    pairs = [(pa, pb) for pa in range(per) for pb in range(pa + 1, per)]
    cls_lo = jnp.array([g * per + pa for g in range(n_groups) for pa, _ in pairs], jnp.int32)
    cls_hi = jnp.array([g * per + pb for g in range(n_groups) for _, pb in pairs], jnp.int32)
    gate_lanes = jnp.pad(jnp.stack([g_lo, g_hi], axis=1), ((0, 0), (0, LANES - 2)))
    tile_onehot = tile_cls[:, None] == jnp.arange(n_classes)[None, :]
    tile_lo = jnp.sum(jnp.where(tile_onehot, cls_lo[None, :], 0), axis=1)
    tile_hi = jnp.sum(jnp.where(tile_onehot, cls_hi[None, :], 0), axis=1)
    return pos, gate_lanes, tile_lo, tile_hi, n_used.reshape(1), n_tiles * tm


def kernel(x_prompt, x_sample, state_ssd_fwd, state_ssd_bwd, cache_attn_k, cache_attn_v, c, c_ctx, ada_w, ada_b, norm_mix, norm_ffn, norm_final, ev_w_in, ev_w_out, hy_conv_w, hy_conv_b, hy_f_w1, hy_f_b1, hy_f_freq1, hy_f_w2, hy_f_b2, hy_f_freq2, hy_f_w3, hy_bias, ssd_conv_w, ssd_conv_b, ssd_dt_bias, ssd_a_log, ssd_d, ssd_norm, at_wq, at_wk, at_wv, at_wo, at_sink, moe_w_group, moe_b_group, moe_w_expert, moe_b_expert, moe_w1, moe_w3, moe_w2):
    bc, lc, d = x_prompt.shape
    bl, ll, _ = x_sample.shape
    tc_rows, tl_rows = bc * lc, bl * ll
    t = tc_rows + tl_rows
    depth = ada_w.shape[0]
    n_seg = 1 + bl
    assert lc % ROW_TILE == 0 and ll % ROW_TILE == 0 and tc_rows % ll == 0
    seg_rows = math.gcd(tc_rows, ll)

    def seg_of_rows(r):
        return jnp.where(r < tc_rows, 0, 1 + (r - tc_rows) // ll)

    def seg_of_tile(i):
        return seg_of_rows(i * ROW_TILE)

    x = jnp.concatenate([x_prompt.reshape(tc_rows, d), x_sample.reshape(tl_rows, d)], axis=0)
    cvec = jnp.concatenate([c_ctx[None], c, jnp.zeros((SUBLANES - n_seg, d), F32)], axis=0)
    mod_all = ada_mod(cvec, ada_w, ada_b)

    tile_row = jnp.arange(t // ROW_TILE, dtype=jnp.int32) * ROW_TILE
    seq_len = jnp.where(tile_row < tc_rows, lc, ll)
    rel = jnp.where(tile_row < tc_rows, tile_row, tile_row - tc_rows)
    conv_flags = jnp.stack([(rel % seq_len == 0), ((rel + ROW_TILE) % seq_len == 0)]).astype(jnp.int32)

    w1_bf, w3_bf, w2_bf = moe_w1.astype(BF16), moe_w3.astype(BF16), moe_w2.astype(BF16)
    new_sf, new_sb, new_k, new_v = [], [], [], []
    for l in range(depth):
        i = l // 2
        mod3 = mod_all[l, :n_seg].reshape(n_seg * 6, 1, d)
        h = rms_norm(x, norm_mix[l], out_dtype=BF16, mod3=mod3, shift_idx=0, scale_idx=1,
                     seg_of_tile=seg_of_tile)
        if l % 2 == 0:
            hyena_w = hy_conv_w.shape[2] // 3
            inner = ssd_norm.shape[1]
            heads = inner // SSD_HEAD_DIM
            conv_ch = ssd_conv_w.shape[2]
            main_cols = 3 * hyena_w + inner + conv_ch
            w_in = ev_w_in[i].astype(BF16)
            proj = dense(h, w_in, n_cols=main_cols, out_dtype=BF16, name="in_proj")
            dt_raw = dense(h, w_in[:, main_cols:], out_dtype=F32, name="dt_proj")
            u = dwconv(proj, hy_conv_w[i], hy_conv_b[i], conv_flags, col0=0, silu=False)
            u_ssd = dwconv(proj, ssd_conv_w[i], ssd_conv_b[i], conv_flags, col0=3 * hyena_w + inner,
                           silu=True)
            y_a = jnp.zeros((t, hyena_w), BF16)
            for row0, nb, L in ((0, bc, lc), (tc_rows, bl, ll)):
                filt = hyena_filters(L, hy_f_w1[i], hy_f_b1[i], hy_f_freq1[i], hy_f_w2[i], hy_f_b2[i],
                                     hy_f_freq2[i], hy_f_w3[i], hyena_w)
                if L % (16 * SPEC_BLOCK) == 0:
                    n1 = 2 * L // SPEC_BLOCK
                    t1f = fft_table("t1f", n1, SPEC_BLOCK)
                    kf = hyena_spectrum_2level(L, filt, t1f, fft_table("f2filt", n1, SPEC_BLOCK))
                    tabs = (t1f, fft_table("f2f", n1, SPEC_BLOCK), fft_table("f2i", n1, SPEC_BLOCK),
                            fft_table("t1i", n1, SPEC_BLOCK))
                    y_a = hyena_stream_2level(u, row0, nb, L, kf, tabs, hy_bias[i], hyena_w, fill=y_a)
                else:
                    a_tab = dft_table(L, "filt")
                    kf = hyena_spectrum(L, filt, a_tab)
                    y_a = hyena_stream(u, row0, nb, L, kf, a_tab, dft_table(L, "inv"), hy_bias[i], hyena_w,
                                       fill=y_a)
            gn = SSD_GROUPS * SSD_STATE
            bt_all = u_ssd[:, inner:inner + gn].T
            fac, cum_r = ssd_prep(dt_raw, dt_raw.T, ssd_dt_bias[i], ssd_a_log[i])
            zero = jnp.zeros((bc, SSD_GROUPS, SSD_STATE, inner // SSD_GROUPS), F32)
            init_f = jnp.concatenate([zero, _state_to_kernel_layout(state_ssd_fwd[:, i], SSD_GROUPS)], axis=0)
            init_b = jnp.concatenate([zero, _state_to_kernel_layout(state_ssd_bwd[:, i], SSD_GROUPS)], axis=0)
            yf, yb, sf, sb = ssd_scan(u_ssd, bt_all, fac, cum_r, init_f, init_b,
                                      seq_lens=[lc] * bc + [ll] * bl, col0=0, inner=inner)
            new_sf.append(_state_from_kernel_layout(sf[:bc], SSD_HEAD_DIM))
            new_sb.append(_state_from_kernel_layout(sb[:bc], SSD_HEAD_DIM))
            y_b = ssd_finish(yf, yb, u_ssd, proj, jnp.repeat(ssd_d[i], SSD_HEAD_DIM), ssd_norm[i],
                             xs_col=0, z_col=3 * hyena_w, inner=inner)
            w_out = ev_w_out[i].astype(BF16)
            x = dense_residual(y_a, w_out, x, mod3, 2, seg_of_rows, seg_rows, name="out_proj_a")
            x = dense_residual(y_b, w_out, x, mod3, 2, seg_of_rows, seg_rows, w_row0=hyena_w, name="out_proj_b")
        else:
            n_heads = at_wq.shape[2] // HEAD_DIM
            w_qkv = jnp.concatenate([at_wq[i], at_wk[i], at_wv[i]], axis=1).astype(BF16)
            qkv = dense(h, w_qkv, out_dtype=F32, name="qkv_proj")
            kvw = N_KV_HEADS * HEAD_DIM
            att_c = context_attention(qkv, at_sink[i], nseq=bc, L=lc, n_heads=n_heads)
            past = cache_attn_k.shape[2]
            att = latent_attention(qkv, cache_attn_k[:, i].reshape(bl, past, kvw),
                                   cache_attn_v[:, i].reshape(bl, past, kvw), at_sink[i], att_c,
                                   row0=tc_rows, nseq=bl, L=ll, n_heads=n_heads)
            x = dense_residual(att, at_wo[i].astype(BF16), x, mod3, 2, seg_of_rows, seg_rows, name="attn_out")
            qw = n_heads * HEAD_DIM
            new_k.append(qkv[:tc_rows, qw:qw + kvw].reshape(bc, lc, N_KV_HEADS, HEAD_DIM))
            new_v.append(qkv[:tc_rows, qw + kvw:].reshape(bc, lc, N_KV_HEADS, HEAD_DIM))
        h2 = rms_norm(x, norm_ffn[l], out_dtype=F32, mod3=mod3, shift_idx=3, scale_idx=4,
                      seg_of_tile=seg_of_tile)
        n_experts = moe_w_expert.shape[2]
        w_r = jnp.concatenate([moe_w_group[l], moe_w_expert[l]], axis=1)
        w_r = jnp.pad(w_r, ((0, 0), (0, LANES - w_r.shape[1]))).astype(BF16)
        b_r = jnp.pad(jnp.concatenate([moe_b_group[l], moe_b_expert[l]]), (0, LANES - MOE_GROUPS - n_experts))
        logits = dense(h2, w_r, out_dtype=F32, extra=(b_r.reshape(1, LANES),),
                       extra_specs=(pl.BlockSpec((1, LANES), lambda bt, ii, j, kk: (0, 0)),),
                       epilogue=lambda acc, b_ref, row_tile: acc + b_ref[...], name="router")
        expert_ids, gates = moe_route(logits, MOE_GROUPS, n_experts)
        pos, gate_lanes, tile_lo, tile_hi, n_used, n_slots = moe_plan(expert_ids, gates, MOE_GROUPS, n_experts)
        xs = moe_dispatch(h2, gate_lanes, pos, n_slots)
        y_sorted = moe_ffn(xs, tile_lo, tile_hi, n_used, w1_bf, w3_bf, w2_bf, l)
        x = moe_combine(x, y_sorted, pos, mod3, 5, seg_of_tile)

    y_prompt = rms_norm(x, norm_final, out_dtype=F32, rows=tc_rows).reshape(bc, lc, d)
    y_sample = rms_norm(x, norm_final, out_dtype=F32, row_off=tc_rows, rows=tl_rows).reshape(bl, ll, d)
    return (y_prompt, y_sample, jnp.stack(new_sf, axis=1), jnp.stack(new_sb, axis=1),
            jnp.stack(new_k, axis=1), jnp.stack(new_v, axis=1))
```

```python
import functools
import math

import jax
import jax.numpy as jnp
from jax import lax
from jax.experimental import pallas as pl
from jax.experimental.pallas import tpu as pltpu

F32 = jnp.float32
BF16 = jnp.bfloat16
HI = lax.Precision.HIGHEST

RMS_EPS = 1e-6
GRID_W = 64
HYENA_BANDS = 16
HYENA_FAST_DECAY = 0.3
HYENA_SLOW_DECAY = 1.5
HYENA_TARGET = 1e-2
SSD_HEAD_DIM = 64
SSD_STATE = 128
SSD_GROUPS = 8
SSD_CHUNK = 128
HEAD_DIM = 64
N_KV_HEADS = 8
WINDOW = 128
ROPE_BASE = 10000.0
MOE_GROUPS = 4
TOP_K = 2

LANES = 128
SUBLANES = 8
BF16_SUBLANES = 16
DMA_PRIORITIES = 2
VMEM_LIMIT_BYTES = 48 * 1024 * 1024

ROW_TILE = 256
SPEC_BLOCK = 128


def _cp(*sem, vmem=VMEM_LIMIT_BYTES):
    return pltpu.CompilerParams(dimension_semantics=sem, vmem_limit_bytes=vmem)


def _tile(n, pref):
    if n <= pref:
        return n
    t = pref
    while n % t:
        t //= 2
    assert t >= LANES, (n, pref)
    return t


def _silu(x):
    return x * (1.0 / (1.0 + jnp.exp(-x)))


def _softplus(x):
    return jnp.maximum(x, 0.0) + jnp.log(1.0 + jnp.exp(-jnp.abs(x)))


def _zero_body(o_ref):
    o_ref[...] = jnp.zeros_like(o_ref)


def zeros_2d(rows, cols, dtype):
    tr = _tile(rows, 1024)
    return pl.pallas_call(
        _zero_body,
        out_shape=jax.ShapeDtypeStruct((rows, cols), dtype),
        grid=(rows // tr,),
        out_specs=pl.BlockSpec((tr, cols), lambda i: (i, 0)),
        compiler_params=_cp("parallel"),
        name="zero_fill",
    )()


def _ada_body(c_ref, w_ref, b_ref, o_ref):
    a = _silu(c_ref[...])
    o_ref[...] = lax.dot_general(a, w_ref[...], (((1,), (0,)), ((), ())), precision=HI,
                                 preferred_element_type=F32) + b_ref[...]


def ada_mod(cvec, ada_w, ada_b):
    depth, d, n = ada_w.shape
    rows = cvec.shape[0]
    tn = _tile(n, 1024)
    return pl.pallas_call(
        _ada_body,
        out_shape=jax.ShapeDtypeStruct((depth, rows, n), F32),
        grid=(depth, n // tn),
        in_specs=[pl.BlockSpec((rows, d), lambda l, j: (0, 0)),
                  pl.BlockSpec((None, d, tn), lambda l, j: (l, 0, j)),
                  pl.BlockSpec((None, 1, tn), lambda l, j: (l, 0, j))],
        out_specs=pl.BlockSpec((None, rows, tn), lambda l, j: (l, 0, j)),
        compiler_params=_cp("parallel", "parallel"),
        name="ada_mod",
    )(cvec, ada_w, ada_b.reshape(depth, 1, n))


def _norm_body(x_ref, g_ref, *rest, modulate):
    o_ref = rest[-1]
    x = x_ref[...]
    y = x * lax.rsqrt(jnp.mean(x * x, axis=-1, keepdims=True) + RMS_EPS)
    y = y * g_ref[...]
    if modulate:
        sh_ref, sc_ref = rest[0], rest[1]
        y = y * (1.0 + sc_ref[...]) + sh_ref[...]
    o_ref[...] = y.astype(o_ref.dtype)


def rms_norm(x, g, *, out_dtype, mod3=None, shift_idx=0, scale_idx=0, seg_of_tile=None,
             row_off=0, rows=None):
    t, d = x.shape
    rows = t if rows is None else rows
    tr = ROW_TILE
    off = row_off // tr
    in_specs = [pl.BlockSpec((tr, d), lambda i: (i + off, 0)),
                pl.BlockSpec((1, d), lambda i: (0, 0))]
    args = [x, g.reshape(1, d)]
    if mod3 is not None:
        in_specs += [pl.BlockSpec((None, 1, d), lambda i: (seg_of_tile(i) * 6 + shift_idx, 0, 0)),
                     pl.BlockSpec((None, 1, d), lambda i: (seg_of_tile(i) * 6 + scale_idx, 0, 0))]
        args += [mod3, mod3]
    return pl.pallas_call(
        functools.partial(_norm_body, modulate=mod3 is not None),
        out_shape=jax.ShapeDtypeStruct((rows, d), out_dtype),
        grid=(rows // tr,),
        in_specs=in_specs,
        out_specs=pl.BlockSpec((tr, d), lambda i: (i, 0)),
        compiler_params=_cp("parallel"),
        name="rms_norm",
    )(*args)


def _mm_body(*refs, nk, n_extra, n_skip, epilogue):
    a_ref, b_ref = refs[0], refs[1]
    extra = refs[2:2 + n_extra]
    o_ref = refs[2 + n_extra + n_skip]
    row_tile = pl.program_id(1)
    b = b_ref[...]
    if b.ndim == 3:
        b = b.reshape(-1, b.shape[-1])
    part = jnp.dot(a_ref[...].astype(b.dtype), b, preferred_element_type=F32)
    if nk == 1:
        o_ref[...] = epilogue(part, *extra, row_tile=row_tile).astype(o_ref.dtype).reshape(o_ref.shape)
        return
    acc_ref = refs[3 + n_extra + n_skip]
    k = pl.program_id(3)

    @pl.when(k == 0)
    def _():
        acc_ref[...] = part

    @pl.when(k > 0)
    def _():
        acc_ref[...] += part

    @pl.when(k == nk - 1)
    def _():
        o_ref[...] = epilogue(acc_ref[...], *extra, row_tile=row_tile).astype(o_ref.dtype)


def matmul(a, b, *, grid, tm, tn, a_spec, b_spec, o_spec, out_shape, out_dtype,
           extra=(), extra_specs=(), epilogue=None, fill=None, name="matmul"):
    nk = grid[3]
    epilogue = epilogue or (lambda acc, row_tile: acc)
    scratch = [pltpu.VMEM((tm, tn), F32)] if nk > 1 else []
    fills = () if fill is None else (fill,)
    return pl.pallas_call(
        functools.partial(_mm_body, nk=nk, n_extra=len(extra), n_skip=len(fills), epilogue=epilogue),
        out_shape=jax.ShapeDtypeStruct(out_shape, out_dtype),
        grid=grid,
        in_specs=[a_spec, b_spec, *extra_specs, *[pl.BlockSpec(memory_space=pl.ANY) for _ in fills]],
        out_specs=o_spec,
        scratch_shapes=scratch,
        input_output_aliases={2 + len(extra): 0} if fills else {},
        compiler_params=_cp("parallel", "parallel", "parallel", "arbitrary"),
        name=name,
    )(a, b, *extra, *fills)


def dense(a, w, *, out_dtype, tm=1024, tn=512, tk=2048, extra=(), extra_specs=(), epilogue=None,
          w_row0=0, n_cols=None, name="dense"):
    m, k = a.shape
    n = w.shape[1] if n_cols is None else n_cols
    tm, tn, tk = _tile(m, tm), _tile(n, tn), _tile(k, tk)
    assert w_row0 % tk == 0
    kb0 = w_row0 // tk
    return matmul(
        a, w, grid=(1, m // tm, n // tn, k // tk), tm=tm, tn=tn,
        a_spec=pl.BlockSpec((tm, tk), lambda bt, i, j, kk: (i, kk)),
        b_spec=pl.BlockSpec((tk, tn), lambda bt, i, j, kk: (kb0 + kk, j)),
        o_spec=pl.BlockSpec((tm, tn), lambda bt, i, j, kk: (i, j)),
        out_shape=(m, n), out_dtype=out_dtype, extra=extra, extra_specs=extra_specs,
        epilogue=epilogue, name=name)


def dense_residual(a, w, x, mod3, gate_idx, seg_of_rows, seg_rows, *, tm=1024, tn=512, tk=2048, w_row0=0,
                   name="dense_res"):
    n = w.shape[1]
    tm_ = _tile(seg_rows, tm)
    tn_ = _tile(n, tn)
    return dense(
        a, w, out_dtype=F32, tm=tm_, tn=tn, tk=tk, w_row0=w_row0, extra=(x, mod3),
        extra_specs=(pl.BlockSpec((tm_, tn_), lambda bt, i, j, kk: (i, j)),
                     pl.BlockSpec((None, 1, tn_),
                                  lambda bt, i, j, kk: (seg_of_rows(i * tm_) * 6 + gate_idx, 0, j))),
        epilogue=lambda acc, x_ref, g_ref, row_tile: x_ref[...] + g_ref[...] * acc, name=name)


def _dwconv_body(flags_ref, x_ref, p_ref, n_ref, w_ref, b_ref, o_ref, *, silu):
    i = pl.program_id(0)
    act = _silu if silu else (lambda v: v)
    x = x_ref[...].astype(F32)
    tr = x.shape[0]
    w = w_ref[...]
    y = pltpu.roll(x, 1, 0) * w[0:1, :] + x * w[1:2, :] + pltpu.roll(x, tr - 1, 0) * w[2:3, :] + b_ref[...]
    o_ref[...] = act(y).astype(o_ref.dtype)
    keep_prev = (flags_ref[0, i] == 0).astype(F32)
    keep_next = (flags_ref[1, i] == 0).astype(F32)
    halo_prev = p_ref[SUBLANES - 1:SUBLANES, :].astype(F32) * keep_prev
    halo_next = n_ref[0:1, :].astype(F32) * keep_next
    slab = BF16_SUBLANES
    row = lax.broadcasted_iota(jnp.int32, (slab, x.shape[1]), 0)
    fix_top = (halo_prev - x[tr - 1:tr, :]) * w[0:1, :]
    fix_bot = (halo_next - x[0:1, :]) * w[2:3, :]
    o_ref[0:slab, :] = act(y[0:slab] + jnp.where(row == 0, fix_top, 0.0)).astype(o_ref.dtype)
    o_ref[tr - slab:tr, :] = act(y[tr - slab:tr] + jnp.where(row == slab - 1, fix_bot, 0.0)).astype(o_ref.dtype)


def dwconv(proj, w, b, flags, *, col0, silu):
    t = proj.shape[0]
    cols = w.shape[1]
    tr = ROW_TILE
    tc = _tile(math.gcd(col0, cols), 2048)
    cb0 = col0 // tc
    hb = tr // SUBLANES
    last_hb = t // SUBLANES - 1
    grid_spec = pltpu.PrefetchScalarGridSpec(
        num_scalar_prefetch=1,
        grid=(t // tr, cols // tc),
        in_specs=[pl.BlockSpec((tr, tc), lambda i, j, f: (i, cb0 + j)),
                  pl.BlockSpec((SUBLANES, tc), lambda i, j, f: (jnp.maximum(i * hb - 1, 0), cb0 + j)),
                  pl.BlockSpec((SUBLANES, tc), lambda i, j, f: (jnp.minimum((i + 1) * hb, last_hb), cb0 + j)),
                  pl.BlockSpec((3, tc), lambda i, j, f: (0, j)),
                  pl.BlockSpec((1, tc), lambda i, j, f: (0, j))],
        out_specs=pl.BlockSpec((tr, tc), lambda i, j, f: (i, j)))
    return pl.pallas_call(
        functools.partial(_dwconv_body, silu=silu),
        out_shape=jax.ShapeDtypeStruct((t, cols), BF16),
        grid_spec=grid_spec,
        compiler_params=_cp("parallel", "parallel"),
        name="dwconv_silu" if silu else "dwconv",
    )(flags, proj, proj, proj, w, b.reshape(1, -1))


def _filter_body(feat_ref, w1_ref, b1_ref, f1_ref, w2_ref, b2_ref, f2_ref, w3_ref, dl_ref, o_ref):
    d = pl.program_id(0)
    i = pl.program_id(1)
    feat = feat_ref[...]
    dn = (((1,), (0,)), ((), ()))
    hid = jnp.sin(f1_ref[...] * (lax.dot_general(feat, w1_ref[...], dn, precision=HI,
                                                 preferred_element_type=F32) + b1_ref[...]))
    hid = jnp.sin(f2_ref[...] * (lax.dot_general(hid, w2_ref[...], dn, precision=HI,
                                                 preferred_element_type=F32) + b2_ref[...]))
    filt = jnp.dot(hid.astype(BF16), w3_ref[...].astype(BF16), preferred_element_type=F32)
    t01 = feat[:, 0:1]
    filt = filt * jnp.exp(-t01 * dl_ref[...])
    row = lax.broadcasted_iota(jnp.int32, filt.shape, 0) + i * filt.shape[0]
    filt = jnp.where((d == 1) & (row == 0), 0.0, filt)
    o_ref[...] = filt.astype(o_ref.dtype)


def hyena_filters(L, f_w1, f_b1, f_freq1, f_w2, f_b2, f_freq2, f_w3, c):
    t = jnp.arange(L, dtype=F32)
    t01 = t / (L - 1)
    bands = jnp.linspace(1e-4, HYENA_BANDS - 1, HYENA_BANDS, dtype=F32)
    ang = (2.0 * math.pi / L) * t[:, None] * bands[None, :]
    feat = jnp.concatenate([t01[:, None], jnp.cos(ang), -jnp.sin(ang)], axis=-1)
    emb = feat.shape[1]
    feat = jnp.pad(feat, ((0, 0), (0, LANES - emb)))
    w1 = jnp.pad(f_w1, ((0, LANES - emb), (0, 0)))
    ffn = f_w1.shape[1]
    order = f_w3.shape[1] // (2 * c)
    deltas = jnp.abs(jnp.linspace(math.log(HYENA_TARGET) / HYENA_SLOW_DECAY,
                                  math.log(HYENA_TARGET) / HYENA_FAST_DECAY, c, dtype=F32))
    deltas = jnp.tile(deltas, order).reshape(1, order * c)
    tl, tn = _tile(L, 512), _tile(order * c, 4096)
    nj = order * c // tn
    return pl.pallas_call(
        _filter_body,
        out_shape=jax.ShapeDtypeStruct((2, L, order * c), BF16),
        grid=(2, L // tl, nj),
        in_specs=[pl.BlockSpec((tl, LANES), lambda d, i, j: (i, 0)),
                  pl.BlockSpec((LANES, ffn), lambda d, i, j: (0, 0)),
                  pl.BlockSpec((1, ffn), lambda d, i, j: (0, 0)),
                  pl.BlockSpec((1, ffn), lambda d, i, j: (0, 0)),
                  pl.BlockSpec((ffn, ffn), lambda d, i, j: (0, 0)),
                  pl.BlockSpec((1, ffn), lambda d, i, j: (0, 0)),
                  pl.BlockSpec((1, ffn), lambda d, i, j: (0, 0)),
                  pl.BlockSpec((ffn, tn), lambda d, i, j: (0, d * nj + j)),
                  pl.BlockSpec((1, tn), lambda d, i, j: (0, j))],
        out_specs=pl.BlockSpec((None, tl, tn), lambda d, i, j: (d, i, j)),
        compiler_params=_cp("parallel", "parallel", "parallel"),
        name="hyena_filters",
    )(feat, w1, f_b1.reshape(1, ffn), f_freq1.reshape(1, ffn), f_w2, f_b2.reshape(1, ffn),
      f_freq2.reshape(1, ffn), f_w3, deltas)


def _cos_turns(m, log2_period):
    period = 1 << log2_period
    quarter = period // 4
    mm = m + quarter // 2
    q = (mm >> (log2_period - 2)) & 3
    r = (mm & (quarter - 1)) - quarter // 2
    phi = r.astype(F32) * (2.0 * math.pi / period)
    p2 = phi * phi
    cosv = 1.0 + p2 * (-1.0 / 2 + p2 * (1.0 / 24 + p2 * (-1.0 / 720 + p2 * (1.0 / 40320 - p2 / 3628800))))
    sinv = phi * (1.0 + p2 * (-1.0 / 6 + p2 * (1.0 / 120 + p2 * (-1.0 / 5040 + p2 / 362880))))
    return jnp.where(q == 0, cosv, jnp.where(q == 1, -sinv, jnp.where(q == 2, -cosv, sinv)))


def _spec_index(r, L):
    blk = r >> 8
    within = r & (2 * SPEC_BLOCK - 1)
    return blk * SPEC_BLOCK + (within & (SPEC_BLOCK - 1)), within >= SPEC_BLOCK


def _dft_body(o_ref, *, L, mode):
    log2n = int(math.log2(2 * L))
    shape = o_ref.shape
    i0 = pl.program_id(0) * shape[0]
    j0 = pl.program_id(1) * shape[1]
    rows = lax.broadcasted_iota(jnp.int32, shape, 0) + i0
    cols = lax.broadcasted_iota(jnp.int32, shape, 1) + j0
    if mode == "inv":
        r, n = cols, rows
    else:
        r, n = rows, cols
    k, im = _spec_index(r, L)
    back = n >= L
    n = jnp.where(back, n - L, n)
    phase = jnp.where(im, jnp.where(back, -(L // 2), L // 2), 0)
    val = _cos_turns((k * n + phase) & (2 * L - 1), log2n)
    nyq = jnp.where((n & 1) == 0, 1.0, -1.0)
    val = jnp.where(im & (k == 0), nyq, val)
    if mode == "inv":
        val = val * jnp.where(k == 0, 1.0 / (2 * L), 2.0 / (2 * L))
    o_ref[...] = val.astype(o_ref.dtype)


def dft_table(L, mode):
    assert L & (L - 1) == 0 and L >= 2 * SPEC_BLOCK
    shape = {"fwd": (2 * L, L), "filt": (2 * L, 2 * L), "inv": (L, 2 * L)}[mode]
    tr, tc = _tile(shape[0], 512), _tile(shape[1], 1024)
    return pl.pallas_call(
        functools.partial(_dft_body, L=L, mode=mode),
        out_shape=jax.ShapeDtypeStruct(shape, BF16),
        grid=(shape[0] // tr, shape[1] // tc),
        out_specs=pl.BlockSpec((tr, tc), lambda i, j: (i, j)),
        compiler_params=_cp("parallel", "parallel"),
        name="dft_table_" + mode,
    )()


def _spectral_epilogue(acc, kf_ref, row_tile):
    tm, tn = acc.shape
    nb = tm // (2 * SPEC_BLOCK)
    z = acc.reshape(nb, 2, SPEC_BLOCK, tn)
    kf = kf_ref[...].reshape(nb, 2, SPEC_BLOCK, tn)
    zr, zi, kr, ki = z[:, 0], z[:, 1], kf[:, 0], kf[:, 1]
    first = row_tile == 0
    blk = lax.broadcasted_iota(jnp.int32, zr.shape, 0)
    row = lax.broadcasted_iota(jnp.int32, zr.shape, 1)
    dc = first & (blk == 0) & (row == 0)
    pr = zr * kr - jnp.where(dc, 0.0, zi * ki)
    pi = jnp.where(dc, zi * ki, zr * ki + zi * kr)
    return jnp.stack([pr, pi], axis=1).reshape(tm, tn)


def hyena_stream(u, row0, nb, L, kf, f_tab, g_tab, hy_bias, c, fill=None):
    out_rows = u.shape[0]
    tn = _tile(c, 2048)
    cb = c // tn
    tm_f = _tile(2 * L, 512)
    tk_f = _tile(L, 2048)
    tm_i = _tile(L, 512)
    tk_i = _tile(2 * L, 2048)
    z, z_rows, z_col = u, row0, 2 * cb
    for o in range(2):
        zrb = z_rows // tk_f
        p = matmul(
            f_tab, z, grid=(nb, 2 * L // tm_f, cb, L // tk_f), tm=tm_f, tn=tn,
            a_spec=pl.BlockSpec((tm_f, tk_f), lambda bt, i, j, k: (i, k)),
            b_spec=pl.BlockSpec((tk_f, tn), lambda bt, i, j, k, zrb=zrb, zc=z_col:
                                (zrb + bt * (L // tk_f) + k, zc + j)),
            o_spec=pl.BlockSpec((None, tm_f, tn), lambda bt, i, j, k: (bt, i, j)),
            out_shape=(nb, 2 * L, c), out_dtype=BF16,
            extra=(kf,), extra_specs=(pl.BlockSpec((tm_f, tn), lambda bt, i, j, k, o=o: (i, o * cb + j)),),
            epilogue=_spectral_epilogue, name="hyena_fwd")
        zrb_i = z_rows // tm_i
        urb_i = row0 // tm_i
        last = o == 1
        orb = urb_i if last else 0
        z_new = matmul(
            g_tab, p, grid=(nb, L // tm_i, cb, 2 * L // tk_i), tm=tm_i, tn=tn,
            a_spec=pl.BlockSpec((tm_i, tk_i), lambda bt, i, j, k: (i, k)),
            b_spec=pl.BlockSpec((None, tk_i, tn), lambda bt, i, j, k: (bt, k, j)),
            o_spec=pl.BlockSpec((tm_i, tn), lambda bt, i, j, k, orb=orb: (orb + bt * (L // tm_i) + i, j)),
            out_shape=(out_rows if last else nb * L, c), out_dtype=BF16, fill=fill if last else None,
            extra=(u, z, hy_bias.reshape(2, 1, c)),
            extra_specs=(pl.BlockSpec((tm_i, tn), lambda bt, i, j, k, o=o:
                                      (urb_i + bt * (L // tm_i) + i, o * cb + j)),
                         pl.BlockSpec((tm_i, tn), lambda bt, i, j, k, zrb_i=zrb_i, zc=z_col:
                                      (zrb_i + bt * (L // tm_i) + i, zc + j)),
                         pl.BlockSpec((None, 1, tn), lambda bt, i, j, k, o=o: (o, 0, j))),
            epilogue=lambda acc, g_ref, z_ref, b_ref, row_tile:
                g_ref[...].astype(F32) * (acc + b_ref[...] * z_ref[...].astype(F32)),
            name="hyena_inv")
        z, z_rows, z_col = z_new, 0, 0
    return z


def hyena_spectrum(L, filt, a_tab):
    n = filt.shape[2]
    b = filt.reshape(2 * L, n)
    tm, tn, tk = _tile(2 * L, 512), _tile(n, 512), _tile(2 * L, 2048)
    return matmul(
        a_tab, b, grid=(1, 2 * L // tm, n // tn, 2 * L // tk), tm=tm, tn=tn,
        a_spec=pl.BlockSpec((tm, tk), lambda bt, i, j, k: (i, k)),
        b_spec=pl.BlockSpec((tk, tn), lambda bt, i, j, k: (k, j)),
        o_spec=pl.BlockSpec((tm, tn), lambda bt, i, j, k: (i, j)),
        out_shape=(2 * L, n), out_dtype=F32, name="hyena_spectrum")


def _fft_tab_body(o_ref, *, kind, n1, n2, k1_len):
    shape = o_ref.shape
    r = lax.broadcasted_iota(jnp.int32, shape, 0) + pl.program_id(0) * shape[0]
    c = lax.broadcasted_iota(jnp.int32, shape, 1)
    n = n1 * n2
    lg = lambda v: int(math.log2(v))
    s8 = BF16_SUBLANES
    if kind in ("t1f", "t1i"):
        if kind == "t1f":
            grp, kp, s, n1i, s2 = r >> lg(2 * n1 * s8), (r >> lg(s8)) & (2 * n1 - 1), r & (s8 - 1), c >> lg(s8), c & (s8 - 1)
        else:
            grp, n1i, s, kp, s2 = r >> lg(k1_len * s8), (r >> lg(s8)) & (k1_len - 1), r & (s8 - 1), c >> lg(s8), c & (s8 - 1)
        t = n2 * n1i + grp * s8 + s
        val = _cos_turns(((kp & (n1 - 1)) * t + (kp >> lg(n1)) * (n // 4)) & (n - 1), lg(n))
        val = jnp.where(s == s2, val * (1.0 / n if kind == "t1i" else 1.0), 0.0)
    else:
        rp = r >= n2
        cblk = c >> lg(n2)
        cp = (cblk & 1) == 1
        sin_sign = -1 if kind == "f2i" else 1
        quarter = jnp.where(rp == cp, 0, jnp.where(cp, -sin_sign, sin_sign))
        val = _cos_turns(((r & (n2 - 1)) * (c & (n2 - 1)) + quarter * (n2 // 4)) & (n2 - 1), lg(n2))
        if kind == "f2filt":
            val = jnp.where((cblk >= 2) & rp, -val, val)
    o_ref[...] = val.astype(o_ref.dtype)


def fft_table(kind, n1, n2):
    k1_len = n1 // 2
    s8 = BF16_SUBLANES
    groups = n2 // s8
    shape = {"t1f": (groups * 2 * n1 * s8, k1_len * s8), "t1i": (groups * k1_len * s8, 2 * n1 * s8),
             "f2f": (2 * n2, 2 * n2), "f2i": (2 * n2, 2 * n2), "f2filt": (2 * n2, 4 * n2)}[kind]
    tr = _tile(shape[0], 1024)
    out = pl.pallas_call(
        functools.partial(_fft_tab_body, kind=kind, n1=n1, n2=n2, k1_len=k1_len),
        out_shape=jax.ShapeDtypeStruct(shape, BF16),
        grid=(shape[0] // tr,),
        out_specs=pl.BlockSpec((tr, shape[1]), lambda i: (i, 0)),
        compiler_params=_cp("parallel"),
        name="fft_table_" + kind,
    )()
    if kind in ("t1f", "t1i"):
        return out.reshape(groups, shape[0] // groups, shape[1])
    return out


FFT_K1_PER_STEP = 4


def _fft_mid_body(f_ref, *refs, mode, n2):
    f = f_ref[...]
    if mode == "inv":
        p_ref, o_ref = refs
        for k in range(FFT_K1_PER_STEP):
            res = jnp.dot(f, p_ref[k * 2 * n2:(k + 1) * 2 * n2, :], preferred_element_type=F32)
            o_ref[0, k * n2:(k + 1) * n2, :] = res[:n2].astype(o_ref.dtype)
            o_ref[1, k * n2:(k + 1) * n2, :] = res[n2:].astype(o_ref.dtype)
        return
    if mode == "conv":
        finv_ref, y_ref, kf_ref, o_ref = refs
        srcs = (y_ref,)
    elif mode == "fwd":
        y_ref, kf_ref, o_ref = refs
        srcs = (y_ref,)
    else:
        y0_ref, y1_ref, o_ref = refs
        srcs = (y0_ref, y1_ref)
    for k in range(FFT_K1_PER_STEP):
        rows = slice(k * n2, (k + 1) * n2)
        stack = jnp.concatenate([s[part, rows, :] for s in srcs for part in range(2)], axis=0)
        acc = jnp.dot(f, stack, preferred_element_type=F32)
        out_rows = slice(k * 2 * n2, (k + 1) * 2 * n2)
        if mode in ("fwd", "conv"):
            zr, zi = acc[:n2], acc[n2:]
            kr = kf_ref[k * 2 * n2:k * 2 * n2 + n2, :].astype(F32)
            ki = kf_ref[k * 2 * n2 + n2:(k + 1) * 2 * n2, :].astype(F32)
            acc = jnp.concatenate([zr * kr - zi * ki, zr * ki + zi * kr], axis=0)
        if mode == "conv":
            res = jnp.dot(finv_ref[...], acc.astype(BF16), preferred_element_type=F32)
            o_ref[0, rows, :] = res[:n2].astype(o_ref.dtype)
            o_ref[1, rows, :] = res[n2:].astype(o_ref.dtype)
        else:
            o_ref[out_rows, :] = acc.astype(o_ref.dtype)


def fft_mid(mode, f_tab, src, n1, n2, *, kf=None, kf_col0=0, f_inv=None, out_dtype=BF16, name):
    kb = FFT_K1_PER_STEP
    c = src.shape[-1]
    tn = _tile(c, 1024)
    cb = c // tn
    nb = 1 if mode == "filt" else src.shape[0]
    y_block = (None, 2, kb * n2, tn)
    p_block = (None, kb * 2 * n2, tn)
    f_spec = pl.BlockSpec(f_tab.shape, lambda i, b, j: (0, 0))
    if mode == "conv":
        in_specs = [f_spec, pl.BlockSpec(f_inv.shape, lambda i, b, j: (0, 0)),
                    pl.BlockSpec(y_block, lambda i, b, j: (b, 0, i, j)),
                    pl.BlockSpec((kb * 2 * n2, tn), lambda i, b, j: (i, kf_col0 // tn + j))]
        args = (f_tab, f_inv, src, kf)
        out_shape, out_spec = (nb, 2, n1 * n2, c), pl.BlockSpec(y_block, lambda i, b, j: (b, 0, i, j))
    elif mode == "fwd":
        in_specs = [f_spec, pl.BlockSpec(y_block, lambda i, b, j: (b, 0, i, j)),
                    pl.BlockSpec((kb * 2 * n2, tn), lambda i, b, j: (i, kf_col0 // tn + j))]
        args = (f_tab, src, kf)
        out_shape, out_spec = (nb, n1 * 2 * n2, c), pl.BlockSpec(p_block, lambda i, b, j: (b, i, j))
    elif mode == "filt":
        in_specs = [f_spec, pl.BlockSpec(y_block, lambda i, b, j: (0, 0, i, j)),
                    pl.BlockSpec(y_block, lambda i, b, j: (1, 0, i, j))]
        args = (f_tab, src, src)
        out_shape, out_spec = (n1 * 2 * n2, c), pl.BlockSpec((kb * 2 * n2, tn), lambda i, b, j: (i, j))
    else:
        in_specs = [f_spec, pl.BlockSpec(p_block, lambda i, b, j: (b, i, j))]
        args = (f_tab, src)
        out_shape, out_spec = (nb, 2, n1 * n2, c), pl.BlockSpec(y_block, lambda i, b, j: (b, 0, i, j))
    return pl.pallas_call(
        functools.partial(_fft_mid_body, mode=mode, n2=n2),
        out_shape=jax.ShapeDtypeStruct(out_shape, out_dtype),
        grid=(n1 // kb, nb, cb),
        in_specs=in_specs,
        out_specs=out_spec,
        compiler_params=_cp("parallel", "parallel", "parallel"),
        name=name,
    )(*args)


def hyena_spectrum_2level(L, filt, t1f, f2filt):
    n2 = SPEC_BLOCK
    n1 = 2 * L // n2
    k1 = n1 // 2
    n = filt.shape[2]
    tn = _tile(n, 1024)
    nj = n // tn
    sl = BF16_SUBLANES
    y1 = matmul(
        t1f, filt.reshape(2 * k1, n2, n), grid=(2, n2 // sl, nj, 1), tm=2 * n1 * sl, tn=tn,
        a_spec=pl.BlockSpec((None, 2 * n1 * sl, k1 * sl), lambda bt, i, j, k: (i, 0, 0)),
        b_spec=pl.BlockSpec((k1, sl, tn), lambda bt, i, j, k: (bt, i, j)),
        o_spec=pl.BlockSpec((None, 2 * n1, sl, tn), lambda bt, i, j, k: (bt, 0, i, j)),
        out_shape=(2, 2 * n1, n2, n), out_dtype=BF16, name="hyena_filt_fft1")
    return fft_mid("filt", f2filt, y1.reshape(2, 2, n1 * n2, n), n1, n2, name="hyena_filt_fft2")


def hyena_stream_2level(u, row0, nb, L, kf, tabs, hy_bias, c, fill=None):
    t1f, f2f, f2i, t1i = tabs
    n2 = SPEC_BLOCK
    n1 = 2 * L // n2
    k1 = n1 // 2
    sl = BF16_SUBLANES
    ng = n2 // sl
    t1n = _tile(c, 1024)
    c1b = c // t1n
    tn = _tile(c, 2048)
    cb = c // tn
    t, u_w = u.shape
    assert row0 % L == 0 and u_w % tn == 0
    uv = u.reshape(t // n2, n2, u_w)
    urow = row0 // L
    zv, zr, zc = uv, urow, 2 * c // t1n
    bias3 = hy_bias.reshape(2, 1, c)
    for o in range(2):
        y1 = matmul(
            t1f, zv, grid=(nb, ng, c1b, 1), tm=2 * n1 * sl, tn=t1n,
            a_spec=pl.BlockSpec((None, 2 * n1 * sl, k1 * sl), lambda bt, i, j, k: (i, 0, 0)),
            b_spec=pl.BlockSpec((k1, sl, t1n), lambda bt, i, j, k, zr=zr, zc=zc: (zr + bt, i, zc + j)),
            o_spec=pl.BlockSpec((None, 2 * n1, sl, t1n), lambda bt, i, j, k: (bt, 0, i, j)),
            out_shape=(nb, 2 * n1, n2, c), out_dtype=BF16, name="hyena_fft1")
        u1 = fft_mid("conv", f2f, y1.reshape(nb, 2, n1 * n2, c), n1, n2, kf=kf, kf_col0=o * c, f_inv=f2i,
                     name="hyena_fft2_ifft1")
        last = o == 1
        orow = urow if last else 0

        def gate_epilogue(acc, g_ref, z_ref, b_ref, row_tile):
            g = g_ref[...].reshape(acc.shape).astype(F32)
            return g * (acc + b_ref[...] * z_ref[...].reshape(acc.shape).astype(F32))

        z = matmul(
            t1i, u1.reshape(nb, 2 * n1, n2, c), grid=(nb, ng, c1b, 1), tm=k1 * sl, tn=t1n,
            a_spec=pl.BlockSpec((None, k1 * sl, 2 * n1 * sl), lambda bt, i, j, k: (i, 0, 0)),
            b_spec=pl.BlockSpec((None, 2 * n1, sl, t1n), lambda bt, i, j, k: (bt, 0, i, j)),
            o_spec=pl.BlockSpec((k1, sl, t1n), lambda bt, i, j, k, orow=orow: (orow + bt, i, j)),
            out_shape=(t // n2 if last else nb * k1, n2, c), out_dtype=BF16,
            fill=None if (fill is None or not last) else fill.reshape(t // n2, n2, c),
            extra=(uv, zv, bias3),
            extra_specs=(pl.BlockSpec((k1, sl, t1n), lambda bt, i, j, k, o=o: (urow + bt, i, o * c1b + j)),
                         pl.BlockSpec((k1, sl, t1n), lambda bt, i, j, k, zr=zr, zc=zc: (zr + bt, i, zc + j)),
                         pl.BlockSpec((None, 1, t1n), lambda bt, i, j, k, o=o: (o, 0, j))),
            epilogue=gate_epilogue, name="hyena_ifft2")
        zv, zr, zc = z, 0, 0
    return zv.reshape(t, c)


SSD_CUM_TERMS = 3


def _ssd_prep_body(dc_ref, dr_ref, biasc_ref, biasr_ref, ac_ref, ar_ref, fac_ref, cumr_ref, *, heads):
    q = dc_ref.shape[0]
    dn = (((1,), (0,)), ((), ()))
    li = lax.broadcasted_iota(jnp.int32, (q, q), 0)
    si = lax.broadcasted_iota(jnp.int32, (q, q), 1)
    lower = (li >= si).astype(F32)
    upper = (li <= si).astype(F32)
    dt_c = _softplus(dc_ref[...] + biasc_ref[...])
    da_c = dt_c * ac_ref[...]
    da_r = _softplus(dr_ref[...] + biasr_ref[...]) * ar_ref[...]
    pre_c = lax.dot_general(lower, da_c, dn, precision=HI, preferred_element_type=F32)
    suf_c = lax.dot_general(upper, da_c, dn, precision=HI, preferred_element_type=F32)
    pre_r = lax.dot_general(da_r, upper, dn, precision=HI, preferred_element_type=F32)
    suf_r = lax.dot_general(da_r, lower, dn, precision=HI, preferred_element_type=F32)
    cum_c = jnp.where(lax.broadcasted_iota(jnp.int32, pre_c.shape, 1) < heads, pre_c, suf_c)
    terms, rest = [], cum_c
    for _ in range(SSD_CUM_TERMS):
        term = rest.astype(BF16)
        rest = rest - term.astype(F32)
        terms.append(term)
    fac_ref[...] = jnp.concatenate(terms + [dt_c.astype(BF16)], axis=1)
    cumr_ref[...] = jnp.where(lax.broadcasted_iota(jnp.int32, pre_r.shape, 0) < heads, pre_r, suf_r)


def ssd_prep(dt_raw, dt_raw_t, dt_bias, a_log):
    t, h2 = dt_raw.shape
    q = SSD_CHUNK
    neg_a = -jnp.exp(a_log.astype(F32)).reshape(1, h2)
    bias = dt_bias.astype(F32).reshape(1, h2)
    const = lambda shape: pl.BlockSpec(shape, lambda i: (0,) * len(shape))
    col = pl.BlockSpec((q, h2), lambda i: (i, 0))
    row = pl.BlockSpec((h2, q), lambda i: (0, i))
    nf = (SSD_CUM_TERMS + 1) * h2
    return pl.pallas_call(
        functools.partial(_ssd_prep_body, heads=h2 // 2),
        out_shape=(jax.ShapeDtypeStruct((t, nf), BF16), jax.ShapeDtypeStruct((h2, t), F32)),
        grid=(t // q,),
        in_specs=[col, row, const((1, h2)), const((h2, 1)), const((1, h2)), const((h2, 1))],
        out_specs=(pl.BlockSpec((q, nf), lambda i: (i, 0)), row),
        compiler_params=_cp("parallel"),
        name="ssd_prep",
    )(dt_raw, dt_raw_t, bias, bias.reshape(h2, 1), neg_a, neg_a.reshape(h2, 1))


def ssd_selectors(h2, j, p, q):
    w_sum, w_wide = j * p, j * q
    k = (SSD_CUM_TERMS + 1) * h2
    rows = jnp.arange(k, dtype=jnp.int32)[None, :, None]
    cols = jnp.arange(w_sum + w_wide + j * p, dtype=jnp.int32)[None, None, :]
    head0 = j * jnp.arange(h2 // j, dtype=jnp.int32)[:, None, None]
    col_head = jnp.where(cols < w_sum, cols // p, jnp.where(cols < w_sum + w_wide, (cols - w_sum) // q,
                                                              (cols - w_sum - w_wide) // p))
    is_dt_row = rows // h2 == SSD_CUM_TERMS
    is_dt_col = cols >= w_sum + w_wide
    sel = (rows % h2 == head0 + col_head) & (is_dt_row == is_dt_col)
    return sel.astype(BF16)


def _ssd_dir(xs, bt, cm, fac, sel, diag_mask, cumr_ref, state_ref, head0, *, reverse, heads_per_group):
    q = xs.shape[0]
    p = SSD_HEAD_DIM
    j = heads_per_group
    spread = jnp.dot(fac, sel, preferred_element_type=F32)
    cc, cc_wide, dtv = spread[:, :j * p], spread[:, j * p:j * p + j * q], spread[:, j * p + j * q:]
    total = cc[0:1, :] if reverse else cc[q - 1:q, :]
    xd = xs.astype(F32) * dtv
    xd_b = xd.astype(BF16)
    li = lax.broadcasted_iota(jnp.int32, (q, q), 0)
    si = lax.broadcasted_iota(jnp.int32, (q, q), 1)
    mask = (li <= si) if reverse else (li >= si)
    cb = jnp.dot(cm, bt, preferred_element_type=F32)
    weights = []
    for jh in range(j):
        cr = cumr_ref[pl.ds(head0 + jh, 1), :]
        decay = jnp.exp(jnp.where(mask, cc_wide[:, jh * q:(jh + 1) * q] - cr, -jnp.inf))
        weights.append((cb * decay).astype(BF16))
    xd_diag = jnp.concatenate([xd_b] * j, axis=0) * diag_mask
    y = jnp.dot(jnp.concatenate(weights, axis=1), xd_diag, preferred_element_type=F32)
    s_prev = state_ref[...]
    y = y + jnp.exp(cc) * jnp.dot(cm, s_prev.astype(BF16), preferred_element_type=F32)
    st = jnp.dot(bt, (xd * jnp.exp(total - cc)).astype(BF16), preferred_element_type=F32)
    state_ref[...] = jnp.exp(total) * s_prev + st
    return y


def _ssd_body(tab_ref, xf_ref, btf_ref, cf_ref, facf_ref, crf_ref, xb_ref, btb_ref, cb_ref, facb_ref, crb_ref,
              self_ref, selb_ref, mask_ref, initf_ref, initb_ref, yf_ref, yb_ref, sf_ref, sb_ref, stf_ref, stb_ref,
              *, heads_per_group, heads):
    g = pl.program_id(0)
    item = pl.program_id(1)

    @pl.when(tab_ref[_SSD_FIRST, item] == 1)
    def _():
        stf_ref[...] = initf_ref[...]
        stb_ref[...] = initb_ref[...]

    jp = heads_per_group * SSD_HEAD_DIM
    n = SSD_STATE
    for gi in range(SSD_GROUPS_PER_STEP):
        head0 = (g * SSD_GROUPS_PER_STEP + gi) * heads_per_group
        cols, srow = slice(gi * jp, (gi + 1) * jp), slice(gi * n, (gi + 1) * n)
        yf_ref[:, cols] = _ssd_dir(xf_ref[:, cols], btf_ref[srow, :], cf_ref[:, srow], facf_ref[...], self_ref[gi],
                                   mask_ref[...], crf_ref, stf_ref.at[gi], head0, reverse=False,
                                   heads_per_group=heads_per_group)
        yb_ref[:, cols] = _ssd_dir(xb_ref[:, cols], btb_ref[srow, :], cb_ref[:, srow], facb_ref[...], selb_ref[gi],
                                   mask_ref[...], crb_ref, stb_ref.at[gi], heads + head0, reverse=True,
                                   heads_per_group=heads_per_group)

    @pl.when(tab_ref[_SSD_LAST, item] == 1)
    def _():
        sf_ref[...] = stf_ref[...]
        sb_ref[...] = stb_ref[...]


_SSD_FWD, _SSD_BWD, _SSD_SEQ, _SSD_FIRST, _SSD_LAST = range(5)
SSD_GROUPS_PER_STEP = 8


def ssd_scan(u, bt_all, fac, cum_r, init_f, init_b, *, seq_lens, col0, inner):
    q, p, n, gq = SSD_CHUNK, SSD_HEAD_DIM, SSD_STATE, SSD_GROUPS
    heads = inner // p
    j = heads // gq
    jp = j * p
    xcb = col0 // jp
    ccb = (col0 + inner + gq * n) // n
    rows, base = [], 0
    for s, length in enumerate(seq_lens):
        nc = length // q
        rows += [(base + c, base + nc - 1 - c, s, int(c == 0), int(c == nc - 1)) for c in range(nc)]
        base += nc
    table = jnp.array(rows, jnp.int32).T
    nseq, n_items = len(seq_lens), len(rows)

    gs = SSD_GROUPS_PER_STEP
    assert xcb % gs == 0 and ccb % gs == 0 and gq % gs == 0

    def specs(which):
        return [pl.BlockSpec((q, gs * jp), lambda g, it, tab: (tab[which, it], xcb // gs + g)),
                pl.BlockSpec((gs * n, q), lambda g, it, tab: (g, tab[which, it])),
                pl.BlockSpec((q, gs * n), lambda g, it, tab: (tab[which, it], ccb // gs + g)),
                pl.BlockSpec((q, fac.shape[1]), lambda g, it, tab: (tab[which, it], 0)),
                pl.BlockSpec((2 * heads, q), lambda g, it, tab: (0, tab[which, it]))]

    st_spec = pl.BlockSpec((None, gs, n, jp), lambda g, it, tab: (tab[_SSD_SEQ, it], g, 0, 0))
    st_shape = jax.ShapeDtypeStruct((nseq, gq, n, jp), F32)
    y_shape = jax.ShapeDtypeStruct((u.shape[0], inner), F32)
    sel_all = ssd_selectors(2 * heads, j, p, q)
    sel_block = (gs,) + sel_all.shape[1:]
    diag_mask = (jnp.arange(j * q)[:, None] // q == jnp.arange(jp)[None, :] // p).astype(BF16)
    grid_spec = pltpu.PrefetchScalarGridSpec(
        num_scalar_prefetch=1,
        grid=(gq // gs, n_items),
        in_specs=[*specs(_SSD_FWD), *specs(_SSD_BWD),
                  pl.BlockSpec(sel_block, lambda g, it, tab: (g, 0, 0)),
                  pl.BlockSpec(sel_block, lambda g, it, tab: (gq // gs + g, 0, 0)),
                  pl.BlockSpec(diag_mask.shape, lambda g, it, tab: (0, 0)),
                  st_spec, st_spec],
        out_specs=(pl.BlockSpec((q, gs * jp), lambda g, it, tab: (tab[_SSD_FWD, it], g)),
                   pl.BlockSpec((q, gs * jp), lambda g, it, tab: (tab[_SSD_BWD, it], g)), st_spec, st_spec),
        scratch_shapes=[pltpu.VMEM((gs, n, jp), F32), pltpu.VMEM((gs, n, jp), F32)])
    return pl.pallas_call(
        functools.partial(_ssd_body, heads_per_group=j, heads=heads),
        out_shape=(y_shape, y_shape, st_shape, st_shape),
        grid_spec=grid_spec,
        compiler_params=_cp("parallel", "arbitrary"),
        name="ssd_scan",
    )(table, u, bt_all, u, fac, cum_r, u, bt_all, u, fac, cum_r, sel_all, sel_all, diag_mask, init_f, init_b)


def _state_to_kernel_layout(s, groups):
    b, h, p, n = s.shape
    return s.reshape(b, groups, h // groups, p, n).transpose(0, 1, 4, 2, 3).reshape(b, groups, n, (h // groups) * p)


def _state_from_kernel_layout(s, head_dim):
    b, g, n, jp = s.shape
    j = jp // head_dim
    return s.reshape(b, g, n, j, head_dim).transpose(0, 1, 3, 4, 2).reshape(b, g * j, head_dim, n)


def _ssd_finish_body(yf_ref, yb_ref, xs_ref, z_ref, d_ref, w_ref, o_ref):
    y = yf_ref[...] + yb_ref[...] + xs_ref[...].astype(F32) * d_ref[...]
    y = y * _silu(z_ref[...].astype(F32))
    y = y * lax.rsqrt(jnp.mean(y * y, axis=-1, keepdims=True) + RMS_EPS)
    o_ref[...] = (y * w_ref[...]).astype(o_ref.dtype)


def ssd_finish(yf, yb, u, proj, d_full, norm_w, *, xs_col, z_col, inner):
    t = yf.shape[0]
    tr = ROW_TILE
    return pl.pallas_call(
        _ssd_finish_body,
        out_shape=jax.ShapeDtypeStruct((t, inner), BF16),
        grid=(t // tr,),
        in_specs=[pl.BlockSpec((tr, inner), lambda i: (i, 0)),
                  pl.BlockSpec((tr, inner), lambda i: (i, 0)),
                  pl.BlockSpec((tr, inner), lambda i: (i, xs_col // inner)),
                  pl.BlockSpec((tr, inner), lambda i: (i, z_col // inner)),
                  pl.BlockSpec((1, inner), lambda i: (0, 0)),
                  pl.BlockSpec((1, inner), lambda i: (0, 0))],
        out_specs=pl.BlockSpec((tr, inner), lambda i: (i, 0)),
        compiler_params=_cp("parallel"),
        name="ssd_finish",
    )(yf, yb, u, proj, d_full.reshape(1, inner), norm_w.reshape(1, inner))


KV_PER_STEP = LANES // HEAD_DIM


_NT = (((1,), (1,)), ((), ()))


def _attend(scores, values, sink):
    m = sink
    for s in scores:
        m = jnp.maximum(m, jnp.max(s, axis=-1, keepdims=True))
    denom = jnp.exp(sink - m)
    acc = None
    for s, v in zip(scores, values):
        e = jnp.exp(s - m)
        denom = denom + jnp.sum(e, axis=-1, keepdims=True)
        part = jnp.dot(e.astype(BF16), v, preferred_element_type=F32)
        acc = part if acc is None else acc + part
    return acc * (1.0 / denom)


def _stack_heads(q, first_head, n, rows):
    return jnp.concatenate([q[:, (first_head + g) * HEAD_DIM:(first_head + g + 1) * HEAD_DIM] for g in range(n)],
                           axis=0)


def _stacked_sinks(sink_ref, first_head, n, rows):
    row = lax.broadcasted_iota(jnp.int32, (n * rows, 1), 0)
    col = jnp.full((n * rows, 1), sink_ref[first_head + n - 1], F32)
    for g in reversed(range(n - 1)):
        col = jnp.where(row < (g + 1) * rows, sink_ref[first_head + g], col)
    return col


def _ctx_attn_body(sink_ref, q_ref, k_ref, v_ref, fill_ref, o_ref, *, q_per_kv, scale):
    del fill_ref
    hb = pl.program_id(1)
    rows = q_ref.shape[0]
    q_all = q_ref[...] * scale
    outs = []
    for kv in range(KV_PER_STEP):
        k = k_ref[:, kv * HEAD_DIM:(kv + 1) * HEAD_DIM].astype(BF16)
        v = v_ref[:, kv * HEAD_DIM:(kv + 1) * HEAD_DIM].astype(BF16)
        qs = _stack_heads(q_all, kv * q_per_kv, q_per_kv, rows).astype(BF16)
        s = lax.dot_general(qs, k, _NT, preferred_element_type=F32)
        sink = _stacked_sinks(sink_ref, (hb * KV_PER_STEP + kv) * q_per_kv, q_per_kv, rows)
        o = _attend([s], [v], sink)
        outs += [o[g * rows:(g + 1) * rows] for g in range(q_per_kv)]
    o_ref[...] = jnp.concatenate(outs, axis=1).astype(o_ref.dtype)


def context_attention(qkv, sink, *, nseq, L, n_heads):
    q_per_kv = n_heads // N_KV_HEADS
    qw = KV_PER_STEP * q_per_kv * HEAD_DIM
    nhb = N_KV_HEADS // KV_PER_STEP
    kcb = n_heads * HEAD_DIM // LANES
    vcb = kcb + N_KV_HEADS * HEAD_DIM // LANES
    return pl.pallas_call(
        functools.partial(_ctx_attn_body, q_per_kv=q_per_kv, scale=HEAD_DIM ** -0.5),
        out_shape=jax.ShapeDtypeStruct((qkv.shape[0], n_heads * HEAD_DIM), BF16),
        grid=(nseq, nhb),
        in_specs=[pl.BlockSpec(memory_space=pltpu.SMEM),
                  pl.BlockSpec((L, qw), lambda b, h: (b, h)),
                  pl.BlockSpec((L, LANES), lambda b, h: (b, kcb + h)),
                  pl.BlockSpec((L, LANES), lambda b, h: (b, vcb + h)),
                  pl.BlockSpec(memory_space=pl.ANY)],
        out_specs=pl.BlockSpec((L, qw), lambda b, h: (b, h)),
        input_output_aliases={4: 0},
        compiler_params=_cp("parallel", "parallel"),
        name="context_attention",
    )(sink, qkv, qkv, qkv, zeros_2d(qkv.shape[0], n_heads * HEAD_DIM, BF16))


def _rope(x, cos, sin):
    outs = []
    lane = lax.broadcasted_iota(jnp.int32, cos.shape, 1)
    first = (lane % (HEAD_DIM // 2)) < (HEAD_DIM // 4)
    for cgrp in range(x.shape[1] // LANES):
        xc = x[:, cgrp * LANES:(cgrp + 1) * LANES]
        partner = jnp.where(first, pltpu.roll(xc, LANES - HEAD_DIM // 4, 1), pltpu.roll(xc, HEAD_DIM // 4, 1))
        outs.append(xc * cos + partner * sin)
    return outs[0] if len(outs) == 1 else jnp.concatenate(outs, axis=1)


def _lat_attn_body(sink_ref, q_ref, k_ref, v_ref, kc_ref, vc_ref, cosq_ref, sinq_ref, cos_ref, sin_ref, fill_ref,
                   o_ref, k_scr, v_scr, kc_scr, vc_scr, *, q_per_kv, scale, L, blk):
    del fill_ref
    hb = pl.program_id(1)
    i = pl.program_id(2)

    @pl.when(i == 0)
    def _():
        k_scr[...] = _rope(k_ref[...], cos_ref[...], sin_ref[...]).astype(BF16)
        v_scr[...] = v_ref[...].astype(BF16)
        kc_scr[...] = kc_ref[...].astype(BF16)
        vc_scr[...] = vc_ref[...].astype(BF16)

    n_loc = 3 * blk
    start = jnp.clip((i - 1) * blk, 0, L - n_loc)
    start = pl.multiple_of(start, blk)
    q_all = _rope(q_ref[...], cosq_ref[...], sinq_ref[...]) * scale
    k_loc = k_scr[pl.ds(start, n_loc), :]
    v_loc = v_scr[pl.ds(start, n_loc), :]
    rows = q_per_kv * blk
    qpos = i * blk + (lax.broadcasted_iota(jnp.int32, (rows, n_loc), 0) & (blk - 1))
    kpos = start + lax.broadcasted_iota(jnp.int32, (rows, n_loc), 1)
    ok = jnp.abs(qpos - kpos) <= WINDOW
    outs = []
    for kv in range(KV_PER_STEP):
        sl = slice(kv * HEAD_DIM, (kv + 1) * HEAD_DIM)
        qs = _stack_heads(q_all, kv * q_per_kv, q_per_kv, blk).astype(BF16)
        s_loc = jnp.where(ok, lax.dot_general(qs, k_loc[:, sl], _NT, preferred_element_type=F32), -jnp.inf)
        s_ctx = lax.dot_general(qs, kc_scr[:, sl], _NT, preferred_element_type=F32)
        sink = _stacked_sinks(sink_ref, (hb * KV_PER_STEP + kv) * q_per_kv, q_per_kv, blk)
        o = _attend([s_loc, s_ctx], [v_loc[:, sl], vc_scr[:, sl]], sink)
        outs += [o[g * blk:(g + 1) * blk] for g in range(q_per_kv)]
    o_ref[...] = jnp.concatenate(outs, axis=1).astype(o_ref.dtype)


def rope_tables(L):
    rows = L // GRID_W
    row = jnp.repeat(jnp.arange(rows, dtype=F32), GRID_W)
    col = jnp.tile(jnp.arange(GRID_W, dtype=F32), rows)
    quarter = HEAD_DIM // 4
    inv = ROPE_BASE ** (-jnp.arange(quarter, dtype=F32) / quarter)
    ang_r = row[:, None] * inv[None, :]
    ang_c = col[:, None] * inv[None, :]
    ang = jnp.concatenate([ang_r, ang_r, ang_c, ang_c], axis=1)
    sign = jnp.tile(jnp.concatenate([-jnp.ones((quarter,), F32), jnp.ones((quarter,), F32)]), 2)
    cos = jnp.tile(jnp.cos(ang), (1, LANES // HEAD_DIM))
    sin = jnp.tile(jnp.sin(ang) * sign[None, :], (1, LANES // HEAD_DIM))
    return cos, sin


def latent_attention(qkv, k_cache, v_cache, sink, fill, *, row0, nseq, L, n_heads):
    blk = WINDOW
    q_per_kv = n_heads // N_KV_HEADS
    qw = KV_PER_STEP * q_per_kv * HEAD_DIM
    nhb = N_KV_HEADS // KV_PER_STEP
    kcb = n_heads * HEAD_DIM // LANES
    vcb = kcb + N_KV_HEADS * HEAD_DIM // LANES
    nb = L // blk
    past = k_cache.shape[1]
    cos, sin = rope_tables(L)
    return pl.pallas_call(
        functools.partial(_lat_attn_body, q_per_kv=q_per_kv, scale=HEAD_DIM ** -0.5, L=L, blk=blk),
        out_shape=jax.ShapeDtypeStruct(fill.shape, BF16),
        grid=(nseq, nhb, nb),
        in_specs=[pl.BlockSpec(memory_space=pltpu.SMEM),
                  pl.BlockSpec((blk, qw), lambda b, h, i: (row0 // blk + b * nb + i, h)),
                  pl.BlockSpec((L, LANES), lambda b, h, i: (row0 // L + b, kcb + h)),
                  pl.BlockSpec((L, LANES), lambda b, h, i: (row0 // L + b, vcb + h)),
                  pl.BlockSpec((None, past, LANES), lambda b, h, i: (b, 0, h)),
                  pl.BlockSpec((None, past, LANES), lambda b, h, i: (b, 0, h)),
                  pl.BlockSpec((blk, LANES), lambda b, h, i: (i, 0)),
                  pl.BlockSpec((blk, LANES), lambda b, h, i: (i, 0)),
                  pl.BlockSpec((L, LANES), lambda b, h, i: (0, 0)),
                  pl.BlockSpec((L, LANES), lambda b, h, i: (0, 0)),
                  pl.BlockSpec(memory_space=pl.ANY)],
        out_specs=pl.BlockSpec((blk, qw), lambda b, h, i: (row0 // blk + b * nb + i, h)),
        scratch_shapes=[pltpu.VMEM((L, LANES), BF16), pltpu.VMEM((L, LANES), BF16),
                        pltpu.VMEM((past, LANES), BF16), pltpu.VMEM((past, LANES), BF16)],
        input_output_aliases={10: 0},
        compiler_params=_cp("parallel", "parallel", "arbitrary"),
        name="latent_attention",
    )(sink, qkv, qkv, qkv, k_cache, v_cache, cos, sin, cos, sin, fill)


MOE_TILE = 256
MOE_FFN_VMEM_BYTES = 56 * 1024 * 1024


def _moe_dispatch_body(pos_ref, h_ref, g_ref, init_hbm, xs_hbm, buf, sem):
    del init_hbm
    i = pl.program_id(0)
    tr, d = h_ref.shape
    buf[:, :d] = h_ref[...]
    buf[:, d:] = g_ref[...]

    def start(r2, carry):
        for prio in range(DMA_PRIORITIES):
            r = r2 * DMA_PRIORITIES + prio
            pltpu.make_async_copy(buf.at[pl.ds(r, 1)], xs_hbm.at[pl.ds(pos_ref[i * tr + r], 1)],
                                  sem).start(priority=prio)
        return carry

    lax.fori_loop(0, tr // DMA_PRIORITIES, start, 0)

    def wait(r, carry):
        pltpu.make_async_copy(buf.at[pl.ds(r, 1)], xs_hbm.at[pl.ds(0, 1)], sem).wait()
        return carry

    lax.fori_loop(0, tr, wait, 0)


def moe_dispatch(h, gate_lanes, pos, n_slots):
    t, d = h.shape
    tr = ROW_TILE
    w = d + LANES
    grid_spec = pltpu.PrefetchScalarGridSpec(
        num_scalar_prefetch=1,
        grid=(t // tr,),
        in_specs=[pl.BlockSpec((tr, d), lambda i, p: (i, 0)),
                  pl.BlockSpec((tr, LANES), lambda i, p: (i, 0)),
                  pl.BlockSpec(memory_space=pl.ANY)],
        out_specs=pl.BlockSpec(memory_space=pl.ANY),
        scratch_shapes=[pltpu.VMEM((tr, w), F32), pltpu.SemaphoreType.DMA(())])
    return pl.pallas_call(
        _moe_dispatch_body,
        out_shape=jax.ShapeDtypeStruct((n_slots, w), F32),
        grid_spec=grid_spec,
        input_output_aliases={3: 0},
        compiler_params=_cp("arbitrary"),
        name="moe_dispatch",
    )(pos, h, gate_lanes, zeros_2d(n_slots, w, F32))


def _moe_ffn_body(ea_ref, eb_ref, nused_ref, xs_ref, w1a_ref, w3a_ref, w2a_ref, w1b_ref, w3b_ref, w2b_ref, o_ref):
    i = pl.program_id(0)
    d = o_ref.shape[1]

    @pl.when(i < nused_ref[0])
    def _():
        x = xs_ref[:, :d].astype(BF16)

        def expert(w1_ref, w3_ref, w2_ref):
            a = jnp.dot(x, w1_ref[...], preferred_element_type=F32)
            b = jnp.dot(x, w3_ref[...], preferred_element_type=F32)
            return jnp.dot((_silu(a) * b).astype(BF16), w2_ref[...], preferred_element_type=F32)

        o_ref[...] = (xs_ref[:, d:d + 1] * expert(w1a_ref, w3a_ref, w2a_ref)
                      + xs_ref[:, d + 1:d + 2] * expert(w1b_ref, w3b_ref, w2b_ref))

    @pl.when(i >= nused_ref[0])
    def _():
        o_ref[...] = jnp.zeros_like(o_ref)


def moe_ffn(xs, tile_ea, tile_eb, n_used, w1, w3, w2, layer):
    _, _, d, de = w1.shape
    tm = MOE_TILE
    n_tiles = tile_ea.shape[0]
    up_a = pl.BlockSpec((None, None, d, de), lambda i, ea, eb, nu: (layer, ea[i], 0, 0))
    up_b = pl.BlockSpec((None, None, d, de), lambda i, ea, eb, nu: (layer, eb[i], 0, 0))
    grid_spec = pltpu.PrefetchScalarGridSpec(
        num_scalar_prefetch=3,
        grid=(n_tiles,),
        in_specs=[pl.BlockSpec((tm, d + LANES), lambda i, ea, eb, nu: (i, 0)),
                  up_a, up_a, pl.BlockSpec((None, None, de, d), lambda i, ea, eb, nu: (layer, ea[i], 0, 0)),
                  up_b, up_b, pl.BlockSpec((None, None, de, d), lambda i, ea, eb, nu: (layer, eb[i], 0, 0))],
        out_specs=pl.BlockSpec((tm, d), lambda i, ea, eb, nu: (i, 0)))
    return pl.pallas_call(
        _moe_ffn_body,
        out_shape=jax.ShapeDtypeStruct((n_tiles * tm, d), F32),
        grid_spec=grid_spec,
        compiler_params=_cp("arbitrary", vmem=MOE_FFN_VMEM_BYTES),
        name="moe_ffn",
    )(tile_ea, tile_eb, n_used, xs, w1, w3, w2, w1, w3, w2)


def _moe_combine_body(pos_ref, y_hbm, x_ref, g_ref, o_ref, ybuf, sem):
    i = pl.program_id(0)
    tr = x_ref.shape[0]

    def start(r2, carry):
        for prio in range(DMA_PRIORITIES):
            r = r2 * DMA_PRIORITIES + prio
            pltpu.make_async_copy(y_hbm.at[pl.ds(pos_ref[i * tr + r], 1)], ybuf.at[pl.ds(r, 1)],
                                  sem).start(priority=prio)
        return carry

    lax.fori_loop(0, tr // DMA_PRIORITIES, start, 0)

    def wait(r, carry):
        pltpu.make_async_copy(y_hbm.at[pl.ds(0, 1)], ybuf.at[pl.ds(r, 1)], sem).wait()
        return carry

    lax.fori_loop(0, tr, wait, 0)
    o_ref[...] = x_ref[...] + g_ref[...] * ybuf[...]


def moe_combine(x, y_sorted, pos, mod3, gate_idx, seg_of_tile):
    t, d = x.shape
    tr = ROW_TILE
    grid_spec = pltpu.PrefetchScalarGridSpec(
        num_scalar_prefetch=1,
        grid=(t // tr,),
        in_specs=[pl.BlockSpec(memory_space=pl.ANY),
                  pl.BlockSpec((tr, d), lambda i, p: (i, 0)),
                  pl.BlockSpec((None, 1, d), lambda i, p: (seg_of_tile(i) * 6 + gate_idx, 0, 0))],
        out_specs=pl.BlockSpec((tr, d), lambda i, p: (i, 0)),
        scratch_shapes=[pltpu.VMEM((tr, d), F32), pltpu.SemaphoreType.DMA(())])
    return pl.pallas_call(
        _moe_combine_body,
        out_shape=jax.ShapeDtypeStruct((t, d), F32),
        grid_spec=grid_spec,
        compiler_params=_cp("arbitrary"),
        name="moe_combine",
    )(pos, y_sorted, x, mod3)


def moe_route(logits, n_groups, n_experts):
    per = n_experts // n_groups
    assert TOP_K == 2
    g_prob = jax.nn.softmax(logits[:, :n_groups], axis=-1)
    g_top = jnp.max(g_prob, axis=-1, keepdims=True)
    g_idx = jnp.argmax(g_prob, axis=-1, keepdims=True).astype(jnp.int32)
    e_logits = logits[:, n_groups:n_groups + n_experts].reshape(-1, n_groups, per)
    g_onehot = g_idx == jnp.arange(n_groups)[None, :]
    e_in = jnp.sum(jnp.where(g_onehot[:, :, None], e_logits, 0.0), axis=1)
    e_prob = jax.nn.softmax(e_in, axis=-1)
    i0 = jnp.argmax(e_prob, axis=-1, keepdims=True).astype(jnp.int32)
    rest = jnp.where(jnp.arange(per)[None, :] == i0, -jnp.inf, e_prob)
    i1 = jnp.argmax(rest, axis=-1, keepdims=True).astype(jnp.int32)
    e_idx = jnp.concatenate([i0, i1], axis=1)
    e_top = jnp.concatenate([jnp.max(e_prob, axis=-1, keepdims=True), jnp.max(rest, axis=-1, keepdims=True)], axis=1)
    e_top = e_top / jnp.sum(e_top, axis=-1, keepdims=True)
    return g_idx * per + e_idx, g_top * e_top


def moe_plan(expert_ids, gates, n_groups, n_experts):
    assert TOP_K == 2
    t = expert_ids.shape[0]
    tm = MOE_TILE
    per = n_experts // n_groups
    n_pairs = per * (per - 1) // 2
    n_classes = n_groups * n_pairs
    n_tiles = t // tm + n_classes
    e0, e1 = expert_ids[:, 0], expert_ids[:, 1]
    lo, hi = jnp.minimum(e0, e1), jnp.maximum(e0, e1)
    g_lo = jnp.where(e0 < e1, gates[:, 0], gates[:, 1])
    g_hi = jnp.where(e0 < e1, gates[:, 1], gates[:, 0])
    a, b = lo % per, hi % per
    cls = (lo // per) * n_pairs + (a * (2 * per - a - 1)) // 2 + (b - a - 1)
    onehot = (cls[:, None] == jnp.arange(n_classes)[None, :]).astype(F32)
    blk = ROW_TILE
    inner = jnp.einsum("ij,bjk->bik", jnp.tril(jnp.ones((blk, blk), F32)), onehot.reshape(-1, blk, n_classes))
    bsum = inner[:, -1, :]
    running = (inner + (jnp.cumsum(bsum, axis=0) - bsum)[:, None, :]).reshape(-1, n_classes)
    counts = jnp.sum(bsum, axis=0).astype(jnp.int32)
    tiles_per = (counts + tm - 1) // tm
    tile_start = jnp.cumsum(tiles_per) - tiles_per
    slot = running - 1.0 + (tile_start * tm).astype(F32)[None, :]
    pos = jnp.sum(jnp.where(onehot > 0, slot, 0.0), axis=1).astype(jnp.int32)
    n_used = jnp.sum(tiles_per).astype(jnp.int32)
    tile_ids = jnp.arange(n_tiles, dtype=jnp.int32)
    tile_cls = jnp.sum((tile_ids[:, None] >= tile_start[None, :]).astype(jnp.int32), axis=1) - 1
    tile_cls = jnp.where(tile_ids < n_used, tile_cls, tile_cls[jnp.maximum(n_used - 1, 0)])
    pairs = [(pa, pb) for pa in range(per) for pb in range(pa + 1, per)]
    cls_lo = jnp.array([g * per + pa for g in range(n_groups) for pa, _ in pairs], jnp.int32)
    cls_hi = jnp.array([g * per + pb for g in range(n_groups) for _, pb in pairs], jnp.int32)
    gate_lanes = jnp.pad(jnp.stack([g_lo, g_hi], axis=1), ((0, 0), (0, LANES - 2)))
    tile_onehot = tile_cls[:, None] == jnp.arange(n_classes)[None, :]
    tile_lo = jnp.sum(jnp.where(tile_onehot, cls_lo[None, :], 0), axis=1)
    tile_hi = jnp.sum(jnp.where(tile_onehot, cls_hi[None, :], 0), axis=1)
    return pos, gate_lanes, tile_lo, tile_hi, n_used.reshape(1), n_tiles * tm


def kernel(x_prompt, x_sample, state_ssd_fwd, state_ssd_bwd, cache_attn_k, cache_attn_v, c, c_ctx, ada_w, ada_b, norm_mix, norm_ffn, norm_final, ev_w_in, ev_w_out, hy_conv_w, hy_conv_b, hy_f_w1, hy_f_b1, hy_f_freq1, hy_f_w2, hy_f_b2, hy_f_freq2, hy_f_w3, hy_bias, ssd_conv_w, ssd_conv_b, ssd_dt_bias, ssd_a_log, ssd_d, ssd_norm, at_wq, at_wk, at_wv, at_wo, at_sink, moe_w_group, moe_b_group, moe_w_expert, moe_b_expert, moe_w1, moe_w3, moe_w2):
    bc, lc, d = x_prompt.shape
    bl, ll, _ = x_sample.shape
    tc_rows, tl_rows = bc * lc, bl * ll
    t = tc_rows + tl_rows
    depth = ada_w.shape[0]
    n_seg = 1 + bl
    assert lc % ROW_TILE == 0 and ll % ROW_TILE == 0 and tc_rows % ll == 0
    seg_rows = math.gcd(tc_rows, ll)

    def seg_of_rows(r):
        return jnp.where(r < tc_rows, 0, 1 + (r - tc_rows) // ll)

    def seg_of_tile(i):
        return seg_of_rows(i * ROW_TILE)

    x = jnp.concatenate([x_prompt.reshape(tc_rows, d), x_sample.reshape(tl_rows, d)], axis=0)
    cvec = jnp.concatenate([c_ctx[None], c, jnp.zeros((SUBLANES - n_seg, d), F32)], axis=0)
    mod_all = ada_mod(cvec, ada_w, ada_b)

    tile_row = jnp.arange(t // ROW_TILE, dtype=jnp.int32) * ROW_TILE
    seq_len = jnp.where(tile_row < tc_rows, lc, ll)
    rel = jnp.where(tile_row < tc_rows, tile_row, tile_row - tc_rows)
    conv_flags = jnp.stack([(rel % seq_len == 0), ((rel + ROW_TILE) % seq_len == 0)]).astype(jnp.int32)

    w1_bf, w3_bf, w2_bf = moe_w1.astype(BF16), moe_w3.astype(BF16), moe_w2.astype(BF16)
    new_sf, new_sb, new_k, new_v = [], [], [], []
    for l in range(depth):
        i = l // 2
        mod3 = mod_all[l, :n_seg].reshape(n_seg * 6, 1, d)
        h = rms_norm(x, norm_mix[l], out_dtype=BF16, mod3=mod3, shift_idx=0, scale_idx=1,
                     seg_of_tile=seg_of_tile)
        if l % 2 == 0:
            hyena_w = hy_conv_w.shape[2] // 3
            inner = ssd_norm.shape[1]
            heads = inner // SSD_HEAD_DIM
            conv_ch = ssd_conv_w.shape[2]
            main_cols = 3 * hyena_w + inner + conv_ch
            w_in = ev_w_in[i].astype(BF16)
            proj = dense(h, w_in, n_cols=main_cols, out_dtype=BF16, name="in_proj")
            dt_raw = dense(h, w_in[:, main_cols:], out_dtype=F32, name="dt_proj")
            u = dwconv(proj, hy_conv_w[i], hy_conv_b[i], conv_flags, col0=0, silu=False)
            u_ssd = dwconv(proj, ssd_conv_w[i], ssd_conv_b[i], conv_flags, col0=3 * hyena_w + inner,
                           silu=True)
            y_a = zeros_2d(t, hyena_w, BF16)
            for row0, nb, L in ((0, bc, lc), (tc_rows, bl, ll)):
                filt = hyena_filters(L, hy_f_w1[i], hy_f_b1[i], hy_f_freq1[i], hy_f_w2[i], hy_f_b2[i],
                                     hy_f_freq2[i], hy_f_w3[i], hyena_w)
                if L % (16 * SPEC_BLOCK) == 0:
                    n1 = 2 * L // SPEC_BLOCK
                    t1f = fft_table("t1f", n1, SPEC_BLOCK)
                    kf = hyena_spectrum_2level(L, filt, t1f, fft_table("f2filt", n1, SPEC_BLOCK))
                    tabs = (t1f, fft_table("f2f", n1, SPEC_BLOCK), fft_table("f2i", n1, SPEC_BLOCK),
                            fft_table("t1i", n1, SPEC_BLOCK))
                    y_a = hyena_stream_2level(u, row0, nb, L, kf, tabs, hy_bias[i], hyena_w, fill=y_a)
                else:
                    a_tab = dft_table(L, "filt")
                    kf = hyena_spectrum(L, filt, a_tab)
                    y_a = hyena_stream(u, row0, nb, L, kf, a_tab, dft_table(L, "inv"), hy_bias[i], hyena_w,
                                       fill=y_a)
            gn = SSD_GROUPS * SSD_STATE
            bt_all = u_ssd[:, inner:inner + gn].T
            fac, cum_r = ssd_prep(dt_raw, dt_raw.T, ssd_dt_bias[i], ssd_a_log[i])
            zero = jnp.zeros((bc, SSD_GROUPS, SSD_STATE, inner // SSD_GROUPS), F32)
            init_f = jnp.concatenate([zero, _state_to_kernel_layout(state_ssd_fwd[:, i], SSD_GROUPS)], axis=0)
            init_b = jnp.concatenate([zero, _state_to_kernel_layout(state_ssd_bwd[:, i], SSD_GROUPS)], axis=0)
            yf, yb, sf, sb = ssd_scan(u_ssd, bt_all, fac, cum_r, init_f, init_b,
                                      seq_lens=[lc] * bc + [ll] * bl, col0=0, inner=inner)
            new_sf.append(_state_from_kernel_layout(sf[:bc], SSD_HEAD_DIM))
            new_sb.append(_state_from_kernel_layout(sb[:bc], SSD_HEAD_DIM))
            y_b = ssd_finish(yf, yb, u_ssd, proj, jnp.repeat(ssd_d[i], SSD_HEAD_DIM), ssd_norm[i],
                             xs_col=0, z_col=3 * hyena_w, inner=inner)
            w_out = ev_w_out[i].astype(BF16)
            x = dense_residual(y_a, w_out, x, mod3, 2, seg_of_rows, seg_rows, name="out_proj_a")
            x = dense_residual(y_b, w_out, x, mod3, 2, seg_of_rows, seg_rows, w_row0=hyena_w, name="out_proj_b")
        else:
            n_heads = at_wq.shape[2] // HEAD_DIM
            w_qkv = jnp.concatenate([at_wq[i], at_wk[i], at_wv[i]], axis=1).astype(BF16)
            qkv = dense(h, w_qkv, out_dtype=F32, name="qkv_proj")
            kvw = N_KV_HEADS * HEAD_DIM
            att_c = context_attention(qkv, at_sink[i], nseq=bc, L=lc, n_heads=n_heads)
            past = cache_attn_k.shape[2]
            att = latent_attention(qkv, cache_attn_k[:, i].reshape(bl, past, kvw),
                                   cache_attn_v[:, i].reshape(bl, past, kvw), at_sink[i], att_c,
                                   row0=tc_rows, nseq=bl, L=ll, n_heads=n_heads)
            x = dense_residual(att, at_wo[i].astype(BF16), x, mod3, 2, seg_of_rows, seg_rows, name="attn_out")
            qw = n_heads * HEAD_DIM
            new_k.append(qkv[:tc_rows, qw:qw + kvw].reshape(bc, lc, N_KV_HEADS, HEAD_DIM))
            new_v.append(qkv[:tc_rows, qw + kvw:].reshape(bc, lc, N_KV_HEADS, HEAD_DIM))
        h2 = rms_norm(x, norm_ffn[l], out_dtype=F32, mod3=mod3, shift_idx=3, scale_idx=4,
                      seg_of_tile=seg_of_tile)
        n_experts = moe_w_expert.shape[2]
        w_r = jnp.concatenate([moe_w_group[l], moe_w_expert[l]], axis=1)
        w_r = jnp.pad(w_r, ((0, 0), (0, LANES - w_r.shape[1]))).astype(BF16)
        b_r = jnp.pad(jnp.concatenate([moe_b_group[l], moe_b_expert[l]]), (0, LANES - MOE_GROUPS - n_experts))
        logits = dense(h2, w_r, out_dtype=F32, extra=(b_r.reshape(1, LANES),),
                       extra_specs=(pl.BlockSpec((1, LANES), lambda bt, ii, j, kk: (0, 0)),),
                       epilogue=lambda acc, b_ref, row_tile: acc + b_ref[...], name="router")
        expert_ids, gates = moe_route(logits, MOE_GROUPS, n_experts)
        pos, gate_lanes, tile_lo, tile_hi, n_used, n_slots = moe_plan(expert_ids, gates, MOE_GROUPS, n_experts)
        xs = moe_dispatch(h2, gate_lanes, pos, n_slots)
        y_sorted = moe_ffn(xs, tile_lo, tile_hi, n_used, w1_bf, w3_bf, w2_bf, l)
        x = moe_combine(x, y_sorted, pos, mod3, 5, seg_of_tile)

    y_prompt = rms_norm(x, norm_final, out_dtype=F32, rows=tc_rows).reshape(bc, lc, d)
    y_sample = rms_norm(x, norm_final, out_dtype=F32, row_off=tc_rows, rows=tl_rows).reshape(bl, ll, d)
    return (y_prompt, y_sample, jnp.stack(new_sf, axis=1), jnp.stack(new_sb, axis=1),
            jnp.stack(new_k, axis=1), jnp.stack(new_v, axis=1))
```

```python
import functools
import math

import jax
import jax.numpy as jnp
from jax import lax
from jax.experimental import pallas as pl
from jax.experimental.pallas import tpu as pltpu

F32 = jnp.float32
BF16 = jnp.bfloat16
HI = lax.Precision.HIGHEST

RMS_EPS = 1e-6
GRID_W = 64
HYENA_BANDS = 16
HYENA_FAST_DECAY = 0.3
HYENA_SLOW_DECAY = 1.5
HYENA_TARGET = 1e-2
SSD_HEAD_DIM = 64
SSD_STATE = 128
SSD_GROUPS = 8
SSD_CHUNK = 128
HEAD_DIM = 64
N_KV_HEADS = 8
WINDOW = 128
ROPE_BASE = 10000.0
MOE_GROUPS = 4
TOP_K = 2

LANES = 128
SUBLANES = 8
BF16_SUBLANES = 16
DMA_PRIORITIES = 2
VMEM_LIMIT_BYTES = 48 * 1024 * 1024

ROW_TILE = 256
SPEC_BLOCK = 128


def _cp(*sem, vmem=VMEM_LIMIT_BYTES):
    return pltpu.CompilerParams(dimension_semantics=sem, vmem_limit_bytes=vmem)


def _tile(n, pref):
    if n <= pref:
        return n
    t = pref
    while n % t:
        t //= 2
    assert t >= LANES, (n, pref)
    return t


def _silu(x):
    return x * (1.0 / (1.0 + jnp.exp(-x)))


def _softplus(x):
    return jnp.maximum(x, 0.0) + jnp.log(1.0 + jnp.exp(-jnp.abs(x)))


def _zero_body(o_ref):
    o_ref[...] = jnp.zeros_like(o_ref)


def zeros_2d(rows, cols, dtype):
    tr = _tile(rows, 1024)
    return pl.pallas_call(
        _zero_body,
        out_shape=jax.ShapeDtypeStruct((rows, cols), dtype),
        grid=(rows // tr,),
        out_specs=pl.BlockSpec((tr, cols), lambda i: (i, 0)),
        compiler_params=_cp("parallel"),
        name="zero_fill",
    )()


def _ada_body(c_ref, w_ref, b_ref, o_ref):
    a = _silu(c_ref[...])
    o_ref[...] = lax.dot_general(a, w_ref[...], (((1,), (0,)), ((), ())), precision=HI,
                                 preferred_element_type=F32) + b_ref[...]


def ada_mod(cvec, ada_w, ada_b):
    depth, d, n = ada_w.shape
    rows = cvec.shape[0]
    tn = _tile(n, 1024)
    return pl.pallas_call(
        _ada_body,
        out_shape=jax.ShapeDtypeStruct((depth, rows, n), F32),
        grid=(depth, n // tn),
        in_specs=[pl.BlockSpec((rows, d), lambda l, j: (0, 0)),
                  pl.BlockSpec((None, d, tn), lambda l, j: (l, 0, j)),
                  pl.BlockSpec((None, 1, tn), lambda l, j: (l, 0, j))],
        out_specs=pl.BlockSpec((None, rows, tn), lambda l, j: (l, 0, j)),
        compiler_params=_cp("parallel", "parallel"),
        name="ada_mod",
    )(cvec, ada_w, ada_b.reshape(depth, 1, n))


def _norm_body(x_ref, g_ref, *rest, modulate):
    o_ref = rest[-1]
    x = x_ref[...]
    y = x * lax.rsqrt(jnp.mean(x * x, axis=-1, keepdims=True) + RMS_EPS)
    y = y * g_ref[...]
    if modulate:
        sh_ref, sc_ref = rest[0], rest[1]
        y = y * (1.0 + sc_ref[...]) + sh_ref[...]
    o_ref[...] = y.astype(o_ref.dtype)


def rms_norm(x, g, *, out_dtype, mod3=None, shift_idx=0, scale_idx=0, seg_of_tile=None,
             row_off=0, rows=None):
    t, d = x.shape
    rows = t if rows is None else rows
    tr = ROW_TILE
    off = row_off // tr
    in_specs = [pl.BlockSpec((tr, d), lambda i: (i + off, 0)),
                pl.BlockSpec((1, d), lambda i: (0, 0))]
    args = [x, g.reshape(1, d)]
    if mod3 is not None:
        in_specs += [pl.BlockSpec((None, 1, d), lambda i: (seg_of_tile(i) * 6 + shift_idx, 0, 0)),
                     pl.BlockSpec((None, 1, d), lambda i: (seg_of_tile(i) * 6 + scale_idx, 0, 0))]
        args += [mod3, mod3]
    return pl.pallas_call(
        functools.partial(_norm_body, modulate=mod3 is not None),
        out_shape=jax.ShapeDtypeStruct((rows, d), out_dtype),
        grid=(rows // tr,),
        in_specs=in_specs,
        out_specs=pl.BlockSpec((tr, d), lambda i: (i, 0)),
        compiler_params=_cp("parallel"),
        name="rms_norm",
    )(*args)


def _mm_body(*refs, nk, n_extra, n_skip, epilogue):
    a_ref, b_ref = refs[0], refs[1]
    extra = refs[2:2 + n_extra]
    o_ref = refs[2 + n_extra + n_skip]
    row_tile = pl.program_id(1)
    b = b_ref[...]
    if b.ndim == 3:
        b = b.reshape(-1, b.shape[-1])
    part = jnp.dot(a_ref[...].astype(b.dtype), b, preferred_element_type=F32)
    if nk == 1:
        o_ref[...] = epilogue(part, *extra, row_tile=row_tile).astype(o_ref.dtype).reshape(o_ref.shape)
        return
    acc_ref = refs[3 + n_extra + n_skip]
    k = pl.program_id(3)

    @pl.when(k == 0)
    def _():
        acc_ref[...] = part

    @pl.when(k > 0)
    def _():
        acc_ref[...] += part

    @pl.when(k == nk - 1)
    def _():
        o_ref[...] = epilogue(acc_ref[...], *extra, row_tile=row_tile).astype(o_ref.dtype)


def matmul(a, b, *, grid, tm, tn, a_spec, b_spec, o_spec, out_shape, out_dtype,
           extra=(), extra_specs=(), epilogue=None, fill=None, name="matmul"):
    nk = grid[3]
    epilogue = epilogue or (lambda acc, row_tile: acc)
    scratch = [pltpu.VMEM((tm, tn), F32)] if nk > 1 else []
    fills = () if fill is None else (fill,)
    return pl.pallas_call(
        functools.partial(_mm_body, nk=nk, n_extra=len(extra), n_skip=len(fills), epilogue=epilogue),
        out_shape=jax.ShapeDtypeStruct(out_shape, out_dtype),
        grid=grid,
        in_specs=[a_spec, b_spec, *extra_specs, *[pl.BlockSpec(memory_space=pl.ANY) for _ in fills]],
        out_specs=o_spec,
        scratch_shapes=scratch,
        input_output_aliases={2 + len(extra): 0} if fills else {},
        compiler_params=_cp("parallel", "parallel", "parallel", "arbitrary"),
        name=name,
    )(a, b, *extra, *fills)


def dense(a, w, *, out_dtype, tm=1024, tn=512, tk=2048, extra=(), extra_specs=(), epilogue=None,
          w_row0=0, n_cols=None, name="dense"):
    m, k = a.shape
    n = w.shape[1] if n_cols is None else n_cols
    tm, tn, tk = _tile(m, tm), _tile(n, tn), _tile(k, tk)
    assert w_row0 % tk == 0
    kb0 = w_row0 // tk
    return matmul(
        a, w, grid=(1, m // tm, n // tn, k // tk), tm=tm, tn=tn,
        a_spec=pl.BlockSpec((tm, tk), lambda bt, i, j, kk: (i, kk)),
        b_spec=pl.BlockSpec((tk, tn), lambda bt, i, j, kk: (kb0 + kk, j)),
        o_spec=pl.BlockSpec((tm, tn), lambda bt, i, j, kk: (i, j)),
        out_shape=(m, n), out_dtype=out_dtype, extra=extra, extra_specs=extra_specs,
        epilogue=epilogue, name=name)


def dense_residual(a, w, x, mod3, gate_idx, seg_of_rows, seg_rows, *, tm=1024, tn=512, tk=2048, w_row0=0,
                   name="dense_res"):
    n = w.shape[1]
    tm_ = _tile(seg_rows, tm)
    tn_ = _tile(n, tn)
    return dense(
        a, w, out_dtype=F32, tm=tm_, tn=tn, tk=tk, w_row0=w_row0, extra=(x, mod3),
        extra_specs=(pl.BlockSpec((tm_, tn_), lambda bt, i, j, kk: (i, j)),
                     pl.BlockSpec((None, 1, tn_),
                                  lambda bt, i, j, kk: (seg_of_rows(i * tm_) * 6 + gate_idx, 0, j))),
        epilogue=lambda acc, x_ref, g_ref, row_tile: x_ref[...] + g_ref[...] * acc, name=name)


def _dwconv_body(flags_ref, x_ref, p_ref, n_ref, w_ref, b_ref, o_ref, *, silu):
    i = pl.program_id(0)
    act = _silu if silu else (lambda v: v)
    x = x_ref[...].astype(F32)
    tr = x.shape[0]
    w = w_ref[...]
    y = pltpu.roll(x, 1, 0) * w[0:1, :] + x * w[1:2, :] + pltpu.roll(x, tr - 1, 0) * w[2:3, :] + b_ref[...]
    o_ref[...] = act(y).astype(o_ref.dtype)
    keep_prev = (flags_ref[0, i] == 0).astype(F32)
    keep_next = (flags_ref[1, i] == 0).astype(F32)
    halo_prev = p_ref[SUBLANES - 1:SUBLANES, :].astype(F32) * keep_prev
    halo_next = n_ref[0:1, :].astype(F32) * keep_next
    slab = BF16_SUBLANES
    row = lax.broadcasted_iota(jnp.int32, (slab, x.shape[1]), 0)
    fix_top = (halo_prev - x[tr - 1:tr, :]) * w[0:1, :]
    fix_bot = (halo_next - x[0:1, :]) * w[2:3, :]
    o_ref[0:slab, :] = act(y[0:slab] + jnp.where(row == 0, fix_top, 0.0)).astype(o_ref.dtype)
    o_ref[tr - slab:tr, :] = act(y[tr - slab:tr] + jnp.where(row == slab - 1, fix_bot, 0.0)).astype(o_ref.dtype)


def dwconv(proj, w, b, flags, *, col0, silu):
    t = proj.shape[0]
    cols = w.shape[1]
    tr = ROW_TILE
    tc = _tile(math.gcd(col0, cols), 2048)
    cb0 = col0 // tc
    hb = tr // SUBLANES
    last_hb = t // SUBLANES - 1
    grid_spec = pltpu.PrefetchScalarGridSpec(
        num_scalar_prefetch=1,
        grid=(t // tr, cols // tc),
        in_specs=[pl.BlockSpec((tr, tc), lambda i, j, f: (i, cb0 + j)),
                  pl.BlockSpec((SUBLANES, tc), lambda i, j, f: (jnp.maximum(i * hb - 1, 0), cb0 + j)),
                  pl.BlockSpec((SUBLANES, tc), lambda i, j, f: (jnp.minimum((i + 1) * hb, last_hb), cb0 + j)),
                  pl.BlockSpec((3, tc), lambda i, j, f: (0, j)),
                  pl.BlockSpec((1, tc), lambda i, j, f: (0, j))],
        out_specs=pl.BlockSpec((tr, tc), lambda i, j, f: (i, j)))
    return pl.pallas_call(
        functools.partial(_dwconv_body, silu=silu),
        out_shape=jax.ShapeDtypeStruct((t, cols), BF16),
        grid_spec=grid_spec,
        compiler_params=_cp("parallel", "parallel"),
        name="dwconv_silu" if silu else "dwconv",
    )(flags, proj, proj, proj, w, b.reshape(1, -1))


def _filter_body(feat_ref, w1_ref, b1_ref, f1_ref, w2_ref, b2_ref, f2_ref, w3_ref, dl_ref, o_ref):
    d = pl.program_id(0)
    i = pl.program_id(1)
    feat = feat_ref[...]
    dn = (((1,), (0,)), ((), ()))
    hid = jnp.sin(f1_ref[...] * (lax.dot_general(feat, w1_ref[...], dn, precision=HI,
                                                 preferred_element_type=F32) + b1_ref[...]))
    hid = jnp.sin(f2_ref[...] * (lax.dot_general(hid, w2_ref[...], dn, precision=HI,
                                                 preferred_element_type=F32) + b2_ref[...]))
    filt = jnp.dot(hid.astype(BF16), w3_ref[...].astype(BF16), preferred_element_type=F32)
    t01 = feat[:, 0:1]
    filt = filt * jnp.exp(-t01 * dl_ref[...])
    row = lax.broadcasted_iota(jnp.int32, filt.shape, 0) + i * filt.shape[0]
    filt = jnp.where((d == 1) & (row == 0), 0.0, filt)
    o_ref[...] = filt.astype(o_ref.dtype)


def hyena_filters(L, f_w1, f_b1, f_freq1, f_w2, f_b2, f_freq2, f_w3, c):
    t = jnp.arange(L, dtype=F32)
    t01 = t / (L - 1)
    bands = jnp.linspace(1e-4, HYENA_BANDS - 1, HYENA_BANDS, dtype=F32)
    ang = (2.0 * math.pi / L) * t[:, None] * bands[None, :]
    feat = jnp.concatenate([t01[:, None], jnp.cos(ang), -jnp.sin(ang)], axis=-1)
    emb = feat.shape[1]
    feat = jnp.pad(feat, ((0, 0), (0, LANES - emb)))
    w1 = jnp.pad(f_w1, ((0, LANES - emb), (0, 0)))
    ffn = f_w1.shape[1]
    order = f_w3.shape[1] // (2 * c)
    deltas = jnp.abs(jnp.linspace(math.log(HYENA_TARGET) / HYENA_SLOW_DECAY,
                                  math.log(HYENA_TARGET) / HYENA_FAST_DECAY, c, dtype=F32))
    deltas = jnp.tile(deltas, order).reshape(1, order * c)
    tl, tn = _tile(L, 512), _tile(order * c, 4096)
    nj = order * c // tn
    return pl.pallas_call(
        _filter_body,
        out_shape=jax.ShapeDtypeStruct((2, L, order * c), BF16),
        grid=(2, L // tl, nj),
        in_specs=[pl.BlockSpec((tl, LANES), lambda d, i, j: (i, 0)),
                  pl.BlockSpec((LANES, ffn), lambda d, i, j: (0, 0)),
                  pl.BlockSpec((1, ffn), lambda d, i, j: (0, 0)),
                  pl.BlockSpec((1, ffn), lambda d, i, j: (0, 0)),
                  pl.BlockSpec((ffn, ffn), lambda d, i, j: (0, 0)),
                  pl.BlockSpec((1, ffn), lambda d, i, j: (0, 0)),
                  pl.BlockSpec((1, ffn), lambda d, i, j: (0, 0)),
                  pl.BlockSpec((ffn, tn), lambda d, i, j: (0, d * nj + j)),
                  pl.BlockSpec((1, tn), lambda d, i, j: (0, j))],
        out_specs=pl.BlockSpec((None, tl, tn), lambda d, i, j: (d, i, j)),
        compiler_params=_cp("parallel", "parallel", "parallel"),
        name="hyena_filters",
    )(feat, w1, f_b1.reshape(1, ffn), f_freq1.reshape(1, ffn), f_w2, f_b2.reshape(1, ffn),
      f_freq2.reshape(1, ffn), f_w3, deltas)


def _cos_turns(m, log2_period):
    period = 1 << log2_period
    quarter = period // 4
    mm = m + quarter // 2
    q = (mm >> (log2_period - 2)) & 3
    r = (mm & (quarter - 1)) - quarter // 2
    phi = r.astype(F32) * (2.0 * math.pi / period)
    p2 = phi * phi
    cosv = 1.0 + p2 * (-1.0 / 2 + p2 * (1.0 / 24 + p2 * (-1.0 / 720 + p2 * (1.0 / 40320 - p2 / 3628800))))
    sinv = phi * (1.0 + p2 * (-1.0 / 6 + p2 * (1.0 / 120 + p2 * (-1.0 / 5040 + p2 / 362880))))
    return jnp.where(q == 0, cosv, jnp.where(q == 1, -sinv, jnp.where(q == 2, -cosv, sinv)))


def _spec_index(r, L):
    blk = r >> 8
    within = r & (2 * SPEC_BLOCK - 1)
    return blk * SPEC_BLOCK + (within & (SPEC_BLOCK - 1)), within >= SPEC_BLOCK


def _dft_body(o_ref, *, L, mode):
    log2n = int(math.log2(2 * L))
    shape = o_ref.shape
    i0 = pl.program_id(0) * shape[0]
    j0 = pl.program_id(1) * shape[1]
    rows = lax.broadcasted_iota(jnp.int32, shape, 0) + i0
    cols = lax.broadcasted_iota(jnp.int32, shape, 1) + j0
    if mode == "inv":
        r, n = cols, rows
    else:
        r, n = rows, cols
    k, im = _spec_index(r, L)
    back = n >= L
    n = jnp.where(back, n - L, n)
    phase = jnp.where(im, jnp.where(back, -(L // 2), L // 2), 0)
    val = _cos_turns((k * n + phase) & (2 * L - 1), log2n)
    nyq = jnp.where((n & 1) == 0, 1.0, -1.0)
    val = jnp.where(im & (k == 0), nyq, val)
    if mode == "inv":
        val = val * jnp.where(k == 0, 1.0 / (2 * L), 2.0 / (2 * L))
    o_ref[...] = val.astype(o_ref.dtype)


def dft_table(L, mode):
    assert L & (L - 1) == 0 and L >= 2 * SPEC_BLOCK
    shape = {"fwd": (2 * L, L), "filt": (2 * L, 2 * L), "inv": (L, 2 * L)}[mode]
    tr, tc = _tile(shape[0], 512), _tile(shape[1], 1024)
    return pl.pallas_call(
        functools.partial(_dft_body, L=L, mode=mode),
        out_shape=jax.ShapeDtypeStruct(shape, BF16),
        grid=(shape[0] // tr, shape[1] // tc),
        out_specs=pl.BlockSpec((tr, tc), lambda i, j: (i, j)),
        compiler_params=_cp("parallel", "parallel"),
        name="dft_table_" + mode,
    )()


def _spectral_epilogue(acc, kf_ref, row_tile):
    tm, tn = acc.shape
    nb = tm // (2 * SPEC_BLOCK)
    z = acc.reshape(nb, 2, SPEC_BLOCK, tn)
    kf = kf_ref[...].reshape(nb, 2, SPEC_BLOCK, tn)
    zr, zi, kr, ki = z[:, 0], z[:, 1], kf[:, 0], kf[:, 1]
    first = row_tile == 0
    blk = lax.broadcasted_iota(jnp.int32, zr.shape, 0)
    row = lax.broadcasted_iota(jnp.int32, zr.shape, 1)
    dc = first & (blk == 0) & (row == 0)
    pr = zr * kr - jnp.where(dc, 0.0, zi * ki)
    pi = jnp.where(dc, zi * ki, zr * ki + zi * kr)
    return jnp.stack([pr, pi], axis=1).reshape(tm, tn)


def hyena_stream(u, row0, nb, L, kf, f_tab, g_tab, hy_bias, c, fill=None):
    out_rows = u.shape[0]
    tn = _tile(c, 2048)
    cb = c // tn
    tm_f = _tile(2 * L, 512)
    tk_f = _tile(L, 2048)
    tm_i = _tile(L, 512)
    tk_i = _tile(2 * L, 2048)
    z, z_rows, z_col = u, row0, 2 * cb
    for o in range(2):
        zrb = z_rows // tk_f
        p = matmul(
            f_tab, z, grid=(nb, 2 * L // tm_f, cb, L // tk_f), tm=tm_f, tn=tn,
            a_spec=pl.BlockSpec((tm_f, tk_f), lambda bt, i, j, k: (i, k)),
            b_spec=pl.BlockSpec((tk_f, tn), lambda bt, i, j, k, zrb=zrb, zc=z_col:
                                (zrb + bt * (L // tk_f) + k, zc + j)),
            o_spec=pl.BlockSpec((None, tm_f, tn), lambda bt, i, j, k: (bt, i, j)),
            out_shape=(nb, 2 * L, c), out_dtype=BF16,
            extra=(kf,), extra_specs=(pl.BlockSpec((tm_f, tn), lambda bt, i, j, k, o=o: (i, o * cb + j)),),
            epilogue=_spectral_epilogue, name="hyena_fwd")
        zrb_i = z_rows // tm_i
        urb_i = row0 // tm_i
        last = o == 1
        orb = urb_i if last else 0
        z_new = matmul(
            g_tab, p, grid=(nb, L // tm_i, cb, 2 * L // tk_i), tm=tm_i, tn=tn,
            a_spec=pl.BlockSpec((tm_i, tk_i), lambda bt, i, j, k: (i, k)),
            b_spec=pl.BlockSpec((None, tk_i, tn), lambda bt, i, j, k: (bt, k, j)),
            o_spec=pl.BlockSpec((tm_i, tn), lambda bt, i, j, k, orb=orb: (orb + bt * (L // tm_i) + i, j)),
            out_shape=fill.shape if last else (nb * L, c), out_dtype=BF16, fill=fill if last else None,
            extra=(u, z, hy_bias.reshape(2, 1, c)),
            extra_specs=(pl.BlockSpec((tm_i, tn), lambda bt, i, j, k, o=o:
                                      (urb_i + bt * (L // tm_i) + i, o * cb + j)),
                         pl.BlockSpec((tm_i, tn), lambda bt, i, j, k, zrb_i=zrb_i, zc=z_col:
                                      (zrb_i + bt * (L // tm_i) + i, zc + j)),
                         pl.BlockSpec((None, 1, tn), lambda bt, i, j, k, o=o: (o, 0, j))),
            epilogue=lambda acc, g_ref, z_ref, b_ref, row_tile:
                g_ref[...].astype(F32) * (acc + b_ref[...] * z_ref[...].astype(F32)),
            name="hyena_inv")
        z, z_rows, z_col = z_new, 0, 0
    return z


def hyena_spectrum(L, filt, a_tab):
    n = filt.shape[2]
    b = filt.reshape(2 * L, n)
    tm, tn, tk = _tile(2 * L, 512), _tile(n, 512), _tile(2 * L, 2048)
    return matmul(
        a_tab, b, grid=(1, 2 * L // tm, n // tn, 2 * L // tk), tm=tm, tn=tn,
        a_spec=pl.BlockSpec((tm, tk), lambda bt, i, j, k: (i, k)),
        b_spec=pl.BlockSpec((tk, tn), lambda bt, i, j, k: (k, j)),
        o_spec=pl.BlockSpec((tm, tn), lambda bt, i, j, k: (i, j)),
        out_shape=(2 * L, n), out_dtype=F32, name="hyena_spectrum")


def _fft_tab_body(o_ref, *, kind, n1, n2, k1_len):
    shape = o_ref.shape
    r = lax.broadcasted_iota(jnp.int32, shape, 0) + pl.program_id(0) * shape[0]
    c = lax.broadcasted_iota(jnp.int32, shape, 1)
    n = n1 * n2
    lg = lambda v: int(math.log2(v))
    s8 = BF16_SUBLANES
    if kind in ("t1f", "t1i"):
        if kind == "t1f":
            grp, kp, s, n1i, s2 = r >> lg(2 * n1 * s8), (r >> lg(s8)) & (2 * n1 - 1), r & (s8 - 1), c >> lg(s8), c & (s8 - 1)
        else:
            grp, n1i, s, kp, s2 = r >> lg(k1_len * s8), (r >> lg(s8)) & (k1_len - 1), r & (s8 - 1), c >> lg(s8), c & (s8 - 1)
        t = n2 * n1i + grp * s8 + s
        val = _cos_turns(((kp & (n1 - 1)) * t + (kp >> lg(n1)) * (n // 4)) & (n - 1), lg(n))
        val = jnp.where(s == s2, val * (1.0 / n if kind == "t1i" else 1.0), 0.0)
    else:
        rp = r >= n2
        cblk = c >> lg(n2)
        cp = (cblk & 1) == 1
        sin_sign = -1 if kind == "f2i" else 1
        quarter = jnp.where(rp == cp, 0, jnp.where(cp, -sin_sign, sin_sign))
        val = _cos_turns(((r & (n2 - 1)) * (c & (n2 - 1)) + quarter * (n2 // 4)) & (n2 - 1), lg(n2))
        if kind == "f2filt":
            val = jnp.where((cblk >= 2) & rp, -val, val)
    o_ref[...] = val.astype(o_ref.dtype)


def fft_table(kind, n1, n2):
    k1_len = n1 // 2
    s8 = BF16_SUBLANES
    groups = n2 // s8
    shape = {"t1f": (groups * 2 * n1 * s8, k1_len * s8), "t1i": (groups * k1_len * s8, 2 * n1 * s8),
             "f2f": (2 * n2, 2 * n2), "f2i": (2 * n2, 2 * n2), "f2filt": (2 * n2, 4 * n2)}[kind]
    tr = _tile(shape[0], 1024)
    out = pl.pallas_call(
        functools.partial(_fft_tab_body, kind=kind, n1=n1, n2=n2, k1_len=k1_len),
        out_shape=jax.ShapeDtypeStruct(shape, BF16),
        grid=(shape[0] // tr,),
        out_specs=pl.BlockSpec((tr, shape[1]), lambda i: (i, 0)),
        compiler_params=_cp("parallel"),
        name="fft_table_" + kind,
    )()
    if kind in ("t1f", "t1i"):
        return out.reshape(groups, shape[0] // groups, shape[1])
    return out


FFT_K1_PER_STEP = 4


def _fft_mid_body(f_ref, *refs, mode, n2):
    f = f_ref[...]
    if mode == "inv":
        p_ref, o_ref = refs
        for k in range(FFT_K1_PER_STEP):
            res = jnp.dot(f, p_ref[k * 2 * n2:(k + 1) * 2 * n2, :], preferred_element_type=F32)
            o_ref[0, k * n2:(k + 1) * n2, :] = res[:n2].astype(o_ref.dtype)
            o_ref[1, k * n2:(k + 1) * n2, :] = res[n2:].astype(o_ref.dtype)
        return
    if mode == "conv":
        finv_ref, y_ref, kf_ref, o_ref = refs
        srcs = (y_ref,)
    elif mode == "fwd":
        y_ref, kf_ref, o_ref = refs
        srcs = (y_ref,)
    else:
        y0_ref, y1_ref, o_ref = refs
        srcs = (y0_ref, y1_ref)
    for k in range(FFT_K1_PER_STEP):
        rows = slice(k * n2, (k + 1) * n2)
        stack = jnp.concatenate([s[part, rows, :] for s in srcs for part in range(2)], axis=0)
        acc = jnp.dot(f, stack, preferred_element_type=F32)
        out_rows = slice(k * 2 * n2, (k + 1) * 2 * n2)
        if mode in ("fwd", "conv"):
            zr, zi = acc[:n2], acc[n2:]
            kr = kf_ref[k * 2 * n2:k * 2 * n2 + n2, :].astype(F32)
            ki = kf_ref[k * 2 * n2 + n2:(k + 1) * 2 * n2, :].astype(F32)
            acc = jnp.concatenate([zr * kr - zi * ki, zr * ki + zi * kr], axis=0)
        if mode == "conv":
            res = jnp.dot(finv_ref[...], acc.astype(BF16), preferred_element_type=F32)
            o_ref[0, rows, :] = res[:n2].astype(o_ref.dtype)
            o_ref[1, rows, :] = res[n2:].astype(o_ref.dtype)
        else:
            o_ref[out_rows, :] = acc.astype(o_ref.dtype)


def fft_mid(mode, f_tab, src, n1, n2, *, kf=None, kf_col0=0, f_inv=None, out_dtype=BF16, name):
    kb = FFT_K1_PER_STEP
    c = src.shape[-1]
    tn = _tile(c, 1024)
    cb = c // tn
    nb = 1 if mode == "filt" else src.shape[0]
    y_block = (None, 2, kb * n2, tn)
    p_block = (None, kb * 2 * n2, tn)
    f_spec = pl.BlockSpec(f_tab.shape, lambda i, b, j: (0, 0))
    if mode == "conv":
        in_specs = [f_spec, pl.BlockSpec(f_inv.shape, lambda i, b, j: (0, 0)),
                    pl.BlockSpec(y_block, lambda i, b, j: (b, 0, i, j)),
                    pl.BlockSpec((kb * 2 * n2, tn), lambda i, b, j: (i, kf_col0 // tn + j))]
        args = (f_tab, f_inv, src, kf)
        out_shape, out_spec = (nb, 2, n1 * n2, c), pl.BlockSpec(y_block, lambda i, b, j: (b, 0, i, j))
    elif mode == "fwd":
        in_specs = [f_spec, pl.BlockSpec(y_block, lambda i, b, j: (b, 0, i, j)),
                    pl.BlockSpec((kb * 2 * n2, tn), lambda i, b, j: (i, kf_col0 // tn + j))]
        args = (f_tab, src, kf)
        out_shape, out_spec = (nb, n1 * 2 * n2, c), pl.BlockSpec(p_block, lambda i, b, j: (b, i, j))
    elif mode == "filt":
        in_specs = [f_spec, pl.BlockSpec(y_block, lambda i, b, j: (0, 0, i, j)),
                    pl.BlockSpec(y_block, lambda i, b, j: (1, 0, i, j))]
        args = (f_tab, src, src)
        out_shape, out_spec = (n1 * 2 * n2, c), pl.BlockSpec((kb * 2 * n2, tn), lambda i, b, j: (i, j))
    else:
        in_specs = [f_spec, pl.BlockSpec(p_block, lambda i, b, j: (b, i, j))]
        args = (f_tab, src)
        out_shape, out_spec = (nb, 2, n1 * n2, c), pl.BlockSpec(y_block, lambda i, b, j: (b, 0, i, j))
    return pl.pallas_call(
        functools.partial(_fft_mid_body, mode=mode, n2=n2),
        out_shape=jax.ShapeDtypeStruct(out_shape, out_dtype),
        grid=(n1 // kb, nb, cb),
        in_specs=in_specs,
        out_specs=out_spec,
        compiler_params=_cp("parallel", "parallel", "parallel"),
        name=name,
    )(*args)


def hyena_spectrum_2level(L, filt, t1f, f2filt):
    n2 = SPEC_BLOCK
    n1 = 2 * L // n2
    k1 = n1 // 2
    n = filt.shape[2]
    tn = _tile(n, 1024)
    nj = n // tn
    sl = BF16_SUBLANES
    y1 = matmul(
        t1f, filt.reshape(2 * k1, n2, n), grid=(2, n2 // sl, nj, 1), tm=2 * n1 * sl, tn=tn,
        a_spec=pl.BlockSpec((None, 2 * n1 * sl, k1 * sl), lambda bt, i, j, k: (i, 0, 0)),
        b_spec=pl.BlockSpec((k1, sl, tn), lambda bt, i, j, k: (bt, i, j)),
        o_spec=pl.BlockSpec((None, 2 * n1, sl, tn), lambda bt, i, j, k: (bt, 0, i, j)),
        out_shape=(2, 2 * n1, n2, n), out_dtype=BF16, name="hyena_filt_fft1")
    return fft_mid("filt", f2filt, y1.reshape(2, 2, n1 * n2, n), n1, n2, name="hyena_filt_fft2")


def hyena_stream_2level(u, row0, nb, L, kf, tabs, hy_bias, c, fill=None):
    t1f, f2f, f2i, t1i = tabs
    n2 = SPEC_BLOCK
    n1 = 2 * L // n2
    k1 = n1 // 2
    sl = BF16_SUBLANES
    ng = n2 // sl
    t1n = _tile(c, 1024)
    c1b = c // t1n
    tn = _tile(c, 2048)
    cb = c // tn
    t, u_w = u.shape
    assert row0 % L == 0 and u_w % tn == 0
    uv = u.reshape(t // n2, n2, u_w)
    urow = row0 // L
    zv, zr, zc = uv, urow, 2 * c // t1n
    bias3 = hy_bias.reshape(2, 1, c)
    for o in range(2):
        y1 = matmul(
            t1f, zv, grid=(nb, ng, c1b, 1), tm=2 * n1 * sl, tn=t1n,
            a_spec=pl.BlockSpec((None, 2 * n1 * sl, k1 * sl), lambda bt, i, j, k: (i, 0, 0)),
            b_spec=pl.BlockSpec((k1, sl, t1n), lambda bt, i, j, k, zr=zr, zc=zc: (zr + bt, i, zc + j)),
            o_spec=pl.BlockSpec((None, 2 * n1, sl, t1n), lambda bt, i, j, k: (bt, 0, i, j)),
            out_shape=(nb, 2 * n1, n2, c), out_dtype=BF16, name="hyena_fft1")
        u1 = fft_mid("conv", f2f, y1.reshape(nb, 2, n1 * n2, c), n1, n2, kf=kf, kf_col0=o * c, f_inv=f2i,
                     name="hyena_fft2_ifft1")
        last = o == 1
        orow = urow if last else 0

        def gate_epilogue(acc, g_ref, z_ref, b_ref, row_tile):
            g = g_ref[...].reshape(acc.shape).astype(F32)
            return g * (acc + b_ref[...] * z_ref[...].reshape(acc.shape).astype(F32))

        z = matmul(
            t1i, u1.reshape(nb, 2 * n1, n2, c), grid=(nb, ng, c1b, 1), tm=k1 * sl, tn=t1n,
            a_spec=pl.BlockSpec((None, k1 * sl, 2 * n1 * sl), lambda bt, i, j, k: (i, 0, 0)),
            b_spec=pl.BlockSpec((None, 2 * n1, sl, t1n), lambda bt, i, j, k: (bt, 0, i, j)),
            o_spec=pl.BlockSpec((k1, sl, t1n), lambda bt, i, j, k, orow=orow: (orow + bt, i, j)),
            out_shape=(t // n2, n2, fill.shape[1]) if last else (nb * k1, n2, c), out_dtype=BF16,
            fill=fill.reshape(t // n2, n2, fill.shape[1]) if last else None,
            extra=(uv, zv, bias3),
            extra_specs=(pl.BlockSpec((k1, sl, t1n), lambda bt, i, j, k, o=o: (urow + bt, i, o * c1b + j)),
                         pl.BlockSpec((k1, sl, t1n), lambda bt, i, j, k, zr=zr, zc=zc: (zr + bt, i, zc + j)),
                         pl.BlockSpec((None, 1, t1n), lambda bt, i, j, k, o=o: (o, 0, j))),
            epilogue=gate_epilogue, name="hyena_ifft2")
        zv, zr, zc = z, 0, 0
    return zv.reshape(t, fill.shape[1])


SSD_CUM_TERMS = 3


def _ssd_prep_body(dc_ref, dr_ref, biasc_ref, biasr_ref, ac_ref, ar_ref, fac_ref, cumr_ref, *, heads):
    q = dc_ref.shape[0]
    dn = (((1,), (0,)), ((), ()))
    li = lax.broadcasted_iota(jnp.int32, (q, q), 0)
    si = lax.broadcasted_iota(jnp.int32, (q, q), 1)
    lower = (li >= si).astype(F32)
    upper = (li <= si).astype(F32)
    dt_c = _softplus(dc_ref[...] + biasc_ref[...])
    da_c = dt_c * ac_ref[...]
    da_r = _softplus(dr_ref[...] + biasr_ref[...]) * ar_ref[...]
    pre_c = lax.dot_general(lower, da_c, dn, precision=HI, preferred_element_type=F32)
    suf_c = lax.dot_general(upper, da_c, dn, precision=HI, preferred_element_type=F32)
    pre_r = lax.dot_general(da_r, upper, dn, precision=HI, preferred_element_type=F32)
    suf_r = lax.dot_general(da_r, lower, dn, precision=HI, preferred_element_type=F32)
    cum_c = jnp.where(lax.broadcasted_iota(jnp.int32, pre_c.shape, 1) < heads, pre_c, suf_c)
    terms, rest = [], cum_c
    for _ in range(SSD_CUM_TERMS):
        term = rest.astype(BF16)
        rest = rest - term.astype(F32)
        terms.append(term)
    fac_ref[...] = jnp.concatenate(terms + [dt_c.astype(BF16)], axis=1)
    cumr_ref[...] = jnp.where(lax.broadcasted_iota(jnp.int32, pre_r.shape, 0) < heads, pre_r, suf_r)


def ssd_prep(dt_raw, dt_raw_t, dt_bias, a_log):
    t, h2 = dt_raw.shape
    q = SSD_CHUNK
    neg_a = -jnp.exp(a_log.astype(F32)).reshape(1, h2)
    bias = dt_bias.astype(F32).reshape(1, h2)
    const = lambda shape: pl.BlockSpec(shape, lambda i: (0,) * len(shape))
    col = pl.BlockSpec((q, h2), lambda i: (i, 0))
    row = pl.BlockSpec((h2, q), lambda i: (0, i))
    nf = (SSD_CUM_TERMS + 1) * h2
    return pl.pallas_call(
        functools.partial(_ssd_prep_body, heads=h2 // 2),
        out_shape=(jax.ShapeDtypeStruct((t, nf), BF16), jax.ShapeDtypeStruct((h2, t), F32)),
        grid=(t // q,),
        in_specs=[col, row, const((1, h2)), const((h2, 1)), const((1, h2)), const((h2, 1))],
        out_specs=(pl.BlockSpec((q, nf), lambda i: (i, 0)), row),
        compiler_params=_cp("parallel"),
        name="ssd_prep",
    )(dt_raw, dt_raw_t, bias, bias.reshape(h2, 1), neg_a, neg_a.reshape(h2, 1))


def ssd_selectors(h2, j, p, q):
    w_sum, w_wide = j * p, j * q
    k = (SSD_CUM_TERMS + 1) * h2
    rows = jnp.arange(k, dtype=jnp.int32)[None, :, None]
    cols = jnp.arange(w_sum + w_wide + j * p, dtype=jnp.int32)[None, None, :]
    head0 = j * jnp.arange(h2 // j, dtype=jnp.int32)[:, None, None]
    col_head = jnp.where(cols < w_sum, cols // p, jnp.where(cols < w_sum + w_wide, (cols - w_sum) // q,
                                                              (cols - w_sum - w_wide) // p))
    is_dt_row = rows // h2 == SSD_CUM_TERMS
    is_dt_col = cols >= w_sum + w_wide
    sel = (rows % h2 == head0 + col_head) & (is_dt_row == is_dt_col)
    return sel.astype(BF16)


def _ssd_dir(xs, bt, cm, fac, sel, diag_mask, cumr_ref, state_ref, head0, *, reverse, heads_per_group):
    q = xs.shape[0]
    p = SSD_HEAD_DIM
    j = heads_per_group
    spread = jnp.dot(fac, sel, preferred_element_type=F32)
    cc, cc_wide, dtv = spread[:, :j * p], spread[:, j * p:j * p + j * q], spread[:, j * p + j * q:]
    total = cc[0:1, :] if reverse else cc[q - 1:q, :]
    xd = xs.astype(F32) * dtv
    xd_b = xd.astype(BF16)
    li = lax.broadcasted_iota(jnp.int32, (q, q), 0)
    si = lax.broadcasted_iota(jnp.int32, (q, q), 1)
    mask = (li <= si) if reverse else (li >= si)
    cb = jnp.dot(cm, bt, preferred_element_type=F32)
    weights = []
    for jh in range(j):
        cr = cumr_ref[pl.ds(head0 + jh, 1), :]
        decay = jnp.exp(jnp.where(mask, cc_wide[:, jh * q:(jh + 1) * q] - cr, -jnp.inf))
        weights.append((cb * decay).astype(BF16))
    xd_diag = jnp.concatenate([xd_b] * j, axis=0) * diag_mask
    y = jnp.dot(jnp.concatenate(weights, axis=1), xd_diag, preferred_element_type=F32)
    s_prev = state_ref[...]
    y = y + jnp.exp(cc) * jnp.dot(cm, s_prev.astype(BF16), preferred_element_type=F32)
    st = jnp.dot(bt, (xd * jnp.exp(total - cc)).astype(BF16), preferred_element_type=F32)
    state_ref[...] = jnp.exp(total) * s_prev + st
    return y


def _ssd_body(tab_ref, xf_ref, btf_ref, cf_ref, facf_ref, crf_ref, xb_ref, btb_ref, cb_ref, facb_ref, crb_ref,
              self_ref, selb_ref, mask_ref, initf_ref, initb_ref, yf_ref, yb_ref, sf_ref, sb_ref, stf_ref, stb_ref,
              *, heads_per_group, heads):
    g = pl.program_id(0)
    item = pl.program_id(1)

    @pl.when(tab_ref[_SSD_FIRST, item] == 1)
    def _():
        stf_ref[...] = initf_ref[...]
        stb_ref[...] = initb_ref[...]

    jp = heads_per_group * SSD_HEAD_DIM
    n = SSD_STATE
    for gi in range(SSD_GROUPS_PER_STEP):
        head0 = (g * SSD_GROUPS_PER_STEP + gi) * heads_per_group
        cols, srow = slice(gi * jp, (gi + 1) * jp), slice(gi * n, (gi + 1) * n)
        yf_ref[:, cols] = _ssd_dir(xf_ref[:, cols], btf_ref[srow, :], cf_ref[:, srow], facf_ref[...], self_ref[gi],
                                   mask_ref[...], crf_ref, stf_ref.at[gi], head0, reverse=False,
                                   heads_per_group=heads_per_group).astype(yf_ref.dtype)
        yb_ref[:, cols] = _ssd_dir(xb_ref[:, cols], btb_ref[srow, :], cb_ref[:, srow], facb_ref[...], selb_ref[gi],
                                   mask_ref[...], crb_ref, stb_ref.at[gi], heads + head0, reverse=True,
                                   heads_per_group=heads_per_group).astype(yb_ref.dtype)

    @pl.when(tab_ref[_SSD_LAST, item] == 1)
    def _():
        sf_ref[...] = stf_ref[...]
        sb_ref[...] = stb_ref[...]


_SSD_FWD, _SSD_BWD, _SSD_SEQ, _SSD_FIRST, _SSD_LAST = range(5)
SSD_GROUPS_PER_STEP = 8


def ssd_scan(u, bt_all, fac, cum_r, init_f, init_b, *, seq_lens, col0, inner):
    q, p, n, gq = SSD_CHUNK, SSD_HEAD_DIM, SSD_STATE, SSD_GROUPS
    heads = inner // p
    j = heads // gq
    jp = j * p
    xcb = col0 // jp
    ccb = (col0 + inner + gq * n) // n
    rows, base = [], 0
    for s, length in enumerate(seq_lens):
        nc = length // q
        rows += [(base + c, base + nc - 1 - c, s, int(c == 0), int(c == nc - 1)) for c in range(nc)]
        base += nc
    table = jnp.array(rows, jnp.int32).T
    nseq, n_items = len(seq_lens), len(rows)

    gs = SSD_GROUPS_PER_STEP
    assert xcb % gs == 0 and ccb % gs == 0 and gq % gs == 0

    def specs(which):
        return [pl.BlockSpec((q, gs * jp), lambda g, it, tab: (tab[which, it], xcb // gs + g)),
                pl.BlockSpec((gs * n, q), lambda g, it, tab: (g, tab[which, it])),
                pl.BlockSpec((q, gs * n), lambda g, it, tab: (tab[which, it], ccb // gs + g)),
                pl.BlockSpec((q, fac.shape[1]), lambda g, it, tab: (tab[which, it], 0)),
                pl.BlockSpec((2 * heads, q), lambda g, it, tab: (0, tab[which, it]))]

    st_spec = pl.BlockSpec((None, gs, n, jp), lambda g, it, tab: (tab[_SSD_SEQ, it], g, 0, 0))
    st_shape = jax.ShapeDtypeStruct((nseq, gq, n, jp), F32)
    y_shape = jax.ShapeDtypeStruct((u.shape[0], inner), BF16)
    sel_all = ssd_selectors(2 * heads, j, p, q)
    sel_block = (gs,) + sel_all.shape[1:]
    diag_mask = (jnp.arange(j * q)[:, None] // q == jnp.arange(jp)[None, :] // p).astype(BF16)
    grid_spec = pltpu.PrefetchScalarGridSpec(
        num_scalar_prefetch=1,
        grid=(gq // gs, n_items),
        in_specs=[*specs(_SSD_FWD), *specs(_SSD_BWD),
                  pl.BlockSpec(sel_block, lambda g, it, tab: (g, 0, 0)),
                  pl.BlockSpec(sel_block, lambda g, it, tab: (gq // gs + g, 0, 0)),
                  pl.BlockSpec(diag_mask.shape, lambda g, it, tab: (0, 0)),
                  st_spec, st_spec],
        out_specs=(pl.BlockSpec((q, gs * jp), lambda g, it, tab: (tab[_SSD_FWD, it], g)),
                   pl.BlockSpec((q, gs * jp), lambda g, it, tab: (tab[_SSD_BWD, it], g)), st_spec, st_spec),
        scratch_shapes=[pltpu.VMEM((gs, n, jp), F32), pltpu.VMEM((gs, n, jp), F32)])
    return pl.pallas_call(
        functools.partial(_ssd_body, heads_per_group=j, heads=heads),
        out_shape=(y_shape, y_shape, st_shape, st_shape),
        grid_spec=grid_spec,
        compiler_params=_cp("parallel", "arbitrary"),
        name="ssd_scan",
    )(table, u, bt_all, u, fac, cum_r, u, bt_all, u, fac, cum_r, sel_all, sel_all, diag_mask, init_f, init_b)


def _state_to_kernel_layout(s, groups):
    b, h, p, n = s.shape
    return s.reshape(b, groups, h // groups, p, n).transpose(0, 1, 4, 2, 3).reshape(b, groups, n, (h // groups) * p)


def _state_from_kernel_layout(s, head_dim):
    b, g, n, jp = s.shape
    j = jp // head_dim
    return s.reshape(b, g, n, j, head_dim).transpose(0, 1, 3, 4, 2).reshape(b, g * j, head_dim, n)


def _ssd_finish_body(yf_ref, yb_ref, xs_ref, z_ref, d_ref, w_ref, o_ref):
    inner = yf_ref.shape[1]
    lead = o_ref.shape[1] - inner
    y = yf_ref[...].astype(F32) + yb_ref[...].astype(F32) + xs_ref[...].astype(F32) * d_ref[...]
    y = y * _silu(z_ref[...].astype(F32))
    y = y * lax.rsqrt(jnp.mean(y * y, axis=-1, keepdims=True) + RMS_EPS)
    o_ref[:, :lead] = jnp.zeros((o_ref.shape[0], lead), o_ref.dtype)
    o_ref[:, lead:] = (y * w_ref[...]).astype(o_ref.dtype)


def ssd_finish(yf, yb, u, proj, d_full, norm_w, *, xs_col, z_col, inner, lead_cols):
    t = yf.shape[0]
    tr = ROW_TILE
    return pl.pallas_call(
        _ssd_finish_body,
        out_shape=jax.ShapeDtypeStruct((t, lead_cols + inner), BF16),
        grid=(t // tr,),
        in_specs=[pl.BlockSpec((tr, inner), lambda i: (i, 0)),
                  pl.BlockSpec((tr, inner), lambda i: (i, 0)),
                  pl.BlockSpec((tr, inner), lambda i: (i, xs_col // inner)),
                  pl.BlockSpec((tr, inner), lambda i: (i, z_col // inner)),
                  pl.BlockSpec((1, inner), lambda i: (0, 0)),
                  pl.BlockSpec((1, inner), lambda i: (0, 0))],
        out_specs=pl.BlockSpec((tr, lead_cols + inner), lambda i: (i, 0)),
        compiler_params=_cp("parallel"),
        name="ssd_finish",
    )(yf, yb, u, proj, d_full.reshape(1, inner), norm_w.reshape(1, inner))


KV_PER_STEP = LANES // HEAD_DIM


_NT = (((1,), (1,)), ((), ()))


def _attend(scores, values, sink):
    m = sink
    for s in scores:
        m = jnp.maximum(m, jnp.max(s, axis=-1, keepdims=True))
    denom = jnp.exp(sink - m)
    acc = None
    for s, v in zip(scores, values):
        e = jnp.exp(s - m)
        denom = denom + jnp.sum(e, axis=-1, keepdims=True)
        part = jnp.dot(e.astype(BF16), v, preferred_element_type=F32)
        acc = part if acc is None else acc + part
    return acc * (1.0 / denom)


def _stack_heads(q, first_head, n, rows):
    return jnp.concatenate([q[:, (first_head + g) * HEAD_DIM:(first_head + g + 1) * HEAD_DIM] for g in range(n)],
                           axis=0)


def _stacked_sinks(sink_ref, first_head, n, rows):
    row = lax.broadcasted_iota(jnp.int32, (n * rows, 1), 0)
    col = jnp.full((n * rows, 1), sink_ref[first_head + n - 1], F32)
    for g in reversed(range(n - 1)):
        col = jnp.where(row < (g + 1) * rows, sink_ref[first_head + g], col)
    return col


def _ctx_attn_body(sink_ref, q_ref, k_ref, v_ref, fill_ref, o_ref, *, q_per_kv, scale):
    del fill_ref
    hb = pl.program_id(1)
    rows = q_ref.shape[0]
    q_all = q_ref[...].astype(F32) * scale
    outs = []
    for kv in range(KV_PER_STEP):
        k = k_ref[:, kv * HEAD_DIM:(kv + 1) * HEAD_DIM].astype(BF16)
        v = v_ref[:, kv * HEAD_DIM:(kv + 1) * HEAD_DIM].astype(BF16)
        qs = _stack_heads(q_all, kv * q_per_kv, q_per_kv, rows).astype(BF16)
        s = lax.dot_general(qs, k, _NT, preferred_element_type=F32)
        sink = _stacked_sinks(sink_ref, (hb * KV_PER_STEP + kv) * q_per_kv, q_per_kv, rows)
        o = _attend([s], [v], sink)
        outs += [o[g * rows:(g + 1) * rows] for g in range(q_per_kv)]
    o_ref[...] = jnp.concatenate(outs, axis=1).astype(o_ref.dtype)


def context_attention(q, kv, sink, *, nseq, L, n_heads):
    q_per_kv = n_heads // N_KV_HEADS
    qw = KV_PER_STEP * q_per_kv * HEAD_DIM
    nhb = N_KV_HEADS // KV_PER_STEP
    kcb = 0
    vcb = N_KV_HEADS * HEAD_DIM // LANES
    return pl.pallas_call(
        functools.partial(_ctx_attn_body, q_per_kv=q_per_kv, scale=HEAD_DIM ** -0.5),
        out_shape=jax.ShapeDtypeStruct(q.shape, BF16),
        grid=(nseq, nhb),
        in_specs=[pl.BlockSpec(memory_space=pltpu.SMEM),
                  pl.BlockSpec((L, qw), lambda b, h: (b, h)),
                  pl.BlockSpec((L, LANES), lambda b, h: (b, kcb + h)),
                  pl.BlockSpec((L, LANES), lambda b, h: (b, vcb + h)),
                  pl.BlockSpec(memory_space=pl.ANY)],
        out_specs=pl.BlockSpec((L, qw), lambda b, h: (b, h)),
        input_output_aliases={4: 0},
        compiler_params=_cp("parallel", "parallel"),
        name="context_attention",
    )(sink, q, kv, kv, zeros_2d(q.shape[0], n_heads * HEAD_DIM, BF16))


def _rope(x, cos, sin):
    outs = []
    lane = lax.broadcasted_iota(jnp.int32, cos.shape, 1)
    first = (lane % (HEAD_DIM // 2)) < (HEAD_DIM // 4)
    for cgrp in range(x.shape[1] // LANES):
        xc = x[:, cgrp * LANES:(cgrp + 1) * LANES]
        partner = jnp.where(first, pltpu.roll(xc, LANES - HEAD_DIM // 4, 1), pltpu.roll(xc, HEAD_DIM // 4, 1))
        outs.append(xc * cos + partner * sin)
    return outs[0] if len(outs) == 1 else jnp.concatenate(outs, axis=1)


def _lat_attn_body(sink_ref, q_ref, k_ref, v_ref, kc_ref, vc_ref, cosq_ref, sinq_ref, cos_ref, sin_ref, fill_ref,
                   o_ref, k_scr, v_scr, kc_scr, vc_scr, *, q_per_kv, scale, L, blk):
    del fill_ref
    hb = pl.program_id(1)
    i = pl.program_id(2)

    @pl.when(i == 0)
    def _():
        k_scr[...] = _rope(k_ref[...], cos_ref[...], sin_ref[...]).astype(BF16)
        v_scr[...] = v_ref[...].astype(BF16)
        kc_scr[...] = kc_ref[...].astype(BF16)
        vc_scr[...] = vc_ref[...].astype(BF16)

    n_loc = 3 * blk
    start = jnp.clip((i - 1) * blk, 0, L - n_loc)
    start = pl.multiple_of(start, blk)
    q_all = _rope(q_ref[...].astype(F32), cosq_ref[...], sinq_ref[...]) * scale
    k_loc = k_scr[pl.ds(start, n_loc), :]
    v_loc = v_scr[pl.ds(start, n_loc), :]
    rows = q_per_kv * blk
    qpos = i * blk + (lax.broadcasted_iota(jnp.int32, (rows, n_loc), 0) & (blk - 1))
    kpos = start + lax.broadcasted_iota(jnp.int32, (rows, n_loc), 1)
    ok = jnp.abs(qpos - kpos) <= WINDOW
    outs = []
    for kv in range(KV_PER_STEP):
        sl = slice(kv * HEAD_DIM, (kv + 1) * HEAD_DIM)
        qs = _stack_heads(q_all, kv * q_per_kv, q_per_kv, blk).astype(BF16)
        s_loc = jnp.where(ok, lax.dot_general(qs, k_loc[:, sl], _NT, preferred_element_type=F32), -jnp.inf)
        s_ctx = lax.dot_general(qs, kc_scr[:, sl], _NT, preferred_element_type=F32)
        sink = _stacked_sinks(sink_ref, (hb * KV_PER_STEP + kv) * q_per_kv, q_per_kv, blk)
        o = _attend([s_loc, s_ctx], [v_loc[:, sl], vc_scr[:, sl]], sink)
        outs += [o[g * blk:(g + 1) * blk] for g in range(q_per_kv)]
    o_ref[...] = jnp.concatenate(outs, axis=1).astype(o_ref.dtype)


def rope_tables(L):
    rows = L // GRID_W
    row = jnp.repeat(jnp.arange(rows, dtype=F32), GRID_W)
    col = jnp.tile(jnp.arange(GRID_W, dtype=F32), rows)
    quarter = HEAD_DIM // 4
    inv = ROPE_BASE ** (-jnp.arange(quarter, dtype=F32) / quarter)
    ang_r = row[:, None] * inv[None, :]
    ang_c = col[:, None] * inv[None, :]
    ang = jnp.concatenate([ang_r, ang_r, ang_c, ang_c], axis=1)
    sign = jnp.tile(jnp.concatenate([-jnp.ones((quarter,), F32), jnp.ones((quarter,), F32)]), 2)
    cos = jnp.tile(jnp.cos(ang), (1, LANES // HEAD_DIM))
    sin = jnp.tile(jnp.sin(ang) * sign[None, :], (1, LANES // HEAD_DIM))
    return cos, sin


def latent_attention(q, kv, k_cache, v_cache, sink, fill, *, row0, nseq, L, n_heads):
    blk = WINDOW
    q_per_kv = n_heads // N_KV_HEADS
    qw = KV_PER_STEP * q_per_kv * HEAD_DIM
    nhb = N_KV_HEADS // KV_PER_STEP
    kcb = 0
    vcb = N_KV_HEADS * HEAD_DIM // LANES
    nb = L // blk
    past = k_cache.shape[1]
    cos, sin = rope_tables(L)
    return pl.pallas_call(
        functools.partial(_lat_attn_body, q_per_kv=q_per_kv, scale=HEAD_DIM ** -0.5, L=L, blk=blk),
        out_shape=jax.ShapeDtypeStruct(fill.shape, BF16),
        grid=(nseq, nhb, nb),
        in_specs=[pl.BlockSpec(memory_space=pltpu.SMEM),
                  pl.BlockSpec((blk, qw), lambda b, h, i: (row0 // blk + b * nb + i, h)),
                  pl.BlockSpec((L, LANES), lambda b, h, i: (row0 // L + b, kcb + h)),
                  pl.BlockSpec((L, LANES), lambda b, h, i: (row0 // L + b, vcb + h)),
                  pl.BlockSpec((None, past, LANES), lambda b, h, i: (b, 0, h)),
                  pl.BlockSpec((None, past, LANES), lambda b, h, i: (b, 0, h)),
                  pl.BlockSpec((blk, LANES), lambda b, h, i: (i, 0)),
                  pl.BlockSpec((blk, LANES), lambda b, h, i: (i, 0)),
                  pl.BlockSpec((L, LANES), lambda b, h, i: (0, 0)),
                  pl.BlockSpec((L, LANES), lambda b, h, i: (0, 0)),
                  pl.BlockSpec(memory_space=pl.ANY)],
        out_specs=pl.BlockSpec((blk, qw), lambda b, h, i: (row0 // blk + b * nb + i, h)),
        scratch_shapes=[pltpu.VMEM((L, LANES), BF16), pltpu.VMEM((L, LANES), BF16),
                        pltpu.VMEM((past, LANES), BF16), pltpu.VMEM((past, LANES), BF16)],
        input_output_aliases={10: 0},
        compiler_params=_cp("parallel", "parallel", "arbitrary"),
        name="latent_attention",
    )(sink, q, kv, kv, k_cache, v_cache, cos, sin, cos, sin, fill)


MOE_TILE = 256
MOE_FFN_VMEM_BYTES = 56 * 1024 * 1024


def _moe_dispatch_body(pos_ref, h_ref, g_ref, init_hbm, xs_hbm, buf, sem):
    del init_hbm
    i = pl.program_id(0)
    tr, d = h_ref.shape
    buf[:, :d] = h_ref[...]
    buf[:, d:] = g_ref[...]

    def start(r2, carry):
        for prio in range(DMA_PRIORITIES):
            r = r2 * DMA_PRIORITIES + prio
            pltpu.make_async_copy(buf.at[pl.ds(r, 1)], xs_hbm.at[pl.ds(pos_ref[i * tr + r], 1)],
                                  sem).start(priority=prio)
        return carry

    lax.fori_loop(0, tr // DMA_PRIORITIES, start, 0)

    def wait(r, carry):
        pltpu.make_async_copy(buf.at[pl.ds(r, 1)], xs_hbm.at[pl.ds(0, 1)], sem).wait()
        return carry

    lax.fori_loop(0, tr, wait, 0)


def moe_dispatch(h, gate_lanes, pos, n_slots):
    t, d = h.shape
    tr = ROW_TILE
    w = d + LANES
    grid_spec = pltpu.PrefetchScalarGridSpec(
        num_scalar_prefetch=1,
        grid=(t // tr,),
        in_specs=[pl.BlockSpec((tr, d), lambda i, p: (i, 0)),
                  pl.BlockSpec((tr, LANES), lambda i, p: (i, 0)),
                  pl.BlockSpec(memory_space=pl.ANY)],
        out_specs=pl.BlockSpec(memory_space=pl.ANY),
        scratch_shapes=[pltpu.VMEM((tr, w), F32), pltpu.SemaphoreType.DMA(())])
    return pl.pallas_call(
        _moe_dispatch_body,
        out_shape=jax.ShapeDtypeStruct((n_slots, w), F32),
        grid_spec=grid_spec,
        input_output_aliases={3: 0},
        compiler_params=_cp("arbitrary"),
        name="moe_dispatch",
    )(pos, h, gate_lanes, zeros_2d(n_slots, w, F32))


def _moe_ffn_body(ea_ref, eb_ref, nused_ref, xs_ref, w1a_ref, w3a_ref, w2a_ref, w1b_ref, w3b_ref, w2b_ref, o_ref):
    i = pl.program_id(0)
    d = o_ref.shape[1]

    @pl.when(i < nused_ref[0])
    def _():
        x = xs_ref[:, :d].astype(BF16)

        def expert(w1_ref, w3_ref, w2_ref):
            a = jnp.dot(x, w1_ref[...], preferred_element_type=F32)
            b = jnp.dot(x, w3_ref[...], preferred_element_type=F32)
            return jnp.dot((_silu(a) * b).astype(BF16), w2_ref[...], preferred_element_type=F32)

        o_ref[...] = (xs_ref[:, d:d + 1] * expert(w1a_ref, w3a_ref, w2a_ref)
                      + xs_ref[:, d + 1:d + 2] * expert(w1b_ref, w3b_ref, w2b_ref))

    @pl.when(i >= nused_ref[0])
    def _():
        o_ref[...] = jnp.zeros_like(o_ref)


def moe_ffn(xs, tile_ea, tile_eb, n_used, w1, w3, w2, layer):
    _, _, d, de = w1.shape
    tm = MOE_TILE
    n_tiles = tile_ea.shape[0]
    up_a = pl.BlockSpec((None, None, d, de), lambda i, ea, eb, nu: (layer, ea[i], 0, 0))
    up_b = pl.BlockSpec((None, None, d, de), lambda i, ea, eb, nu: (layer, eb[i], 0, 0))
    grid_spec = pltpu.PrefetchScalarGridSpec(
        num_scalar_prefetch=3,
        grid=(n_tiles,),
        in_specs=[pl.BlockSpec((tm, d + LANES), lambda i, ea, eb, nu: (i, 0)),
                  up_a, up_a, pl.BlockSpec((None, None, de, d), lambda i, ea, eb, nu: (layer, ea[i], 0, 0)),
                  up_b, up_b, pl.BlockSpec((None, None, de, d), lambda i, ea, eb, nu: (layer, eb[i], 0, 0))],
        out_specs=pl.BlockSpec((tm, d), lambda i, ea, eb, nu: (i, 0)))
    return pl.pallas_call(
        _moe_ffn_body,
        out_shape=jax.ShapeDtypeStruct((n_tiles * tm, d), F32),
        grid_spec=grid_spec,
        compiler_params=_cp("arbitrary", vmem=MOE_FFN_VMEM_BYTES),
        name="moe_ffn",
    )(tile_ea, tile_eb, n_used, xs, w1, w3, w2, w1, w3, w2)


def _moe_combine_body(pos_ref, y_hbm, x_ref, g_ref, o_ref, ybuf, sem):
    i = pl.program_id(0)
    tr = x_ref.shape[0]

    def start(r2, carry):
        for prio in range(DMA_PRIORITIES):
            r = r2 * DMA_PRIORITIES + prio
            pltpu.make_async_copy(y_hbm.at[pl.ds(pos_ref[i * tr + r], 1)], ybuf.at[pl.ds(r, 1)],
                                  sem).start(priority=prio)
        return carry

    lax.fori_loop(0, tr // DMA_PRIORITIES, start, 0)

    def wait(r, carry):
        pltpu.make_async_copy(y_hbm.at[pl.ds(0, 1)], ybuf.at[pl.ds(r, 1)], sem).wait()
        return carry

    lax.fori_loop(0, tr, wait, 0)
    o_ref[...] = x_ref[...] + g_ref[...] * ybuf[...]


def moe_combine(x, y_sorted, pos, mod3, gate_idx, seg_of_tile):
    t, d = x.shape
    tr = ROW_TILE
    grid_spec = pltpu.PrefetchScalarGridSpec(
        num_scalar_prefetch=1,
        grid=(t // tr,),
        in_specs=[pl.BlockSpec(memory_space=pl.ANY),
                  pl.BlockSpec((tr, d), lambda i, p: (i, 0)),
                  pl.BlockSpec((None, 1, d), lambda i, p: (seg_of_tile(i) * 6 + gate_idx, 0, 0))],
        out_specs=pl.BlockSpec((tr, d), lambda i, p: (i, 0)),
        scratch_shapes=[pltpu.VMEM((tr, d), F32), pltpu.SemaphoreType.DMA(())])
    return pl.pallas_call(
        _moe_combine_body,
        out_shape=jax.ShapeDtypeStruct((t, d), F32),
        grid_spec=grid_spec,
        compiler_params=_cp("arbitrary"),
        name="moe_combine",
    )(pos, y_sorted, x, mod3)


def moe_route(logits, n_groups, n_experts):
    per = n_experts // n_groups
    assert TOP_K == 2
    g_prob = jax.nn.softmax(logits[:, :n_groups], axis=-1)
    g_top = jnp.max(g_prob, axis=-1, keepdims=True)
    g_idx = jnp.argmax(g_prob, axis=-1, keepdims=True).astype(jnp.int32)
    e_logits = logits[:, n_groups:n_groups + n_experts].reshape(-1, n_groups, per)
    g_onehot = g_idx == jnp.arange(n_groups)[None, :]
    e_in = jnp.sum(jnp.where(g_onehot[:, :, None], e_logits, 0.0), axis=1)
    e_prob = jax.nn.softmax(e_in, axis=-1)
    i0 = jnp.argmax(e_prob, axis=-1, keepdims=True).astype(jnp.int32)
    rest = jnp.where(jnp.arange(per)[None, :] == i0, -jnp.inf, e_prob)
    i1 = jnp.argmax(rest, axis=-1, keepdims=True).astype(jnp.int32)
    e_idx = jnp.concatenate([i0, i1], axis=1)
    e_top = jnp.concatenate([jnp.max(e_prob, axis=-1, keepdims=True), jnp.max(rest, axis=-1, keepdims=True)], axis=1)
    e_top = e_top / jnp.sum(e_top, axis=-1, keepdims=True)
    return g_idx * per + e_idx, g_top * e_top


def moe_plan(expert_ids, gates, n_groups, n_experts):
    assert TOP_K == 2
    t = expert_ids.shape[0]
    tm = MOE_TILE
    per = n_experts // n_groups
    n_pairs = per * (per - 1) // 2
    n_classes = n_groups * n_pairs
    n_tiles = t // tm + n_classes
    e0, e1 = expert_ids[:, 0], expert_ids[:, 1]
    lo, hi = jnp.minimum(e0, e1), jnp.maximum(e0, e1)
    g_lo = jnp.where(e0 < e1, gates[:, 0], gates[:, 1])
    g_hi = jnp.where(e0 < e1, gates[:, 1], gates[:, 0])
    a, b = lo % per, hi % per
    cls = (lo // per) * n_pairs + (a * (2 * per - a - 1)) // 2 + (b - a - 1)
    onehot = (cls[:, None] == jnp.arange(n_classes)[None, :]).astype(F32)
    blk = ROW_TILE
    inner = jnp.einsum("ij,bjk->bik", jnp.tril(jnp.ones((blk, blk), F32)), onehot.reshape(-1, blk, n_classes))
    bsum = inner[:, -1, :]
    running = (inner + (jnp.cumsum(bsum, axis=0) - bsum)[:, None, :]).reshape(-1, n_classes)
    counts = jnp.sum(bsum, axis=0).astype(jnp.int32)
    tiles_per = (counts + tm - 1) // tm
    tile_start = jnp.cumsum(tiles_per) - tiles_per
    slot = running - 1.0 + (tile_start * tm).astype(F32)[None, :]
    pos = jnp.sum(jnp.where(onehot > 0, slot, 0.0), axis=1).astype(jnp.int32)
    n_used = jnp.sum(tiles_per).astype(jnp.int32)
    tile_ids = jnp.arange(n_tiles, dtype=jnp.int32)
    tile_cls = jnp.sum((tile_ids[:, None] >= tile_start[None, :]).astype(jnp.int32), axis=1) - 1
    tile_cls = jnp.where(tile_ids < n_used, tile_cls, tile_cls[jnp.maximum(n_used - 1, 0)])
    pairs = [(pa, pb) for pa in range(per) for pb in range(pa + 1, per)]
    cls_lo = jnp.array([g * per + pa for g in range(n_groups) for pa, _ in pairs], jnp.int32)
    cls_hi = jnp.array([g * per + pb for g in range(n_groups) for _, pb in pairs], jnp.int32)
    gate_lanes = jnp.pad(jnp.stack([g_lo, g_hi], axis=1), ((0, 0), (0, LANES - 2)))
    tile_onehot = tile_cls[:, None] == jnp.arange(n_classes)[None, :]
    tile_lo = jnp.sum(jnp.where(tile_onehot, cls_lo[None, :], 0), axis=1)
    tile_hi = jnp.sum(jnp.where(tile_onehot, cls_hi[None, :], 0), axis=1)
    return pos, gate_lanes, tile_lo, tile_hi, n_used.reshape(1), n_tiles * tm


def kernel(x_prompt, x_sample, state_ssd_fwd, state_ssd_bwd, cache_attn_k, cache_attn_v, c, c_ctx, ada_w, ada_b, norm_mix, norm_ffn, norm_final, ev_w_in, ev_w_out, hy_conv_w, hy_conv_b, hy_f_w1, hy_f_b1, hy_f_freq1, hy_f_w2, hy_f_b2, hy_f_freq2, hy_f_w3, hy_bias, ssd_conv_w, ssd_conv_b, ssd_dt_bias, ssd_a_log, ssd_d, ssd_norm, at_wq, at_wk, at_wv, at_wo, at_sink, moe_w_group, moe_b_group, moe_w_expert, moe_b_expert, moe_w1, moe_w3, moe_w2):
    bc, lc, d = x_prompt.shape
    bl, ll, _ = x_sample.shape
    tc_rows, tl_rows = bc * lc, bl * ll
    t = tc_rows + tl_rows
    depth = ada_w.shape[0]
    n_seg = 1 + bl
    assert lc % ROW_TILE == 0 and ll % ROW_TILE == 0 and tc_rows % ll == 0
    seg_rows = math.gcd(tc_rows, ll)

    def seg_of_rows(r):
        return jnp.where(r < tc_rows, 0, 1 + (r - tc_rows) // ll)

    def seg_of_tile(i):
        return seg_of_rows(i * ROW_TILE)

    x = jnp.concatenate([x_prompt.reshape(tc_rows, d), x_sample.reshape(tl_rows, d)], axis=0)
    cvec = jnp.concatenate([c_ctx[None], c, jnp.zeros((SUBLANES - n_seg, d), F32)], axis=0)
    mod_all = ada_mod(cvec, ada_w, ada_b)

    tile_row = jnp.arange(t // ROW_TILE, dtype=jnp.int32) * ROW_TILE
    seq_len = jnp.where(tile_row < tc_rows, lc, ll)
    rel = jnp.where(tile_row < tc_rows, tile_row, tile_row - tc_rows)
    conv_flags = jnp.stack([(rel % seq_len == 0), ((rel + ROW_TILE) % seq_len == 0)]).astype(jnp.int32)

    w1_bf, w3_bf, w2_bf = moe_w1.astype(BF16), moe_w3.astype(BF16), moe_w2.astype(BF16)
    new_sf, new_sb, new_k, new_v = [], [], [], []
    for l in range(depth):
        i = l // 2
        mod3 = mod_all[l, :n_seg].reshape(n_seg * 6, 1, d)
        h = rms_norm(x, norm_mix[l], out_dtype=BF16, mod3=mod3, shift_idx=0, scale_idx=1,
                     seg_of_tile=seg_of_tile)
        if l % 2 == 0:
            hyena_w = hy_conv_w.shape[2] // 3
            inner = ssd_norm.shape[1]
            heads = inner // SSD_HEAD_DIM
            conv_ch = ssd_conv_w.shape[2]
            main_cols = 3 * hyena_w + inner + conv_ch
            w_in = ev_w_in[i].astype(BF16)
            proj = dense(h, w_in, n_cols=main_cols, out_dtype=BF16, name="in_proj")
            dt_raw = dense(h, w_in[:, main_cols:], out_dtype=F32, name="dt_proj")
            u = dwconv(proj, hy_conv_w[i], hy_conv_b[i], conv_flags, col0=0, silu=False)
            u_ssd = dwconv(proj, ssd_conv_w[i], ssd_conv_b[i], conv_flags, col0=3 * hyena_w + inner,
                           silu=True)
            gn = SSD_GROUPS * SSD_STATE
            bt_all = u_ssd[:, inner:inner + gn].T
            fac, cum_r = ssd_prep(dt_raw, dt_raw.T, ssd_dt_bias[i], ssd_a_log[i])
            zero = jnp.zeros((bc, SSD_GROUPS, SSD_STATE, inner // SSD_GROUPS), F32)
            init_f = jnp.concatenate([zero, _state_to_kernel_layout(state_ssd_fwd[:, i], SSD_GROUPS)], axis=0)
            init_b = jnp.concatenate([zero, _state_to_kernel_layout(state_ssd_bwd[:, i], SSD_GROUPS)], axis=0)
            yf, yb, sf, sb = ssd_scan(u_ssd, bt_all, fac, cum_r, init_f, init_b,
                                      seq_lens=[lc] * bc + [ll] * bl, col0=0, inner=inner)
            new_sf.append(_state_from_kernel_layout(sf[:bc], SSD_HEAD_DIM))
            new_sb.append(_state_from_kernel_layout(sb[:bc], SSD_HEAD_DIM))
            y_a = ssd_finish(yf, yb, u_ssd, proj, jnp.repeat(ssd_d[i], SSD_HEAD_DIM), ssd_norm[i],
                             xs_col=0, z_col=3 * hyena_w, inner=inner, lead_cols=hyena_w)
            for row0, nb, L in ((0, bc, lc), (tc_rows, bl, ll)):
                filt = hyena_filters(L, hy_f_w1[i], hy_f_b1[i], hy_f_freq1[i], hy_f_w2[i], hy_f_b2[i],
                                     hy_f_freq2[i], hy_f_w3[i], hyena_w)
                if L % (16 * SPEC_BLOCK) == 0:
                    n1 = 2 * L // SPEC_BLOCK
                    t1f = fft_table("t1f", n1, SPEC_BLOCK)
                    kf = hyena_spectrum_2level(L, filt, t1f, fft_table("f2filt", n1, SPEC_BLOCK))
                    tabs = (t1f, fft_table("f2f", n1, SPEC_BLOCK), fft_table("f2i", n1, SPEC_BLOCK),
                            fft_table("t1i", n1, SPEC_BLOCK))
                    y_a = hyena_stream_2level(u, row0, nb, L, kf, tabs, hy_bias[i], hyena_w, fill=y_a)
                else:
                    a_tab = dft_table(L, "filt")
                    kf = hyena_spectrum(L, filt, a_tab)
                    y_a = hyena_stream(u, row0, nb, L, kf, a_tab, dft_table(L, "inv"), hy_bias[i], hyena_w,
                                       fill=y_a)
            x = dense_residual(y_a, ev_w_out[i].astype(BF16), x, mod3, 2, seg_of_rows, seg_rows, name="out_proj")
        else:
            n_heads = at_wq.shape[2] // HEAD_DIM
            q = dense(h, at_wq[i].astype(BF16), out_dtype=BF16, name="q_proj")
            kv = dense(h, jnp.concatenate([at_wk[i], at_wv[i]], axis=1).astype(BF16), out_dtype=F32, name="kv_proj")
            kvw = N_KV_HEADS * HEAD_DIM
            att_c = context_attention(q, kv, at_sink[i], nseq=bc, L=lc, n_heads=n_heads)
            past = cache_attn_k.shape[2]
            att = latent_attention(q, kv, cache_attn_k[:, i].reshape(bl, past, kvw),
                                   cache_attn_v[:, i].reshape(bl, past, kvw), at_sink[i], att_c,
                                   row0=tc_rows, nseq=bl, L=ll, n_heads=n_heads)
            x = dense_residual(att, at_wo[i].astype(BF16), x, mod3, 2, seg_of_rows, seg_rows, name="attn_out")
            new_k.append(kv[:tc_rows, :kvw].reshape(bc, lc, N_KV_HEADS, HEAD_DIM))
            new_v.append(kv[:tc_rows, kvw:].reshape(bc, lc, N_KV_HEADS, HEAD_DIM))
        h2 = rms_norm(x, norm_ffn[l], out_dtype=F32, mod3=mod3, shift_idx=3, scale_idx=4,
                      seg_of_tile=seg_of_tile)
        n_experts = moe_w_expert.shape[2]
        w_r = jnp.concatenate([moe_w_group[l], moe_w_expert[l]], axis=1)
        w_r = jnp.pad(w_r, ((0, 0), (0, LANES - w_r.shape[1]))).astype(BF16)
        b_r = jnp.pad(jnp.concatenate([moe_b_group[l], moe_b_expert[l]]), (0, LANES - MOE_GROUPS - n_experts))
        logits = dense(h2, w_r, out_dtype=F32, extra=(b_r.reshape(1, LANES),),
                       extra_specs=(pl.BlockSpec((1, LANES), lambda bt, ii, j, kk: (0, 0)),),
                       epilogue=lambda acc, b_ref, row_tile: acc + b_ref[...], name="router")
        expert_ids, gates = moe_route(logits, MOE_GROUPS, n_experts)
        pos, gate_lanes, tile_lo, tile_hi, n_used, n_slots = moe_plan(expert_ids, gates, MOE_GROUPS, n_experts)
        xs = moe_dispatch(h2, gate_lanes, pos, n_slots)
        y_sorted = moe_ffn(xs, tile_lo, tile_hi, n_used, w1_bf, w3_bf, w2_bf, l)
        x = moe_combine(x, y_sorted, pos, mod3, 5, seg_of_tile)

    y_prompt = rms_norm(x, norm_final, out_dtype=F32, rows=tc_rows).reshape(bc, lc, d)
    y_sample = rms_norm(x, norm_final, out_dtype=F32, row_off=tc_rows, rows=tl_rows).reshape(bl, ll, d)
    return (y_prompt, y_sample, jnp.stack(new_sf, axis=1), jnp.stack(new_sb, axis=1),
            jnp.stack(new_k, axis=1), jnp.stack(new_v, axis=1))
```

```python
import functools
import math

import jax
import jax.numpy as jnp
from jax import lax
from jax.experimental import pallas as pl
from jax.experimental.pallas import tpu as pltpu

F32 = jnp.float32
BF16 = jnp.bfloat16
HI = lax.Precision.HIGHEST

RMS_EPS = 1e-6
GRID_W = 64
HYENA_BANDS = 16
HYENA_FAST_DECAY = 0.3
HYENA_SLOW_DECAY = 1.5
HYENA_TARGET = 1e-2
SSD_HEAD_DIM = 64
SSD_STATE = 128
SSD_GROUPS = 8
SSD_CHUNK = 128
HEAD_DIM = 64
N_KV_HEADS = 8
WINDOW = 128
ROPE_BASE = 10000.0
MOE_GROUPS = 4
TOP_K = 2

LANES = 128
SUBLANES = 8
BF16_SUBLANES = 16
DMA_PRIORITIES = 2
VMEM_LIMIT_BYTES = 48 * 1024 * 1024

ROW_TILE = 256
SPEC_BLOCK = 128


def _cp(*sem, vmem=VMEM_LIMIT_BYTES):
    return pltpu.CompilerParams(dimension_semantics=sem, vmem_limit_bytes=vmem)


def _tile(n, pref):
    if n <= pref:
        return n
    t = pref
    while n % t:
        t //= 2
    assert t >= LANES, (n, pref)
    return t


def _silu(x):
    return x * (1.0 / (1.0 + jnp.exp(-x)))


def _softplus(x):
    return jnp.maximum(x, 0.0) + jnp.log(1.0 + jnp.exp(-jnp.abs(x)))


def _zero_body(o_ref):
    o_ref[...] = jnp.zeros_like(o_ref)


def zeros_2d(rows, cols, dtype):
    tr = _tile(rows, 1024)
    return pl.pallas_call(
        _zero_body,
        out_shape=jax.ShapeDtypeStruct((rows, cols), dtype),
        grid=(rows // tr,),
        out_specs=pl.BlockSpec((tr, cols), lambda i: (i, 0)),
        compiler_params=_cp("parallel"),
        name="zero_fill",
    )()


def _ada_body(c_ref, w_ref, b_ref, o_ref):
    a = _silu(c_ref[...])
    o_ref[...] = lax.dot_general(a, w_ref[...], (((1,), (0,)), ((), ())), precision=HI,
                                 preferred_element_type=F32) + b_ref[...]


def ada_mod(cvec, ada_w, ada_b):
    depth, d, n = ada_w.shape
    rows = cvec.shape[0]
    tn = _tile(n, 1024)
    return pl.pallas_call(
        _ada_body,
        out_shape=jax.ShapeDtypeStruct((depth, rows, n), F32),
        grid=(depth, n // tn),
        in_specs=[pl.BlockSpec((rows, d), lambda l, j: (0, 0)),
                  pl.BlockSpec((None, d, tn), lambda l, j: (l, 0, j)),
                  pl.BlockSpec((None, 1, tn), lambda l, j: (l, 0, j))],
        out_specs=pl.BlockSpec((None, rows, tn), lambda l, j: (l, 0, j)),
        compiler_params=_cp("parallel", "parallel"),
        name="ada_mod",
    )(cvec, ada_w, ada_b.reshape(depth, 1, n))


def _norm_body(x_ref, g_ref, *rest, modulate):
    o_ref = rest[-1]
    x = x_ref[...]
    y = x * lax.rsqrt(jnp.mean(x * x, axis=-1, keepdims=True) + RMS_EPS)
    y = y * g_ref[...]
    if modulate:
        sh_ref, sc_ref = rest[0], rest[1]
        y = y * (1.0 + sc_ref[...]) + sh_ref[...]
    o_ref[...] = y.astype(o_ref.dtype)


def rms_norm(x, g, *, out_dtype, mod3=None, shift_idx=0, scale_idx=0, seg_of_tile=None,
             row_off=0, rows=None):
    t, d = x.shape
    rows = t if rows is None else rows
    tr = ROW_TILE
    off = row_off // tr
    in_specs = [pl.BlockSpec((tr, d), lambda i: (i + off, 0)),
                pl.BlockSpec((1, d), lambda i: (0, 0))]
    args = [x, g.reshape(1, d)]
    if mod3 is not None:
        in_specs += [pl.BlockSpec((None, 1, d), lambda i: (seg_of_tile(i) * 6 + shift_idx, 0, 0)),
                     pl.BlockSpec((None, 1, d), lambda i: (seg_of_tile(i) * 6 + scale_idx, 0, 0))]
        args += [mod3, mod3]
    return pl.pallas_call(
        functools.partial(_norm_body, modulate=mod3 is not None),
        out_shape=jax.ShapeDtypeStruct((rows, d), out_dtype),
        grid=(rows // tr,),
        in_specs=in_specs,
        out_specs=pl.BlockSpec((tr, d), lambda i: (i, 0)),
        compiler_params=_cp("parallel"),
        name="rms_norm",
    )(*args)


def _mm_body(*refs, nk, n_extra, n_skip, epilogue):
    a_ref, b_ref = refs[0], refs[1]
    extra = refs[2:2 + n_extra]
    o_ref = refs[2 + n_extra + n_skip]
    row_tile = pl.program_id(1)
    b = b_ref[...]
    if b.ndim == 3:
        b = b.reshape(-1, b.shape[-1])
    part = jnp.dot(a_ref[...].astype(b.dtype), b, preferred_element_type=F32)
    if nk == 1:
        o_ref[...] = epilogue(part, *extra, row_tile=row_tile).astype(o_ref.dtype).reshape(o_ref.shape)
        return
    acc_ref = refs[3 + n_extra + n_skip]
    k = pl.program_id(3)

    @pl.when(k == 0)
    def _():
        acc_ref[...] = part

    @pl.when(k > 0)
    def _():
        acc_ref[...] += part

    @pl.when(k == nk - 1)
    def _():
        o_ref[...] = epilogue(acc_ref[...], *extra, row_tile=row_tile).astype(o_ref.dtype)


def matmul(a, b, *, grid, tm, tn, a_spec, b_spec, o_spec, out_shape, out_dtype,
           extra=(), extra_specs=(), epilogue=None, fill=None, name="matmul"):
    nk = grid[3]
    epilogue = epilogue or (lambda acc, row_tile: acc)
    scratch = [pltpu.VMEM((tm, tn), F32)] if nk > 1 else []
    fills = () if fill is None else (fill,)
    return pl.pallas_call(
        functools.partial(_mm_body, nk=nk, n_extra=len(extra), n_skip=len(fills), epilogue=epilogue),
        out_shape=jax.ShapeDtypeStruct(out_shape, out_dtype),
        grid=grid,
        in_specs=[a_spec, b_spec, *extra_specs, *[pl.BlockSpec(memory_space=pl.ANY) for _ in fills]],
        out_specs=o_spec,
        scratch_shapes=scratch,
        input_output_aliases={2 + len(extra): 0} if fills else {},
        compiler_params=_cp("parallel", "parallel", "parallel", "arbitrary"),
        name=name,
    )(a, b, *extra, *fills)


def dense(a, w, *, out_dtype, tm=1024, tn=512, tk=2048, extra=(), extra_specs=(), epilogue=None,
          w_row0=0, n_cols=None, name="dense"):
    m, k = a.shape
    n = w.shape[1] if n_cols is None else n_cols
    tm, tn, tk = _tile(m, tm), _tile(n, tn), _tile(k, tk)
    assert w_row0 % tk == 0
    kb0 = w_row0 // tk
    return matmul(
        a, w, grid=(1, m // tm, n // tn, k // tk), tm=tm, tn=tn,
        a_spec=pl.BlockSpec((tm, tk), lambda bt, i, j, kk: (i, kk)),
        b_spec=pl.BlockSpec((tk, tn), lambda bt, i, j, kk: (kb0 + kk, j)),
        o_spec=pl.BlockSpec((tm, tn), lambda bt, i, j, kk: (i, j)),
        out_shape=(m, n), out_dtype=out_dtype, extra=extra, extra_specs=extra_specs,
        epilogue=epilogue, name=name)


def dense_residual(a, w, x, mod3, gate_idx, seg_of_rows, seg_rows, *, tm=1024, tn=512, tk=2048, w_row0=0,
                   name="dense_res"):
    n = w.shape[1]
    tm_ = _tile(seg_rows, tm)
    tn_ = _tile(n, tn)
    return dense(
        a, w, out_dtype=F32, tm=tm_, tn=tn, tk=tk, w_row0=w_row0, extra=(x, mod3),
        extra_specs=(pl.BlockSpec((tm_, tn_), lambda bt, i, j, kk: (i, j)),
                     pl.BlockSpec((None, 1, tn_),
                                  lambda bt, i, j, kk: (seg_of_rows(i * tm_) * 6 + gate_idx, 0, j))),
        epilogue=lambda acc, x_ref, g_ref, row_tile: x_ref[...] + g_ref[...] * acc, name=name)


def _dwconv_body(flags_ref, x_ref, p_ref, n_ref, w_ref, b_ref, o_ref, *, silu):
    i = pl.program_id(0)
    act = _silu if silu else (lambda v: v)
    x = x_ref[...].astype(F32)
    tr = x.shape[0]
    w = w_ref[...]
    y = pltpu.roll(x, 1, 0) * w[0:1, :] + x * w[1:2, :] + pltpu.roll(x, tr - 1, 0) * w[2:3, :] + b_ref[...]
    o_ref[...] = act(y).astype(o_ref.dtype)
    keep_prev = (flags_ref[0, i] == 0).astype(F32)
    keep_next = (flags_ref[1, i] == 0).astype(F32)
    halo_prev = p_ref[SUBLANES - 1:SUBLANES, :].astype(F32) * keep_prev
    halo_next = n_ref[0:1, :].astype(F32) * keep_next
    slab = BF16_SUBLANES
    row = lax.broadcasted_iota(jnp.int32, (slab, x.shape[1]), 0)
    fix_top = (halo_prev - x[tr - 1:tr, :]) * w[0:1, :]
    fix_bot = (halo_next - x[0:1, :]) * w[2:3, :]
    o_ref[0:slab, :] = act(y[0:slab] + jnp.where(row == 0, fix_top, 0.0)).astype(o_ref.dtype)
    o_ref[tr - slab:tr, :] = act(y[tr - slab:tr] + jnp.where(row == slab - 1, fix_bot, 0.0)).astype(o_ref.dtype)


def dwconv(proj, w, b, flags, *, col0, silu):
    t = proj.shape[0]
    cols = w.shape[1]
    tr = ROW_TILE
    tc = _tile(math.gcd(col0, cols), 2048)
    cb0 = col0 // tc
    hb = tr // SUBLANES
    last_hb = t // SUBLANES - 1
    grid_spec = pltpu.PrefetchScalarGridSpec(
        num_scalar_prefetch=1,
        grid=(t // tr, cols // tc),
        in_specs=[pl.BlockSpec((tr, tc), lambda i, j, f: (i, cb0 + j)),
                  pl.BlockSpec((SUBLANES, tc), lambda i, j, f: (jnp.maximum(i * hb - 1, 0), cb0 + j)),
                  pl.BlockSpec((SUBLANES, tc), lambda i, j, f: (jnp.minimum((i + 1) * hb, last_hb), cb0 + j)),
                  pl.BlockSpec((3, tc), lambda i, j, f: (0, j)),
                  pl.BlockSpec((1, tc), lambda i, j, f: (0, j))],
        out_specs=pl.BlockSpec((tr, tc), lambda i, j, f: (i, j)))
    return pl.pallas_call(
        functools.partial(_dwconv_body, silu=silu),
        out_shape=jax.ShapeDtypeStruct((t, cols), BF16),
        grid_spec=grid_spec,
        compiler_params=_cp("parallel", "parallel"),
        name="dwconv_silu" if silu else "dwconv",
    )(flags, proj, proj, proj, w, b.reshape(1, -1))


def _filter_body(feat_ref, w1_ref, b1_ref, f1_ref, w2_ref, b2_ref, f2_ref, w3_ref, dl_ref, o_ref):
    d = pl.program_id(0)
    i = pl.program_id(1)
    feat = feat_ref[...]
    dn = (((1,), (0,)), ((), ()))
    hid = jnp.sin(f1_ref[...] * (lax.dot_general(feat, w1_ref[...], dn, precision=HI,
                                                 preferred_element_type=F32) + b1_ref[...]))
    hid = jnp.sin(f2_ref[...] * (lax.dot_general(hid, w2_ref[...], dn, precision=HI,
                                                 preferred_element_type=F32) + b2_ref[...]))
    filt = jnp.dot(hid.astype(BF16), w3_ref[...].astype(BF16), preferred_element_type=F32)
    t01 = feat[:, 0:1]
    filt = filt * jnp.exp(-t01 * dl_ref[...])
    row = lax.broadcasted_iota(jnp.int32, filt.shape, 0) + i * filt.shape[0]
    filt = jnp.where((d == 1) & (row == 0), 0.0, filt)
    o_ref[...] = filt.astype(o_ref.dtype)


def hyena_filters(L, f_w1, f_b1, f_freq1, f_w2, f_b2, f_freq2, f_w3, c):
    t = jnp.arange(L, dtype=F32)
    t01 = t / (L - 1)
    bands = jnp.linspace(1e-4, HYENA_BANDS - 1, HYENA_BANDS, dtype=F32)
    ang = (2.0 * math.pi / L) * t[:, None] * bands[None, :]
    feat = jnp.concatenate([t01[:, None], jnp.cos(ang), -jnp.sin(ang)], axis=-1)
    emb = feat.shape[1]
    feat = jnp.pad(feat, ((0, 0), (0, LANES - emb)))
    w1 = jnp.pad(f_w1, ((0, LANES - emb), (0, 0)))
    ffn = f_w1.shape[1]
    order = f_w3.shape[1] // (2 * c)
    deltas = jnp.abs(jnp.linspace(math.log(HYENA_TARGET) / HYENA_SLOW_DECAY,
                                  math.log(HYENA_TARGET) / HYENA_FAST_DECAY, c, dtype=F32))
    deltas = jnp.tile(deltas, order).reshape(1, order * c)
    tl, tn = _tile(L, 512), _tile(order * c, 4096)
    nj = order * c // tn
    return pl.pallas_call(
        _filter_body,
        out_shape=jax.ShapeDtypeStruct((2, L, order * c), BF16),
        grid=(2, L // tl, nj),
        in_specs=[pl.BlockSpec((tl, LANES), lambda d, i, j: (i, 0)),
                  pl.BlockSpec((LANES, ffn), lambda d, i, j: (0, 0)),
                  pl.BlockSpec((1, ffn), lambda d, i, j: (0, 0)),
                  pl.BlockSpec((1, ffn), lambda d, i, j: (0, 0)),
                  pl.BlockSpec((ffn, ffn), lambda d, i, j: (0, 0)),
                  pl.BlockSpec((1, ffn), lambda d, i, j: (0, 0)),
                  pl.BlockSpec((1, ffn), lambda d, i, j: (0, 0)),
                  pl.BlockSpec((ffn, tn), lambda d, i, j: (0, d * nj + j)),
                  pl.BlockSpec((1, tn), lambda d, i, j: (0, j))],
        out_specs=pl.BlockSpec((None, tl, tn), lambda d, i, j: (d, i, j)),
        compiler_params=_cp("parallel", "parallel", "parallel"),
        name="hyena_filters",
    )(feat, w1, f_b1.reshape(1, ffn), f_freq1.reshape(1, ffn), f_w2, f_b2.reshape(1, ffn),
      f_freq2.reshape(1, ffn), f_w3, deltas)


def _cos_turns(m, log2_period):
    period = 1 << log2_period
    quarter = period // 4
    mm = m + quarter // 2
    q = (mm >> (log2_period - 2)) & 3
    r = (mm & (quarter - 1)) - quarter // 2
    phi = r.astype(F32) * (2.0 * math.pi / period)
    p2 = phi * phi
    cosv = 1.0 + p2 * (-1.0 / 2 + p2 * (1.0 / 24 + p2 * (-1.0 / 720 + p2 * (1.0 / 40320 - p2 / 3628800))))
    sinv = phi * (1.0 + p2 * (-1.0 / 6 + p2 * (1.0 / 120 + p2 * (-1.0 / 5040 + p2 / 362880))))
    return jnp.where(q == 0, cosv, jnp.where(q == 1, -sinv, jnp.where(q == 2, -cosv, sinv)))


def _spec_index(r, L):
    blk = r >> 8
    within = r & (2 * SPEC_BLOCK - 1)
    return blk * SPEC_BLOCK + (within & (SPEC_BLOCK - 1)), within >= SPEC_BLOCK


def _dft_body(o_ref, *, L, mode):
    log2n = int(math.log2(2 * L))
    shape = o_ref.shape
    i0 = pl.program_id(0) * shape[0]
    j0 = pl.program_id(1) * shape[1]
    rows = lax.broadcasted_iota(jnp.int32, shape, 0) + i0
    cols = lax.broadcasted_iota(jnp.int32, shape, 1) + j0
    if mode == "inv":
        r, n = cols, rows
    else:
        r, n = rows, cols
    k, im = _spec_index(r, L)
    back = n >= L
    n = jnp.where(back, n - L, n)
    phase = jnp.where(im, jnp.where(back, -(L // 2), L // 2), 0)
    val = _cos_turns((k * n + phase) & (2 * L - 1), log2n)
    nyq = jnp.where((n & 1) == 0, 1.0, -1.0)
    val = jnp.where(im & (k == 0), nyq, val)
    if mode == "inv":
        val = val * jnp.where(k == 0, 1.0 / (2 * L), 2.0 / (2 * L))
    o_ref[...] = val.astype(o_ref.dtype)


def dft_table(L, mode):
    assert L & (L - 1) == 0 and L >= 2 * SPEC_BLOCK
    shape = {"fwd": (2 * L, L), "filt": (2 * L, 2 * L), "inv": (L, 2 * L)}[mode]
    tr, tc = _tile(shape[0], 512), _tile(shape[1], 1024)
    return pl.pallas_call(
        functools.partial(_dft_body, L=L, mode=mode),
        out_shape=jax.ShapeDtypeStruct(shape, BF16),
        grid=(shape[0] // tr, shape[1] // tc),
        out_specs=pl.BlockSpec((tr, tc), lambda i, j: (i, j)),
        compiler_params=_cp("parallel", "parallel"),
        name="dft_table_" + mode,
    )()


def _spectral_epilogue(acc, kf_ref, row_tile):
    tm, tn = acc.shape
    nb = tm // (2 * SPEC_BLOCK)
    z = acc.reshape(nb, 2, SPEC_BLOCK, tn)
    kf = kf_ref[...].reshape(nb, 2, SPEC_BLOCK, tn)
    zr, zi, kr, ki = z[:, 0], z[:, 1], kf[:, 0], kf[:, 1]
    first = row_tile == 0
    blk = lax.broadcasted_iota(jnp.int32, zr.shape, 0)
    row = lax.broadcasted_iota(jnp.int32, zr.shape, 1)
    dc = first & (blk == 0) & (row == 0)
    pr = zr * kr - jnp.where(dc, 0.0, zi * ki)
    pi = jnp.where(dc, zi * ki, zr * ki + zi * kr)
    return jnp.stack([pr, pi], axis=1).reshape(tm, tn)


def hyena_stream(u, row0, nb, L, kf, f_tab, g_tab, hy_bias, c, fill=None):
    out_rows = u.shape[0]
    tn = _tile(c, 2048)
    cb = c // tn
    tm_f = _tile(2 * L, 512)
    tk_f = _tile(L, 2048)
    tm_i = _tile(L, 512)
    tk_i = _tile(2 * L, 2048)
    z, z_rows, z_col = u, row0, 2 * cb
    for o in range(2):
        zrb = z_rows // tk_f
        p = matmul(
            f_tab, z, grid=(nb, 2 * L // tm_f, cb, L // tk_f), tm=tm_f, tn=tn,
            a_spec=pl.BlockSpec((tm_f, tk_f), lambda bt, i, j, k: (i, k)),
            b_spec=pl.BlockSpec((tk_f, tn), lambda bt, i, j, k, zrb=zrb, zc=z_col:
                                (zrb + bt * (L // tk_f) + k, zc + j)),
            o_spec=pl.BlockSpec((None, tm_f, tn), lambda bt, i, j, k: (bt, i, j)),
            out_shape=(nb, 2 * L, c), out_dtype=BF16,
            extra=(kf,), extra_specs=(pl.BlockSpec((tm_f, tn), lambda bt, i, j, k, o=o: (i, o * cb + j)),),
            epilogue=_spectral_epilogue, name="hyena_fwd")
        zrb_i = z_rows // tm_i
        urb_i = row0 // tm_i
        last = o == 1
        orb = urb_i if last else 0
        z_new = matmul(
            g_tab, p, grid=(nb, L // tm_i, cb, 2 * L // tk_i), tm=tm_i, tn=tn,
            a_spec=pl.BlockSpec((tm_i, tk_i), lambda bt, i, j, k: (i, k)),
            b_spec=pl.BlockSpec((None, tk_i, tn), lambda bt, i, j, k: (bt, k, j)),
            o_spec=pl.BlockSpec((tm_i, tn), lambda bt, i, j, k, orb=orb: (orb + bt * (L // tm_i) + i, j)),
            out_shape=fill.shape if last else (nb * L, c), out_dtype=BF16, fill=fill if last else None,
            extra=(u, z, hy_bias.reshape(2, 1, c)),
            extra_specs=(pl.BlockSpec((tm_i, tn), lambda bt, i, j, k, o=o:
                                      (urb_i + bt * (L // tm_i) + i, o * cb + j)),
                         pl.BlockSpec((tm_i, tn), lambda bt, i, j, k, zrb_i=zrb_i, zc=z_col:
                                      (zrb_i + bt * (L // tm_i) + i, zc + j)),
                         pl.BlockSpec((None, 1, tn), lambda bt, i, j, k, o=o: (o, 0, j))),
            epilogue=lambda acc, g_ref, z_ref, b_ref, row_tile:
                g_ref[...].astype(F32) * (acc + b_ref[...] * z_ref[...].astype(F32)),
            name="hyena_inv")
        z, z_rows, z_col = z_new, 0, 0
    return z


def hyena_spectrum(L, filt, a_tab):
    n = filt.shape[2]
    b = filt.reshape(2 * L, n)
    tm, tn, tk = _tile(2 * L, 512), _tile(n, 512), _tile(2 * L, 2048)
    return matmul(
        a_tab, b, grid=(1, 2 * L // tm, n // tn, 2 * L // tk), tm=tm, tn=tn,
        a_spec=pl.BlockSpec((tm, tk), lambda bt, i, j, k: (i, k)),
        b_spec=pl.BlockSpec((tk, tn), lambda bt, i, j, k: (k, j)),
        o_spec=pl.BlockSpec((tm, tn), lambda bt, i, j, k: (i, j)),
        out_shape=(2 * L, n), out_dtype=F32, name="hyena_spectrum")


def _fft_tab_body(o_ref, *, kind, n1, n2, k1_len):
    shape = o_ref.shape
    r = lax.broadcasted_iota(jnp.int32, shape, 0) + pl.program_id(0) * shape[0]
    c = lax.broadcasted_iota(jnp.int32, shape, 1)
    n = n1 * n2
    lg = lambda v: int(math.log2(v))
    s8 = BF16_SUBLANES
    if kind in ("t1f", "t1i"):
        if kind == "t1f":
            grp, kp, s, n1i, s2 = r >> lg(2 * n1 * s8), (r >> lg(s8)) & (2 * n1 - 1), r & (s8 - 1), c >> lg(s8), c & (s8 - 1)
        else:
            grp, n1i, s, kp, s2 = r >> lg(k1_len * s8), (r >> lg(s8)) & (k1_len - 1), r & (s8 - 1), c >> lg(s8), c & (s8 - 1)
        t = n2 * n1i + grp * s8 + s
        val = _cos_turns(((kp & (n1 - 1)) * t + (kp >> lg(n1)) * (n // 4)) & (n - 1), lg(n))
        val = jnp.where(s == s2, val * (1.0 / n if kind == "t1i" else 1.0), 0.0)
    else:
        rp = r >= n2
        cblk = c >> lg(n2)
        cp = (cblk & 1) == 1
        sin_sign = -1 if kind == "f2i" else 1
        quarter = jnp.where(rp == cp, 0, jnp.where(cp, -sin_sign, sin_sign))
        val = _cos_turns(((r & (n2 - 1)) * (c & (n2 - 1)) + quarter * (n2 // 4)) & (n2 - 1), lg(n2))
        if kind == "f2filt":
            val = jnp.where((cblk >= 2) & rp, -val, val)
    o_ref[...] = val.astype(o_ref.dtype)


def fft_table(kind, n1, n2):
    k1_len = n1 // 2
    s8 = BF16_SUBLANES
    groups = n2 // s8
    shape = {"t1f": (groups * 2 * n1 * s8, k1_len * s8), "t1i": (groups * k1_len * s8, 2 * n1 * s8),
             "f2f": (2 * n2, 2 * n2), "f2i": (2 * n2, 2 * n2), "f2filt": (2 * n2, 4 * n2)}[kind]
    tr = _tile(shape[0], 1024)
    out = pl.pallas_call(
        functools.partial(_fft_tab_body, kind=kind, n1=n1, n2=n2, k1_len=k1_len),
        out_shape=jax.ShapeDtypeStruct(shape, BF16),
        grid=(shape[0] // tr,),
        out_specs=pl.BlockSpec((tr, shape[1]), lambda i: (i, 0)),
        compiler_params=_cp("parallel"),
        name="fft_table_" + kind,
    )()
    if kind in ("t1f", "t1i"):
        return out.reshape(groups, shape[0] // groups, shape[1])
    return out


FFT_K1_PER_STEP = 8


def _fft_mid_body(f_ref, *refs, mode, n2):
    f = f_ref[...]
    if mode == "inv":
        p_ref, o_ref = refs
        for k in range(FFT_K1_PER_STEP):
            res = jnp.dot(f, p_ref[k * 2 * n2:(k + 1) * 2 * n2, :], preferred_element_type=F32)
            o_ref[0, k * n2:(k + 1) * n2, :] = res[:n2].astype(o_ref.dtype)
            o_ref[1, k * n2:(k + 1) * n2, :] = res[n2:].astype(o_ref.dtype)
        return
    if mode == "conv":
        finv_ref, y_ref, kf_ref, o_ref = refs
        srcs = (y_ref,)
    elif mode == "fwd":
        y_ref, kf_ref, o_ref = refs
        srcs = (y_ref,)
    else:
        y0_ref, y1_ref, o_ref = refs
        srcs = (y0_ref, y1_ref)
    for k in range(FFT_K1_PER_STEP):
        rows = slice(k * n2, (k + 1) * n2)
        stack = jnp.concatenate([s[part, rows, :] for s in srcs for part in range(2)], axis=0)
        acc = jnp.dot(f, stack, preferred_element_type=F32)
        out_rows = slice(k * 2 * n2, (k + 1) * 2 * n2)
        if mode in ("fwd", "conv"):
            zr, zi = acc[:n2], acc[n2:]
            kr = kf_ref[k * 2 * n2:k * 2 * n2 + n2, :].astype(F32)
            ki = kf_ref[k * 2 * n2 + n2:(k + 1) * 2 * n2, :].astype(F32)
            acc = jnp.concatenate([zr * kr - zi * ki, zr * ki + zi * kr], axis=0)
        if mode == "conv":
            res = jnp.dot(finv_ref[...], acc.astype(BF16), preferred_element_type=F32)
            o_ref[0, rows, :] = res[:n2].astype(o_ref.dtype)
            o_ref[1, rows, :] = res[n2:].astype(o_ref.dtype)
        else:
            o_ref[out_rows, :] = acc.astype(o_ref.dtype)


def fft_mid(mode, f_tab, src, n1, n2, *, kf=None, kf_col0=0, f_inv=None, out_dtype=BF16, name):
    kb = FFT_K1_PER_STEP
    c = src.shape[-1]
    tn = _tile(c, 1024)
    cb = c // tn
    nb = 1 if mode == "filt" else src.shape[0]
    y_block = (None, 2, kb * n2, tn)
    p_block = (None, kb * 2 * n2, tn)
    f_spec = pl.BlockSpec(f_tab.shape, lambda i, b, j: (0, 0))
    if mode == "conv":
        in_specs = [f_spec, pl.BlockSpec(f_inv.shape, lambda i, b, j: (0, 0)),
                    pl.BlockSpec(y_block, lambda i, b, j: (b, 0, i, j)),
                    pl.BlockSpec((kb * 2 * n2, tn), lambda i, b, j: (i, kf_col0 // tn + j))]
        args = (f_tab, f_inv, src, kf)
        out_shape, out_spec = (nb, 2, n1 * n2, c), pl.BlockSpec(y_block, lambda i, b, j: (b, 0, i, j))
    elif mode == "fwd":
        in_specs = [f_spec, pl.BlockSpec(y_block, lambda i, b, j: (b, 0, i, j)),
                    pl.BlockSpec((kb * 2 * n2, tn), lambda i, b, j: (i, kf_col0 // tn + j))]
        args = (f_tab, src, kf)
        out_shape, out_spec = (nb, n1 * 2 * n2, c), pl.BlockSpec(p_block, lambda i, b, j: (b, i, j))
    elif mode == "filt":
        in_specs = [f_spec, pl.BlockSpec(y_block, lambda i, b, j: (0, 0, i, j)),
                    pl.BlockSpec(y_block, lambda i, b, j: (1, 0, i, j))]
        args = (f_tab, src, src)
        out_shape, out_spec = (n1 * 2 * n2, c), pl.BlockSpec((kb * 2 * n2, tn), lambda i, b, j: (i, j))
    else:
        in_specs = [f_spec, pl.BlockSpec(p_block, lambda i, b, j: (b, i, j))]
        args = (f_tab, src)
        out_shape, out_spec = (nb, 2, n1 * n2, c), pl.BlockSpec(y_block, lambda i, b, j: (b, 0, i, j))
    return pl.pallas_call(
        functools.partial(_fft_mid_body, mode=mode, n2=n2),
        out_shape=jax.ShapeDtypeStruct(out_shape, out_dtype),
        grid=(n1 // kb, nb, cb),
        in_specs=in_specs,
        out_specs=out_spec,
        compiler_params=_cp("parallel", "parallel", "parallel"),
        name=name,
    )(*args)


def hyena_spectrum_2level(L, filt, t1f, f2filt):
    n2 = SPEC_BLOCK
    n1 = 2 * L // n2
    k1 = n1 // 2
    n = filt.shape[2]
    tn = _tile(n, 1024)
    nj = n // tn
    sl = BF16_SUBLANES
    y1 = matmul(
        t1f, filt.reshape(2 * k1, n2, n), grid=(2, n2 // sl, nj, 1), tm=2 * n1 * sl, tn=tn,
        a_spec=pl.BlockSpec((None, 2 * n1 * sl, k1 * sl), lambda bt, i, j, k: (i, 0, 0)),
        b_spec=pl.BlockSpec((k1, sl, tn), lambda bt, i, j, k: (bt, i, j)),
        o_spec=pl.BlockSpec((None, 2 * n1, sl, tn), lambda bt, i, j, k: (bt, 0, i, j)),
        out_shape=(2, 2 * n1, n2, n), out_dtype=BF16, name="hyena_filt_fft1")
    return fft_mid("filt", f2filt, y1.reshape(2, 2, n1 * n2, n), n1, n2, name="hyena_filt_fft2")


def hyena_stream_2level(u, row0, nb, L, kf, tabs, hy_bias, c, fill=None):
    t1f, f2f, f2i, t1i = tabs
    n2 = SPEC_BLOCK
    n1 = 2 * L // n2
    k1 = n1 // 2
    sl = BF16_SUBLANES
    ng = n2 // sl
    t1n = _tile(c, 1024)
    c1b = c // t1n
    tn = _tile(c, 2048)
    cb = c // tn
    t, u_w = u.shape
    assert row0 % L == 0 and u_w % tn == 0
    uv = u.reshape(t // n2, n2, u_w)
    urow = row0 // L
    zv, zr, zc = uv, urow, 2 * c // t1n
    bias3 = hy_bias.reshape(2, 1, c)
    for o in range(2):
        y1 = matmul(
            t1f, zv, grid=(nb, ng, c1b, 1), tm=2 * n1 * sl, tn=t1n,
            a_spec=pl.BlockSpec((None, 2 * n1 * sl, k1 * sl), lambda bt, i, j, k: (i, 0, 0)),
            b_spec=pl.BlockSpec((k1, sl, t1n), lambda bt, i, j, k, zr=zr, zc=zc: (zr + bt, i, zc + j)),
            o_spec=pl.BlockSpec((None, 2 * n1, sl, t1n), lambda bt, i, j, k: (bt, 0, i, j)),
            out_shape=(nb, 2 * n1, n2, c), out_dtype=BF16, name="hyena_fft1")
        u1 = fft_mid("conv", f2f, y1.reshape(nb, 2, n1 * n2, c), n1, n2, kf=kf, kf_col0=o * c, f_inv=f2i,
                     name="hyena_fft2_ifft1")
        last = o == 1
        orow = urow if last else 0

        def gate_epilogue(acc, g_ref, z_ref, b_ref, row_tile):
            g = g_ref[...].reshape(acc.shape).astype(F32)
            return g * (acc + b_ref[...] * z_ref[...].reshape(acc.shape).astype(F32))

        z = matmul(
            t1i, u1.reshape(nb, 2 * n1, n2, c), grid=(nb, ng, c1b, 1), tm=k1 * sl, tn=t1n,
            a_spec=pl.BlockSpec((None, k1 * sl, 2 * n1 * sl), lambda bt, i, j, k: (i, 0, 0)),
            b_spec=pl.BlockSpec((None, 2 * n1, sl, t1n), lambda bt, i, j, k: (bt, 0, i, j)),
            o_spec=pl.BlockSpec((k1, sl, t1n), lambda bt, i, j, k, orow=orow: (orow + bt, i, j)),
            out_shape=(t // n2, n2, fill.shape[1]) if last else (nb * k1, n2, c), out_dtype=BF16,
            fill=fill.reshape(t // n2, n2, fill.shape[1]) if last else None,
            extra=(uv, zv, bias3),
            extra_specs=(pl.BlockSpec((k1, sl, t1n), lambda bt, i, j, k, o=o: (urow + bt, i, o * c1b + j)),
                         pl.BlockSpec((k1, sl, t1n), lambda bt, i, j, k, zr=zr, zc=zc: (zr + bt, i, zc + j)),
                         pl.BlockSpec((None, 1, t1n), lambda bt, i, j, k, o=o: (o, 0, j))),
            epilogue=gate_epilogue, name="hyena_ifft2")
        zv, zr, zc = z, 0, 0
    return zv.reshape(t, fill.shape[1])


SSD_CUM_TERMS = 3


def _ssd_prep_body(dc_ref, dr_ref, biasc_ref, biasr_ref, ac_ref, ar_ref, fac_ref, cumr_ref, *, heads):
    q = dc_ref.shape[0]
    dn = (((1,), (0,)), ((), ()))
    li = lax.broadcasted_iota(jnp.int32, (q, q), 0)
    si = lax.broadcasted_iota(jnp.int32, (q, q), 1)
    lower = (li >= si).astype(F32)
    upper = (li <= si).astype(F32)
    dt_c = _softplus(dc_ref[...] + biasc_ref[...])
    da_c = dt_c * ac_ref[...]
    da_r = _softplus(dr_ref[...] + biasr_ref[...]) * ar_ref[...]
    pre_c = lax.dot_general(lower, da_c, dn, precision=HI, preferred_element_type=F32)
    suf_c = lax.dot_general(upper, da_c, dn, precision=HI, preferred_element_type=F32)
    pre_r = lax.dot_general(da_r, upper, dn, precision=HI, preferred_element_type=F32)
    suf_r = lax.dot_general(da_r, lower, dn, precision=HI, preferred_element_type=F32)
    cum_c = jnp.where(lax.broadcasted_iota(jnp.int32, pre_c.shape, 1) < heads, pre_c, suf_c)
    terms, rest = [], cum_c
    for _ in range(SSD_CUM_TERMS):
        term = rest.astype(BF16)
        rest = rest - term.astype(F32)
        terms.append(term)
    fac_ref[...] = jnp.concatenate(terms + [dt_c.astype(BF16)], axis=1)
    cumr_ref[...] = jnp.where(lax.broadcasted_iota(jnp.int32, pre_r.shape, 0) < heads, pre_r, suf_r)


def ssd_prep(dt_raw, dt_raw_t, dt_bias, a_log):
    t, h2 = dt_raw.shape
    q = SSD_CHUNK
    neg_a = -jnp.exp(a_log.astype(F32)).reshape(1, h2)
    bias = dt_bias.astype(F32).reshape(1, h2)
    const = lambda shape: pl.BlockSpec(shape, lambda i: (0,) * len(shape))
    col = pl.BlockSpec((q, h2), lambda i: (i, 0))
    row = pl.BlockSpec((h2, q), lambda i: (0, i))
    nf = (SSD_CUM_TERMS + 1) * h2
    return pl.pallas_call(
        functools.partial(_ssd_prep_body, heads=h2 // 2),
        out_shape=(jax.ShapeDtypeStruct((t, nf), BF16), jax.ShapeDtypeStruct((h2, t), F32)),
        grid=(t // q,),
        in_specs=[col, row, const((1, h2)), const((h2, 1)), const((1, h2)), const((h2, 1))],
        out_specs=(pl.BlockSpec((q, nf), lambda i: (i, 0)), row),
        compiler_params=_cp("parallel"),
        name="ssd_prep",
    )(dt_raw, dt_raw_t, bias, bias.reshape(h2, 1), neg_a, neg_a.reshape(h2, 1))


def ssd_selectors(h2, j, p, q):
    w_sum, w_wide = j * p, j * q
    k = (SSD_CUM_TERMS + 1) * h2
    rows = jnp.arange(k, dtype=jnp.int32)[None, :, None]
    cols = jnp.arange(w_sum + w_wide + j * p, dtype=jnp.int32)[None, None, :]
    head0 = j * jnp.arange(h2 // j, dtype=jnp.int32)[:, None, None]
    col_head = jnp.where(cols < w_sum, cols // p, jnp.where(cols < w_sum + w_wide, (cols - w_sum) // q,
                                                              (cols - w_sum - w_wide) // p))
    is_dt_row = rows // h2 == SSD_CUM_TERMS
    is_dt_col = cols >= w_sum + w_wide
    sel = (rows % h2 == head0 + col_head) & (is_dt_row == is_dt_col)
    return sel.astype(BF16)


def _ssd_dir(xs, bt, cm, fac, sel, diag_mask, cumr_ref, state_ref, head0, *, reverse, heads_per_group):
    q = xs.shape[0]
    p = SSD_HEAD_DIM
    j = heads_per_group
    spread = jnp.dot(fac, sel, preferred_element_type=F32)
    cc, cc_wide, dtv = spread[:, :j * p], spread[:, j * p:j * p + j * q], spread[:, j * p + j * q:]
    total = cc[0:1, :] if reverse else cc[q - 1:q, :]
    xd = xs.astype(F32) * dtv
    xd_b = xd.astype(BF16)
    li = lax.broadcasted_iota(jnp.int32, (q, q), 0)
    si = lax.broadcasted_iota(jnp.int32, (q, q), 1)
    mask = (li <= si) if reverse else (li >= si)
    cb = jnp.dot(cm, bt, preferred_element_type=F32)
    weights = []
    for jh in range(j):
        cr = cumr_ref[pl.ds(head0 + jh, 1), :]
        decay = jnp.exp(jnp.where(mask, cc_wide[:, jh * q:(jh + 1) * q] - cr, -jnp.inf))
        weights.append((cb * decay).astype(BF16))
    xd_diag = jnp.concatenate([xd_b] * j, axis=0) * diag_mask
    y = jnp.dot(jnp.concatenate(weights, axis=1), xd_diag, preferred_element_type=F32)
    s_prev = state_ref[...]
    y = y + jnp.exp(cc) * jnp.dot(cm, s_prev.astype(BF16), preferred_element_type=F32)
    st = jnp.dot(bt, (xd * jnp.exp(total - cc)).astype(BF16), preferred_element_type=F32)
    state_ref[...] = jnp.exp(total) * s_prev + st
    return y


def _ssd_body(tab_ref, xf_ref, btf_ref, cf_ref, facf_ref, crf_ref, xb_ref, btb_ref, cb_ref, facb_ref, crb_ref,
              self_ref, selb_ref, mask_ref, initf_ref, initb_ref, yf_ref, yb_ref, sf_ref, sb_ref, stf_ref, stb_ref,
              *, heads_per_group, heads):
    g = pl.program_id(0)
    item = pl.program_id(1)

    @pl.when(tab_ref[_SSD_FIRST, item] == 1)
    def _():
        stf_ref[...] = initf_ref[...]
        stb_ref[...] = initb_ref[...]

    jp = heads_per_group * SSD_HEAD_DIM
    n = SSD_STATE
    for gi in range(SSD_GROUPS_PER_STEP):
        head0 = (g * SSD_GROUPS_PER_STEP + gi) * heads_per_group
        cols, srow = slice(gi * jp, (gi + 1) * jp), slice(gi * n, (gi + 1) * n)
        yf_ref[:, cols] = _ssd_dir(xf_ref[:, cols], btf_ref[srow, :], cf_ref[:, srow], facf_ref[...], self_ref[gi],
                                   mask_ref[...], crf_ref, stf_ref.at[gi], head0, reverse=False,
                                   heads_per_group=heads_per_group).astype(yf_ref.dtype)
        yb_ref[:, cols] = _ssd_dir(xb_ref[:, cols], btb_ref[srow, :], cb_ref[:, srow], facb_ref[...], selb_ref[gi],
                                   mask_ref[...], crb_ref, stb_ref.at[gi], heads + head0, reverse=True,
                                   heads_per_group=heads_per_group).astype(yb_ref.dtype)

    @pl.when(tab_ref[_SSD_LAST, item] == 1)
    def _():
        sf_ref[...] = stf_ref[...]
        sb_ref[...] = stb_ref[...]


_SSD_FWD, _SSD_BWD, _SSD_SEQ, _SSD_FIRST, _SSD_LAST = range(5)
SSD_GROUPS_PER_STEP = 8


def ssd_scan(u, bt_all, fac, cum_r, init_f, init_b, *, seq_lens, col0, inner):
    q, p, n, gq = SSD_CHUNK, SSD_HEAD_DIM, SSD_STATE, SSD_GROUPS
    heads = inner // p
    j = heads // gq
    jp = j * p
    xcb = col0 // jp
    ccb = (col0 + inner + gq * n) // n
    rows, base = [], 0
    for s, length in enumerate(seq_lens):
        nc = length // q
        rows += [(base + c, base + nc - 1 - c, s, int(c == 0), int(c == nc - 1)) for c in range(nc)]
        base += nc
    table = jnp.array(rows, jnp.int32).T
    nseq, n_items = len(seq_lens), len(rows)

    gs = SSD_GROUPS_PER_STEP
    assert xcb % gs == 0 and ccb % gs == 0 and gq % gs == 0

    def specs(which):
        return [pl.BlockSpec((q, gs * jp), lambda g, it, tab: (tab[which, it], xcb // gs + g)),
                pl.BlockSpec((gs * n, q), lambda g, it, tab: (g, tab[which, it])),
                pl.BlockSpec((q, gs * n), lambda g, it, tab: (tab[which, it], ccb // gs + g)),
                pl.BlockSpec((q, fac.shape[1]), lambda g, it, tab: (tab[which, it], 0)),
                pl.BlockSpec((2 * heads, q), lambda g, it, tab: (0, tab[which, it]))]

    st_spec = pl.BlockSpec((None, gs, n, jp), lambda g, it, tab: (tab[_SSD_SEQ, it], g, 0, 0))
    st_shape = jax.ShapeDtypeStruct((nseq, gq, n, jp), F32)
    y_shape = jax.ShapeDtypeStruct((u.shape[0], inner), BF16)
    sel_all = ssd_selectors(2 * heads, j, p, q)
    sel_block = (gs,) + sel_all.shape[1:]
    diag_mask = (jnp.arange(j * q)[:, None] // q == jnp.arange(jp)[None, :] // p).astype(BF16)
    grid_spec = pltpu.PrefetchScalarGridSpec(
        num_scalar_prefetch=1,
        grid=(gq // gs, n_items),
        in_specs=[*specs(_SSD_FWD), *specs(_SSD_BWD),
                  pl.BlockSpec(sel_block, lambda g, it, tab: (g, 0, 0)),
                  pl.BlockSpec(sel_block, lambda g, it, tab: (gq // gs + g, 0, 0)),
                  pl.BlockSpec(diag_mask.shape, lambda g, it, tab: (0, 0)),
                  st_spec, st_spec],
        out_specs=(pl.BlockSpec((q, gs * jp), lambda g, it, tab: (tab[_SSD_FWD, it], g)),
                   pl.BlockSpec((q, gs * jp), lambda g, it, tab: (tab[_SSD_BWD, it], g)), st_spec, st_spec),
        scratch_shapes=[pltpu.VMEM((gs, n, jp), F32), pltpu.VMEM((gs, n, jp), F32)])
    return pl.pallas_call(
        functools.partial(_ssd_body, heads_per_group=j, heads=heads),
        out_shape=(y_shape, y_shape, st_shape, st_shape),
        grid_spec=grid_spec,
        compiler_params=_cp("parallel", "arbitrary"),
        name="ssd_scan",
    )(table, u, bt_all, u, fac, cum_r, u, bt_all, u, fac, cum_r, sel_all, sel_all, diag_mask, init_f, init_b)


def _state_to_kernel_layout(s, groups):
    b, h, p, n = s.shape
    return s.reshape(b, groups, h // groups, p, n).transpose(0, 1, 4, 2, 3).reshape(b, groups, n, (h // groups) * p)


def _state_from_kernel_layout(s, head_dim):
    b, g, n, jp = s.shape
    j = jp // head_dim
    return s.reshape(b, g, n, j, head_dim).transpose(0, 1, 3, 4, 2).reshape(b, g * j, head_dim, n)


def _ssd_finish_body(yf_ref, yb_ref, xs_ref, z_ref, d_ref, w_ref, o_ref):
    inner = yf_ref.shape[1]
    lead = o_ref.shape[1] - inner
    y = yf_ref[...].astype(F32) + yb_ref[...].astype(F32) + xs_ref[...].astype(F32) * d_ref[...]
    y = y * _silu(z_ref[...].astype(F32))
    y = y * lax.rsqrt(jnp.mean(y * y, axis=-1, keepdims=True) + RMS_EPS)
    o_ref[:, :lead] = jnp.zeros((o_ref.shape[0], lead), o_ref.dtype)
    o_ref[:, lead:] = (y * w_ref[...]).astype(o_ref.dtype)


def ssd_finish(yf, yb, u, proj, d_full, norm_w, *, xs_col, z_col, inner, lead_cols):
    t = yf.shape[0]
    tr = ROW_TILE
    return pl.pallas_call(
        _ssd_finish_body,
        out_shape=jax.ShapeDtypeStruct((t, lead_cols + inner), BF16),
        grid=(t // tr,),
        in_specs=[pl.BlockSpec((tr, inner), lambda i: (i, 0)),
                  pl.BlockSpec((tr, inner), lambda i: (i, 0)),
                  pl.BlockSpec((tr, inner), lambda i: (i, xs_col // inner)),
                  pl.BlockSpec((tr, inner), lambda i: (i, z_col // inner)),
                  pl.BlockSpec((1, inner), lambda i: (0, 0)),
                  pl.BlockSpec((1, inner), lambda i: (0, 0))],
        out_specs=pl.BlockSpec((tr, lead_cols + inner), lambda i: (i, 0)),
        compiler_params=_cp("parallel"),
        name="ssd_finish",
    )(yf, yb, u, proj, d_full.reshape(1, inner), norm_w.reshape(1, inner))


KV_PER_STEP = LANES // HEAD_DIM


_NT = (((1,), (1,)), ((), ()))


def _attend(scores, values, sink):
    m = sink
    for s in scores:
        m = jnp.maximum(m, jnp.max(s, axis=-1, keepdims=True))
    denom = jnp.exp(sink - m)
    acc = None
    for s, v in zip(scores, values):
        e = jnp.exp(s - m)
        denom = denom + jnp.sum(e, axis=-1, keepdims=True)
        part = jnp.dot(e.astype(BF16), v, preferred_element_type=F32)
        acc = part if acc is None else acc + part
    return acc * (1.0 / denom)


def _stack_heads(q, first_head, n, rows):
    return jnp.concatenate([q[:, (first_head + g) * HEAD_DIM:(first_head + g + 1) * HEAD_DIM] for g in range(n)],
                           axis=0)


def _stacked_sinks(sink_ref, first_head, n, rows):
    row = lax.broadcasted_iota(jnp.int32, (n * rows, 1), 0)
    col = jnp.full((n * rows, 1), sink_ref[first_head + n - 1], F32)
    for g in reversed(range(n - 1)):
        col = jnp.where(row < (g + 1) * rows, sink_ref[first_head + g], col)
    return col


def _ctx_attn_body(sink_ref, q_ref, k_ref, v_ref, fill_ref, o_ref, *, q_per_kv, scale):
    del fill_ref
    hb = pl.program_id(1)
    rows = q_ref.shape[0]
    q_all = q_ref[...].astype(F32) * scale
    outs = []
    for kv in range(KV_PER_STEP):
        k = k_ref[:, kv * HEAD_DIM:(kv + 1) * HEAD_DIM].astype(BF16)
        v = v_ref[:, kv * HEAD_DIM:(kv + 1) * HEAD_DIM].astype(BF16)
        qs = _stack_heads(q_all, kv * q_per_kv, q_per_kv, rows).astype(BF16)
        s = lax.dot_general(qs, k, _NT, preferred_element_type=F32)
        sink = _stacked_sinks(sink_ref, (hb * KV_PER_STEP + kv) * q_per_kv, q_per_kv, rows)
        o = _attend([s], [v], sink)
        outs += [o[g * rows:(g + 1) * rows] for g in range(q_per_kv)]
    o_ref[...] = jnp.concatenate(outs, axis=1).astype(o_ref.dtype)


def context_attention(q, kv, sink, *, nseq, L, n_heads):
    q_per_kv = n_heads // N_KV_HEADS
    qw = KV_PER_STEP * q_per_kv * HEAD_DIM
    nhb = N_KV_HEADS // KV_PER_STEP
    kcb = 0
    vcb = N_KV_HEADS * HEAD_DIM // LANES
    return pl.pallas_call(
        functools.partial(_ctx_attn_body, q_per_kv=q_per_kv, scale=HEAD_DIM ** -0.5),
        out_shape=jax.ShapeDtypeStruct(q.shape, BF16),
        grid=(nseq, nhb),
        in_specs=[pl.BlockSpec(memory_space=pltpu.SMEM),
                  pl.BlockSpec((L, qw), lambda b, h: (b, h)),
                  pl.BlockSpec((L, LANES), lambda b, h: (b, kcb + h)),
                  pl.BlockSpec((L, LANES), lambda b, h: (b, vcb + h)),
                  pl.BlockSpec(memory_space=pl.ANY)],
        out_specs=pl.BlockSpec((L, qw), lambda b, h: (b, h)),
        input_output_aliases={4: 0},
        compiler_params=_cp("parallel", "parallel"),
        name="context_attention",
    )(sink, q, kv, kv, zeros_2d(q.shape[0], n_heads * HEAD_DIM, BF16))


def _rope(x, cos, sin):
    outs = []
    lane = lax.broadcasted_iota(jnp.int32, cos.shape, 1)
    first = (lane % (HEAD_DIM // 2)) < (HEAD_DIM // 4)
    for cgrp in range(x.shape[1] // LANES):
        xc = x[:, cgrp * LANES:(cgrp + 1) * LANES]
        partner = jnp.where(first, pltpu.roll(xc, LANES - HEAD_DIM // 4, 1), pltpu.roll(xc, HEAD_DIM // 4, 1))
        outs.append(xc * cos + partner * sin)
    return outs[0] if len(outs) == 1 else jnp.concatenate(outs, axis=1)


def _lat_attn_body(sink_ref, q_ref, k_ref, v_ref, kc_ref, vc_ref, cosq_ref, sinq_ref, cos_ref, sin_ref, fill_ref,
                   o_ref, k_scr, v_scr, kc_scr, vc_scr, *, q_per_kv, scale, L, blk):
    del fill_ref
    hb = pl.program_id(1)
    i = pl.program_id(2)

    @pl.when(i == 0)
    def _():
        k_scr[...] = _rope(k_ref[...], cos_ref[...], sin_ref[...]).astype(BF16)
        v_scr[...] = v_ref[...].astype(BF16)
        kc_scr[...] = kc_ref[...].astype(BF16)
        vc_scr[...] = vc_ref[...].astype(BF16)

    n_loc = 3 * blk
    start = jnp.clip((i - 1) * blk, 0, L - n_loc)
    start = pl.multiple_of(start, blk)
    q_all = _rope(q_ref[...].astype(F32), cosq_ref[...], sinq_ref[...]) * scale
    k_loc = k_scr[pl.ds(start, n_loc), :]
    v_loc = v_scr[pl.ds(start, n_loc), :]
    rows = q_per_kv * blk
    qpos = i * blk + (lax.broadcasted_iota(jnp.int32, (rows, n_loc), 0) & (blk - 1))
    kpos = start + lax.broadcasted_iota(jnp.int32, (rows, n_loc), 1)
    ok = jnp.abs(qpos - kpos) <= WINDOW
    outs = []
    for kv in range(KV_PER_STEP):
        sl = slice(kv * HEAD_DIM, (kv + 1) * HEAD_DIM)
        qs = _stack_heads(q_all, kv * q_per_kv, q_per_kv, blk).astype(BF16)
        s_loc = jnp.where(ok, lax.dot_general(qs, k_loc[:, sl], _NT, preferred_element_type=F32), -jnp.inf)
        s_ctx = lax.dot_general(qs, kc_scr[:, sl], _NT, preferred_element_type=F32)
        sink = _stacked_sinks(sink_ref, (hb * KV_PER_STEP + kv) * q_per_kv, q_per_kv, blk)
        o = _attend([s_loc, s_ctx], [v_loc[:, sl], vc_scr[:, sl]], sink)
        outs += [o[g * blk:(g + 1) * blk] for g in range(q_per_kv)]
    o_ref[...] = jnp.concatenate(outs, axis=1).astype(o_ref.dtype)


def rope_tables(L):
    rows = L // GRID_W
    row = jnp.repeat(jnp.arange(rows, dtype=F32), GRID_W)
    col = jnp.tile(jnp.arange(GRID_W, dtype=F32), rows)
    quarter = HEAD_DIM // 4
    inv = ROPE_BASE ** (-jnp.arange(quarter, dtype=F32) / quarter)
    ang_r = row[:, None] * inv[None, :]
    ang_c = col[:, None] * inv[None, :]
    ang = jnp.concatenate([ang_r, ang_r, ang_c, ang_c], axis=1)
    sign = jnp.tile(jnp.concatenate([-jnp.ones((quarter,), F32), jnp.ones((quarter,), F32)]), 2)
    cos = jnp.tile(jnp.cos(ang), (1, LANES // HEAD_DIM))
    sin = jnp.tile(jnp.sin(ang) * sign[None, :], (1, LANES // HEAD_DIM))
    return cos, sin


def latent_attention(q, kv, k_cache, v_cache, sink, fill, *, row0, nseq, L, n_heads):
    blk = WINDOW
    q_per_kv = n_heads // N_KV_HEADS
    qw = KV_PER_STEP * q_per_kv * HEAD_DIM
    nhb = N_KV_HEADS // KV_PER_STEP
    kcb = 0
    vcb = N_KV_HEADS * HEAD_DIM // LANES
    nb = L // blk
    past = k_cache.shape[1]
    cos, sin = rope_tables(L)
    return pl.pallas_call(
        functools.partial(_lat_attn_body, q_per_kv=q_per_kv, scale=HEAD_DIM ** -0.5, L=L, blk=blk),
        out_shape=jax.ShapeDtypeStruct(fill.shape, BF16),
        grid=(nseq, nhb, nb),
        in_specs=[pl.BlockSpec(memory_space=pltpu.SMEM),
                  pl.BlockSpec((blk, qw), lambda b, h, i: (row0 // blk + b * nb + i, h)),
                  pl.BlockSpec((L, LANES), lambda b, h, i: (row0 // L + b, kcb + h)),
                  pl.BlockSpec((L, LANES), lambda b, h, i: (row0 // L + b, vcb + h)),
                  pl.BlockSpec((None, past, LANES), lambda b, h, i: (b, 0, h)),
                  pl.BlockSpec((None, past, LANES), lambda b, h, i: (b, 0, h)),
                  pl.BlockSpec((blk, LANES), lambda b, h, i: (i, 0)),
                  pl.BlockSpec((blk, LANES), lambda b, h, i: (i, 0)),
                  pl.BlockSpec((L, LANES), lambda b, h, i: (0, 0)),
                  pl.BlockSpec((L, LANES), lambda b, h, i: (0, 0)),
                  pl.BlockSpec(memory_space=pl.ANY)],
        out_specs=pl.BlockSpec((blk, qw), lambda b, h, i: (row0 // blk + b * nb + i, h)),
        scratch_shapes=[pltpu.VMEM((L, LANES), BF16), pltpu.VMEM((L, LANES), BF16),
                        pltpu.VMEM((past, LANES), BF16), pltpu.VMEM((past, LANES), BF16)],
        input_output_aliases={10: 0},
        compiler_params=_cp("parallel", "parallel", "arbitrary"),
        name="latent_attention",
    )(sink, q, kv, kv, k_cache, v_cache, cos, sin, cos, sin, fill)


MOE_TILE = 256
MOE_ROWS_PER_STEP = 512
MOE_FFN_VMEM_BYTES = 56 * 1024 * 1024


def _moe_dispatch_body(pos_ref, h_ref, g_ref, init_hbm, xs_hbm, buf, sem):
    del init_hbm
    i = pl.program_id(0)
    tr, d = h_ref.shape
    buf[:, :d] = h_ref[...]
    buf[:, d:] = g_ref[...]

    def start(r2, carry):
        for prio in range(DMA_PRIORITIES):
            r = r2 * DMA_PRIORITIES + prio
            pltpu.make_async_copy(buf.at[pl.ds(r, 1)], xs_hbm.at[pl.ds(pos_ref[i * tr + r], 1)],
                                  sem).start(priority=prio)
        return carry

    lax.fori_loop(0, tr // DMA_PRIORITIES, start, 0)

    def wait(r, carry):
        pltpu.make_async_copy(buf.at[pl.ds(r, 1)], xs_hbm.at[pl.ds(0, 1)], sem).wait()
        return carry

    lax.fori_loop(0, tr, wait, 0)


def moe_dispatch(h, gate_lanes, pos, n_slots, tr):
    t, d = h.shape
    w = d + LANES
    grid_spec = pltpu.PrefetchScalarGridSpec(
        num_scalar_prefetch=1,
        grid=(t // tr,),
        in_specs=[pl.BlockSpec((tr, d), lambda i, p: (i, 0)),
                  pl.BlockSpec((tr, LANES), lambda i, p: (i, 0)),
                  pl.BlockSpec(memory_space=pl.ANY)],
        out_specs=pl.BlockSpec(memory_space=pl.ANY),
        scratch_shapes=[pltpu.VMEM((tr, w), F32), pltpu.SemaphoreType.DMA(())])
    return pl.pallas_call(
        _moe_dispatch_body,
        out_shape=jax.ShapeDtypeStruct((n_slots, w), F32),
        grid_spec=grid_spec,
        input_output_aliases={3: 0},
        compiler_params=_cp("arbitrary"),
        name="moe_dispatch",
    )(pos, h, gate_lanes, zeros_2d(n_slots, w, F32))


def _moe_ffn_body(ea_ref, eb_ref, nused_ref, xs_ref, w1a_ref, w3a_ref, w2a_ref, w1b_ref, w3b_ref, w2b_ref, o_ref):
    i = pl.program_id(0)
    d = o_ref.shape[1]

    @pl.when(i < nused_ref[0])
    def _():
        x = xs_ref[:, :d].astype(BF16)

        def expert(w1_ref, w3_ref, w2_ref):
            a = jnp.dot(x, w1_ref[...], preferred_element_type=F32)
            b = jnp.dot(x, w3_ref[...], preferred_element_type=F32)
            return jnp.dot((_silu(a) * b).astype(BF16), w2_ref[...], preferred_element_type=F32)

        o_ref[...] = (xs_ref[:, d:d + 1] * expert(w1a_ref, w3a_ref, w2a_ref)
                      + xs_ref[:, d + 1:d + 2] * expert(w1b_ref, w3b_ref, w2b_ref))

    @pl.when(i >= nused_ref[0])
    def _():
        o_ref[...] = jnp.zeros_like(o_ref)


def moe_ffn(xs, tile_ea, tile_eb, n_used, w1, w3, w2, layer):
    _, _, d, de = w1.shape
    tm = MOE_TILE
    n_tiles = tile_ea.shape[0]
    up_a = pl.BlockSpec((None, None, d, de), lambda i, ea, eb, nu: (layer, ea[i], 0, 0))
    up_b = pl.BlockSpec((None, None, d, de), lambda i, ea, eb, nu: (layer, eb[i], 0, 0))
    grid_spec = pltpu.PrefetchScalarGridSpec(
        num_scalar_prefetch=3,
        grid=(n_tiles,),
        in_specs=[pl.BlockSpec((tm, d + LANES), lambda i, ea, eb, nu: (i, 0)),
                  up_a, up_a, pl.BlockSpec((None, None, de, d), lambda i, ea, eb, nu: (layer, ea[i], 0, 0)),
                  up_b, up_b, pl.BlockSpec((None, None, de, d), lambda i, ea, eb, nu: (layer, eb[i], 0, 0))],
        out_specs=pl.BlockSpec((tm, d), lambda i, ea, eb, nu: (i, 0)))
    return pl.pallas_call(
        _moe_ffn_body,
        out_shape=jax.ShapeDtypeStruct((n_tiles * tm, d), F32),
        grid_spec=grid_spec,
        compiler_params=_cp("arbitrary", vmem=MOE_FFN_VMEM_BYTES),
        name="moe_ffn",
    )(tile_ea, tile_eb, n_used, xs, w1, w3, w2, w1, w3, w2)


def _moe_combine_body(pos_ref, y_hbm, x_ref, g_ref, o_ref, ybuf, sem):
    i = pl.program_id(0)
    tr = x_ref.shape[0]

    def start(r2, carry):
        for prio in range(DMA_PRIORITIES):
            r = r2 * DMA_PRIORITIES + prio
            pltpu.make_async_copy(y_hbm.at[pl.ds(pos_ref[i * tr + r], 1)], ybuf.at[pl.ds(r, 1)],
                                  sem).start(priority=prio)
        return carry

    lax.fori_loop(0, tr // DMA_PRIORITIES, start, 0)

    def wait(r, carry):
        pltpu.make_async_copy(y_hbm.at[pl.ds(0, 1)], ybuf.at[pl.ds(r, 1)], sem).wait()
        return carry

    lax.fori_loop(0, tr, wait, 0)
    o_ref[...] = x_ref[...] + g_ref[...] * ybuf[...]


def moe_combine(x, y_sorted, pos, mod3, gate_idx, seg_of_rows, tr):
    t, d = x.shape
    grid_spec = pltpu.PrefetchScalarGridSpec(
        num_scalar_prefetch=1,
        grid=(t // tr,),
        in_specs=[pl.BlockSpec(memory_space=pl.ANY),
                  pl.BlockSpec((tr, d), lambda i, p: (i, 0)),
                  pl.BlockSpec((None, 1, d), lambda i, p: (seg_of_rows(i * tr) * 6 + gate_idx, 0, 0))],
        out_specs=pl.BlockSpec((tr, d), lambda i, p: (i, 0)),
        scratch_shapes=[pltpu.VMEM((tr, d), F32), pltpu.SemaphoreType.DMA(())])
    return pl.pallas_call(
        _moe_combine_body,
        out_shape=jax.ShapeDtypeStruct((t, d), F32),
        grid_spec=grid_spec,
        compiler_params=_cp("arbitrary"),
        name="moe_combine",
    )(pos, y_sorted, x, mod3)


def moe_route(logits, n_groups, n_experts):
    per = n_experts // n_groups
    assert TOP_K == 2
    g_prob = jax.nn.softmax(logits[:, :n_groups], axis=-1)
    g_top = jnp.max(g_prob, axis=-1, keepdims=True)
    g_idx = jnp.argmax(g_prob, axis=-1, keepdims=True).astype(jnp.int32)
    e_logits = logits[:, n_groups:n_groups + n_experts].reshape(-1, n_groups, per)
    g_onehot = g_idx == jnp.arange(n_groups)[None, :]
    e_in = jnp.sum(jnp.where(g_onehot[:, :, None], e_logits, 0.0), axis=1)
    e_prob = jax.nn.softmax(e_in, axis=-1)
    i0 = jnp.argmax(e_prob, axis=-1, keepdims=True).astype(jnp.int32)
    rest = jnp.where(jnp.arange(per)[None, :] == i0, -jnp.inf, e_prob)
    i1 = jnp.argmax(rest, axis=-1, keepdims=True).astype(jnp.int32)
    e_idx = jnp.concatenate([i0, i1], axis=1)
    e_top = jnp.concatenate([jnp.max(e_prob, axis=-1, keepdims=True), jnp.max(rest, axis=-1, keepdims=True)], axis=1)
    e_top = e_top / jnp.sum(e_top, axis=-1, keepdims=True)
    return g_idx * per + e_idx, g_top * e_top


def moe_plan(expert_ids, gates, n_groups, n_experts):
    assert TOP_K == 2
    t = expert_ids.shape[0]
    tm = MOE_TILE
    per = n_experts // n_groups
    n_pairs = per * (per - 1) // 2
    n_classes = n_groups * n_pairs
    n_tiles = t // tm + n_classes
    e0, e1 = expert_ids[:, 0], expert_ids[:, 1]
    lo, hi = jnp.minimum(e0, e1), jnp.maximum(e0, e1)
    g_lo = jnp.where(e0 < e1, gates[:, 0], gates[:, 1])
    g_hi = jnp.where(e0 < e1, gates[:, 1], gates[:, 0])
    a, b = lo % per, hi % per
    cls = (lo // per) * n_pairs + (a * (2 * per - a - 1)) // 2 + (b - a - 1)
    onehot = (cls[:, None] == jnp.arange(n_classes)[None, :]).astype(F32)
    blk = ROW_TILE
    inner = jnp.einsum("ij,bjk->bik", jnp.tril(jnp.ones((blk, blk), F32)), onehot.reshape(-1, blk, n_classes))
    bsum = inner[:, -1, :]
    running = (inner + (jnp.cumsum(bsum, axis=0) - bsum)[:, None, :]).reshape(-1, n_classes)
    counts = jnp.sum(bsum, axis=0).astype(jnp.int32)
    tiles_per = (counts + tm - 1) // tm
    tile_start = jnp.cumsum(tiles_per) - tiles_per
    slot = running - 1.0 + (tile_start * tm).astype(F32)[None, :]
    pos = jnp.sum(jnp.where(onehot > 0, slot, 0.0), axis=1).astype(jnp.int32)
    n_used = jnp.sum(tiles_per).astype(jnp.int32)
    tile_ids = jnp.arange(n_tiles, dtype=jnp.int32)
    tile_cls = jnp.sum((tile_ids[:, None] >= tile_start[None, :]).astype(jnp.int32), axis=1) - 1
    tile_cls = jnp.where(tile_ids < n_used, tile_cls, tile_cls[jnp.maximum(n_used - 1, 0)])
    pairs = [(pa, pb) for pa in range(per) for pb in range(pa + 1, per)]
    cls_lo = jnp.array([g * per + pa for g in range(n_groups) for pa, _ in pairs], jnp.int32)
    cls_hi = jnp.array([g * per + pb for g in range(n_groups) for _, pb in pairs], jnp.int32)
    gate_lanes = jnp.pad(jnp.stack([g_lo, g_hi], axis=1), ((0, 0), (0, LANES - 2)))
    tile_onehot = tile_cls[:, None] == jnp.arange(n_classes)[None, :]
    tile_lo = jnp.sum(jnp.where(tile_onehot, cls_lo[None, :], 0), axis=1)
    tile_hi = jnp.sum(jnp.where(tile_onehot, cls_hi[None, :], 0), axis=1)
    return pos, gate_lanes, tile_lo, tile_hi, n_used.reshape(1), n_tiles * tm


def kernel(x_prompt, x_sample, state_ssd_fwd, state_ssd_bwd, cache_attn_k, cache_attn_v, c, c_ctx, ada_w, ada_b, norm_mix, norm_ffn, norm_final, ev_w_in, ev_w_out, hy_conv_w, hy_conv_b, hy_f_w1, hy_f_b1, hy_f_freq1, hy_f_w2, hy_f_b2, hy_f_freq2, hy_f_w3, hy_bias, ssd_conv_w, ssd_conv_b, ssd_dt_bias, ssd_a_log, ssd_d, ssd_norm, at_wq, at_wk, at_wv, at_wo, at_sink, moe_w_group, moe_b_group, moe_w_expert, moe_b_expert, moe_w1, moe_w3, moe_w2):
    bc, lc, d = x_prompt.shape
    bl, ll, _ = x_sample.shape
    tc_rows, tl_rows = bc * lc, bl * ll
    t = tc_rows + tl_rows
    depth = ada_w.shape[0]
    n_seg = 1 + bl
    assert lc % ROW_TILE == 0 and ll % ROW_TILE == 0 and tc_rows % ll == 0
    seg_rows = math.gcd(tc_rows, ll)

    def seg_of_rows(r):
        return jnp.where(r < tc_rows, 0, 1 + (r - tc_rows) // ll)

    def seg_of_tile(i):
        return seg_of_rows(i * ROW_TILE)

    x = jnp.concatenate([x_prompt.reshape(tc_rows, d), x_sample.reshape(tl_rows, d)], axis=0)
    cvec = jnp.concatenate([c_ctx[None], c, jnp.zeros((SUBLANES - n_seg, d), F32)], axis=0)
    mod_all = ada_mod(cvec, ada_w, ada_b)

    tile_row = jnp.arange(t // ROW_TILE, dtype=jnp.int32) * ROW_TILE
    seq_len = jnp.where(tile_row < tc_rows, lc, ll)
    rel = jnp.where(tile_row < tc_rows, tile_row, tile_row - tc_rows)
    conv_flags = jnp.stack([(rel % seq_len == 0), ((rel + ROW_TILE) % seq_len == 0)]).astype(jnp.int32)

    w1_bf, w3_bf, w2_bf = moe_w1.astype(BF16), moe_w3.astype(BF16), moe_w2.astype(BF16)
    new_sf, new_sb, new_k, new_v = [], [], [], []
    for l in range(depth):
        i = l // 2
        mod3 = mod_all[l, :n_seg].reshape(n_seg * 6, 1, d)
        h = rms_norm(x, norm_mix[l], out_dtype=BF16, mod3=mod3, shift_idx=0, scale_idx=1,
                     seg_of_tile=seg_of_tile)
        if l % 2 == 0:
            hyena_w = hy_conv_w.shape[2] // 3
            inner = ssd_norm.shape[1]
            heads = inner // SSD_HEAD_DIM
            conv_ch = ssd_conv_w.shape[2]
            main_cols = 3 * hyena_w + inner + conv_ch
            w_in = ev_w_in[i].astype(BF16)
            proj = dense(h, w_in, n_cols=main_cols, out_dtype=BF16, name="in_proj")
            dt_raw = dense(h, w_in[:, main_cols:], out_dtype=F32, name="dt_proj")
            u = dwconv(proj, hy_conv_w[i], hy_conv_b[i], conv_flags, col0=0, silu=False)
            u_ssd = dwconv(proj, ssd_conv_w[i], ssd_conv_b[i], conv_flags, col0=3 * hyena_w + inner,
                           silu=True)
            gn = SSD_GROUPS * SSD_STATE
            bt_all = u_ssd[:, inner:inner + gn].T
            fac, cum_r = ssd_prep(dt_raw, dt_raw.T, ssd_dt_bias[i], ssd_a_log[i])
            zero = jnp.zeros((bc, SSD_GROUPS, SSD_STATE, inner // SSD_GROUPS), F32)
            init_f = jnp.concatenate([zero, _state_to_kernel_layout(state_ssd_fwd[:, i], SSD_GROUPS)], axis=0)
            init_b = jnp.concatenate([zero, _state_to_kernel_layout(state_ssd_bwd[:, i], SSD_GROUPS)], axis=0)
            yf, yb, sf, sb = ssd_scan(u_ssd, bt_all, fac, cum_r, init_f, init_b,
                                      seq_lens=[lc] * bc + [ll] * bl, col0=0, inner=inner)
            new_sf.append(_state_from_kernel_layout(sf[:bc], SSD_HEAD_DIM))
            new_sb.append(_state_from_kernel_layout(sb[:bc], SSD_HEAD_DIM))
            y_a = ssd_finish(yf, yb, u_ssd, proj, jnp.repeat(ssd_d[i], SSD_HEAD_DIM), ssd_norm[i],
                             xs_col=0, z_col=3 * hyena_w, inner=inner, lead_cols=hyena_w)
            for row0, nb, L in ((0, bc, lc), (tc_rows, bl, ll)):
                filt = hyena_filters(L, hy_f_w1[i], hy_f_b1[i], hy_f_freq1[i], hy_f_w2[i], hy_f_b2[i],
                                     hy_f_freq2[i], hy_f_w3[i], hyena_w)
                if L % (16 * SPEC_BLOCK) == 0:
                    n1 = 2 * L // SPEC_BLOCK
                    t1f = fft_table("t1f", n1, SPEC_BLOCK)
                    kf = hyena_spectrum_2level(L, filt, t1f, fft_table("f2filt", n1, SPEC_BLOCK))
                    tabs = (t1f, fft_table("f2f", n1, SPEC_BLOCK), fft_table("f2i", n1, SPEC_BLOCK),
                            fft_table("t1i", n1, SPEC_BLOCK))
                    y_a = hyena_stream_2level(u, row0, nb, L, kf, tabs, hy_bias[i], hyena_w, fill=y_a)
                else:
                    a_tab = dft_table(L, "filt")
                    kf = hyena_spectrum(L, filt, a_tab)
                    y_a = hyena_stream(u, row0, nb, L, kf, a_tab, dft_table(L, "inv"), hy_bias[i], hyena_w,
                                       fill=y_a)
            x = dense_residual(y_a, ev_w_out[i].astype(BF16), x, mod3, 2, seg_of_rows, seg_rows, name="out_proj")
        else:
            n_heads = at_wq.shape[2] // HEAD_DIM
            q = dense(h, at_wq[i].astype(BF16), out_dtype=BF16, name="q_proj")
            kv = dense(h, jnp.concatenate([at_wk[i], at_wv[i]], axis=1).astype(BF16), out_dtype=F32, name="kv_proj")
            kvw = N_KV_HEADS * HEAD_DIM
            att_c = context_attention(q, kv, at_sink[i], nseq=bc, L=lc, n_heads=n_heads)
            past = cache_attn_k.shape[2]
            att = latent_attention(q, kv, cache_attn_k[:, i].reshape(bl, past, kvw),
                                   cache_attn_v[:, i].reshape(bl, past, kvw), at_sink[i], att_c,
                                   row0=tc_rows, nseq=bl, L=ll, n_heads=n_heads)
            x = dense_residual(att, at_wo[i].astype(BF16), x, mod3, 2, seg_of_rows, seg_rows, name="attn_out")
            new_k.append(kv[:tc_rows, :kvw].reshape(bc, lc, N_KV_HEADS, HEAD_DIM))
            new_v.append(kv[:tc_rows, kvw:].reshape(bc, lc, N_KV_HEADS, HEAD_DIM))
        h2 = rms_norm(x, norm_ffn[l], out_dtype=F32, mod3=mod3, shift_idx=3, scale_idx=4,
                      seg_of_tile=seg_of_tile)
        n_experts = moe_w_expert.shape[2]
        w_r = jnp.concatenate([moe_w_group[l], moe_w_expert[l]], axis=1)
        w_r = jnp.pad(w_r, ((0, 0), (0, LANES - w_r.shape[1]))).astype(BF16)
        b_r = jnp.pad(jnp.concatenate([moe_b_group[l], moe_b_expert[l]]), (0, LANES - MOE_GROUPS - n_experts))
        logits = dense(h2, w_r, out_dtype=F32, extra=(b_r.reshape(1, LANES),),
                       extra_specs=(pl.BlockSpec((1, LANES), lambda bt, ii, j, kk: (0, 0)),),
                       epilogue=lambda acc, b_ref, row_tile: acc + b_ref[...], name="router")
        expert_ids, gates = moe_route(logits, MOE_GROUPS, n_experts)
        pos, gate_lanes, tile_lo, tile_hi, n_used, n_slots = moe_plan(expert_ids, gates, MOE_GROUPS, n_experts)
        moe_rows = _tile(seg_rows, MOE_ROWS_PER_STEP)
        xs = moe_dispatch(h2, gate_lanes, pos, n_slots, moe_rows)
        y_sorted = moe_ffn(xs, tile_lo, tile_hi, n_used, w1_bf, w3_bf, w2_bf, l)
        x = moe_combine(x, y_sorted, pos, mod3, 5, seg_of_rows, moe_rows)

    y_prompt = rms_norm(x, norm_final, out_dtype=F32, rows=tc_rows).reshape(bc, lc, d)
    y_sample = rms_norm(x, norm_final, out_dtype=F32, row_off=tc_rows, rows=tl_rows).reshape(bl, ll, d)
    return (y_prompt, y_sample, jnp.stack(new_sf, axis=1), jnp.stack(new_sb, axis=1),
            jnp.stack(new_k, axis=1), jnp.stack(new_v, axis=1))
```

```python
import functools
import math

import jax
import jax.numpy as jnp
from jax import lax
from jax.experimental import pallas as pl
from jax.experimental.pallas import tpu as pltpu

F32 = jnp.float32
BF16 = jnp.bfloat16
HI = lax.Precision.HIGHEST

RMS_EPS = 1e-6
GRID_W = 64
HYENA_BANDS = 16
HYENA_FAST_DECAY = 0.3
HYENA_SLOW_DECAY = 1.5
HYENA_TARGET = 1e-2
SSD_HEAD_DIM = 64
SSD_STATE = 128
SSD_GROUPS = 8
SSD_CHUNK = 128
HEAD_DIM = 64
N_KV_HEADS = 8
WINDOW = 128
ROPE_BASE = 10000.0
MOE_GROUPS = 4
TOP_K = 2

LANES = 128
SUBLANES = 8
BF16_SUBLANES = 16
DMA_PRIORITIES = 2
VMEM_LIMIT_BYTES = 48 * 1024 * 1024

ROW_TILE = 256
SPEC_BLOCK = 128


def _cp(*sem, vmem=VMEM_LIMIT_BYTES):
    return pltpu.CompilerParams(dimension_semantics=sem, vmem_limit_bytes=vmem)


def _tile(n, pref):
    if n <= pref:
        return n
    t = pref
    while n % t:
        t //= 2
    assert t >= LANES, (n, pref)
    return t


def _silu(x):
    return x * (1.0 / (1.0 + jnp.exp(-x)))


def _softplus(x):
    return jnp.maximum(x, 0.0) + jnp.log(1.0 + jnp.exp(-jnp.abs(x)))


def _zero_body(o_ref):
    o_ref[...] = jnp.zeros_like(o_ref)


def zeros_2d(rows, cols, dtype):
    tr = _tile(rows, 1024)
    return pl.pallas_call(
        _zero_body,
        out_shape=jax.ShapeDtypeStruct((rows, cols), dtype),
        grid=(rows // tr,),
        out_specs=pl.BlockSpec((tr, cols), lambda i: (i, 0)),
        compiler_params=_cp("parallel"),
        name="zero_fill",
    )()


def _ada_body(c_ref, w_ref, b_ref, o_ref):
    a = _silu(c_ref[...])
    o_ref[...] = lax.dot_general(a, w_ref[...], (((1,), (0,)), ((), ())), precision=HI,
                                 preferred_element_type=F32) + b_ref[...]


def ada_mod(cvec, ada_w, ada_b):
    depth, d, n = ada_w.shape
    rows = cvec.shape[0]
    tn = _tile(n, 1024)
    return pl.pallas_call(
        _ada_body,
        out_shape=jax.ShapeDtypeStruct((depth, rows, n), F32),
        grid=(depth, n // tn),
        in_specs=[pl.BlockSpec((rows, d), lambda l, j: (0, 0)),
                  pl.BlockSpec((None, d, tn), lambda l, j: (l, 0, j)),
                  pl.BlockSpec((None, 1, tn), lambda l, j: (l, 0, j))],
        out_specs=pl.BlockSpec((None, rows, tn), lambda l, j: (l, 0, j)),
        compiler_params=_cp("parallel", "parallel"),
        name="ada_mod",
    )(cvec, ada_w, ada_b.reshape(depth, 1, n))


def _norm_body(x_ref, g_ref, *rest, modulate):
    o_ref = rest[-1]
    x = x_ref[...]
    y = x * lax.rsqrt(jnp.mean(x * x, axis=-1, keepdims=True) + RMS_EPS)
    y = y * g_ref[...]
    if modulate:
        sh_ref, sc_ref = rest[0], rest[1]
        y = y * (1.0 + sc_ref[...]) + sh_ref[...]
    o_ref[...] = y.astype(o_ref.dtype)


def rms_norm(x, g, *, out_dtype, mod3=None, shift_idx=0, scale_idx=0, seg_of_tile=None,
             row_off=0, rows=None):
    t, d = x.shape
    rows = t if rows is None else rows
    tr = ROW_TILE
    off = row_off // tr
    in_specs = [pl.BlockSpec((tr, d), lambda i: (i + off, 0)),
                pl.BlockSpec((1, d), lambda i: (0, 0))]
    args = [x, g.reshape(1, d)]
    if mod3 is not None:
        in_specs += [pl.BlockSpec((None, 1, d), lambda i: (seg_of_tile(i) * 6 + shift_idx, 0, 0)),
                     pl.BlockSpec((None, 1, d), lambda i: (seg_of_tile(i) * 6 + scale_idx, 0, 0))]
        args += [mod3, mod3]
    return pl.pallas_call(
        functools.partial(_norm_body, modulate=mod3 is not None),
        out_shape=jax.ShapeDtypeStruct((rows, d), out_dtype),
        grid=(rows // tr,),
        in_specs=in_specs,
        out_specs=pl.BlockSpec((tr, d), lambda i: (i, 0)),
        compiler_params=_cp("parallel"),
        name="rms_norm",
    )(*args)


def _mm_body(*refs, nk, n_extra, n_skip, epilogue):
    a_ref, b_ref = refs[0], refs[1]
    extra = refs[2:2 + n_extra]
    o_ref = refs[2 + n_extra + n_skip]
    row_tile = pl.program_id(1)
    b = b_ref[...]
    if b.ndim == 3:
        b = b.reshape(-1, b.shape[-1])
    part = jnp.dot(a_ref[...].astype(b.dtype), b, preferred_element_type=F32)
    if nk == 1:
        o_ref[...] = epilogue(part, *extra, row_tile=row_tile).astype(o_ref.dtype).reshape(o_ref.shape)
        return
    acc_ref = refs[3 + n_extra + n_skip]
    k = pl.program_id(3)

    @pl.when(k == 0)
    def _():
        acc_ref[...] = part

    @pl.when(k > 0)
    def _():
        acc_ref[...] += part

    @pl.when(k == nk - 1)
    def _():
        o_ref[...] = epilogue(acc_ref[...], *extra, row_tile=row_tile).astype(o_ref.dtype)


def matmul(a, b, *, grid, tm, tn, a_spec, b_spec, o_spec, out_shape, out_dtype,
           extra=(), extra_specs=(), epilogue=None, fill=None, name="matmul"):
    nk = grid[3]
    epilogue = epilogue or (lambda acc, row_tile: acc)
    scratch = [pltpu.VMEM((tm, tn), F32)] if nk > 1 else []
    fills = () if fill is None else (fill,)
    return pl.pallas_call(
        functools.partial(_mm_body, nk=nk, n_extra=len(extra), n_skip=len(fills), epilogue=epilogue),
        out_shape=jax.ShapeDtypeStruct(out_shape, out_dtype),
        grid=grid,
        in_specs=[a_spec, b_spec, *extra_specs, *[pl.BlockSpec(memory_space=pl.ANY) for _ in fills]],
        out_specs=o_spec,
        scratch_shapes=scratch,
        input_output_aliases={2 + len(extra): 0} if fills else {},
        compiler_params=_cp("parallel", "parallel", "parallel", "arbitrary"),
        name=name,
    )(a, b, *extra, *fills)


def dense(a, w, *, out_dtype, tm=1024, tn=512, tk=2048, extra=(), extra_specs=(), epilogue=None,
          w_row0=0, n_cols=None, name="dense"):
    m, k = a.shape
    n = w.shape[1] if n_cols is None else n_cols
    tm, tn, tk = _tile(m, tm), _tile(n, tn), _tile(k, tk)
    assert w_row0 % tk == 0
    kb0 = w_row0 // tk
    return matmul(
        a, w, grid=(1, m // tm, n // tn, k // tk), tm=tm, tn=tn,
        a_spec=pl.BlockSpec((tm, tk), lambda bt, i, j, kk: (i, kk)),
        b_spec=pl.BlockSpec((tk, tn), lambda bt, i, j, kk: (kb0 + kk, j)),
        o_spec=pl.BlockSpec((tm, tn), lambda bt, i, j, kk: (i, j)),
        out_shape=(m, n), out_dtype=out_dtype, extra=extra, extra_specs=extra_specs,
        epilogue=epilogue, name=name)


def dense_residual(a, w, x, mod3, gate_idx, seg_of_rows, seg_rows, *, tm=1024, tn=512, tk=2048, w_row0=0,
                   name="dense_res"):
    n = w.shape[1]
    tm_ = _tile(seg_rows, tm)
    tn_ = _tile(n, tn)
    return dense(
        a, w, out_dtype=F32, tm=tm_, tn=tn, tk=tk, w_row0=w_row0, extra=(x, mod3),
        extra_specs=(pl.BlockSpec((tm_, tn_), lambda bt, i, j, kk: (i, j)),
                     pl.BlockSpec((None, 1, tn_),
                                  lambda bt, i, j, kk: (seg_of_rows(i * tm_) * 6 + gate_idx, 0, j))),
        epilogue=lambda acc, x_ref, g_ref, row_tile: x_ref[...] + g_ref[...] * acc, name=name)


def _dwconv_body(flags_ref, x_ref, p_ref, n_ref, w_ref, b_ref, o_ref, *, silu):
    i = pl.program_id(0)
    act = _silu if silu else (lambda v: v)
    x = x_ref[...].astype(F32)
    tr = x.shape[0]
    w = w_ref[...]
    y = pltpu.roll(x, 1, 0) * w[0:1, :] + x * w[1:2, :] + pltpu.roll(x, tr - 1, 0) * w[2:3, :] + b_ref[...]
    o_ref[...] = act(y).astype(o_ref.dtype)
    keep_prev = (flags_ref[0, i] == 0).astype(F32)
    keep_next = (flags_ref[1, i] == 0).astype(F32)
    halo_prev = p_ref[SUBLANES - 1:SUBLANES, :].astype(F32) * keep_prev
    halo_next = n_ref[0:1, :].astype(F32) * keep_next
    slab = BF16_SUBLANES
    row = lax.broadcasted_iota(jnp.int32, (slab, x.shape[1]), 0)
    fix_top = (halo_prev - x[tr - 1:tr, :]) * w[0:1, :]
    fix_bot = (halo_next - x[0:1, :]) * w[2:3, :]
    o_ref[0:slab, :] = act(y[0:slab] + jnp.where(row == 0, fix_top, 0.0)).astype(o_ref.dtype)
    o_ref[tr - slab:tr, :] = act(y[tr - slab:tr] + jnp.where(row == slab - 1, fix_bot, 0.0)).astype(o_ref.dtype)


def dwconv(proj, w, b, flags, *, col0, silu):
    t = proj.shape[0]
    cols = w.shape[1]
    tr = ROW_TILE
    tc = _tile(math.gcd(col0, cols), 2048)
    cb0 = col0 // tc
    hb = tr // SUBLANES
    last_hb = t // SUBLANES - 1
    grid_spec = pltpu.PrefetchScalarGridSpec(
        num_scalar_prefetch=1,
        grid=(t // tr, cols // tc),
        in_specs=[pl.BlockSpec((tr, tc), lambda i, j, f: (i, cb0 + j)),
                  pl.BlockSpec((SUBLANES, tc), lambda i, j, f: (jnp.maximum(i * hb - 1, 0), cb0 + j)),
                  pl.BlockSpec((SUBLANES, tc), lambda i, j, f: (jnp.minimum((i + 1) * hb, last_hb), cb0 + j)),
                  pl.BlockSpec((3, tc), lambda i, j, f: (0, j)),
                  pl.BlockSpec((1, tc), lambda i, j, f: (0, j))],
        out_specs=pl.BlockSpec((tr, tc), lambda i, j, f: (i, j)))
    return pl.pallas_call(
        functools.partial(_dwconv_body, silu=silu),
        out_shape=jax.ShapeDtypeStruct((t, cols), BF16),
        grid_spec=grid_spec,
        compiler_params=_cp("parallel", "parallel"),
        name="dwconv_silu" if silu else "dwconv",
    )(flags, proj, proj, proj, w, b.reshape(1, -1))


def _filter_body(feat_ref, w1_ref, b1_ref, f1_ref, w2_ref, b2_ref, f2_ref, w3_ref, dl_ref, o_ref):
    d = pl.program_id(0)
    i = pl.program_id(1)
    feat = feat_ref[...]
    dn = (((1,), (0,)), ((), ()))
    hid = jnp.sin(f1_ref[...] * (lax.dot_general(feat, w1_ref[...], dn, precision=HI,
                                                 preferred_element_type=F32) + b1_ref[...]))
    hid = jnp.sin(f2_ref[...] * (lax.dot_general(hid, w2_ref[...], dn, precision=HI,
                                                 preferred_element_type=F32) + b2_ref[...]))
    filt = jnp.dot(hid.astype(BF16), w3_ref[...].astype(BF16), preferred_element_type=F32)
    t01 = feat[:, 0:1]
    filt = filt * jnp.exp(-t01 * dl_ref[...])
    row = lax.broadcasted_iota(jnp.int32, filt.shape, 0) + i * filt.shape[0]
    filt = jnp.where((d == 1) & (row == 0), 0.0, filt)
    o_ref[...] = filt.astype(o_ref.dtype)


def hyena_filters(L, f_w1, f_b1, f_freq1, f_w2, f_b2, f_freq2, f_w3, c):
    t = jnp.arange(L, dtype=F32)
    t01 = t / (L - 1)
    bands = jnp.linspace(1e-4, HYENA_BANDS - 1, HYENA_BANDS, dtype=F32)
    ang = (2.0 * math.pi / L) * t[:, None] * bands[None, :]
    feat = jnp.concatenate([t01[:, None], jnp.cos(ang), -jnp.sin(ang)], axis=-1)
    emb = feat.shape[1]
    feat = jnp.pad(feat, ((0, 0), (0, LANES - emb)))
    w1 = jnp.pad(f_w1, ((0, LANES - emb), (0, 0)))
    ffn = f_w1.shape[1]
    order = f_w3.shape[1] // (2 * c)
    deltas = jnp.abs(jnp.linspace(math.log(HYENA_TARGET) / HYENA_SLOW_DECAY,
                                  math.log(HYENA_TARGET) / HYENA_FAST_DECAY, c, dtype=F32))
    deltas = jnp.tile(deltas, order).reshape(1, order * c)
    tl, tn = _tile(L, 512), _tile(order * c, 4096)
    nj = order * c // tn
    return pl.pallas_call(
        _filter_body,
        out_shape=jax.ShapeDtypeStruct((2, L, order * c), BF16),
        grid=(2, L // tl, nj),
        in_specs=[pl.BlockSpec((tl, LANES), lambda d, i, j: (i, 0)),
                  pl.BlockSpec((LANES, ffn), lambda d, i, j: (0, 0)),
                  pl.BlockSpec((1, ffn), lambda d, i, j: (0, 0)),
                  pl.BlockSpec((1, ffn), lambda d, i, j: (0, 0)),
                  pl.BlockSpec((ffn, ffn), lambda d, i, j: (0, 0)),
                  pl.BlockSpec((1, ffn), lambda d, i, j: (0, 0)),
                  pl.BlockSpec((1, ffn), lambda d, i, j: (0, 0)),
                  pl.BlockSpec((ffn, tn), lambda d, i, j: (0, d * nj + j)),
                  pl.BlockSpec((1, tn), lambda d, i, j: (0, j))],
        out_specs=pl.BlockSpec((None, tl, tn), lambda d, i, j: (d, i, j)),
        compiler_params=_cp("parallel", "parallel", "parallel"),
        name="hyena_filters",
    )(feat, w1, f_b1.reshape(1, ffn), f_freq1.reshape(1, ffn), f_w2, f_b2.reshape(1, ffn),
      f_freq2.reshape(1, ffn), f_w3, deltas)


def _cos_turns(m, log2_period):
    period = 1 << log2_period
    quarter = period // 4
    mm = m + quarter // 2
    q = (mm >> (log2_period - 2)) & 3
    r = (mm & (quarter - 1)) - quarter // 2
    phi = r.astype(F32) * (2.0 * math.pi / period)
    p2 = phi * phi
    cosv = 1.0 + p2 * (-1.0 / 2 + p2 * (1.0 / 24 + p2 * (-1.0 / 720 + p2 * (1.0 / 40320 - p2 / 3628800))))
    sinv = phi * (1.0 + p2 * (-1.0 / 6 + p2 * (1.0 / 120 + p2 * (-1.0 / 5040 + p2 / 362880))))
    return jnp.where(q == 0, cosv, jnp.where(q == 1, -sinv, jnp.where(q == 2, -cosv, sinv)))


def _spec_index(r, L):
    blk = r >> 8
    within = r & (2 * SPEC_BLOCK - 1)
    return blk * SPEC_BLOCK + (within & (SPEC_BLOCK - 1)), within >= SPEC_BLOCK


def _dft_body(o_ref, *, L, mode):
    log2n = int(math.log2(2 * L))
    shape = o_ref.shape
    i0 = pl.program_id(0) * shape[0]
    j0 = pl.program_id(1) * shape[1]
    rows = lax.broadcasted_iota(jnp.int32, shape, 0) + i0
    cols = lax.broadcasted_iota(jnp.int32, shape, 1) + j0
    if mode == "inv":
        r, n = cols, rows
    else:
        r, n = rows, cols
    k, im = _spec_index(r, L)
    back = n >= L
    n = jnp.where(back, n - L, n)
    phase = jnp.where(im, jnp.where(back, -(L // 2), L // 2), 0)
    val = _cos_turns((k * n + phase) & (2 * L - 1), log2n)
    nyq = jnp.where((n & 1) == 0, 1.0, -1.0)
    val = jnp.where(im & (k == 0), nyq, val)
    if mode == "inv":
        val = val * jnp.where(k == 0, 1.0 / (2 * L), 2.0 / (2 * L))
    o_ref[...] = val.astype(o_ref.dtype)


def dft_table(L, mode):
    assert L & (L - 1) == 0 and L >= 2 * SPEC_BLOCK
    shape = {"fwd": (2 * L, L), "filt": (2 * L, 2 * L), "inv": (L, 2 * L)}[mode]
    tr, tc = _tile(shape[0], 512), _tile(shape[1], 1024)
    return pl.pallas_call(
        functools.partial(_dft_body, L=L, mode=mode),
        out_shape=jax.ShapeDtypeStruct(shape, BF16),
        grid=(shape[0] // tr, shape[1] // tc),
        out_specs=pl.BlockSpec((tr, tc), lambda i, j: (i, j)),
        compiler_params=_cp("parallel", "parallel"),
        name="dft_table_" + mode,
    )()


def _spectral_epilogue(acc, kf_ref, row_tile):
    tm, tn = acc.shape
    nb = tm // (2 * SPEC_BLOCK)
    z = acc.reshape(nb, 2, SPEC_BLOCK, tn)
    kf = kf_ref[...].reshape(nb, 2, SPEC_BLOCK, tn)
    zr, zi, kr, ki = z[:, 0], z[:, 1], kf[:, 0], kf[:, 1]
    first = row_tile == 0
    blk = lax.broadcasted_iota(jnp.int32, zr.shape, 0)
    row = lax.broadcasted_iota(jnp.int32, zr.shape, 1)
    dc = first & (blk == 0) & (row == 0)
    pr = zr * kr - jnp.where(dc, 0.0, zi * ki)
    pi = jnp.where(dc, zi * ki, zr * ki + zi * kr)
    return jnp.stack([pr, pi], axis=1).reshape(tm, tn)


def hyena_stream(u, row0, nb, L, kf, f_tab, g_tab, hy_bias, c, fill=None):
    out_rows = u.shape[0]
    tn = _tile(c, 2048)
    cb = c // tn
    tm_f = _tile(2 * L, 512)
    tk_f = _tile(L, 2048)
    tm_i = _tile(L, 512)
    tk_i = _tile(2 * L, 2048)
    z, z_rows, z_col = u, row0, 2 * cb
    for o in range(2):
        zrb = z_rows // tk_f
        p = matmul(
            f_tab, z, grid=(nb, 2 * L // tm_f, cb, L // tk_f), tm=tm_f, tn=tn,
            a_spec=pl.BlockSpec((tm_f, tk_f), lambda bt, i, j, k: (i, k)),
            b_spec=pl.BlockSpec((tk_f, tn), lambda bt, i, j, k, zrb=zrb, zc=z_col:
                                (zrb + bt * (L // tk_f) + k, zc + j)),
            o_spec=pl.BlockSpec((None, tm_f, tn), lambda bt, i, j, k: (bt, i, j)),
            out_shape=(nb, 2 * L, c), out_dtype=BF16,
            extra=(kf,), extra_specs=(pl.BlockSpec((tm_f, tn), lambda bt, i, j, k, o=o: (i, o * cb + j)),),
            epilogue=_spectral_epilogue, name="hyena_fwd")
        zrb_i = z_rows // tm_i
        urb_i = row0 // tm_i
        last = o == 1
        orb = urb_i if last else 0
        z_new = matmul(
            g_tab, p, grid=(nb, L // tm_i, cb, 2 * L // tk_i), tm=tm_i, tn=tn,
            a_spec=pl.BlockSpec((tm_i, tk_i), lambda bt, i, j, k: (i, k)),
            b_spec=pl.BlockSpec((None, tk_i, tn), lambda bt, i, j, k: (bt, k, j)),
            o_spec=pl.BlockSpec((tm_i, tn), lambda bt, i, j, k, orb=orb: (orb + bt * (L // tm_i) + i, j)),
            out_shape=fill.shape if last else (nb * L, c), out_dtype=BF16, fill=fill if last else None,
            extra=(u, z, hy_bias.reshape(2, 1, c)),
            extra_specs=(pl.BlockSpec((tm_i, tn), lambda bt, i, j, k, o=o:
                                      (urb_i + bt * (L // tm_i) + i, o * cb + j)),
                         pl.BlockSpec((tm_i, tn), lambda bt, i, j, k, zrb_i=zrb_i, zc=z_col:
                                      (zrb_i + bt * (L // tm_i) + i, zc + j)),
                         pl.BlockSpec((None, 1, tn), lambda bt, i, j, k, o=o: (o, 0, j))),
            epilogue=lambda acc, g_ref, z_ref, b_ref, row_tile:
                g_ref[...].astype(F32) * (acc + b_ref[...] * z_ref[...].astype(F32)),
            name="hyena_inv")
        z, z_rows, z_col = z_new, 0, 0
    return z


def hyena_spectrum(L, filt, a_tab):
    n = filt.shape[2]
    b = filt.reshape(2 * L, n)
    tm, tn, tk = _tile(2 * L, 512), _tile(n, 512), _tile(2 * L, 2048)
    return matmul(
        a_tab, b, grid=(1, 2 * L // tm, n // tn, 2 * L // tk), tm=tm, tn=tn,
        a_spec=pl.BlockSpec((tm, tk), lambda bt, i, j, k: (i, k)),
        b_spec=pl.BlockSpec((tk, tn), lambda bt, i, j, k: (k, j)),
        o_spec=pl.BlockSpec((tm, tn), lambda bt, i, j, k: (i, j)),
        out_shape=(2 * L, n), out_dtype=F32, name="hyena_spectrum")


def _fft_tab_body(o_ref, *, kind, n1, n2, k1_len):
    shape = o_ref.shape
    r = lax.broadcasted_iota(jnp.int32, shape, 0) + pl.program_id(0) * shape[0]
    c = lax.broadcasted_iota(jnp.int32, shape, 1)
    n = n1 * n2
    lg = lambda v: int(math.log2(v))
    s8 = BF16_SUBLANES
    if kind in ("t1f", "t1i"):
        if kind == "t1f":
            grp, kp, s, n1i, s2 = r >> lg(2 * n1 * s8), (r >> lg(s8)) & (2 * n1 - 1), r & (s8 - 1), c >> lg(s8), c & (s8 - 1)
        else:
            grp, n1i, s, kp, s2 = r >> lg(k1_len * s8), (r >> lg(s8)) & (k1_len - 1), r & (s8 - 1), c >> lg(s8), c & (s8 - 1)
        t = n2 * n1i + grp * s8 + s
        val = _cos_turns(((kp & (n1 - 1)) * t + (kp >> lg(n1)) * (n // 4)) & (n - 1), lg(n))
        val = jnp.where(s == s2, val * (1.0 / n if kind == "t1i" else 1.0), 0.0)
    else:
        rp = r >= n2
        cblk = c >> lg(n2)
        cp = (cblk & 1) == 1
        sin_sign = -1 if kind == "f2i" else 1
        quarter = jnp.where(rp == cp, 0, jnp.where(cp, -sin_sign, sin_sign))
        val = _cos_turns(((r & (n2 - 1)) * (c & (n2 - 1)) + quarter * (n2 // 4)) & (n2 - 1), lg(n2))
        if kind == "f2filt":
            val = jnp.where((cblk >= 2) & rp, -val, val)
    o_ref[...] = val.astype(o_ref.dtype)


def fft_table(kind, n1, n2):
    k1_len = n1 // 2
    s8 = BF16_SUBLANES
    groups = n2 // s8
    shape = {"t1f": (groups * 2 * n1 * s8, k1_len * s8), "t1i": (groups * k1_len * s8, 2 * n1 * s8),
             "f2f": (2 * n2, 2 * n2), "f2i": (2 * n2, 2 * n2), "f2filt": (2 * n2, 4 * n2)}[kind]
    tr = _tile(shape[0], 1024)
    out = pl.pallas_call(
        functools.partial(_fft_tab_body, kind=kind, n1=n1, n2=n2, k1_len=k1_len),
        out_shape=jax.ShapeDtypeStruct(shape, BF16),
        grid=(shape[0] // tr,),
        out_specs=pl.BlockSpec((tr, shape[1]), lambda i: (i, 0)),
        compiler_params=_cp("parallel"),
        name="fft_table_" + kind,
    )()
    if kind in ("t1f", "t1i"):
        return out.reshape(groups, shape[0] // groups, shape[1])
    return out


FFT_K1_PER_STEP = 8


def _fft_mid_body(f_ref, *refs, mode, n2):
    f = f_ref[...]
    if mode == "inv":
        p_ref, o_ref = refs
        for k in range(FFT_K1_PER_STEP):
            res = jnp.dot(f, p_ref[k * 2 * n2:(k + 1) * 2 * n2, :], preferred_element_type=F32)
            o_ref[0, k * n2:(k + 1) * n2, :] = res[:n2].astype(o_ref.dtype)
            o_ref[1, k * n2:(k + 1) * n2, :] = res[n2:].astype(o_ref.dtype)
        return
    if mode == "conv":
        finv_ref, y_ref, kf_ref, o_ref = refs
        srcs = (y_ref,)
    elif mode == "fwd":
        y_ref, kf_ref, o_ref = refs
        srcs = (y_ref,)
    else:
        y0_ref, y1_ref, o_ref = refs
        srcs = (y0_ref, y1_ref)
    for k in range(FFT_K1_PER_STEP):
        rows = slice(k * n2, (k + 1) * n2)
        stack = jnp.concatenate([s[part, rows, :] for s in srcs for part in range(2)], axis=0)
        acc = jnp.dot(f, stack, preferred_element_type=F32)
        out_rows = slice(k * 2 * n2, (k + 1) * 2 * n2)
        if mode in ("fwd", "conv"):
            zr, zi = acc[:n2], acc[n2:]
            kr = kf_ref[k * 2 * n2:k * 2 * n2 + n2, :].astype(F32)
            ki = kf_ref[k * 2 * n2 + n2:(k + 1) * 2 * n2, :].astype(F32)
            acc = jnp.concatenate([zr * kr - zi * ki, zr * ki + zi * kr], axis=0)
        if mode == "conv":
            res = jnp.dot(finv_ref[...], acc.astype(BF16), preferred_element_type=F32)
            o_ref[0, rows, :] = res[:n2].astype(o_ref.dtype)
            o_ref[1, rows, :] = res[n2:].astype(o_ref.dtype)
        else:
            o_ref[out_rows, :] = acc.astype(o_ref.dtype)


def fft_mid(mode, f_tab, src, n1, n2, *, kf=None, kf_col0=0, f_inv=None, out_dtype=BF16, name):
    kb = FFT_K1_PER_STEP
    c = src.shape[-1]
    tn = _tile(c, 1024)
    cb = c // tn
    nb = 1 if mode == "filt" else src.shape[0]
    y_block = (None, 2, kb * n2, tn)
    p_block = (None, kb * 2 * n2, tn)
    f_spec = pl.BlockSpec(f_tab.shape, lambda i, b, j: (0, 0))
    if mode == "conv":
        in_specs = [f_spec, pl.BlockSpec(f_inv.shape, lambda i, b, j: (0, 0)),
                    pl.BlockSpec(y_block, lambda i, b, j: (b, 0, i, j)),
                    pl.BlockSpec((kb * 2 * n2, tn), lambda i, b, j: (i, kf_col0 // tn + j))]
        args = (f_tab, f_inv, src, kf)
        out_shape, out_spec = (nb, 2, n1 * n2, c), pl.BlockSpec(y_block, lambda i, b, j: (b, 0, i, j))
    elif mode == "fwd":
        in_specs = [f_spec, pl.BlockSpec(y_block, lambda i, b, j: (b, 0, i, j)),
                    pl.BlockSpec((kb * 2 * n2, tn), lambda i, b, j: (i, kf_col0 // tn + j))]
        args = (f_tab, src, kf)
        out_shape, out_spec = (nb, n1 * 2 * n2, c), pl.BlockSpec(p_block, lambda i, b, j: (b, i, j))
    elif mode == "filt":
        in_specs = [f_spec, pl.BlockSpec(y_block, lambda i, b, j: (0, 0, i, j)),
                    pl.BlockSpec(y_block, lambda i, b, j: (1, 0, i, j))]
        args = (f_tab, src, src)
        out_shape, out_spec = (n1 * 2 * n2, c), pl.BlockSpec((kb * 2 * n2, tn), lambda i, b, j: (i, j))
    else:
        in_specs = [f_spec, pl.BlockSpec(p_block, lambda i, b, j: (b, i, j))]
        args = (f_tab, src)
        out_shape, out_spec = (nb, 2, n1 * n2, c), pl.BlockSpec(y_block, lambda i, b, j: (b, 0, i, j))
    return pl.pallas_call(
        functools.partial(_fft_mid_body, mode=mode, n2=n2),
        out_shape=jax.ShapeDtypeStruct(out_shape, out_dtype),
        grid=(n1 // kb, nb, cb),
        in_specs=in_specs,
        out_specs=out_spec,
        compiler_params=_cp("parallel", "parallel", "parallel"),
        name=name,
    )(*args)


def hyena_spectrum_2level(L, filt, t1f, f2filt):
    n2 = SPEC_BLOCK
    n1 = 2 * L // n2
    k1 = n1 // 2
    n = filt.shape[2]
    tn = _tile(n, 1024)
    nj = n // tn
    sl = BF16_SUBLANES
    y1 = matmul(
        t1f, filt.reshape(2 * k1, n2, n), grid=(2, n2 // sl, nj, 1), tm=2 * n1 * sl, tn=tn,
        a_spec=pl.BlockSpec((None, 2 * n1 * sl, k1 * sl), lambda bt, i, j, k: (i, 0, 0)),
        b_spec=pl.BlockSpec((k1, sl, tn), lambda bt, i, j, k: (bt, i, j)),
        o_spec=pl.BlockSpec((None, 2 * n1, sl, tn), lambda bt, i, j, k: (bt, 0, i, j)),
        out_shape=(2, 2 * n1, n2, n), out_dtype=BF16, name="hyena_filt_fft1")
    return fft_mid("filt", f2filt, y1.reshape(2, 2, n1 * n2, n), n1, n2, name="hyena_filt_fft2")


def hyena_stream_2level(u, row0, nb, L, kf, tabs, hy_bias, c, fill=None):
    t1f, f2f, f2i, t1i = tabs
    n2 = SPEC_BLOCK
    n1 = 2 * L // n2
    k1 = n1 // 2
    sl = BF16_SUBLANES
    ng = n2 // sl
    t1n = _tile(c, 1024)
    c1b = c // t1n
    tn = _tile(c, 2048)
    cb = c // tn
    t, u_w = u.shape
    assert row0 % L == 0 and u_w % tn == 0
    uv = u.reshape(t // n2, n2, u_w)
    urow = row0 // L
    zv, zr, zc = uv, urow, 2 * c // t1n
    bias3 = hy_bias.reshape(2, 1, c)
    for o in range(2):
        y1 = matmul(
            t1f, zv, grid=(nb, ng, c1b, 1), tm=2 * n1 * sl, tn=t1n,
            a_spec=pl.BlockSpec((None, 2 * n1 * sl, k1 * sl), lambda bt, i, j, k: (i, 0, 0)),
            b_spec=pl.BlockSpec((k1, sl, t1n), lambda bt, i, j, k, zr=zr, zc=zc: (zr + bt, i, zc + j)),
            o_spec=pl.BlockSpec((None, 2 * n1, sl, t1n), lambda bt, i, j, k: (bt, 0, i, j)),
            out_shape=(nb, 2 * n1, n2, c), out_dtype=BF16, name="hyena_fft1")
        u1 = fft_mid("conv", f2f, y1.reshape(nb, 2, n1 * n2, c), n1, n2, kf=kf, kf_col0=o * c, f_inv=f2i,
                     name="hyena_fft2_ifft1")
        last = o == 1
        orow = urow if last else 0

        def gate_epilogue(acc, g_ref, z_ref, b_ref, row_tile):
            g = g_ref[...].reshape(acc.shape).astype(F32)
            return g * (acc + b_ref[...] * z_ref[...].reshape(acc.shape).astype(F32))

        z = matmul(
            t1i, u1.reshape(nb, 2 * n1, n2, c), grid=(nb, ng, c1b, 1), tm=k1 * sl, tn=t1n,
            a_spec=pl.BlockSpec((None, k1 * sl, 2 * n1 * sl), lambda bt, i, j, k: (i, 0, 0)),
            b_spec=pl.BlockSpec((None, 2 * n1, sl, t1n), lambda bt, i, j, k: (bt, 0, i, j)),
            o_spec=pl.BlockSpec((k1, sl, t1n), lambda bt, i, j, k, orow=orow: (orow + bt, i, j)),
            out_shape=(t // n2, n2, fill.shape[1]) if last else (nb * k1, n2, c), out_dtype=BF16,
            fill=fill.reshape(t // n2, n2, fill.shape[1]) if last else None,
            extra=(uv, zv, bias3),
            extra_specs=(pl.BlockSpec((k1, sl, t1n), lambda bt, i, j, k, o=o: (urow + bt, i, o * c1b + j)),
                         pl.BlockSpec((k1, sl, t1n), lambda bt, i, j, k, zr=zr, zc=zc: (zr + bt, i, zc + j)),
                         pl.BlockSpec((None, 1, t1n), lambda bt, i, j, k, o=o: (o, 0, j))),
            epilogue=gate_epilogue, name="hyena_ifft2")
        zv, zr, zc = z, 0, 0
    return zv.reshape(t, fill.shape[1])


SSD_CUM_TERMS = 3


def _ssd_prep_body(dc_ref, dr_ref, biasc_ref, biasr_ref, ac_ref, ar_ref, fac_ref, cumr_ref, *, heads):
    q = dc_ref.shape[0]
    dn = (((1,), (0,)), ((), ()))
    li = lax.broadcasted_iota(jnp.int32, (q, q), 0)
    si = lax.broadcasted_iota(jnp.int32, (q, q), 1)
    lower = (li >= si).astype(F32)
    upper = (li <= si).astype(F32)
    dt_c = _softplus(dc_ref[...] + biasc_ref[...])
    da_c = dt_c * ac_ref[...]
    da_r = _softplus(dr_ref[...] + biasr_ref[...]) * ar_ref[...]
    pre_c = lax.dot_general(lower, da_c, dn, precision=HI, preferred_element_type=F32)
    suf_c = lax.dot_general(upper, da_c, dn, precision=HI, preferred_element_type=F32)
    pre_r = lax.dot_general(da_r, upper, dn, precision=HI, preferred_element_type=F32)
    suf_r = lax.dot_general(da_r, lower, dn, precision=HI, preferred_element_type=F32)
    cum_c = jnp.where(lax.broadcasted_iota(jnp.int32, pre_c.shape, 1) < heads, pre_c, suf_c)
    terms, rest = [], cum_c
    for _ in range(SSD_CUM_TERMS):
        term = rest.astype(BF16)
        rest = rest - term.astype(F32)
        terms.append(term)
    fac_ref[...] = jnp.concatenate(terms + [dt_c.astype(BF16)], axis=1)
    cumr_ref[...] = jnp.where(lax.broadcasted_iota(jnp.int32, pre_r.shape, 0) < heads, pre_r, suf_r)


def ssd_prep(dt_raw, dt_raw_t, dt_bias, a_log):
    t, h2 = dt_raw.shape
    q = SSD_CHUNK
    neg_a = -jnp.exp(a_log.astype(F32)).reshape(1, h2)
    bias = dt_bias.astype(F32).reshape(1, h2)
    const = lambda shape: pl.BlockSpec(shape, lambda i: (0,) * len(shape))
    col = pl.BlockSpec((q, h2), lambda i: (i, 0))
    row = pl.BlockSpec((h2, q), lambda i: (0, i))
    nf = (SSD_CUM_TERMS + 1) * h2
    return pl.pallas_call(
        functools.partial(_ssd_prep_body, heads=h2 // 2),
        out_shape=(jax.ShapeDtypeStruct((t, nf), BF16), jax.ShapeDtypeStruct((h2, t), F32)),
        grid=(t // q,),
        in_specs=[col, row, const((1, h2)), const((h2, 1)), const((1, h2)), const((h2, 1))],
        out_specs=(pl.BlockSpec((q, nf), lambda i: (i, 0)), row),
        compiler_params=_cp("parallel"),
        name="ssd_prep",
    )(dt_raw, dt_raw_t, bias, bias.reshape(h2, 1), neg_a, neg_a.reshape(h2, 1))


def ssd_selectors(h2, j, p, q):
    w_sum, w_wide = j * p, j * q
    k = (SSD_CUM_TERMS + 1) * h2
    rows = jnp.arange(k, dtype=jnp.int32)[None, :, None]
    cols = jnp.arange(w_sum + w_wide + j * p, dtype=jnp.int32)[None, None, :]
    head0 = j * jnp.arange(h2 // j, dtype=jnp.int32)[:, None, None]
    col_head = jnp.where(cols < w_sum, cols // p, jnp.where(cols < w_sum + w_wide, (cols - w_sum) // q,
                                                              (cols - w_sum - w_wide) // p))
    is_dt_row = rows // h2 == SSD_CUM_TERMS
    is_dt_col = cols >= w_sum + w_wide
    sel = (rows % h2 == head0 + col_head) & (is_dt_row == is_dt_col)
    return sel.astype(BF16)


def _ssd_dir(xs, bt, cm, fac, sel, diag_mask, cumr_ref, state_ref, head0, *, reverse, heads_per_group):
    q = xs.shape[0]
    p = SSD_HEAD_DIM
    j = heads_per_group
    spread = jnp.dot(fac, sel, preferred_element_type=F32)
    cc, cc_wide, dtv = spread[:, :j * p], spread[:, j * p:j * p + j * q], spread[:, j * p + j * q:]
    total = cc[0:1, :] if reverse else cc[q - 1:q, :]
    xd = xs.astype(F32) * dtv
    xd_b = xd.astype(BF16)
    li = lax.broadcasted_iota(jnp.int32, (q, q), 0)
    si = lax.broadcasted_iota(jnp.int32, (q, q), 1)
    mask = (li <= si) if reverse else (li >= si)
    cb = jnp.dot(cm, bt, preferred_element_type=F32)
    weights = []
    for jh in range(j):
        cr = cumr_ref[pl.ds(head0 + jh, 1), :]
        decay = jnp.exp(jnp.where(mask, cc_wide[:, jh * q:(jh + 1) * q] - cr, -jnp.inf))
        weights.append((cb * decay).astype(BF16))
    xd_diag = jnp.concatenate([xd_b] * j, axis=0) * diag_mask
    y = jnp.dot(jnp.concatenate(weights, axis=1), xd_diag, preferred_element_type=F32)
    s_prev = state_ref[...]
    y = y + jnp.exp(cc) * jnp.dot(cm, s_prev.astype(BF16), preferred_element_type=F32)
    st = jnp.dot(bt, (xd * jnp.exp(total - cc)).astype(BF16), preferred_element_type=F32)
    state_ref[...] = jnp.exp(total) * s_prev + st
    return y


def _ssd_body(tab_ref, xf_ref, btf_ref, cf_ref, facf_ref, crf_ref, xb_ref, btb_ref, cb_ref, facb_ref, crb_ref,
              self_ref, selb_ref, mask_ref, initf_ref, initb_ref, yf_ref, yb_ref, sf_ref, sb_ref, stf_ref, stb_ref,
              *, heads_per_group, heads):
    g = pl.program_id(0)
    item = pl.program_id(1)

    @pl.when(tab_ref[_SSD_FIRST, item] == 1)
    def _():
        stf_ref[...] = initf_ref[...]
        stb_ref[...] = initb_ref[...]

    jp = heads_per_group * SSD_HEAD_DIM
    n = SSD_STATE
    for gi in range(SSD_GROUPS_PER_STEP):
        head0 = (g * SSD_GROUPS_PER_STEP + gi) * heads_per_group
        cols, srow = slice(gi * jp, (gi + 1) * jp), slice(gi * n, (gi + 1) * n)
        yf_ref[:, cols] = _ssd_dir(xf_ref[:, cols], btf_ref[srow, :], cf_ref[:, srow], facf_ref[...], self_ref[gi],
                                   mask_ref[...], crf_ref, stf_ref.at[gi], head0, reverse=False,
                                   heads_per_group=heads_per_group).astype(yf_ref.dtype)
        yb_ref[:, cols] = _ssd_dir(xb_ref[:, cols], btb_ref[srow, :], cb_ref[:, srow], facb_ref[...], selb_ref[gi],
                                   mask_ref[...], crb_ref, stb_ref.at[gi], heads + head0, reverse=True,
                                   heads_per_group=heads_per_group).astype(yb_ref.dtype)

    @pl.when(tab_ref[_SSD_LAST, item] == 1)
    def _():
        sf_ref[...] = stf_ref[...]
        sb_ref[...] = stb_ref[...]


_SSD_FWD, _SSD_BWD, _SSD_SEQ, _SSD_FIRST, _SSD_LAST = range(5)
SSD_GROUPS_PER_STEP = 8


def ssd_scan(u, bt_all, fac, cum_r, init_f, init_b, *, seq_lens, col0, inner):
    q, p, n, gq = SSD_CHUNK, SSD_HEAD_DIM, SSD_STATE, SSD_GROUPS
    heads = inner // p
    j = heads // gq
    jp = j * p
    xcb = col0 // jp
    ccb = (col0 + inner + gq * n) // n
    rows, base = [], 0
    for s, length in enumerate(seq_lens):
        nc = length // q
        rows += [(base + c, base + nc - 1 - c, s, int(c == 0), int(c == nc - 1)) for c in range(nc)]
        base += nc
    table = jnp.array(rows, jnp.int32).T
    nseq, n_items = len(seq_lens), len(rows)

    gs = SSD_GROUPS_PER_STEP
    assert xcb % gs == 0 and ccb % gs == 0 and gq % gs == 0

    def specs(which):
        return [pl.BlockSpec((q, gs * jp), lambda g, it, tab: (tab[which, it], xcb // gs + g)),
                pl.BlockSpec((gs * n, q), lambda g, it, tab: (g, tab[which, it])),
                pl.BlockSpec((q, gs * n), lambda g, it, tab: (tab[which, it], ccb // gs + g)),
                pl.BlockSpec((q, fac.shape[1]), lambda g, it, tab: (tab[which, it], 0)),
                pl.BlockSpec((2 * heads, q), lambda g, it, tab: (0, tab[which, it]))]

    st_spec = pl.BlockSpec((None, gs, n, jp), lambda g, it, tab: (tab[_SSD_SEQ, it], g, 0, 0))
    st_shape = jax.ShapeDtypeStruct((nseq, gq, n, jp), F32)
    y_shape = jax.ShapeDtypeStruct((u.shape[0], inner), BF16)
    sel_all = ssd_selectors(2 * heads, j, p, q)
    sel_block = (gs,) + sel_all.shape[1:]
    diag_mask = (jnp.arange(j * q)[:, None] // q == jnp.arange(jp)[None, :] // p).astype(BF16)
    grid_spec = pltpu.PrefetchScalarGridSpec(
        num_scalar_prefetch=1,
        grid=(gq // gs, n_items),
        in_specs=[*specs(_SSD_FWD), *specs(_SSD_BWD),
                  pl.BlockSpec(sel_block, lambda g, it, tab: (g, 0, 0)),
                  pl.BlockSpec(sel_block, lambda g, it, tab: (gq // gs + g, 0, 0)),
                  pl.BlockSpec(diag_mask.shape, lambda g, it, tab: (0, 0)),
                  st_spec, st_spec],
        out_specs=(pl.BlockSpec((q, gs * jp), lambda g, it, tab: (tab[_SSD_FWD, it], g)),
                   pl.BlockSpec((q, gs * jp), lambda g, it, tab: (tab[_SSD_BWD, it], g)), st_spec, st_spec),
        scratch_shapes=[pltpu.VMEM((gs, n, jp), F32), pltpu.VMEM((gs, n, jp), F32)])
    return pl.pallas_call(
        functools.partial(_ssd_body, heads_per_group=j, heads=heads),
        out_shape=(y_shape, y_shape, st_shape, st_shape),
        grid_spec=grid_spec,
        compiler_params=_cp("parallel", "arbitrary"),
        name="ssd_scan",
    )(table, u, bt_all, u, fac, cum_r, u, bt_all, u, fac, cum_r, sel_all, sel_all, diag_mask, init_f, init_b)


def _state_to_kernel_layout(s, groups):
    b, h, p, n = s.shape
    return s.reshape(b, groups, h // groups, p, n).transpose(0, 1, 4, 2, 3).reshape(b, groups, n, (h // groups) * p)


def _state_from_kernel_layout(s, head_dim):
    b, g, n, jp = s.shape
    j = jp // head_dim
    return s.reshape(b, g, n, j, head_dim).transpose(0, 1, 3, 4, 2).reshape(b, g * j, head_dim, n)


def _ssd_finish_body(yf_ref, yb_ref, xs_ref, z_ref, d_ref, w_ref, o_ref):
    inner = yf_ref.shape[1]
    lead = o_ref.shape[1] - inner
    y = yf_ref[...].astype(F32) + yb_ref[...].astype(F32) + xs_ref[...].astype(F32) * d_ref[...]
    y = y * _silu(z_ref[...].astype(F32))
    y = y * lax.rsqrt(jnp.mean(y * y, axis=-1, keepdims=True) + RMS_EPS)
    o_ref[:, :lead] = jnp.zeros((o_ref.shape[0], lead), o_ref.dtype)
    o_ref[:, lead:] = (y * w_ref[...]).astype(o_ref.dtype)


def ssd_finish(yf, yb, u, proj, d_full, norm_w, *, xs_col, z_col, inner, lead_cols):
    t = yf.shape[0]
    tr = ROW_TILE
    return pl.pallas_call(
        _ssd_finish_body,
        out_shape=jax.ShapeDtypeStruct((t, lead_cols + inner), BF16),
        grid=(t // tr,),
        in_specs=[pl.BlockSpec((tr, inner), lambda i: (i, 0)),
                  pl.BlockSpec((tr, inner), lambda i: (i, 0)),
                  pl.BlockSpec((tr, inner), lambda i: (i, xs_col // inner)),
                  pl.BlockSpec((tr, inner), lambda i: (i, z_col // inner)),
                  pl.BlockSpec((1, inner), lambda i: (0, 0)),
                  pl.BlockSpec((1, inner), lambda i: (0, 0))],
        out_specs=pl.BlockSpec((tr, lead_cols + inner), lambda i: (i, 0)),
        compiler_params=_cp("parallel"),
        name="ssd_finish",
    )(yf, yb, u, proj, d_full.reshape(1, inner), norm_w.reshape(1, inner))


KV_PER_STEP = LANES // HEAD_DIM


_NT = (((1,), (1,)), ((), ()))


def _attend(scores, values, sink):
    m = sink
    for s in scores:
        m = jnp.maximum(m, jnp.max(s, axis=-1, keepdims=True))
    denom = jnp.exp(sink - m)
    acc = None
    for s, v in zip(scores, values):
        e = jnp.exp(s - m)
        denom = denom + jnp.sum(e, axis=-1, keepdims=True)
        part = jnp.dot(e.astype(BF16), v, preferred_element_type=F32)
        acc = part if acc is None else acc + part
    return acc * (1.0 / denom)


def _stack_heads(q, first_head, n, rows):
    return jnp.concatenate([q[:, (first_head + g) * HEAD_DIM:(first_head + g + 1) * HEAD_DIM] for g in range(n)],
                           axis=0)


def _stacked_sinks(sink_ref, first_head, n, rows):
    row = lax.broadcasted_iota(jnp.int32, (n * rows, 1), 0)
    col = jnp.full((n * rows, 1), sink_ref[first_head + n - 1], F32)
    for g in reversed(range(n - 1)):
        col = jnp.where(row < (g + 1) * rows, sink_ref[first_head + g], col)
    return col


def _ctx_attn_body(sink_ref, q_ref, k_ref, v_ref, fill_ref, o_ref, *, q_per_kv, scale):
    del fill_ref
    hb = pl.program_id(1)
    rows = q_ref.shape[0]
    q_all = q_ref[...].astype(F32) * scale
    outs = []
    for kv in range(KV_PER_STEP):
        k = k_ref[:, kv * HEAD_DIM:(kv + 1) * HEAD_DIM].astype(BF16)
        v = v_ref[:, kv * HEAD_DIM:(kv + 1) * HEAD_DIM].astype(BF16)
        qs = _stack_heads(q_all, kv * q_per_kv, q_per_kv, rows).astype(BF16)
        s = lax.dot_general(qs, k, _NT, preferred_element_type=F32)
        sink = _stacked_sinks(sink_ref, (hb * KV_PER_STEP + kv) * q_per_kv, q_per_kv, rows)
        o = _attend([s], [v], sink)
        outs += [o[g * rows:(g + 1) * rows] for g in range(q_per_kv)]
    o_ref[...] = jnp.concatenate(outs, axis=1).astype(o_ref.dtype)


def context_attention(q, kv, sink, *, nseq, L, n_heads):
    q_per_kv = n_heads // N_KV_HEADS
    qw = KV_PER_STEP * q_per_kv * HEAD_DIM
    nhb = N_KV_HEADS // KV_PER_STEP
    kcb = 0
    vcb = N_KV_HEADS * HEAD_DIM // LANES
    return pl.pallas_call(
        functools.partial(_ctx_attn_body, q_per_kv=q_per_kv, scale=HEAD_DIM ** -0.5),
        out_shape=jax.ShapeDtypeStruct(q.shape, BF16),
        grid=(nseq, nhb),
        in_specs=[pl.BlockSpec(memory_space=pltpu.SMEM),
                  pl.BlockSpec((L, qw), lambda b, h: (b, h)),
                  pl.BlockSpec((L, LANES), lambda b, h: (b, kcb + h)),
                  pl.BlockSpec((L, LANES), lambda b, h: (b, vcb + h)),
                  pl.BlockSpec(memory_space=pl.ANY)],
        out_specs=pl.BlockSpec((L, qw), lambda b, h: (b, h)),
        input_output_aliases={4: 0},
        compiler_params=_cp("parallel", "parallel"),
        name="context_attention",
    )(sink, q, kv, kv, zeros_2d(q.shape[0], n_heads * HEAD_DIM, BF16))


def _rope(x, cos, sin):
    outs = []
    lane = lax.broadcasted_iota(jnp.int32, cos.shape, 1)
    first = (lane % (HEAD_DIM // 2)) < (HEAD_DIM // 4)
    for cgrp in range(x.shape[1] // LANES):
        xc = x[:, cgrp * LANES:(cgrp + 1) * LANES]
        partner = jnp.where(first, pltpu.roll(xc, LANES - HEAD_DIM // 4, 1), pltpu.roll(xc, HEAD_DIM // 4, 1))
        outs.append(xc * cos + partner * sin)
    return outs[0] if len(outs) == 1 else jnp.concatenate(outs, axis=1)


def _lat_attn_body(sink_ref, q_ref, k_ref, v_ref, kc_ref, vc_ref, cosq_ref, sinq_ref, cos_ref, sin_ref, fill_ref,
                   o_ref, k_scr, v_scr, kc_scr, vc_scr, *, q_per_kv, scale, L, blk):
    del fill_ref
    hb = pl.program_id(1)
    i = pl.program_id(2)

    @pl.when(i == 0)
    def _():
        k_scr[...] = _rope(k_ref[...], cos_ref[...], sin_ref[...]).astype(BF16)
        v_scr[...] = v_ref[...].astype(BF16)
        kc_scr[...] = kc_ref[...].astype(BF16)
        vc_scr[...] = vc_ref[...].astype(BF16)

    n_loc = 3 * blk
    start = jnp.clip((i - 1) * blk, 0, L - n_loc)
    start = pl.multiple_of(start, blk)
    q_all = _rope(q_ref[...].astype(F32), cosq_ref[...], sinq_ref[...]) * scale
    k_loc = k_scr[pl.ds(start, n_loc), :]
    v_loc = v_scr[pl.ds(start, n_loc), :]
    rows = q_per_kv * blk
    qpos = i * blk + (lax.broadcasted_iota(jnp.int32, (rows, n_loc), 0) & (blk - 1))
    kpos = start + lax.broadcasted_iota(jnp.int32, (rows, n_loc), 1)
    ok = jnp.abs(qpos - kpos) <= WINDOW
    outs = []
    for kv in range(KV_PER_STEP):
        sl = slice(kv * HEAD_DIM, (kv + 1) * HEAD_DIM)
        qs = _stack_heads(q_all, kv * q_per_kv, q_per_kv, blk).astype(BF16)
        s_loc = jnp.where(ok, lax.dot_general(qs, k_loc[:, sl], _NT, preferred_element_type=F32), -jnp.inf)
        s_ctx = lax.dot_general(qs, kc_scr[:, sl], _NT, preferred_element_type=F32)
        sink = _stacked_sinks(sink_ref, (hb * KV_PER_STEP + kv) * q_per_kv, q_per_kv, blk)
        o = _attend([s_loc, s_ctx], [v_loc[:, sl], vc_scr[:, sl]], sink)
        outs += [o[g * blk:(g + 1) * blk] for g in range(q_per_kv)]
    o_ref[...] = jnp.concatenate(outs, axis=1).astype(o_ref.dtype)


def rope_tables(L):
    rows = L // GRID_W
    row = jnp.repeat(jnp.arange(rows, dtype=F32), GRID_W)
    col = jnp.tile(jnp.arange(GRID_W, dtype=F32), rows)
    quarter = HEAD_DIM // 4
    inv = ROPE_BASE ** (-jnp.arange(quarter, dtype=F32) / quarter)
    ang_r = row[:, None] * inv[None, :]
    ang_c = col[:, None] * inv[None, :]
    ang = jnp.concatenate([ang_r, ang_r, ang_c, ang_c], axis=1)
    sign = jnp.tile(jnp.concatenate([-jnp.ones((quarter,), F32), jnp.ones((quarter,), F32)]), 2)
    cos = jnp.tile(jnp.cos(ang), (1, LANES // HEAD_DIM))
    sin = jnp.tile(jnp.sin(ang) * sign[None, :], (1, LANES // HEAD_DIM))
    return cos, sin


def latent_attention(q, kv, k_cache, v_cache, sink, fill, *, row0, nseq, L, n_heads):
    blk = WINDOW
    q_per_kv = n_heads // N_KV_HEADS
    qw = KV_PER_STEP * q_per_kv * HEAD_DIM
    nhb = N_KV_HEADS // KV_PER_STEP
    kcb = 0
    vcb = N_KV_HEADS * HEAD_DIM // LANES
    nb = L // blk
    past = k_cache.shape[1]
    cos, sin = rope_tables(L)
    return pl.pallas_call(
        functools.partial(_lat_attn_body, q_per_kv=q_per_kv, scale=HEAD_DIM ** -0.5, L=L, blk=blk),
        out_shape=jax.ShapeDtypeStruct(fill.shape, BF16),
        grid=(nseq, nhb, nb),
        in_specs=[pl.BlockSpec(memory_space=pltpu.SMEM),
                  pl.BlockSpec((blk, qw), lambda b, h, i: (row0 // blk + b * nb + i, h)),
                  pl.BlockSpec((L, LANES), lambda b, h, i: (row0 // L + b, kcb + h)),
                  pl.BlockSpec((L, LANES), lambda b, h, i: (row0 // L + b, vcb + h)),
                  pl.BlockSpec((None, past, LANES), lambda b, h, i: (b, 0, h)),
                  pl.BlockSpec((None, past, LANES), lambda b, h, i: (b, 0, h)),
                  pl.BlockSpec((blk, LANES), lambda b, h, i: (i, 0)),
                  pl.BlockSpec((blk, LANES), lambda b, h, i: (i, 0)),
                  pl.BlockSpec((L, LANES), lambda b, h, i: (0, 0)),
                  pl.BlockSpec((L, LANES), lambda b, h, i: (0, 0)),
                  pl.BlockSpec(memory_space=pl.ANY)],
        out_specs=pl.BlockSpec((blk, qw), lambda b, h, i: (row0 // blk + b * nb + i, h)),
        scratch_shapes=[pltpu.VMEM((L, LANES), BF16), pltpu.VMEM((L, LANES), BF16),
                        pltpu.VMEM((past, LANES), BF16), pltpu.VMEM((past, LANES), BF16)],
        input_output_aliases={10: 0},
        compiler_params=_cp("parallel", "parallel", "arbitrary"),
        name="latent_attention",
    )(sink, q, kv, kv, k_cache, v_cache, cos, sin, cos, sin, fill)


MOE_TILE = 256
MOE_ROWS_PER_STEP = 512
MOE_FFN_VMEM_BYTES = 56 * 1024 * 1024


def _moe_dispatch_body(pos_ref, h_ref, g_ref, init_hbm, xs_hbm, buf, sem):
    del init_hbm
    i = pl.program_id(0)
    tr, d = h_ref.shape
    buf[:, :d] = h_ref[...]
    buf[:, d:] = g_ref[...]

    def start(r2, carry):
        for prio in range(DMA_PRIORITIES):
            r = r2 * DMA_PRIORITIES + prio
            pltpu.make_async_copy(buf.at[pl.ds(r, 1)], xs_hbm.at[pl.ds(pos_ref[i * tr + r], 1)],
                                  sem).start(priority=prio)
        return carry

    lax.fori_loop(0, tr // DMA_PRIORITIES, start, 0)

    def wait(r, carry):
        pltpu.make_async_copy(buf.at[pl.ds(r, 1)], xs_hbm.at[pl.ds(0, 1)], sem).wait()
        return carry

    lax.fori_loop(0, tr, wait, 0)


def moe_dispatch(h, gate_lanes, pos, n_slots, tr):
    t, d = h.shape
    w = d + LANES
    grid_spec = pltpu.PrefetchScalarGridSpec(
        num_scalar_prefetch=1,
        grid=(t // tr,),
        in_specs=[pl.BlockSpec((tr, d), lambda i, p: (i, 0)),
                  pl.BlockSpec((tr, LANES), lambda i, p: (i, 0)),
                  pl.BlockSpec(memory_space=pl.ANY)],
        out_specs=pl.BlockSpec(memory_space=pl.ANY),
        scratch_shapes=[pltpu.VMEM((tr, w), F32), pltpu.SemaphoreType.DMA(())])
    return pl.pallas_call(
        _moe_dispatch_body,
        out_shape=jax.ShapeDtypeStruct((n_slots, w), F32),
        grid_spec=grid_spec,
        input_output_aliases={3: 0},
        compiler_params=_cp("arbitrary"),
        name="moe_dispatch",
    )(pos, h, gate_lanes, zeros_2d(n_slots, w, F32))


def _moe_ffn_body(ea_ref, eb_ref, nused_ref, xs_ref, w1a_ref, w3a_ref, w2a_ref, w1b_ref, w3b_ref, w2b_ref, o_ref):
    i = pl.program_id(0)
    d = o_ref.shape[1]

    @pl.when(i < nused_ref[0])
    def _():
        x = xs_ref[:, :d].astype(BF16)

        def expert(w1_ref, w3_ref, w2_ref):
            a = jnp.dot(x, w1_ref[...], preferred_element_type=F32)
            b = jnp.dot(x, w3_ref[...], preferred_element_type=F32)
            return jnp.dot((_silu(a) * b).astype(BF16), w2_ref[...], preferred_element_type=F32)

        o_ref[...] = (xs_ref[:, d:d + 1] * expert(w1a_ref, w3a_ref, w2a_ref)
                      + xs_ref[:, d + 1:d + 2] * expert(w1b_ref, w3b_ref, w2b_ref))

    @pl.when(i >= nused_ref[0])
    def _():
        o_ref[...] = jnp.zeros_like(o_ref)


def moe_ffn(xs, tile_ea, tile_eb, n_used, w1, w3, w2, layer):
    _, _, d, de = w1.shape
    tm = MOE_TILE
    n_tiles = tile_ea.shape[0]
    up_a = pl.BlockSpec((None, None, d, de), lambda i, ea, eb, nu: (layer, ea[i], 0, 0))
    up_b = pl.BlockSpec((None, None, d, de), lambda i, ea, eb, nu: (layer, eb[i], 0, 0))
    grid_spec = pltpu.PrefetchScalarGridSpec(
        num_scalar_prefetch=3,
        grid=(n_tiles,),
        in_specs=[pl.BlockSpec((tm, d + LANES), lambda i, ea, eb, nu: (i, 0)),
                  up_a, up_a, pl.BlockSpec((None, None, de, d), lambda i, ea, eb, nu: (layer, ea[i], 0, 0)),
                  up_b, up_b, pl.BlockSpec((None, None, de, d), lambda i, ea, eb, nu: (layer, eb[i], 0, 0))],
        out_specs=pl.BlockSpec((tm, d), lambda i, ea, eb, nu: (i, 0)))
    return pl.pallas_call(
        _moe_ffn_body,
        out_shape=jax.ShapeDtypeStruct((n_tiles * tm, d), F32),
        grid_spec=grid_spec,
        compiler_params=_cp("arbitrary", vmem=MOE_FFN_VMEM_BYTES),
        name="moe_ffn",
    )(tile_ea, tile_eb, n_used, xs, w1, w3, w2, w1, w3, w2)


def _moe_combine_body(pos_ref, y_hbm, x_ref, g_ref, *rest, next_norm):
    if next_norm:
        ng_ref, sh_ref, sc_ref, o_ref, h_ref, ybuf, sem = rest
    else:
        o_ref, ybuf, sem = rest
    i = pl.program_id(0)
    tr = x_ref.shape[0]

    def start(r2, carry):
        for prio in range(DMA_PRIORITIES):
            r = r2 * DMA_PRIORITIES + prio
            pltpu.make_async_copy(y_hbm.at[pl.ds(pos_ref[i * tr + r], 1)], ybuf.at[pl.ds(r, 1)],
                                  sem).start(priority=prio)
        return carry

    lax.fori_loop(0, tr // DMA_PRIORITIES, start, 0)

    def wait(r, carry):
        pltpu.make_async_copy(y_hbm.at[pl.ds(0, 1)], ybuf.at[pl.ds(r, 1)], sem).wait()
        return carry

    lax.fori_loop(0, tr, wait, 0)
    x_new = x_ref[...] + g_ref[...] * ybuf[...]
    o_ref[...] = x_new
    if next_norm:
        y = x_new * lax.rsqrt(jnp.mean(x_new * x_new, axis=-1, keepdims=True) + RMS_EPS)
        h_ref[...] = (y * ng_ref[...] * (1.0 + sc_ref[...]) + sh_ref[...]).astype(h_ref.dtype)


def moe_combine(x, y_sorted, pos, mod3, gate_idx, seg_of_rows, tr, next_norm=None):
    t, d = x.shape
    row = pl.BlockSpec((tr, d), lambda i, p: (i, 0))
    mod_spec = lambda idx: pl.BlockSpec((None, 1, d), lambda i, p: (seg_of_rows(i * tr) * 6 + idx, 0, 0))
    in_specs = [pl.BlockSpec(memory_space=pl.ANY), row, mod_spec(gate_idx)]
    args = [pos, y_sorted, x, mod3]
    out_shape, out_specs = jax.ShapeDtypeStruct((t, d), F32), row
    if next_norm is not None:
        g_next, mod3_next = next_norm
        in_specs += [pl.BlockSpec((1, d), lambda i, p: (0, 0)), mod_spec(0), mod_spec(1)]
        args += [g_next.reshape(1, d), mod3_next, mod3_next]
        out_shape, out_specs = (out_shape, jax.ShapeDtypeStruct((t, d), BF16)), (row, row)
    grid_spec = pltpu.PrefetchScalarGridSpec(
        num_scalar_prefetch=1,
        grid=(t // tr,),
        in_specs=in_specs,
        out_specs=out_specs,
        scratch_shapes=[pltpu.VMEM((tr, d), F32), pltpu.SemaphoreType.DMA(())])
    return pl.pallas_call(
        functools.partial(_moe_combine_body, next_norm=next_norm is not None),
        out_shape=out_shape,
        grid_spec=grid_spec,
        compiler_params=_cp("arbitrary"),
        name="moe_combine",
    )(*args)


def moe_route(logits, n_groups, n_experts):
    per = n_experts // n_groups
    assert TOP_K == 2
    g_prob = jax.nn.softmax(logits[:, :n_groups], axis=-1)
    g_top = jnp.max(g_prob, axis=-1, keepdims=True)
    g_idx = jnp.argmax(g_prob, axis=-1, keepdims=True).astype(jnp.int32)
    e_logits = logits[:, n_groups:n_groups + n_experts].reshape(-1, n_groups, per)
    g_onehot = g_idx == jnp.arange(n_groups)[None, :]
    e_in = jnp.sum(jnp.where(g_onehot[:, :, None], e_logits, 0.0), axis=1)
    e_prob = jax.nn.softmax(e_in, axis=-1)
    i0 = jnp.argmax(e_prob, axis=-1, keepdims=True).astype(jnp.int32)
    rest = jnp.where(jnp.arange(per)[None, :] == i0, -jnp.inf, e_prob)
    i1 = jnp.argmax(rest, axis=-1, keepdims=True).astype(jnp.int32)
    e_idx = jnp.concatenate([i0, i1], axis=1)
    e_top = jnp.concatenate([jnp.max(e_prob, axis=-1, keepdims=True), jnp.max(rest, axis=-1, keepdims=True)], axis=1)
    e_top = e_top / jnp.sum(e_top, axis=-1, keepdims=True)
    return g_idx * per + e_idx, g_top * e_top


def moe_plan(expert_ids, gates, n_groups, n_experts):
    assert TOP_K == 2
    t = expert_ids.shape[0]
    tm = MOE_TILE
    per = n_experts // n_groups
    n_pairs = per * (per - 1) // 2
    n_classes = n_groups * n_pairs
    n_tiles = t // tm + n_classes
    e0, e1 = expert_ids[:, 0], expert_ids[:, 1]
    lo, hi = jnp.minimum(e0, e1), jnp.maximum(e0, e1)
    g_lo = jnp.where(e0 < e1, gates[:, 0], gates[:, 1])
    g_hi = jnp.where(e0 < e1, gates[:, 1], gates[:, 0])
    a, b = lo % per, hi % per
    cls = (lo // per) * n_pairs + (a * (2 * per - a - 1)) // 2 + (b - a - 1)
    onehot = (cls[:, None] == jnp.arange(n_classes)[None, :]).astype(F32)
    blk = ROW_TILE
    inner = jnp.einsum("ij,bjk->bik", jnp.tril(jnp.ones((blk, blk), F32)), onehot.reshape(-1, blk, n_classes))
    bsum = inner[:, -1, :]
    running = (inner + (jnp.cumsum(bsum, axis=0) - bsum)[:, None, :]).reshape(-1, n_classes)
    counts = jnp.sum(bsum, axis=0).astype(jnp.int32)
    tiles_per = (counts + tm - 1) // tm
    tile_start = jnp.cumsum(tiles_per) - tiles_per
    slot = running - 1.0 + (tile_start * tm).astype(F32)[None, :]
    pos = jnp.sum(jnp.where(onehot > 0, slot, 0.0), axis=1).astype(jnp.int32)
    n_used = jnp.sum(tiles_per).astype(jnp.int32)
    tile_ids = jnp.arange(n_tiles, dtype=jnp.int32)
    tile_cls = jnp.sum((tile_ids[:, None] >= tile_start[None, :]).astype(jnp.int32), axis=1) - 1
    tile_cls = jnp.where(tile_ids < n_used, tile_cls, tile_cls[jnp.maximum(n_used - 1, 0)])
    pairs = [(pa, pb) for pa in range(per) for pb in range(pa + 1, per)]
    cls_lo = jnp.array([g * per + pa for g in range(n_groups) for pa, _ in pairs], jnp.int32)
    cls_hi = jnp.array([g * per + pb for g in range(n_groups) for _, pb in pairs], jnp.int32)
    gate_lanes = jnp.pad(jnp.stack([g_lo, g_hi], axis=1), ((0, 0), (0, LANES - 2)))
    tile_onehot = tile_cls[:, None] == jnp.arange(n_classes)[None, :]
    tile_lo = jnp.sum(jnp.where(tile_onehot, cls_lo[None, :], 0), axis=1)
    tile_hi = jnp.sum(jnp.where(tile_onehot, cls_hi[None, :], 0), axis=1)
    return pos, gate_lanes, tile_lo, tile_hi, n_used.reshape(1), n_tiles * tm


def kernel(x_prompt, x_sample, state_ssd_fwd, state_ssd_bwd, cache_attn_k, cache_attn_v, c, c_ctx, ada_w, ada_b, norm_mix, norm_ffn, norm_final, ev_w_in, ev_w_out, hy_conv_w, hy_conv_b, hy_f_w1, hy_f_b1, hy_f_freq1, hy_f_w2, hy_f_b2, hy_f_freq2, hy_f_w3, hy_bias, ssd_conv_w, ssd_conv_b, ssd_dt_bias, ssd_a_log, ssd_d, ssd_norm, at_wq, at_wk, at_wv, at_wo, at_sink, moe_w_group, moe_b_group, moe_w_expert, moe_b_expert, moe_w1, moe_w3, moe_w2):
    bc, lc, d = x_prompt.shape
    bl, ll, _ = x_sample.shape
    tc_rows, tl_rows = bc * lc, bl * ll
    t = tc_rows + tl_rows
    depth = ada_w.shape[0]
    n_seg = 1 + bl
    assert lc % ROW_TILE == 0 and ll % ROW_TILE == 0 and tc_rows % ll == 0
    seg_rows = math.gcd(tc_rows, ll)

    def seg_of_rows(r):
        return jnp.where(r < tc_rows, 0, 1 + (r - tc_rows) // ll)

    def seg_of_tile(i):
        return seg_of_rows(i * ROW_TILE)

    x = jnp.concatenate([x_prompt.reshape(tc_rows, d), x_sample.reshape(tl_rows, d)], axis=0)
    cvec = jnp.concatenate([c_ctx[None], c, jnp.zeros((SUBLANES - n_seg, d), F32)], axis=0)
    mod_all = ada_mod(cvec, ada_w, ada_b)

    tile_row = jnp.arange(t // ROW_TILE, dtype=jnp.int32) * ROW_TILE
    seq_len = jnp.where(tile_row < tc_rows, lc, ll)
    rel = jnp.where(tile_row < tc_rows, tile_row, tile_row - tc_rows)
    conv_flags = jnp.stack([(rel % seq_len == 0), ((rel + ROW_TILE) % seq_len == 0)]).astype(jnp.int32)

    w1_bf, w3_bf, w2_bf = moe_w1.astype(BF16), moe_w3.astype(BF16), moe_w2.astype(BF16)
    new_sf, new_sb, new_k, new_v = [], [], [], []
    h_next = None
    for l in range(depth):
        i = l // 2
        mod3 = mod_all[l, :n_seg].reshape(n_seg * 6, 1, d)
        if h_next is None:
            h = rms_norm(x, norm_mix[l], out_dtype=BF16, mod3=mod3, shift_idx=0, scale_idx=1,
                         seg_of_tile=seg_of_tile)
        else:
            h = h_next
        if l % 2 == 0:
            hyena_w = hy_conv_w.shape[2] // 3
            inner = ssd_norm.shape[1]
            heads = inner // SSD_HEAD_DIM
            conv_ch = ssd_conv_w.shape[2]
            main_cols = 3 * hyena_w + inner + conv_ch
            w_in = ev_w_in[i].astype(BF16)
            proj = dense(h, w_in, n_cols=main_cols, out_dtype=BF16, name="in_proj")
            dt_raw = dense(h, w_in[:, main_cols:], out_dtype=F32, name="dt_proj")
            u = dwconv(proj, hy_conv_w[i], hy_conv_b[i], conv_flags, col0=0, silu=False)
            u_ssd = dwconv(proj, ssd_conv_w[i], ssd_conv_b[i], conv_flags, col0=3 * hyena_w + inner,
                           silu=True)
            gn = SSD_GROUPS * SSD_STATE
            bt_all = u_ssd[:, inner:inner + gn].T
            fac, cum_r = ssd_prep(dt_raw, dt_raw.T, ssd_dt_bias[i], ssd_a_log[i])
            zero = jnp.zeros((bc, SSD_GROUPS, SSD_STATE, inner // SSD_GROUPS), F32)
            init_f = jnp.concatenate([zero, _state_to_kernel_layout(state_ssd_fwd[:, i], SSD_GROUPS)], axis=0)
            init_b = jnp.concatenate([zero, _state_to_kernel_layout(state_ssd_bwd[:, i], SSD_GROUPS)], axis=0)
            yf, yb, sf, sb = ssd_scan(u_ssd, bt_all, fac, cum_r, init_f, init_b,
                                      seq_lens=[lc] * bc + [ll] * bl, col0=0, inner=inner)
            new_sf.append(_state_from_kernel_layout(sf[:bc], SSD_HEAD_DIM))
            new_sb.append(_state_from_kernel_layout(sb[:bc], SSD_HEAD_DIM))
            y_a = ssd_finish(yf, yb, u_ssd, proj, jnp.repeat(ssd_d[i], SSD_HEAD_DIM), ssd_norm[i],
                             xs_col=0, z_col=3 * hyena_w, inner=inner, lead_cols=hyena_w)
            for row0, nb, L in ((0, bc, lc), (tc_rows, bl, ll)):
                filt = hyena_filters(L, hy_f_w1[i], hy_f_b1[i], hy_f_freq1[i], hy_f_w2[i], hy_f_b2[i],
                                     hy_f_freq2[i], hy_f_w3[i], hyena_w)
                if L % (16 * SPEC_BLOCK) == 0:
                    n1 = 2 * L // SPEC_BLOCK
                    t1f = fft_table("t1f", n1, SPEC_BLOCK)
                    kf = hyena_spectrum_2level(L, filt, t1f, fft_table("f2filt", n1, SPEC_BLOCK))
                    tabs = (t1f, fft_table("f2f", n1, SPEC_BLOCK), fft_table("f2i", n1, SPEC_BLOCK),
                            fft_table("t1i", n1, SPEC_BLOCK))
                    y_a = hyena_stream_2level(u, row0, nb, L, kf, tabs, hy_bias[i], hyena_w, fill=y_a)
                else:
                    a_tab = dft_table(L, "filt")
                    kf = hyena_spectrum(L, filt, a_tab)
                    y_a = hyena_stream(u, row0, nb, L, kf, a_tab, dft_table(L, "inv"), hy_bias[i], hyena_w,
                                       fill=y_a)
            x = dense_residual(y_a, ev_w_out[i].astype(BF16), x, mod3, 2, seg_of_rows, seg_rows, name="out_proj")
        else:
            n_heads = at_wq.shape[2] // HEAD_DIM
            q = dense(h, at_wq[i].astype(BF16), out_dtype=BF16, name="q_proj")
            kv = dense(h, jnp.concatenate([at_wk[i], at_wv[i]], axis=1).astype(BF16), out_dtype=F32, name="kv_proj")
            kvw = N_KV_HEADS * HEAD_DIM
            att_c = context_attention(q, kv, at_sink[i], nseq=bc, L=lc, n_heads=n_heads)
            past = cache_attn_k.shape[2]
            att = latent_attention(q, kv, cache_attn_k[:, i].reshape(bl, past, kvw),
                                   cache_attn_v[:, i].reshape(bl, past, kvw), at_sink[i], att_c,
                                   row0=tc_rows, nseq=bl, L=ll, n_heads=n_heads)
            x = dense_residual(att, at_wo[i].astype(BF16), x, mod3, 2, seg_of_rows, seg_rows, name="attn_out")
            new_k.append(kv[:tc_rows, :kvw].reshape(bc, lc, N_KV_HEADS, HEAD_DIM))
            new_v.append(kv[:tc_rows, kvw:].reshape(bc, lc, N_KV_HEADS, HEAD_DIM))
        h2 = rms_norm(x, norm_ffn[l], out_dtype=F32, mod3=mod3, shift_idx=3, scale_idx=4,
                      seg_of_tile=seg_of_tile)
        n_experts = moe_w_expert.shape[2]
        w_r = jnp.concatenate([moe_w_group[l], moe_w_expert[l]], axis=1)
        w_r = jnp.pad(w_r, ((0, 0), (0, LANES - w_r.shape[1]))).astype(BF16)
        b_r = jnp.pad(jnp.concatenate([moe_b_group[l], moe_b_expert[l]]), (0, LANES - MOE_GROUPS - n_experts))
        logits = dense(h2, w_r, out_dtype=F32, extra=(b_r.reshape(1, LANES),),
                       extra_specs=(pl.BlockSpec((1, LANES), lambda bt, ii, j, kk: (0, 0)),),
                       epilogue=lambda acc, b_ref, row_tile: acc + b_ref[...], name="router")
        expert_ids, gates = moe_route(logits, MOE_GROUPS, n_experts)
        pos, gate_lanes, tile_lo, tile_hi, n_used, n_slots = moe_plan(expert_ids, gates, MOE_GROUPS, n_experts)
        moe_rows = _tile(seg_rows, MOE_ROWS_PER_STEP)
        xs = moe_dispatch(h2, gate_lanes, pos, n_slots, moe_rows)
        y_sorted = moe_ffn(xs, tile_lo, tile_hi, n_used, w1_bf, w3_bf, w2_bf, l)
        if l + 1 < depth:
            mod3_next = mod_all[l + 1, :n_seg].reshape(n_seg * 6, 1, d)
            x, h_next = moe_combine(x, y_sorted, pos, mod3, 5, seg_of_rows, moe_rows,
                                    next_norm=(norm_mix[l + 1], mod3_next))
        else:
            x = moe_combine(x, y_sorted, pos, mod3, 5, seg_of_rows, moe_rows)

    y_prompt = rms_norm(x, norm_final, out_dtype=F32, rows=tc_rows).reshape(bc, lc, d)
    y_sample = rms_norm(x, norm_final, out_dtype=F32, row_off=tc_rows, rows=tl_rows).reshape(bl, ll, d)
    return (y_prompt, y_sample, jnp.stack(new_sf, axis=1), jnp.stack(new_sb, axis=1),
            jnp.stack(new_k, axis=1), jnp.stack(new_v, axis=1))
```
